```python
import jax, jax.numpy as jnp
from jax import lax
import numpy as np

D_MODEL = 1024
BATCH = 4
SEQ = 8192
DEPTH = 2

D_FF = 2816
FFN_HALF = 0.5
N_HEADS = 8
Q_LORA = 352
KV_LORA = 128
QK_NOPE = 64
QK_ROPE = 32
V_HEAD = 64
ROPE_THETA = 10000.0
Q_BLOCK = 128
CONV_CH = 256
CONV_WIDTH = 31
POOL_WINDOWS = (2, 4, 8, 16)
POOL_GROUPS = 4
POOL_GC = 128
FNET_GROUPS = 4
FNET_GC = 128
N_BRANCH = 4
W_A = Q_LORA + KV_LORA + QK_ROPE
W_B = 2 * CONV_CH
W_C = POOL_GROUPS * POOL_GC
W_D = FNET_GROUPS * FNET_GC
W_MIX = W_A + W_B + W_C + W_D
W_IN = W_MIX + N_BRANCH * D_MODEL
N_SUB = 3
N_MOD = 3 * N_SUB
EPS = 1e-6

kernel_name = "hybrid_gated_mla_conv_pool_fnet_encoder"


def rmsnorm(x, g):
    x32 = x.astype(jnp.float32)
    y = x32 * lax.rsqrt(jnp.mean(x32 * x32, axis=-1, keepdims=True) + EPS)
    return (y * g.astype(jnp.float32)).astype(x.dtype)


def layernorm(x, g, b):
    x32 = x.astype(jnp.float32)
    mu = jnp.mean(x32, axis=-1, keepdims=True)
    xc = x32 - mu
    y = xc * lax.rsqrt(jnp.mean(xc * xc, axis=-1, keepdims=True) + EPS)
    return (y * g.astype(jnp.float32) + b.astype(jnp.float32)).astype(x.dtype)


def modulate(h, shift, scale):
    return h * (1 + scale[:, None, :]) + shift[:, None, :]


def swiglu(h, w_in, w_out):
    gu = h @ w_in
    g, u = jnp.split(gu, 2, axis=-1)
    return (jax.nn.silu(g) * u) @ w_out


def rope_tables(seq, dim):
    pos = jnp.arange(seq, dtype=jnp.float32)
    inv = ROPE_THETA ** (-jnp.arange(0, dim, 2, dtype=jnp.float32) / dim)
    ang = pos[:, None] * inv[None, :]
    return jnp.cos(ang), jnp.sin(ang)


def apply_rope(x, cos, sin):
    x1, x2 = jnp.split(x, 2, axis=-1)
    c = cos.astype(x.dtype)
    s = sin.astype(x.dtype)
    return jnp.concatenate([x1 * c - x2 * s, x1 * s + x2 * c], axis=-1)


def dense_attention(q, k, v):
    B, S, H, Dq = q.shape
    nb = S // Q_BLOCK
    qb = q.reshape(B, nb, Q_BLOCK, H, Dq).transpose(1, 0, 2, 3, 4)
    sm_scale = Dq ** -0.5

    def one_block(qblk):
        s = jnp.einsum('bqhd,bkhd->bhqk', qblk, k).astype(jnp.float32) * sm_scale
        p = jax.nn.softmax(s, axis=-1).astype(v.dtype)
        return jnp.einsum('bhqk,bkhd->bqhd', p, v)

    o = lax.map(one_block, qb)
    return o.transpose(1, 0, 2, 3, 4).reshape(B, S, H, v.shape[-1])


def mla_mixer(za, q_norm_g, w_uq, kv_norm_g, w_ukv, w_a):
    B, S, _ = za.shape
    cq = za[..., :Q_LORA]
    ckv = za[..., Q_LORA:Q_LORA + KV_LORA]
    kr = za[..., Q_LORA + KV_LORA:]
    cos, sin = rope_tables(S, QK_ROPE)
    q = (rmsnorm(cq, q_norm_g) @ w_uq).reshape(B, S, N_HEADS, QK_NOPE + QK_ROPE)
    q_nope, q_rope = q[..., :QK_NOPE], q[..., QK_NOPE:]
    q_rope = apply_rope(q_rope, cos[:, None, :], sin[:, None, :])
    kv = (rmsnorm(ckv, kv_norm_g) @ w_ukv).reshape(B, S, N_HEADS, QK_NOPE + V_HEAD)
    k_nope, v = kv[..., :QK_NOPE], kv[..., QK_NOPE:]
    k_rope = apply_rope(kr, cos, sin)
    k_rope = jnp.broadcast_to(k_rope[:, :, None, :], (B, S, N_HEADS, QK_ROPE))
    qf = jnp.concatenate([q_nope, q_rope], axis=-1)
    kf = jnp.concatenate([k_nope, k_rope], axis=-1)
    o = dense_attention(qf, kf, v).reshape(B, S, N_HEADS * V_HEAD)
    return o @ w_a


def conv_mixer(zb, conv_w, conv_b, ln_g, ln_b, w_b):
    a, gt = jnp.split(zb, 2, axis=-1)
    u = a * jax.nn.sigmoid(gt)
    pad = CONV_WIDTH // 2
    y = lax.conv_general_dilated(
        u, conv_w[:, None, :].astype(u.dtype), window_strides=(1,), padding=[(pad, pad)],
        dimension_numbers=('NWC', 'WIO', 'NWC'), feature_group_count=CONV_CH) + conv_b
    y = jax.nn.silu(layernorm(y, ln_g, ln_b))
    return y @ w_b


def pool_mixer(zc, pool_w, pool_scale, w_c):
    B, S, _ = zc.shape
    ug = zc.reshape(B, S, POOL_GROUPS, POOL_GC)
    cs = jnp.cumsum(ug.astype(jnp.float32), axis=1)
    cs = jnp.pad(cs, ((0, 0), (1, 0), (0, 0), (0, 0)))
    t = jnp.arange(S)
    means = []
    for g, w in enumerate(POOL_WINDOWS):
        lo = w // 2
        hi = w - 1 - lo
        start = jnp.clip(t - lo, 0, S)
        end = jnp.clip(t + hi + 1, 0, S)
        win = cs[:, end, g] - cs[:, start, g]
        cnt = (end - start).astype(jnp.float32)[None, :, None]
        means.append(win / cnt)
    mean = jnp.stack(means, axis=2)
    d = (mean - ug.astype(jnp.float32)).astype(zc.dtype)
    y = jnp.einsum('bsgc,gcd->bsgd', d, pool_w).reshape(B, S, W_C) * pool_scale
    return y @ w_c


def fourier_mixer(zd, w_d):
    B, S, _ = zd.shape
    v = zd.astype(jnp.float32).reshape(B, S, FNET_GROUPS, FNET_GC)
    f = jnp.fft.fft2(v, axes=(1, 3), norm='ortho').real
    return f.astype(zd.dtype).reshape(B, S, W_D) @ w_d


def token_mixer(h, w_in, q_norm_g, w_uq, kv_norm_g, w_ukv, w_a, conv_w, conv_b,
                conv_ln_g, conv_ln_b, w_b, pool_w, pool_scale, w_c, w_d, w_out):
    B, S, D = h.shape
    z = h @ w_in
    za = z[..., :W_A]
    zb = z[..., W_A:W_A + W_B]
    zc = z[..., W_A + W_B:W_A + W_B + W_C]
    zd = z[..., W_A + W_B + W_C:W_MIX]
    gates = jax.nn.sigmoid(z[..., W_MIX:]).reshape(B, S, N_BRANCH, D)
    y_a = mla_mixer(za, q_norm_g, w_uq, kv_norm_g, w_ukv, w_a)
    y_b = conv_mixer(zb, conv_w, conv_b, conv_ln_g, conv_ln_b, w_b)
    y_c = pool_mixer(zc, pool_w, pool_scale, w_c)
    y_d = fourier_mixer(zd, w_d)
    merged = (gates[:, :, 0] * y_a + gates[:, :, 1] * y_b
              + gates[:, :, 2] * y_c + gates[:, :, 3] * y_d)
    return merged @ w_out


def setup_inputs(seed: int = 0) -> dict:
    key = jax.random.key(seed)
    ks = jax.random.split(key, 32)
    f32 = jnp.float32
    L, D = DEPTH, D_MODEL

    def nrm(k, shape, scale):
        return jax.random.normal(k, shape, f32) * scale

    return {
        "x": nrm(ks[0], (BATCH, SEQ, D), 1.0),
        "c": nrm(ks[1], (BATCH, D), 1.0),
        "ada_w": nrm(ks[2], (L, D, N_MOD * D), 0.5 * D ** -0.5),
        "ada_b": nrm(ks[3], (L, N_MOD * D), 0.02),
        "norm_g": 1.0 + nrm(ks[4], (L, 2 * N_SUB, D), 0.02),
        "ffn1_w_in": nrm(ks[5], (L, D, 2 * D_FF), D ** -0.5),
        "ffn1_w_out": nrm(ks[6], (L, D_FF, D), D_FF ** -0.5),
        "ffn2_w_in": nrm(ks[7], (L, D, 2 * D_FF), D ** -0.5),
        "ffn2_w_out": nrm(ks[8], (L, D_FF, D), D_FF ** -0.5),
        "w_in": nrm(ks[9], (L, D, W_IN), D ** -0.5),
        "q_norm_g": 1.0 + nrm(ks[10], (L, Q_LORA), 0.02),
        "w_uq": nrm(ks[11], (L, Q_LORA, N_HEADS * (QK_NOPE + QK_ROPE)), Q_LORA ** -0.5),
        "kv_norm_g": 1.0 + nrm(ks[12], (L, KV_LORA), 0.02),
        "w_ukv": nrm(ks[13], (L, KV_LORA, N_HEADS * (QK_NOPE + V_HEAD)), KV_LORA ** -0.5),
        "w_a": nrm(ks[14], (L, N_HEADS * V_HEAD, D), (N_HEADS * V_HEAD) ** -0.5),
        "conv_w": nrm(ks[15], (L, CONV_WIDTH, CONV_CH), CONV_WIDTH ** -0.5),
        "conv_b": nrm(ks[16], (L, CONV_CH), 0.02),
        "conv_ln_g": 1.0 + nrm(ks[17], (L, CONV_CH), 0.02),
        "conv_ln_b": nrm(ks[18], (L, CONV_CH), 0.02),
        "w_b": nrm(ks[19], (L, CONV_CH, D), CONV_CH ** -0.5),
        "pool_w": nrm(ks[20], (L, POOL_GROUPS, POOL_GC, POOL_GC), POOL_GC ** -0.5),
        "pool_scale": 1.0 + nrm(ks[21], (L, W_C), 0.1),
        "w_c": nrm(ks[22], (L, W_C, D), W_C ** -0.5),
        "w_d": nrm(ks[23], (L, W_D, D), W_D ** -0.5),
        "w_out": nrm(ks[24], (L, D, D), D ** -0.5),
    }


def reference(x, c, ada_w, ada_b, norm_g, ffn1_w_in, ffn1_w_out, ffn2_w_in, ffn2_w_out,
              w_in, q_norm_g, w_uq, kv_norm_g, w_ukv, w_a, conv_w, conv_b, conv_ln_g,
              conv_ln_b, w_b, pool_w, pool_scale, w_c, w_d, w_out):
    B = x.shape[0]
    c_act = jax.nn.silu(c)
    for l in range(DEPTH):
        mod = (c_act @ ada_w[l] + ada_b[l]).reshape(B, N_MOD, D_MODEL)
        h = modulate(rmsnorm(x, norm_g[l, 0]), mod[:, 0], mod[:, 1])
        y = rmsnorm(swiglu(h, ffn1_w_in[l], ffn1_w_out[l]), norm_g[l, 1])
        x = x + FFN_HALF * mod[:, 2][:, None, :] * y
        h = modulate(rmsnorm(x, norm_g[l, 2]), mod[:, 3], mod[:, 4])
        y = token_mixer(h, w_in[l], q_norm_g[l], w_uq[l], kv_norm_g[l], w_ukv[l], w_a[l],
                        conv_w[l], conv_b[l], conv_ln_g[l], conv_ln_b[l], w_b[l],
                        pool_w[l], pool_scale[l], w_c[l], w_d[l], w_out[l])
        x = x + mod[:, 5][:, None, :] * rmsnorm(y, norm_g[l, 3])
        h = modulate(rmsnorm(x, norm_g[l, 4]), mod[:, 6], mod[:, 7])
        y = rmsnorm(swiglu(h, ffn2_w_in[l], ffn2_w_out[l]), norm_g[l, 5])
        x = x + FFN_HALF * mod[:, 8][:, None, :] * y
    return x
```

```python
import functools
import math

import jax
import jax.numpy as jnp
import numpy as np
from jax import lax
from jax.experimental import pallas as pl
from jax.experimental.pallas import tpu as pltpu

F32 = jnp.float32
BF16 = jnp.bfloat16

D_MODEL = 1024
D_FF = 2816
N_HEADS = 8
Q_LORA = 352
KV_LORA = 128
QK_NOPE = 64
QK_ROPE = 32
V_HEAD = 64
ROPE_THETA = 10000.0
CONV_CH = 256
CONV_WIDTH = 31
POOL_WINDOWS = (2, 4, 8, 16)
GC = 128
N_GROUPS = 4
W_BR = 512
N_MOD = 9
EPS = 1e-6

LANES = 128
Q_PAD = 384
HALO = 16
FFT_N1 = 128
VMEM_LIMIT = 56 * 1024 * 1024

C_CKV = 0
C_CQ = C_CKV + KV_LORA
C_KRP = C_CQ + Q_PAD
C_KRR = C_KRP + LANES
C_ZB = C_KRR + LANES
C_ZC = C_ZB + W_BR
C_ZD = C_ZC + W_BR
W1_COLS = C_ZD + W_BR


def _params(*sem):
    return pltpu.CompilerParams(dimension_semantics=sem, vmem_limit_bytes=VMEM_LIMIT)


def _const_spec(shape):
    zeros = (0,) * len(shape)
    return pl.BlockSpec(shape, lambda *_: zeros, pipeline_mode=pl.Buffered(1))


def _dot(a, b):
    return jnp.dot(a, b, preferred_element_type=F32)


def _rms(x):
    return x * lax.rsqrt(jnp.mean(x * x, axis=-1, keepdims=True) + EPS)


def _sigmoid(x):
    return 1.0 / (1.0 + jnp.exp(-x))


def _mod_kernel(c_ref, w_ref, b_ref, o_ref):
    c = c_ref[...]
    ca = c * _sigmoid(c)
    o_ref[0] = jnp.dot(ca, w_ref[0], preferred_element_type=F32,
                       precision=lax.Precision.HIGHEST) + b_ref[0]


def _mod_call(c, ada_w, ada_b):
    L, D, N = ada_w.shape
    B = c.shape[0]
    tn = 1152
    return pl.pallas_call(
        _mod_kernel,
        grid=(L, N // tn),
        in_specs=[pl.BlockSpec((B, D), lambda l, j: (0, 0)),
                  pl.BlockSpec((1, D, tn), lambda l, j: (l, 0, j)),
                  pl.BlockSpec((1, 1, tn), lambda l, j: (l, 0, j))],
        out_specs=pl.BlockSpec((1, B, tn), lambda l, j: (l, 0, j)),
        out_shape=jax.ShapeDtypeStruct((L, B, N), F32),
        compiler_params=_params("parallel", "parallel"),
        name="adaln_mod",
    )(c, ada_w, ada_b.reshape(L, 1, N))


def _ffn_kernel(x_ref, mod_ref, g_ref, win_ref, wout_ref, o_ref, a_scr, *, sub, chunk):
    x = x_ref[0]
    shift = mod_ref[0, 3 * sub:3 * sub + 1, :]
    scale = mod_ref[0, 3 * sub + 1:3 * sub + 2, :]
    gate = mod_ref[0, 3 * sub + 2:3 * sub + 3, :]
    g_in = g_ref[2 * sub:2 * sub + 1, :]
    g_out = g_ref[2 * sub + 1:2 * sub + 2, :]
    h = (_rms(x) * g_in) * (1.0 + scale) + shift
    hb = h.astype(BF16)
    for j in range(D_FF // chunk):
        gj = _dot(hb, win_ref[:, j * chunk:(j + 1) * chunk])
        uj = _dot(hb, win_ref[:, D_FF + j * chunk:D_FF + (j + 1) * chunk])
        a_scr[:, j * chunk:(j + 1) * chunk] = (gj * _sigmoid(gj) * uj).astype(BF16)
    y = _dot(a_scr[...], wout_ref[...])
    o_ref[0] = x + (0.5 * gate) * (_rms(y) * g_out)


def _ffn_call(x, mod_l, g_l, w_in, w_out, sub, tm):
    B, S, D = x.shape
    kern = functools.partial(_ffn_kernel, sub=sub, chunk=256)
    return pl.pallas_call(
        kern,
        grid=(B, S // tm),
        in_specs=[pl.BlockSpec((1, tm, D), lambda b, i: (b, i, 0)),
                  pl.BlockSpec((1, N_MOD, D), lambda b, i: (b, 0, 0)),
                  _const_spec(g_l.shape),
                  _const_spec(w_in.shape),
                  _const_spec(w_out.shape)],
        out_specs=pl.BlockSpec((1, tm, D), lambda b, i: (b, i, 0)),
        out_shape=jax.ShapeDtypeStruct(x.shape, F32),
        scratch_shapes=[pltpu.VMEM((tm, D_FF), BF16)],
        compiler_params=_params("parallel", "parallel"),
        name=f"ffn{sub}",
    )(x, mod_l, g_l, w_in, w_out)


def _mixin_kernel(x_ref, mod_ref, g_ref, w1_ref, gq_ref, gkv_ref, wq_ref, wqr_ref, wkv_ref,
                  cq_ref, sq_ref, ck_ref, sk_ref, cs_ref,
                  q_ref, k_ref, v_ref, u_ref, zc_ref, p_ref, qq_ref):
    x = x_ref[0]
    shift = mod_ref[0, 3:4, :]
    scale = mod_ref[0, 4:5, :]
    h = (_rms(x) * g_ref[2:3, :]) * (1.0 + scale) + shift
    hb = h.astype(BF16)
    z = _dot(hb, w1_ref[...])

    cq = z[:, C_CQ:C_CQ + Q_PAD]
    cqn = cq * lax.rsqrt(jnp.sum(cq * cq, axis=-1, keepdims=True) * (1.0 / Q_LORA) + EPS)
    cqn = (cqn * gq_ref[...]).astype(BF16)
    qa = _dot(cqn, wq_ref[...])
    qb = _dot(cqn, wqr_ref[...])
    cosq = cq_ref[...]
    sinq = sq_ref[...]
    for hd in range(N_HEADS):
        sl = slice(hd * LANES, (hd + 1) * LANES)
        q_ref[0, hd] = (qa[:, sl] * cosq + qb[:, sl] * sinq).astype(BF16)

    ckv = z[:, C_CKV:C_CKV + KV_LORA]
    ckvn = (_rms(ckv) * gkv_ref[...]).astype(BF16)
    kv = _dot(ckvn, wkv_ref[...])
    k_rope = z[:, C_KRP:C_KRP + LANES] * ck_ref[...] + z[:, C_KRR:C_KRR + LANES] * sk_ref[...]
    lane = lax.broadcasted_iota(jnp.int32, (1, LANES), 1)
    ones_col = (lane == V_HEAD).astype(F32)
    for hd in range(N_HEADS):
        k_ref[0, hd] = (kv[:, hd * LANES:(hd + 1) * LANES] + k_rope).astype(BF16)
        v_ref[0, hd] = (kv[:, (N_HEADS + hd) * LANES:(N_HEADS + hd + 1) * LANES] + ones_col).astype(BF16)

    a = z[:, C_ZB:C_ZB + CONV_CH]
    gt = z[:, C_ZB + CONV_CH:C_ZB + 2 * CONV_CH]
    u_ref[0] = a * _sigmoid(gt)

    zc_ref[0] = z[:, C_ZC:C_ZC + W_BR]

    for g in range(N_GROUPS):
        zd = z[:, C_ZD + g * GC:C_ZD + (g + 1) * GC].astype(BF16)
        pq = _dot(zd, cs_ref[...])
        p_ref[0, :, g * GC:(g + 1) * GC] = pq[:, :GC].astype(BF16)
        qq_ref[0, :, g * GC:(g + 1) * GC] = pq[:, GC:].astype(BF16)


def _mixin_call(x, mod_l, g_l, lw, tabs, tm):
    B, S, D = x.shape
    tok = lambda w: pl.BlockSpec((1, tm, w), lambda b, i: (b, i, 0))
    head = pl.BlockSpec((1, N_HEADS, tm, LANES), lambda b, i: (b, 0, i, 0))
    tab = pl.BlockSpec((tm, LANES), lambda b, i: (i, 0))
    hshape = jax.ShapeDtypeStruct((B, N_HEADS, S, LANES), BF16)
    return pl.pallas_call(
        _mixin_kernel,
        grid=(B, S // tm),
        in_specs=[tok(D),
                  pl.BlockSpec((1, N_MOD, D), lambda b, i: (b, 0, 0)),
                  _const_spec(g_l.shape),
                  _const_spec(lw["w1"].shape),
                  _const_spec(lw["gq"].shape),
                  _const_spec(lw["gkv"].shape),
                  _const_spec(lw["wq"].shape),
                  _const_spec(lw["wqr"].shape),
                  _const_spec(lw["wkv"].shape),
                  tab, tab, tab, tab,
                  _const_spec(tabs["cs"].shape)],
        out_specs=[head, head, head, tok(CONV_CH), tok(W_BR), tok(W_BR), tok(W_BR)],
        out_shape=[hshape, hshape, hshape,
                   jax.ShapeDtypeStruct((B, S, CONV_CH), F32),
                   jax.ShapeDtypeStruct((B, S, W_BR), F32),
                   jax.ShapeDtypeStruct((B, S, W_BR), BF16),
                   jax.ShapeDtypeStruct((B, S, W_BR), BF16)],
        compiler_params=_params("parallel", "parallel"),
        name="mixer_in",
    )(x, mod_l, g_l, lw["w1"], lw["gq"], lw["gkv"], lw["wq"], lw["wqr"], lw["wkv"],
      tabs["cosq"], tabs["sinq"], tabs["cosk"], tabs["sink"], tabs["cs"])


def _attn_kernel(q_ref, k_ref, v_ref, o_ref, m_scr, acc_scr, *, tk):
    q = q_ref[0, 0]
    m_scr[...] = jnp.full(m_scr.shape, -jnp.inf, F32)
    acc_scr[...] = jnp.zeros(acc_scr.shape, F32)

    def body(i, carry):
        off = pl.multiple_of(i * tk, tk)
        kb = k_ref[0, 0, pl.ds(off, tk), :]
        vb = v_ref[0, 0, pl.ds(off, tk), :]
        s = lax.dot_general(q, kb, (((1,), (1,)), ((), ())), preferred_element_type=F32)
        m_prev = m_scr[...]
        m_new = jnp.maximum(m_prev, jnp.max(s, axis=-1, keepdims=True))
        alpha = jnp.exp(m_prev - m_new)
        p = jnp.exp(s - m_new)
        acc_scr[...] = alpha * acc_scr[...] + _dot(p.astype(BF16), vb)
        m_scr[...] = m_new
        return carry

    lax.fori_loop(0, k_ref.shape[2] // tk, body, 0)
    acc = acc_scr[...]
    denom = acc[:, V_HEAD:V_HEAD + 1]
    o_ref[0] = (acc / denom).astype(o_ref.dtype)


def _attn_call(q, k, v, tq, tk):
    B, H, S, _ = q.shape
    kern = functools.partial(_attn_kernel, tk=tk)
    return pl.pallas_call(
        kern,
        grid=(B, H, S // tq),
        in_specs=[pl.BlockSpec((1, 1, tq, LANES), lambda b, h, i: (b, h, i, 0)),
                  pl.BlockSpec((1, 1, S, LANES), lambda b, h, i: (b, h, 0, 0)),
                  pl.BlockSpec((1, 1, S, LANES), lambda b, h, i: (b, h, 0, 0))],
        out_specs=pl.BlockSpec((1, tq, LANES), lambda b, h, i: (b, i, h)),
        out_shape=jax.ShapeDtypeStruct((B, S, H * LANES), BF16),
        scratch_shapes=[pltpu.VMEM((tq, 1), F32), pltpu.VMEM((tq, LANES), F32)],
        compiler_params=_params("parallel", "parallel", "arbitrary"),
        name="attention",
    )(q, k, v)


def _local_kernel(u_ref, up_ref, un_ref, z_ref, zp_ref, zn_ref, cw_ref, cb_ref, lg_ref, lb_ref,
                  pw_ref, ps_ref, oc_ref, op_ref, ubuf, zbuf, *, ts, seq, rc, rp):
    i = pl.program_id(1)
    has_prev = i > 0
    has_next = i < pl.num_programs(1) - 1
    ubuf[0:HALO, :] = jnp.where(has_prev, up_ref[0], 0.0)
    ubuf[HALO:HALO + ts, :] = u_ref[0]
    ubuf[HALO + ts:, :] = jnp.where(has_next, un_ref[0], 0.0)
    zbuf[0:HALO, :] = jnp.where(has_prev, zp_ref[0], 0.0)
    zbuf[HALO:HALO + ts, :] = z_ref[0]
    zbuf[HALO + ts:, :] = jnp.where(has_next, zn_ref[0], 0.0)

    pad = CONV_WIDTH // 2
    for r in range(0, ts, rc):
        acc = jnp.zeros((rc, CONV_CH), F32)
        for kk in range(CONV_WIDTH):
            st = HALO + r + kk - pad
            acc = acc + ubuf[st:st + rc, :] * cw_ref[kk:kk + 1, :]
        y = acc + cb_ref[...]
        mu = jnp.mean(y, axis=-1, keepdims=True)
        yc = y - mu
        yn = yc * lax.rsqrt(jnp.mean(yc * yc, axis=-1, keepdims=True) + EPS)
        yn = yn * lg_ref[...] + lb_ref[...]
        oc_ref[0, r:r + rc, :] = (yn * _sigmoid(yn)).astype(oc_ref.dtype)

    for r in range(0, ts, rp):
        t = i * ts + r + lax.broadcasted_iota(jnp.int32, (rp, 1), 0)
        for g, w in enumerate(POOL_WINDOWS):
            lo = w // 2
            hi = w - 1 - lo
            cols = slice(g * GC, (g + 1) * GC)
            win = jnp.zeros((rp, GC), F32)
            for j in range(-lo, hi + 1):
                st = HALO + r + j
                win = win + zbuf[st:st + rp, cols]
            cnt = (jnp.minimum(t + hi + 1, seq) - jnp.maximum(t - lo, 0)).astype(F32)
            d = win / cnt - zbuf[HALO + r:HALO + r + rp, cols]
            yg = _dot(d.astype(BF16), pw_ref[g]) * ps_ref[:, cols]
            op_ref[0, r:r + rp, cols] = yg.astype(op_ref.dtype)


def _local_call(u, zc, lw, ts):
    B, S, _ = u.shape
    nh = ts // HALO
    last = S // HALO - 1
    cur = lambda w: pl.BlockSpec((1, ts, w), lambda b, i: (b, i, 0))
    prev = lambda w: pl.BlockSpec((1, HALO, w), lambda b, i: (b, jnp.maximum(i * nh - 1, 0), 0))
    nxt = lambda w: pl.BlockSpec((1, HALO, w), lambda b, i: (b, jnp.minimum((i + 1) * nh, last), 0))
    kern = functools.partial(_local_kernel, ts=ts, seq=S, rc=64, rp=min(ts, 256))
    return pl.pallas_call(
        kern,
        grid=(B, S // ts),
        in_specs=[cur(CONV_CH), prev(CONV_CH), nxt(CONV_CH), cur(W_BR), prev(W_BR), nxt(W_BR),
                  _const_spec(lw["conv_w"].shape), _const_spec(lw["conv_b"].shape),
                  _const_spec(lw["ln_g"].shape), _const_spec(lw["ln_b"].shape),
                  _const_spec(lw["pool_w"].shape), _const_spec(lw["pool_scale"].shape)],
        out_specs=[cur(CONV_CH), cur(W_BR)],
        out_shape=[jax.ShapeDtypeStruct((B, S, CONV_CH), BF16),
                   jax.ShapeDtypeStruct((B, S, W_BR), BF16)],
        scratch_shapes=[pltpu.VMEM((ts + 2 * HALO, CONV_CH), F32),
                        pltpu.VMEM((ts + 2 * HALO, W_BR), F32)],
        compiler_params=_params("parallel", "parallel"),
        name="conv_pool",
    )(u, u, u, zc, zc, zc, lw["conv_w"], lw["conv_b"], lw["ln_g"], lw["ln_b"],
      lw["pool_w"], lw["pool_scale"])


def _fft1_kernel(p_ref, q_ref, fa_ref, fb_ref, o_ref):
    a = _dot(fa_ref[...], p_ref[0]) + _dot(fb_ref[...], q_ref[0])
    o_ref[0] = a.astype(o_ref.dtype)


def _fft1_call(p, q, tabs, tn):
    B, S, W = p.shape
    n2w = (S // FFT_N1) * W
    pv = p.reshape(B, FFT_N1, n2w)
    qv = q.reshape(B, FFT_N1, n2w)
    blk = pl.BlockSpec((1, FFT_N1, tn), lambda b, j: (b, 0, j))
    return pl.pallas_call(
        _fft1_kernel,
        grid=(B, n2w // tn),
        in_specs=[blk, blk, _const_spec(tabs["f1a"].shape), _const_spec(tabs["f1b"].shape)],
        out_specs=pl.BlockSpec((1, 2 * FFT_N1, tn), lambda b, j: (b, 0, j)),
        out_shape=jax.ShapeDtypeStruct((B, 2 * FFT_N1, n2w), BF16),
        compiler_params=_params("parallel", "parallel"),
        name="fft_stage1",
    )(pv, qv, tabs["f1a"], tabs["f1b"])


def _fft2_kernel(a_ref, gc_ref, gs_ref, o_ref, *, kb, w):
    for j in range(kb):
        f = _dot(gc_ref[j], a_ref[0, 0, j]) + _dot(gs_ref[j], a_ref[0, 1, j])
        o_ref[0, :, j * w:(j + 1) * w] = f.astype(o_ref.dtype)


def _fft2_call(a, tabs, seq, kb):
    B = a.shape[0]
    n2 = seq // FFT_N1
    av = a.reshape(B, 2, FFT_N1, n2, W_BR)
    kern = functools.partial(_fft2_kernel, kb=kb, w=W_BR)
    out = pl.pallas_call(
        kern,
        grid=(B, FFT_N1 // kb),
        in_specs=[pl.BlockSpec((1, 2, kb, n2, W_BR), lambda b, j: (b, 0, j, 0, 0)),
                  pl.BlockSpec((kb, n2, n2), lambda b, j: (j, 0, 0)),
                  pl.BlockSpec((kb, n2, n2), lambda b, j: (j, 0, 0))],
        out_specs=pl.BlockSpec((1, n2, kb * W_BR), lambda b, j: (b, 0, j)),
        out_shape=jax.ShapeDtypeStruct((B, n2, FFT_N1 * W_BR), BF16),
        compiler_params=_params("parallel", "parallel"),
        name="fft_stage2",
    )(av, tabs["g2c"], tabs["g2s"])
    return out.reshape(B, seq, W_BR)


def _mixout_kernel(x_ref, mod_ref, g_ref, wg_ref, oa_ref, wa_ref, cb_ref, wb_ref, pc_ref, wc_ref,
                   f_ref, wd_ref, wo_ref, o_ref):
    x = x_ref[0]
    shift = mod_ref[0, 3:4, :]
    scale = mod_ref[0, 4:5, :]
    gate = mod_ref[0, 5:6, :]
    h = (_rms(x) * g_ref[2:3, :]) * (1.0 + scale) + shift
    hb = h.astype(BF16)
    d = x.shape[-1]
    merged = None
    for br, (b_ref, w_ref) in enumerate(((oa_ref, wa_ref), (cb_ref, wb_ref), (pc_ref, wc_ref), (f_ref, wd_ref))):
        gl = _sigmoid(_dot(hb, wg_ref[:, br * d:(br + 1) * d]))
        term = gl * _dot(b_ref[0], w_ref[...])
        merged = term if merged is None else merged + term
    y = _dot(merged.astype(BF16), wo_ref[...])
    o_ref[0] = x + gate * (_rms(y) * g_ref[3:4, :])


def _mixout_call(x, mod_l, g_l, lw, oa, cb, pc, f, tm):
    B, S, D = x.shape
    tok = lambda w: pl.BlockSpec((1, tm, w), lambda b, i: (b, i, 0))
    return pl.pallas_call(
        _mixout_kernel,
        grid=(B, S // tm),
        in_specs=[tok(D),
                  pl.BlockSpec((1, N_MOD, D), lambda b, i: (b, 0, 0)),
                  _const_spec(g_l.shape),
                  _const_spec(lw["wg"].shape),
                  tok(N_HEADS * LANES), _const_spec(lw["wa"].shape),
                  tok(CONV_CH), _const_spec(lw["wb"].shape),
                  tok(W_BR), _const_spec(lw["wc"].shape),
                  tok(W_BR), _const_spec(lw["wd"].shape),
                  _const_spec(lw["wo"].shape)],
        out_specs=tok(D),
        out_shape=jax.ShapeDtypeStruct(x.shape, F32),
        compiler_params=_params("parallel", "parallel"),
        name="mixer_out",
    )(x, mod_l, g_l, lw["wg"], oa, lw["wa"], cb, lw["wb"], pc, lw["wc"], f, lw["wd"], lw["wo"])


def _rot_half_cols(w):
    half = w.shape[-1] // 2
    return jnp.concatenate([-w[..., half:], w[..., :half]], axis=-1)


def _tables(seq):
    pos = jnp.arange(seq, dtype=F32)
    inv = ROPE_THETA ** (-jnp.arange(0, QK_ROPE, 2, dtype=F32) / QK_ROPE)
    ang = pos[:, None] * inv[None, :]
    cos, sin = jnp.cos(ang), jnp.sin(ang)
    cc = jnp.concatenate([cos, cos], axis=-1)
    ss = jnp.concatenate([sin, sin], axis=-1)
    z64 = jnp.zeros((seq, QK_NOPE), F32)
    z32 = jnp.zeros((seq, LANES - QK_NOPE - QK_ROPE), F32)
    sm_scale = (QK_NOPE + QK_ROPE) ** -0.5
    cosk = jnp.concatenate([z64, cc, z32], axis=-1)
    sink = jnp.concatenate([z64, ss, z32], axis=-1)
    cosq = jnp.concatenate([jnp.ones_like(z64), cc, z32], axis=-1) * sm_scale
    sinq = sink * sm_scale

    def cos_sin(m, period):
        th = m.astype(F32) * (2.0 * math.pi / period)
        return jnp.cos(th), jnp.sin(th)

    def dft(n):
        a = jnp.arange(n, dtype=jnp.int32)
        return cos_sin((a[:, None] * a[None, :]) % n, n)

    c_ch, s_ch = dft(GC)
    cs = (jnp.concatenate([c_ch, s_ch], axis=-1) * GC ** -0.5).astype(BF16)

    n1 = FFT_N1
    n2 = seq // n1
    c1, s1 = dft(n1)
    f1a = (jnp.concatenate([c1, -s1], axis=0) * n1 ** -0.5).astype(BF16)
    f1b = (jnp.concatenate([-s1, -c1], axis=0) * n1 ** -0.5).astype(BF16)
    k1 = jnp.arange(n1, dtype=jnp.int32)[:, None, None]
    k2 = jnp.arange(n2, dtype=jnp.int32)[None, :, None]
    t2 = jnp.arange(n2, dtype=jnp.int32)[None, None, :]
    m = ((k1 + n1 * k2) * t2) % seq
    c2, s2 = cos_sin(m, seq)
    g2c = (c2 * n2 ** -0.5).astype(BF16)
    g2s = (s2 * n2 ** -0.5).astype(BF16)
    return dict(cosq=cosq, sinq=sinq, cosk=cosk, sink=sink, cs=cs, f1a=f1a, f1b=f1b, g2c=g2c, g2s=g2s)


def _layer_weights(l, w_in, q_norm_g, w_uq, kv_norm_g, w_ukv, w_a, conv_w, conv_b, conv_ln_g,
                   conv_ln_b, w_b, pool_w, pool_scale, w_c, w_d, w_out):
    D = w_in.shape[1]
    wi = w_in[l]
    zeros = lambda n: jnp.zeros((D, n), F32)
    kr = wi[:, Q_LORA + KV_LORA:W_BR]
    place = lambda w: jnp.concatenate([zeros(QK_NOPE), w, zeros(LANES - QK_NOPE - QK_ROPE)], axis=-1)
    w1 = jnp.concatenate([
        wi[:, Q_LORA:Q_LORA + KV_LORA],
        wi[:, :Q_LORA], zeros(Q_PAD - Q_LORA),
        place(kr), place(_rot_half_cols(kr)),
        wi[:, W_BR:4 * W_BR]], axis=-1).astype(BF16)

    uq = w_uq[l].reshape(Q_LORA, N_HEADS, QK_NOPE + QK_ROPE)
    nope, rope = uq[..., :QK_NOPE], uq[..., QK_NOPE:]
    zq = lambda n: jnp.zeros((Q_LORA, N_HEADS, n), F32)
    rows = ((0, Q_PAD - Q_LORA), (0, 0))
    wq = jnp.pad(jnp.concatenate([nope, rope, zq(LANES - QK_NOPE - QK_ROPE)], axis=-1)
                 .reshape(Q_LORA, N_HEADS * LANES), rows).astype(BF16)
    wqr = jnp.pad(jnp.concatenate([zq(QK_NOPE), _rot_half_cols(rope), zq(LANES - QK_NOPE - QK_ROPE)], axis=-1)
                  .reshape(Q_LORA, N_HEADS * LANES), rows).astype(BF16)

    ukv = w_ukv[l].reshape(KV_LORA, N_HEADS, QK_NOPE + V_HEAD)
    zk = jnp.zeros((KV_LORA, N_HEADS, LANES - QK_NOPE), F32)
    zv = jnp.zeros((KV_LORA, N_HEADS, LANES - V_HEAD), F32)
    wkv = jnp.concatenate([
        jnp.concatenate([ukv[..., :QK_NOPE], zk], axis=-1).reshape(KV_LORA, N_HEADS * LANES),
        jnp.concatenate([ukv[..., QK_NOPE:], zv], axis=-1).reshape(KV_LORA, N_HEADS * LANES)],
        axis=-1).astype(BF16)

    wa = jnp.pad(w_a[l].reshape(N_HEADS, V_HEAD, D), ((0, 0), (0, LANES - V_HEAD), (0, 0)))
    wa = wa.reshape(N_HEADS * LANES, D).astype(BF16)
    gq = jnp.pad(q_norm_g[l], (0, Q_PAD - Q_LORA)).reshape(1, Q_PAD)
    return dict(
        w1=w1, wq=wq, wqr=wqr, wkv=wkv, gq=gq, gkv=kv_norm_g[l].reshape(1, KV_LORA),
        wg=wi[:, 4 * W_BR:].astype(BF16), wa=wa,
        wb=w_b[l].astype(BF16), wc=w_c[l].astype(BF16), wd=w_d[l].astype(BF16), wo=w_out[l].astype(BF16),
        conv_w=conv_w[l], conv_b=conv_b[l].reshape(1, CONV_CH),
        ln_g=conv_ln_g[l].reshape(1, CONV_CH), ln_b=conv_ln_b[l].reshape(1, CONV_CH),
        pool_w=pool_w[l].astype(BF16), pool_scale=pool_scale[l].reshape(1, W_BR))


def _tiles(seq):
    t = lambda n: min(n, seq)
    return dict(ffn=t(512), mix=t(512), tq=t(512), tk=t(512), local=t(512))


def kernel(x, c, ada_w, ada_b, norm_g, ffn1_w_in, ffn1_w_out, ffn2_w_in, ffn2_w_out, w_in, q_norm_g,
           w_uq, kv_norm_g, w_ukv, w_a, conv_w, conv_b, conv_ln_g, conv_ln_b, w_b, pool_w, pool_scale,
           w_c, w_d, w_out):
    B, S, D = x.shape
    L = ada_w.shape[0]
    assert D == D_MODEL and S % (FFT_N1 * 16) == 0
    ts = _tiles(S)
    tabs = _tables(S)
    mod = _mod_call(c, ada_w, ada_b).reshape(L, B, N_MOD, D)
    n2w = (S // FFT_N1) * W_BR
    for l in range(L):
        lw = _layer_weights(l, w_in, q_norm_g, w_uq, kv_norm_g, w_ukv, w_a, conv_w, conv_b, conv_ln_g,
                            conv_ln_b, w_b, pool_w, pool_scale, w_c, w_d, w_out)
        mod_l, g_l = mod[l], norm_g[l]
        x = _ffn_call(x, mod_l, g_l, ffn1_w_in[l].astype(BF16), ffn1_w_out[l].astype(BF16), 0, ts["ffn"])
        q, k, v, u, zc, p, qq = _mixin_call(x, mod_l, g_l, lw, tabs, ts["mix"])
        oa = _attn_call(q, k, v, ts["tq"], ts["tk"])
        cb, pc = _local_call(u, zc, lw, ts["local"])
        a = _fft1_call(p, qq, tabs, min(4096, n2w))
        f = _fft2_call(a, tabs, S, 8)
        x = _mixout_call(x, mod_l, g_l, lw, oa, cb, pc, f, ts["mix"])
        x = _ffn_call(x, mod_l, g_l, ffn2_w_in[l].astype(BF16), ffn2_w_out[l].astype(BF16), 2, ts["ffn"])
    return x
```

```python
import functools
import math

import jax
import jax.numpy as jnp
import numpy as np
from jax import lax
from jax.experimental import pallas as pl
from jax.experimental.pallas import tpu as pltpu

F32 = jnp.float32
BF16 = jnp.bfloat16

D_MODEL = 1024
D_FF = 2816
N_HEADS = 8
Q_LORA = 352
KV_LORA = 128
QK_NOPE = 64
QK_ROPE = 32
V_HEAD = 64
ROPE_THETA = 10000.0
CONV_CH = 256
CONV_WIDTH = 31
POOL_WINDOWS = (2, 4, 8, 16)
GC = 128
N_GROUPS = 4
W_BR = 512
N_MOD = 9
EPS = 1e-6

LANES = 128
Q_PAD = 384
HALO = 16
FFT_N1 = 128
VMEM_LIMIT = 56 * 1024 * 1024

C_CKV = 0
C_CQ = C_CKV + KV_LORA
C_KRP = C_CQ + Q_PAD
C_KRR = C_KRP + LANES
C_ZB = C_KRR + LANES
C_ZC = C_ZB + W_BR
C_ZD = C_ZC + W_BR
W1_COLS = C_ZD + W_BR


def _params(*sem):
    return pltpu.CompilerParams(dimension_semantics=sem, vmem_limit_bytes=VMEM_LIMIT)


def _const_spec(shape):
    zeros = (0,) * len(shape)
    return pl.BlockSpec(shape, lambda *_: zeros, pipeline_mode=pl.Buffered(1))


def _dot(a, b):
    return jnp.dot(a, b, preferred_element_type=F32)


def _rms(x):
    return x * lax.rsqrt(jnp.mean(x * x, axis=-1, keepdims=True) + EPS)


def _sigmoid(x):
    return 1.0 / (1.0 + jnp.exp(-x))


def _mod_kernel(c_ref, w_ref, b_ref, o_ref):
    c = c_ref[...]
    ca = c * _sigmoid(c)
    o_ref[0] = jnp.dot(ca, w_ref[0], preferred_element_type=F32,
                       precision=lax.Precision.HIGHEST) + b_ref[0]


def _mod_call(c, ada_w, ada_b):
    L, D, N = ada_w.shape
    B = c.shape[0]
    tn = 1152
    return pl.pallas_call(
        _mod_kernel,
        grid=(L, N // tn),
        in_specs=[pl.BlockSpec((B, D), lambda l, j: (0, 0)),
                  pl.BlockSpec((1, D, tn), lambda l, j: (l, 0, j)),
                  pl.BlockSpec((1, 1, tn), lambda l, j: (l, 0, j))],
        out_specs=pl.BlockSpec((1, B, tn), lambda l, j: (l, 0, j)),
        out_shape=jax.ShapeDtypeStruct((L, B, N), F32),
        compiler_params=_params("parallel", "parallel"),
        name="adaln_mod",
    )(c, ada_w, ada_b.reshape(L, 1, N))


def _ffn_kernel(x_ref, mod_ref, g_ref, win_ref, wout_ref, o_ref, a_scr, *, sub, chunk):
    x = x_ref[0]
    shift = mod_ref[0, 3 * sub:3 * sub + 1, :]
    scale = mod_ref[0, 3 * sub + 1:3 * sub + 2, :]
    gate = mod_ref[0, 3 * sub + 2:3 * sub + 3, :]
    g_in = g_ref[2 * sub:2 * sub + 1, :]
    g_out = g_ref[2 * sub + 1:2 * sub + 2, :]
    h = (_rms(x) * g_in) * (1.0 + scale) + shift
    hb = h.astype(BF16)
    for j in range(D_FF // chunk):
        gj = _dot(hb, win_ref[:, j * chunk:(j + 1) * chunk])
        uj = _dot(hb, win_ref[:, D_FF + j * chunk:D_FF + (j + 1) * chunk])
        a_scr[:, j * chunk:(j + 1) * chunk] = (gj * _sigmoid(gj) * uj).astype(BF16)
    y = _dot(a_scr[...], wout_ref[...])
    o_ref[0] = x + (0.5 * gate) * (_rms(y) * g_out)


def _ffn_call(x, mod_l, g_l, w_in, w_out, sub, tm):
    B, S, D = x.shape
    kern = functools.partial(_ffn_kernel, sub=sub, chunk=256)
    return pl.pallas_call(
        kern,
        grid=(B, S // tm),
        in_specs=[pl.BlockSpec((1, tm, D), lambda b, i: (b, i, 0)),
                  pl.BlockSpec((1, N_MOD, D), lambda b, i: (b, 0, 0)),
                  _const_spec(g_l.shape),
                  _const_spec(w_in.shape),
                  _const_spec(w_out.shape)],
        out_specs=pl.BlockSpec((1, tm, D), lambda b, i: (b, i, 0)),
        out_shape=jax.ShapeDtypeStruct(x.shape, F32),
        scratch_shapes=[pltpu.VMEM((tm, D_FF), BF16)],
        compiler_params=_params("parallel", "parallel"),
        name=f"ffn{sub}",
    )(x, mod_l, g_l, w_in, w_out)


def _mixin_kernel(x_ref, mod_ref, g_ref, w1_ref, gq_ref, gkv_ref, wq_ref, wqr_ref, wkv_ref,
                  cq_ref, sq_ref, ck_ref, sk_ref, cs_ref,
                  q_ref, k_ref, v_ref, u_ref, zc_ref, p_ref, qq_ref):
    x = x_ref[0]
    shift = mod_ref[0, 3:4, :]
    scale = mod_ref[0, 4:5, :]
    h = (_rms(x) * g_ref[2:3, :]) * (1.0 + scale) + shift
    hb = h.astype(BF16)
    z = _dot(hb, w1_ref[...])

    cq = z[:, C_CQ:C_CQ + Q_PAD]
    cqn = cq * lax.rsqrt(jnp.sum(cq * cq, axis=-1, keepdims=True) * (1.0 / Q_LORA) + EPS)
    cqn = (cqn * gq_ref[...]).astype(BF16)
    qa = _dot(cqn, wq_ref[...])
    qb = _dot(cqn, wqr_ref[...])
    cosq = cq_ref[...]
    sinq = sq_ref[...]
    for hd in range(N_HEADS):
        sl = slice(hd * LANES, (hd + 1) * LANES)
        q_ref[0, hd] = (qa[:, sl] * cosq + qb[:, sl] * sinq).T.astype(BF16)

    ckv = z[:, C_CKV:C_CKV + KV_LORA]
    ckvn = (_rms(ckv) * gkv_ref[...]).astype(BF16)
    kv = _dot(ckvn, wkv_ref[...])
    k_rope = z[:, C_KRP:C_KRP + LANES] * ck_ref[...] + z[:, C_KRR:C_KRR + LANES] * sk_ref[...]
    lane = lax.broadcasted_iota(jnp.int32, (1, LANES), 1)
    ones_col = (lane == V_HEAD).astype(F32)
    for hd in range(N_HEADS):
        k_ref[0, hd] = (kv[:, hd * LANES:(hd + 1) * LANES] + k_rope).astype(BF16)
        v_ref[0, hd] = (kv[:, (N_HEADS + hd) * LANES:(N_HEADS + hd + 1) * LANES] + ones_col).T.astype(BF16)

    a = z[:, C_ZB:C_ZB + CONV_CH]
    gt = z[:, C_ZB + CONV_CH:C_ZB + 2 * CONV_CH]
    u_ref[0] = a * _sigmoid(gt)

    zc_ref[0] = z[:, C_ZC:C_ZC + W_BR]

    for g in range(N_GROUPS):
        zd = z[:, C_ZD + g * GC:C_ZD + (g + 1) * GC].astype(BF16)
        pq = _dot(zd, cs_ref[...])
        p_ref[0, :, g * GC:(g + 1) * GC] = pq[:, :GC].astype(BF16)
        qq_ref[0, :, g * GC:(g + 1) * GC] = pq[:, GC:].astype(BF16)


def _mixin_call(x, mod_l, g_l, lw, tabs, tm):
    B, S, D = x.shape
    tok = lambda w: pl.BlockSpec((1, tm, w), lambda b, i: (b, i, 0))
    head = pl.BlockSpec((1, N_HEADS, tm, LANES), lambda b, i: (b, 0, i, 0))
    head_t = pl.BlockSpec((1, N_HEADS, LANES, tm), lambda b, i: (b, 0, 0, i))
    tab = pl.BlockSpec((tm, LANES), lambda b, i: (i, 0))
    hshape = jax.ShapeDtypeStruct((B, N_HEADS, S, LANES), BF16)
    hshape_t = jax.ShapeDtypeStruct((B, N_HEADS, LANES, S), BF16)
    return pl.pallas_call(
        _mixin_kernel,
        grid=(B, S // tm),
        in_specs=[tok(D),
                  pl.BlockSpec((1, N_MOD, D), lambda b, i: (b, 0, 0)),
                  _const_spec(g_l.shape),
                  _const_spec(lw["w1"].shape),
                  _const_spec(lw["gq"].shape),
                  _const_spec(lw["gkv"].shape),
                  _const_spec(lw["wq"].shape),
                  _const_spec(lw["wqr"].shape),
                  _const_spec(lw["wkv"].shape),
                  tab, tab, tab, tab,
                  _const_spec(tabs["cs"].shape)],
        out_specs=[head_t, head, head_t, tok(CONV_CH), tok(W_BR), tok(W_BR), tok(W_BR)],
        out_shape=[hshape_t, hshape, hshape_t,
                   jax.ShapeDtypeStruct((B, S, CONV_CH), F32),
                   jax.ShapeDtypeStruct((B, S, W_BR), F32),
                   jax.ShapeDtypeStruct((B, S, W_BR), BF16),
                   jax.ShapeDtypeStruct((B, S, W_BR), BF16)],
        compiler_params=_params("parallel", "parallel"),
        name="mixer_in",
    )(x, mod_l, g_l, lw["w1"], lw["gq"], lw["gkv"], lw["wq"], lw["wqr"], lw["wkv"],
      tabs["cosq"], tabs["sinq"], tabs["cosk"], tabs["sink"], tabs["cs"])


def _attn_kernel(qt_ref, k_ref, vt_ref, o_ref, m_scr, acc_scr, st_scr, *, tk):
    nk = k_ref.shape[2] // tk
    m_scr[...] = jnp.full(m_scr.shape, -jnp.inf, F32)
    acc_scr[...] = jnp.zeros(acc_scr.shape, F32)

    def scores(j, slot):
        off = pl.multiple_of(j * tk, tk)
        st_scr[slot] = _dot(k_ref[0, 0, pl.ds(off, tk), :], qt_ref[0, 0])

    def step(j, slot, prefetch):
        if prefetch:
            scores(j + 1, 1 - slot)
        off = pl.multiple_of(j * tk, tk)
        st = st_scr[slot]
        m_prev = m_scr[...]
        m_new = jnp.maximum(m_prev, jnp.max(st, axis=0, keepdims=True))
        alpha = jnp.exp2(m_prev - m_new)
        pt = jnp.exp2(st - m_new).astype(BF16)
        acc_scr[...] = alpha * acc_scr[...] + _dot(vt_ref[0, 0, :, pl.ds(off, tk)], pt)
        m_scr[...] = m_new

    scores(0, 0)

    def pair(jj, carry):
        step(2 * jj, 0, True)
        step(2 * jj + 1, 1, True)
        return carry

    lax.fori_loop(0, nk // 2 - 1, pair, 0)
    step(nk - 2, 0, True)
    step(nk - 1, 1, False)
    acc = acc_scr[...]
    denom = acc[V_HEAD:V_HEAD + 1, :]
    o_ref[0] = (acc / denom).T.astype(o_ref.dtype)


def _attn_call(qt, k, vt, tq, tk):
    B, H, S, _ = k.shape
    kern = functools.partial(_attn_kernel, tk=tk)
    return pl.pallas_call(
        kern,
        grid=(B, H, S // tq),
        in_specs=[pl.BlockSpec((1, 1, LANES, tq), lambda b, h, i: (b, h, 0, i)),
                  pl.BlockSpec((1, 1, S, LANES), lambda b, h, i: (b, h, 0, 0)),
                  pl.BlockSpec((1, 1, LANES, S), lambda b, h, i: (b, h, 0, 0))],
        out_specs=pl.BlockSpec((1, tq, LANES), lambda b, h, i: (b, i, h)),
        out_shape=jax.ShapeDtypeStruct((B, S, H * LANES), BF16),
        scratch_shapes=[pltpu.VMEM((1, tq), F32), pltpu.VMEM((LANES, tq), F32),
                        pltpu.VMEM((2, tk, tq), F32)],
        compiler_params=_params("parallel", "parallel", "arbitrary"),
        name="attention",
    )(qt, k, vt)


def _local_kernel(u_ref, up_ref, un_ref, z_ref, zp_ref, zn_ref, cw_ref, cb_ref, lg_ref, lb_ref,
                  pw_ref, ps_ref, oc_ref, op_ref, ubuf, zbuf, *, ts, seq, rc, rp):
    i = pl.program_id(1)
    has_prev = i > 0
    has_next = i < pl.num_programs(1) - 1
    ubuf[0:HALO, :] = jnp.where(has_prev, up_ref[0], 0.0)
    ubuf[HALO:HALO + ts, :] = u_ref[0]
    ubuf[HALO + ts:, :] = jnp.where(has_next, un_ref[0], 0.0)
    zbuf[0:HALO, :] = jnp.where(has_prev, zp_ref[0], 0.0)
    zbuf[HALO:HALO + ts, :] = z_ref[0]
    zbuf[HALO + ts:, :] = jnp.where(has_next, zn_ref[0], 0.0)

    pad = CONV_WIDTH // 2
    for r in range(0, ts, rc):
        acc = jnp.zeros((rc, CONV_CH), F32)
        for kk in range(CONV_WIDTH):
            st = HALO + r + kk - pad
            acc = acc + ubuf[st:st + rc, :] * cw_ref[kk:kk + 1, :]
        y = acc + cb_ref[...]
        mu = jnp.mean(y, axis=-1, keepdims=True)
        yc = y - mu
        yn = yc * lax.rsqrt(jnp.mean(yc * yc, axis=-1, keepdims=True) + EPS)
        yn = yn * lg_ref[...] + lb_ref[...]
        oc_ref[0, r:r + rc, :] = (yn * _sigmoid(yn)).astype(oc_ref.dtype)

    for r in range(0, ts, rp):
        t = i * ts + r + lax.broadcasted_iota(jnp.int32, (rp, 1), 0)
        for g, w in enumerate(POOL_WINDOWS):
            lo = w // 2
            hi = w - 1 - lo
            cols = slice(g * GC, (g + 1) * GC)
            win = jnp.zeros((rp, GC), F32)
            for j in range(-lo, hi + 1):
                st = HALO + r + j
                win = win + zbuf[st:st + rp, cols]
            cnt = (jnp.minimum(t + hi + 1, seq) - jnp.maximum(t - lo, 0)).astype(F32)
            d = win / cnt - zbuf[HALO + r:HALO + r + rp, cols]
            yg = _dot(d.astype(BF16), pw_ref[g]) * ps_ref[:, cols]
            op_ref[0, r:r + rp, cols] = yg.astype(op_ref.dtype)


def _local_call(u, zc, lw, ts):
    B, S, _ = u.shape
    nh = ts // HALO
    last = S // HALO - 1
    cur = lambda w: pl.BlockSpec((1, ts, w), lambda b, i: (b, i, 0))
    prev = lambda w: pl.BlockSpec((1, HALO, w), lambda b, i: (b, jnp.maximum(i * nh - 1, 0), 0))
    nxt = lambda w: pl.BlockSpec((1, HALO, w), lambda b, i: (b, jnp.minimum((i + 1) * nh, last), 0))
    kern = functools.partial(_local_kernel, ts=ts, seq=S, rc=64, rp=min(ts, 256))
    return pl.pallas_call(
        kern,
        grid=(B, S // ts),
        in_specs=[cur(CONV_CH), prev(CONV_CH), nxt(CONV_CH), cur(W_BR), prev(W_BR), nxt(W_BR),
                  _const_spec(lw["conv_w"].shape), _const_spec(lw["conv_b"].shape),
                  _const_spec(lw["ln_g"].shape), _const_spec(lw["ln_b"].shape),
                  _const_spec(lw["pool_w"].shape), _const_spec(lw["pool_scale"].shape)],
        out_specs=[cur(CONV_CH), cur(W_BR)],
        out_shape=[jax.ShapeDtypeStruct((B, S, CONV_CH), BF16),
                   jax.ShapeDtypeStruct((B, S, W_BR), BF16)],
        scratch_shapes=[pltpu.VMEM((ts + 2 * HALO, CONV_CH), F32),
                        pltpu.VMEM((ts + 2 * HALO, W_BR), F32)],
        compiler_params=_params("parallel", "parallel"),
        name="conv_pool",
    )(u, u, u, zc, zc, zc, lw["conv_w"], lw["conv_b"], lw["ln_g"], lw["ln_b"],
      lw["pool_w"], lw["pool_scale"])


def _fft1_kernel(p_ref, q_ref, fa_ref, fb_ref, o_ref):
    a = _dot(fa_ref[...], p_ref[0]) + _dot(fb_ref[...], q_ref[0])
    o_ref[0] = a.astype(o_ref.dtype)


def _fft1_call(p, q, tabs, tn):
    B, S, W = p.shape
    n2w = (S // FFT_N1) * W
    pv = p.reshape(B, FFT_N1, n2w)
    qv = q.reshape(B, FFT_N1, n2w)
    blk = pl.BlockSpec((1, FFT_N1, tn), lambda b, j: (b, 0, j))
    return pl.pallas_call(
        _fft1_kernel,
        grid=(B, n2w // tn),
        in_specs=[blk, blk, _const_spec(tabs["f1a"].shape), _const_spec(tabs["f1b"].shape)],
        out_specs=pl.BlockSpec((1, 2 * FFT_N1, tn), lambda b, j: (b, 0, j)),
        out_shape=jax.ShapeDtypeStruct((B, 2 * FFT_N1, n2w), BF16),
        compiler_params=_params("parallel", "parallel"),
        name="fft_stage1",
    )(pv, qv, tabs["f1a"], tabs["f1b"])


def _fft2_kernel(a_ref, gc_ref, gs_ref, o_ref, *, kb, w):
    for j in range(kb):
        f = _dot(gc_ref[j], a_ref[0, 0, j]) + _dot(gs_ref[j], a_ref[0, 1, j])
        o_ref[0, :, j * w:(j + 1) * w] = f.astype(o_ref.dtype)


def _fft2_call(a, tabs, seq, kb):
    B = a.shape[0]
    n2 = seq // FFT_N1
    av = a.reshape(B, 2, FFT_N1, n2, W_BR)
    kern = functools.partial(_fft2_kernel, kb=kb, w=W_BR)
    out = pl.pallas_call(
        kern,
        grid=(B, FFT_N1 // kb),
        in_specs=[pl.BlockSpec((1, 2, kb, n2, W_BR), lambda b, j: (b, 0, j, 0, 0)),
                  pl.BlockSpec((kb, n2, n2), lambda b, j: (j, 0, 0)),
                  pl.BlockSpec((kb, n2, n2), lambda b, j: (j, 0, 0))],
        out_specs=pl.BlockSpec((1, n2, kb * W_BR), lambda b, j: (b, 0, j)),
        out_shape=jax.ShapeDtypeStruct((B, n2, FFT_N1 * W_BR), BF16),
        compiler_params=_params("parallel", "parallel"),
        name="fft_stage2",
    )(av, tabs["g2c"], tabs["g2s"])
    return out.reshape(B, seq, W_BR)


def _mixout_kernel(x_ref, mod_ref, g_ref, wg_ref, oa_ref, wa_ref, cb_ref, wb_ref, pc_ref, wc_ref,
                   f_ref, wd_ref, wo_ref, o_ref):
    x = x_ref[0]
    shift = mod_ref[0, 3:4, :]
    scale = mod_ref[0, 4:5, :]
    gate = mod_ref[0, 5:6, :]
    h = (_rms(x) * g_ref[2:3, :]) * (1.0 + scale) + shift
    hb = h.astype(BF16)
    d = x.shape[-1]
    merged = None
    for br, (b_ref, w_ref) in enumerate(((oa_ref, wa_ref), (cb_ref, wb_ref), (pc_ref, wc_ref), (f_ref, wd_ref))):
        gl = _sigmoid(_dot(hb, wg_ref[:, br * d:(br + 1) * d]))
        term = gl * _dot(b_ref[0], w_ref[...])
        merged = term if merged is None else merged + term
    y = _dot(merged.astype(BF16), wo_ref[...])
    o_ref[0] = x + gate * (_rms(y) * g_ref[3:4, :])


def _mixout_call(x, mod_l, g_l, lw, oa, cb, pc, f, tm):
    B, S, D = x.shape
    tok = lambda w: pl.BlockSpec((1, tm, w), lambda b, i: (b, i, 0))
    return pl.pallas_call(
        _mixout_kernel,
        grid=(B, S // tm),
        in_specs=[tok(D),
                  pl.BlockSpec((1, N_MOD, D), lambda b, i: (b, 0, 0)),
                  _const_spec(g_l.shape),
                  _const_spec(lw["wg"].shape),
                  tok(N_HEADS * LANES), _const_spec(lw["wa"].shape),
                  tok(CONV_CH), _const_spec(lw["wb"].shape),
                  tok(W_BR), _const_spec(lw["wc"].shape),
                  tok(W_BR), _const_spec(lw["wd"].shape),
                  _const_spec(lw["wo"].shape)],
        out_specs=tok(D),
        out_shape=jax.ShapeDtypeStruct(x.shape, F32),
        compiler_params=_params("parallel", "parallel"),
        name="mixer_out",
    )(x, mod_l, g_l, lw["wg"], oa, lw["wa"], cb, lw["wb"], pc, lw["wc"], f, lw["wd"], lw["wo"])


def _rot_half_cols(w):
    half = w.shape[-1] // 2
    return jnp.concatenate([-w[..., half:], w[..., :half]], axis=-1)


def _tables(seq):
    pos = jnp.arange(seq, dtype=F32)
    inv = ROPE_THETA ** (-jnp.arange(0, QK_ROPE, 2, dtype=F32) / QK_ROPE)
    ang = pos[:, None] * inv[None, :]
    cos, sin = jnp.cos(ang), jnp.sin(ang)
    cc = jnp.concatenate([cos, cos], axis=-1)
    ss = jnp.concatenate([sin, sin], axis=-1)
    z64 = jnp.zeros((seq, QK_NOPE), F32)
    z32 = jnp.zeros((seq, LANES - QK_NOPE - QK_ROPE), F32)
    sm_scale = (QK_NOPE + QK_ROPE) ** -0.5 * math.log2(math.e)
    cosk = jnp.concatenate([z64, cc, z32], axis=-1)
    sink = jnp.concatenate([z64, ss, z32], axis=-1)
    cosq = jnp.concatenate([jnp.ones_like(z64), cc, z32], axis=-1) * sm_scale
    sinq = sink * sm_scale

    def cos_sin(m, period):
        th = m.astype(F32) * (2.0 * math.pi / period)
        return jnp.cos(th), jnp.sin(th)

    def dft(n):
        a = jnp.arange(n, dtype=jnp.int32)
        return cos_sin((a[:, None] * a[None, :]) % n, n)

    c_ch, s_ch = dft(GC)
    cs = (jnp.concatenate([c_ch, s_ch], axis=-1) * GC ** -0.5).astype(BF16)

    n1 = FFT_N1
    n2 = seq // n1
    c1, s1 = dft(n1)
    f1a = (jnp.concatenate([c1, -s1], axis=0) * n1 ** -0.5).astype(BF16)
    f1b = (jnp.concatenate([-s1, -c1], axis=0) * n1 ** -0.5).astype(BF16)
    k1 = jnp.arange(n1, dtype=jnp.int32)[:, None, None]
    k2 = jnp.arange(n2, dtype=jnp.int32)[None, :, None]
    t2 = jnp.arange(n2, dtype=jnp.int32)[None, None, :]
    m = ((k1 + n1 * k2) * t2) % seq
    c2, s2 = cos_sin(m, seq)
    g2c = (c2 * n2 ** -0.5).astype(BF16)
    g2s = (s2 * n2 ** -0.5).astype(BF16)
    return dict(cosq=cosq, sinq=sinq, cosk=cosk, sink=sink, cs=cs, f1a=f1a, f1b=f1b, g2c=g2c, g2s=g2s)


def _layer_weights(l, w_in, q_norm_g, w_uq, kv_norm_g, w_ukv, w_a, conv_w, conv_b, conv_ln_g,
                   conv_ln_b, w_b, pool_w, pool_scale, w_c, w_d, w_out):
    D = w_in.shape[1]
    wi = w_in[l]
    zeros = lambda n: jnp.zeros((D, n), F32)
    kr = wi[:, Q_LORA + KV_LORA:W_BR]
    place = lambda w: jnp.concatenate([zeros(QK_NOPE), w, zeros(LANES - QK_NOPE - QK_ROPE)], axis=-1)
    w1 = jnp.concatenate([
        wi[:, Q_LORA:Q_LORA + KV_LORA],
        wi[:, :Q_LORA], zeros(Q_PAD - Q_LORA),
        place(kr), place(_rot_half_cols(kr)),
        wi[:, W_BR:4 * W_BR]], axis=-1).astype(BF16)

    uq = w_uq[l].reshape(Q_LORA, N_HEADS, QK_NOPE + QK_ROPE)
    nope, rope = uq[..., :QK_NOPE], uq[..., QK_NOPE:]
    zq = lambda n: jnp.zeros((Q_LORA, N_HEADS, n), F32)
    rows = ((0, Q_PAD - Q_LORA), (0, 0))
    wq = jnp.pad(jnp.concatenate([nope, rope, zq(LANES - QK_NOPE - QK_ROPE)], axis=-1)
                 .reshape(Q_LORA, N_HEADS * LANES), rows).astype(BF16)
    wqr = jnp.pad(jnp.concatenate([zq(QK_NOPE), _rot_half_cols(rope), zq(LANES - QK_NOPE - QK_ROPE)], axis=-1)
                  .reshape(Q_LORA, N_HEADS * LANES), rows).astype(BF16)

    ukv = w_ukv[l].reshape(KV_LORA, N_HEADS, QK_NOPE + V_HEAD)
    zk = jnp.zeros((KV_LORA, N_HEADS, LANES - QK_NOPE), F32)
    zv = jnp.zeros((KV_LORA, N_HEADS, LANES - V_HEAD), F32)
    wkv = jnp.concatenate([
        jnp.concatenate([ukv[..., :QK_NOPE], zk], axis=-1).reshape(KV_LORA, N_HEADS * LANES),
        jnp.concatenate([ukv[..., QK_NOPE:], zv], axis=-1).reshape(KV_LORA, N_HEADS * LANES)],
        axis=-1).astype(BF16)

    wa = jnp.pad(w_a[l].reshape(N_HEADS, V_HEAD, D), ((0, 0), (0, LANES - V_HEAD), (0, 0)))
    wa = wa.reshape(N_HEADS * LANES, D).astype(BF16)
    gq = jnp.pad(q_norm_g[l], (0, Q_PAD - Q_LORA)).reshape(1, Q_PAD)
    return dict(
        w1=w1, wq=wq, wqr=wqr, wkv=wkv, gq=gq, gkv=kv_norm_g[l].reshape(1, KV_LORA),
        wg=wi[:, 4 * W_BR:].astype(BF16), wa=wa,
        wb=w_b[l].astype(BF16), wc=w_c[l].astype(BF16), wd=w_d[l].astype(BF16), wo=w_out[l].astype(BF16),
        conv_w=conv_w[l], conv_b=conv_b[l].reshape(1, CONV_CH),
        ln_g=conv_ln_g[l].reshape(1, CONV_CH), ln_b=conv_ln_b[l].reshape(1, CONV_CH),
        pool_w=pool_w[l].astype(BF16), pool_scale=pool_scale[l].reshape(1, W_BR))


def _tiles(seq):
    t = lambda n: min(n, seq)
    return dict(ffn=t(512), mix=t(512), tq=t(512), tk=t(1024), local=t(512))


def kernel(x, c, ada_w, ada_b, norm_g, ffn1_w_in, ffn1_w_out, ffn2_w_in, ffn2_w_out, w_in, q_norm_g,
           w_uq, kv_norm_g, w_ukv, w_a, conv_w, conv_b, conv_ln_g, conv_ln_b, w_b, pool_w, pool_scale,
           w_c, w_d, w_out):
    B, S, D = x.shape
    L = ada_w.shape[0]
    assert D == D_MODEL and S % (FFT_N1 * 16) == 0
    ts = _tiles(S)
    tabs = _tables(S)
    mod = _mod_call(c, ada_w, ada_b).reshape(L, B, N_MOD, D)
    n2w = (S // FFT_N1) * W_BR
    for l in range(L):
        lw = _layer_weights(l, w_in, q_norm_g, w_uq, kv_norm_g, w_ukv, w_a, conv_w, conv_b, conv_ln_g,
                            conv_ln_b, w_b, pool_w, pool_scale, w_c, w_d, w_out)
        mod_l, g_l = mod[l], norm_g[l]
        x = _ffn_call(x, mod_l, g_l, ffn1_w_in[l].astype(BF16), ffn1_w_out[l].astype(BF16), 0, ts["ffn"])
        q, k, v, u, zc, p, qq = _mixin_call(x, mod_l, g_l, lw, tabs, ts["mix"])
        oa = _attn_call(q, k, v, ts["tq"], ts["tk"])
        cb, pc = _local_call(u, zc, lw, ts["local"])
        a = _fft1_call(p, qq, tabs, min(4096, n2w))
        f = _fft2_call(a, tabs, S, 8)
        x = _mixout_call(x, mod_l, g_l, lw, oa, cb, pc, f, ts["mix"])
        x = _ffn_call(x, mod_l, g_l, ffn2_w_in[l].astype(BF16), ffn2_w_out[l].astype(BF16), 2, ts["ffn"])
    return x
```

```python
import functools
import math

import jax
import jax.numpy as jnp
import numpy as np
from jax import lax
from jax.experimental import pallas as pl
from jax.experimental.pallas import tpu as pltpu

F32 = jnp.float32
BF16 = jnp.bfloat16

D_MODEL = 1024
D_FF = 2816
N_HEADS = 8
Q_LORA = 352
KV_LORA = 128
QK_NOPE = 64
QK_ROPE = 32
V_HEAD = 64
ROPE_THETA = 10000.0
CONV_CH = 256
CONV_WIDTH = 31
POOL_WINDOWS = (2, 4, 8, 16)
GC = 128
N_GROUPS = 4
W_BR = 512
N_MOD = 9
EPS = 1e-6

LANES = 128
SUBLANES = 8
Q_PAD = 384
HALO = 16
FFT_N1 = 128
VMEM_LIMIT = 56 * 1024 * 1024

C_CKV = 0
C_CQ = C_CKV + KV_LORA
C_KRP = C_CQ + Q_PAD
C_KRR = C_KRP + LANES
C_ZB = C_KRR + LANES
C_ZC = C_ZB + W_BR
C_ZD = C_ZC + W_BR
W1_COLS = C_ZD + W_BR


def _params(*sem):
    return pltpu.CompilerParams(dimension_semantics=sem, vmem_limit_bytes=VMEM_LIMIT)


def _const_spec(shape):
    zeros = (0,) * len(shape)
    return pl.BlockSpec(shape, lambda *_: zeros, pipeline_mode=pl.Buffered(1))


def _dot(a, b):
    return jnp.dot(a, b, preferred_element_type=F32)


def _rms(x):
    return x * lax.rsqrt(jnp.mean(x * x, axis=-1, keepdims=True) + EPS)


def _sigmoid(x):
    return 1.0 / (1.0 + jnp.exp(-x))


def _mod_kernel(c_ref, w_ref, b_ref, o_ref):
    c = c_ref[...]
    ca = c * _sigmoid(c)
    o_ref[0] = jnp.dot(ca, w_ref[0], preferred_element_type=F32,
                       precision=lax.Precision.HIGHEST) + b_ref[0]


def _mod_call(c, ada_w, ada_b):
    L, D, N = ada_w.shape
    B = c.shape[0]
    tn = 1152
    return pl.pallas_call(
        _mod_kernel,
        grid=(L, N // tn),
        in_specs=[pl.BlockSpec((B, D), lambda l, j: (0, 0)),
                  pl.BlockSpec((1, D, tn), lambda l, j: (l, 0, j)),
                  pl.BlockSpec((1, 1, tn), lambda l, j: (l, 0, j))],
        out_specs=pl.BlockSpec((1, B, tn), lambda l, j: (l, 0, j)),
        out_shape=jax.ShapeDtypeStruct((L, B, N), F32),
        compiler_params=_params("parallel", "parallel"),
        name="adaln_mod",
    )(c, ada_w, ada_b.reshape(L, 1, N))


def _ffn_kernel(x_ref, mod_ref, g_ref, win_ref, wout_ref, o_ref, a_scr, *, sub, chunk):
    x = x_ref[0]
    shift = mod_ref[0, 3 * sub:3 * sub + 1, :]
    scale = mod_ref[0, 3 * sub + 1:3 * sub + 2, :]
    gate = mod_ref[0, 3 * sub + 2:3 * sub + 3, :]
    g_in = g_ref[2 * sub:2 * sub + 1, :]
    g_out = g_ref[2 * sub + 1:2 * sub + 2, :]
    h = (_rms(x) * g_in) * (1.0 + scale) + shift
    hb = h.astype(BF16)
    for j in range(D_FF // chunk):
        gj = _dot(hb, win_ref[:, j * chunk:(j + 1) * chunk])
        uj = _dot(hb, win_ref[:, D_FF + j * chunk:D_FF + (j + 1) * chunk])
        a_scr[:, j * chunk:(j + 1) * chunk] = (gj * _sigmoid(gj) * uj).astype(BF16)
    y = _dot(a_scr[...], wout_ref[...])
    o_ref[0] = x + (0.5 * gate) * (_rms(y) * g_out)


def _ffn_call(x, mod_l, g_l, w_in, w_out, sub, tm):
    B, S, D = x.shape
    kern = functools.partial(_ffn_kernel, sub=sub, chunk=256)
    return pl.pallas_call(
        kern,
        grid=(B, S // tm),
        in_specs=[pl.BlockSpec((1, tm, D), lambda b, i: (b, i, 0)),
                  pl.BlockSpec((1, N_MOD, D), lambda b, i: (b, 0, 0)),
                  _const_spec(g_l.shape),
                  _const_spec(w_in.shape),
                  _const_spec(w_out.shape)],
        out_specs=pl.BlockSpec((1, tm, D), lambda b, i: (b, i, 0)),
        out_shape=jax.ShapeDtypeStruct(x.shape, F32),
        scratch_shapes=[pltpu.VMEM((tm, D_FF), BF16)],
        compiler_params=_params("parallel", "parallel"),
        name=f"ffn{sub}",
    )(x, mod_l, g_l, w_in, w_out)


def _mixin_kernel(x_ref, mod_ref, g_ref, w1_ref, gq_ref, gkv_ref, wq_ref, wqr_ref, wkv_ref,
                  cq_ref, sq_ref, ck_ref, sk_ref, cs_ref,
                  q_ref, k_ref, v_ref, u_ref, zc_ref, p_ref, qq_ref):
    x = x_ref[0]
    shift = mod_ref[0, 3:4, :]
    scale = mod_ref[0, 4:5, :]
    h = (_rms(x) * g_ref[2:3, :]) * (1.0 + scale) + shift
    hb = h.astype(BF16)
    z = _dot(hb, w1_ref[...])

    cq = z[:, C_CQ:C_CQ + Q_PAD]
    cqn = cq * lax.rsqrt(jnp.sum(cq * cq, axis=-1, keepdims=True) * (1.0 / Q_LORA) + EPS)
    cqn = (cqn * gq_ref[...]).astype(BF16)
    qa = _dot(cqn, wq_ref[...])
    qb = _dot(cqn, wqr_ref[...])
    cosq = cq_ref[...]
    sinq = sq_ref[...]
    for hd in range(N_HEADS):
        sl = slice(hd * LANES, (hd + 1) * LANES)
        q_ref[0, hd] = (qa[:, sl] * cosq + qb[:, sl] * sinq).T.astype(BF16)

    ckv = z[:, C_CKV:C_CKV + KV_LORA]
    ckvn = (_rms(ckv) * gkv_ref[...]).astype(BF16)
    kv = _dot(ckvn, wkv_ref[...])
    k_rope = z[:, C_KRP:C_KRP + LANES] * ck_ref[...] + z[:, C_KRR:C_KRR + LANES] * sk_ref[...]
    lane = lax.broadcasted_iota(jnp.int32, (1, LANES), 1)
    ones_col = (lane == V_HEAD).astype(F32)
    for hd in range(N_HEADS):
        k_ref[0, hd] = (kv[:, hd * LANES:(hd + 1) * LANES] + k_rope).astype(BF16)
        v_ref[0, hd] = (kv[:, (N_HEADS + hd) * LANES:(N_HEADS + hd + 1) * LANES] + ones_col).T.astype(BF16)

    a = z[:, C_ZB:C_ZB + CONV_CH]
    gt = z[:, C_ZB + CONV_CH:C_ZB + 2 * CONV_CH]
    u_ref[0] = a * _sigmoid(gt)

    zc_ref[0] = z[:, C_ZC:C_ZC + W_BR]

    for g in range(N_GROUPS):
        zd = z[:, C_ZD + g * GC:C_ZD + (g + 1) * GC].astype(BF16)
        pq = _dot(zd, cs_ref[...])
        p_ref[0, :, g * GC:(g + 1) * GC] = pq[:, :GC].astype(BF16)
        qq_ref[0, :, g * GC:(g + 1) * GC] = pq[:, GC:].astype(BF16)


def _mixin_call(x, mod_l, g_l, lw, tabs, tm):
    B, S, D = x.shape
    tok = lambda w: pl.BlockSpec((1, tm, w), lambda b, i: (b, i, 0))
    head = pl.BlockSpec((1, N_HEADS, tm, LANES), lambda b, i: (b, 0, i, 0))
    head_t = pl.BlockSpec((1, N_HEADS, LANES, tm), lambda b, i: (b, 0, 0, i))
    tab = pl.BlockSpec((tm, LANES), lambda b, i: (i, 0))
    hshape = jax.ShapeDtypeStruct((B, N_HEADS, S, LANES), BF16)
    hshape_t = jax.ShapeDtypeStruct((B, N_HEADS, LANES, S), BF16)
    return pl.pallas_call(
        _mixin_kernel,
        grid=(B, S // tm),
        in_specs=[tok(D),
                  pl.BlockSpec((1, N_MOD, D), lambda b, i: (b, 0, 0)),
                  _const_spec(g_l.shape),
                  _const_spec(lw["w1"].shape),
                  _const_spec(lw["gq"].shape),
                  _const_spec(lw["gkv"].shape),
                  _const_spec(lw["wq"].shape),
                  _const_spec(lw["wqr"].shape),
                  _const_spec(lw["wkv"].shape),
                  tab, tab, tab, tab,
                  _const_spec(tabs["cs"].shape)],
        out_specs=[head_t, head, head_t, tok(CONV_CH), tok(W_BR), tok(W_BR), tok(W_BR)],
        out_shape=[hshape_t, hshape, hshape_t,
                   jax.ShapeDtypeStruct((B, S, CONV_CH), F32),
                   jax.ShapeDtypeStruct((B, S, W_BR), F32),
                   jax.ShapeDtypeStruct((B, S, W_BR), BF16),
                   jax.ShapeDtypeStruct((B, S, W_BR), BF16)],
        compiler_params=_params("parallel", "parallel"),
        name="mixer_in",
    )(x, mod_l, g_l, lw["w1"], lw["gq"], lw["gkv"], lw["wq"], lw["wqr"], lw["wkv"],
      tabs["cosq"], tabs["sinq"], tabs["cosk"], tabs["sink"], tabs["cs"])


def _attn_kernel(qt_ref, k_ref, vt_ref, o_ref, m_scr, acc_scr, st_scr, *, tq, tk):
    seq = k_ref.shape[2]
    nq, nk = seq // tq, seq // tk
    assert nk % 2 == 0

    def scores(qoff, j, slot):
        st_scr[slot] = _dot(k_ref[0, 0, j * tk:(j + 1) * tk, :], qt_ref[0, 0, :, pl.ds(qoff, tq)])

    scores(0, 0, 0)

    def q_tile(qi, carry):
        qoff = pl.multiple_of(qi * tq, tq)
        qoff_next = pl.multiple_of(jnp.minimum(qi + 1, nq - 1) * tq, tq)
        m_scr[...] = jnp.full(m_scr.shape, -jnp.inf, F32)
        acc_scr[...] = jnp.zeros(acc_scr.shape, F32)
        for j in range(nk):
            slot = j % 2
            if j + 1 < nk:
                scores(qoff, j + 1, 1 - slot)
            else:
                scores(qoff_next, 0, 1 - slot)
            st = st_scr[slot]
            m_prev = m_scr[...]
            m_new = jnp.maximum(m_prev, jnp.max(st, axis=0, keepdims=True))
            alpha = jnp.exp2(m_prev - m_new)
            pt = jnp.exp2(st - m_new).astype(BF16)
            acc_scr[...] = alpha * acc_scr[...] + _dot(vt_ref[0, 0, :, j * tk:(j + 1) * tk], pt)
            m_scr[...] = m_new
        acc = acc_scr[...]
        denom = acc[V_HEAD:V_HEAD + 1, :]
        o_ref[0, pl.ds(qoff, tq), :] = (acc / denom).T.astype(o_ref.dtype)
        return carry

    lax.fori_loop(0, nq, q_tile, 0)


def _attn_call(qt, k, vt, tq, tk):
    B, H, S, _ = k.shape
    kern = functools.partial(_attn_kernel, tq=tq, tk=tk)
    return pl.pallas_call(
        kern,
        grid=(B, H),
        in_specs=[pl.BlockSpec((1, 1, LANES, S), lambda b, h: (b, h, 0, 0)),
                  pl.BlockSpec((1, 1, S, LANES), lambda b, h: (b, h, 0, 0)),
                  pl.BlockSpec((1, 1, LANES, S), lambda b, h: (b, h, 0, 0))],
        out_specs=pl.BlockSpec((1, S, LANES), lambda b, h: (b, 0, h)),
        out_shape=jax.ShapeDtypeStruct((B, S, H * LANES), BF16),
        scratch_shapes=[pltpu.VMEM((1, tq), F32), pltpu.VMEM((LANES, tq), F32),
                        pltpu.VMEM((2, tk, tq), F32)],
        compiler_params=_params("parallel", "arbitrary"),
        name="attention",
    )(qt, k, vt)


def _local_kernel(u_ref, up_ref, un_ref, z_ref, zp_ref, zn_ref, cw_ref, cb_ref, lg_ref, lb_ref,
                  pw_ref, ps_ref, oc_ref, op_ref, ubuf, zbuf, ush, *, ts, seq, rc, rp):
    i = pl.program_id(1)
    has_prev = i > 0
    has_next = i < pl.num_programs(1) - 1
    ubuf[0:HALO, :] = jnp.where(has_prev, up_ref[0], 0.0)
    ubuf[HALO:HALO + ts, :] = u_ref[0]
    ubuf[HALO + ts:, :] = jnp.where(has_next, un_ref[0], 0.0)
    zbuf[0:HALO, :] = jnp.where(has_prev, zp_ref[0], 0.0)
    zbuf[HALO:HALO + ts, :] = z_ref[0]
    zbuf[HALO + ts:, :] = jnp.where(has_next, zn_ref[0], 0.0)

    span = ts + 2 * HALO - SUBLANES
    for s in range(1, SUBLANES):
        ush[s - 1, 0:span, :] = ubuf[s:s + span, :]

    pad = CONV_WIDTH // 2
    for r in range(0, ts, rc):
        acc = jnp.zeros((rc, CONV_CH), F32)
        for kk in range(CONV_WIDTH):
            st = HALO + r + kk - pad
            s, base = st % SUBLANES, st - st % SUBLANES
            tap = ubuf[base:base + rc, :] if s == 0 else ush[s - 1, base:base + rc, :]
            acc = acc + tap * cw_ref[kk:kk + 1, :]
        y = acc + cb_ref[...]
        mu = jnp.mean(y, axis=-1, keepdims=True)
        yc = y - mu
        yn = yc * lax.rsqrt(jnp.mean(yc * yc, axis=-1, keepdims=True) + EPS)
        yn = yn * lg_ref[...] + lb_ref[...]
        oc_ref[0, r:r + rc, :] = (yn * _sigmoid(yn)).astype(oc_ref.dtype)

    for r in range(0, ts, rp):
        t = i * ts + r + lax.broadcasted_iota(jnp.int32, (rp, 1), 0)
        for g, w in enumerate(POOL_WINDOWS):
            lo = w // 2
            hi = w - 1 - lo
            cols = slice(g * GC, (g + 1) * GC)
            win = jnp.zeros((rp, GC), F32)
            for j in range(-lo, hi + 1):
                st = HALO + r + j
                win = win + zbuf[st:st + rp, cols]
            cnt = (jnp.minimum(t + hi + 1, seq) - jnp.maximum(t - lo, 0)).astype(F32)
            d = win / cnt - zbuf[HALO + r:HALO + r + rp, cols]
            yg = _dot(d.astype(BF16), pw_ref[g]) * ps_ref[:, cols]
            op_ref[0, r:r + rp, cols] = yg.astype(op_ref.dtype)


def _local_call(u, zc, lw, ts):
    B, S, _ = u.shape
    nh = ts // HALO
    last = S // HALO - 1
    cur = lambda w: pl.BlockSpec((1, ts, w), lambda b, i: (b, i, 0))
    prev = lambda w: pl.BlockSpec((1, HALO, w), lambda b, i: (b, jnp.maximum(i * nh - 1, 0), 0))
    nxt = lambda w: pl.BlockSpec((1, HALO, w), lambda b, i: (b, jnp.minimum((i + 1) * nh, last), 0))
    kern = functools.partial(_local_kernel, ts=ts, seq=S, rc=64, rp=min(ts, 256))
    return pl.pallas_call(
        kern,
        grid=(B, S // ts),
        in_specs=[cur(CONV_CH), prev(CONV_CH), nxt(CONV_CH), cur(W_BR), prev(W_BR), nxt(W_BR),
                  _const_spec(lw["conv_w"].shape), _const_spec(lw["conv_b"].shape),
                  _const_spec(lw["ln_g"].shape), _const_spec(lw["ln_b"].shape),
                  _const_spec(lw["pool_w"].shape), _const_spec(lw["pool_scale"].shape)],
        out_specs=[cur(CONV_CH), cur(W_BR)],
        out_shape=[jax.ShapeDtypeStruct((B, S, CONV_CH), BF16),
                   jax.ShapeDtypeStruct((B, S, W_BR), BF16)],
        scratch_shapes=[pltpu.VMEM((ts + 2 * HALO, CONV_CH), F32),
                        pltpu.VMEM((ts + 2 * HALO, W_BR), F32),
                        pltpu.VMEM((SUBLANES - 1, ts + 2 * HALO, CONV_CH), F32)],
        compiler_params=_params("parallel", "parallel"),
        name="conv_pool",
    )(u, u, u, zc, zc, zc, lw["conv_w"], lw["conv_b"], lw["ln_g"], lw["ln_b"],
      lw["pool_w"], lw["pool_scale"])


def _fft1_kernel(p_ref, q_ref, fa_ref, fb_ref, o_ref):
    a = _dot(fa_ref[...], p_ref[0]) + _dot(fb_ref[...], q_ref[0])
    o_ref[0] = a.astype(o_ref.dtype)


def _fft1_call(p, q, tabs, tn):
    B, S, W = p.shape
    n2w = (S // FFT_N1) * W
    pv = p.reshape(B, FFT_N1, n2w)
    qv = q.reshape(B, FFT_N1, n2w)
    blk = pl.BlockSpec((1, FFT_N1, tn), lambda b, j: (b, 0, j))
    return pl.pallas_call(
        _fft1_kernel,
        grid=(B, n2w // tn),
        in_specs=[blk, blk, _const_spec(tabs["f1a"].shape), _const_spec(tabs["f1b"].shape)],
        out_specs=pl.BlockSpec((1, 2 * FFT_N1, tn), lambda b, j: (b, 0, j)),
        out_shape=jax.ShapeDtypeStruct((B, 2 * FFT_N1, n2w), BF16),
        compiler_params=_params("parallel", "parallel"),
        name="fft_stage1",
    )(pv, qv, tabs["f1a"], tabs["f1b"])


def _fft2_kernel(a_ref, gc_ref, gs_ref, o_ref, *, kb, w):
    for j in range(kb):
        f = _dot(gc_ref[j], a_ref[0, 0, j]) + _dot(gs_ref[j], a_ref[0, 1, j])
        o_ref[0, :, j * w:(j + 1) * w] = f.astype(o_ref.dtype)


def _fft2_call(a, tabs, seq, kb):
    B = a.shape[0]
    n2 = seq // FFT_N1
    av = a.reshape(B, 2, FFT_N1, n2, W_BR)
    kern = functools.partial(_fft2_kernel, kb=kb, w=W_BR)
    out = pl.pallas_call(
        kern,
        grid=(B, FFT_N1 // kb),
        in_specs=[pl.BlockSpec((1, 2, kb, n2, W_BR), lambda b, j: (b, 0, j, 0, 0)),
                  pl.BlockSpec((kb, n2, n2), lambda b, j: (j, 0, 0)),
                  pl.BlockSpec((kb, n2, n2), lambda b, j: (j, 0, 0))],
        out_specs=pl.BlockSpec((1, n2, kb * W_BR), lambda b, j: (b, 0, j)),
        out_shape=jax.ShapeDtypeStruct((B, n2, FFT_N1 * W_BR), BF16),
        compiler_params=_params("parallel", "parallel"),
        name="fft_stage2",
    )(av, tabs["g2c"], tabs["g2s"])
    return out.reshape(B, seq, W_BR)


def _mixout_kernel(x_ref, mod_ref, g_ref, wg_ref, oa_ref, wa_ref, cb_ref, wb_ref, pc_ref, wc_ref,
                   f_ref, wd_ref, wo_ref, o_ref):
    x = x_ref[0]
    shift = mod_ref[0, 3:4, :]
    scale = mod_ref[0, 4:5, :]
    gate = mod_ref[0, 5:6, :]
    h = (_rms(x) * g_ref[2:3, :]) * (1.0 + scale) + shift
    hb = h.astype(BF16)
    d = x.shape[-1]
    merged = None
    for br, (b_ref, w_ref) in enumerate(((oa_ref, wa_ref), (cb_ref, wb_ref), (pc_ref, wc_ref), (f_ref, wd_ref))):
        gl = _sigmoid(_dot(hb, wg_ref[:, br * d:(br + 1) * d]))
        term = gl * _dot(b_ref[0], w_ref[...])
        merged = term if merged is None else merged + term
    y = _dot(merged.astype(BF16), wo_ref[...])
    o_ref[0] = x + gate * (_rms(y) * g_ref[3:4, :])


def _mixout_call(x, mod_l, g_l, lw, oa, cb, pc, f, tm):
    B, S, D = x.shape
    tok = lambda w: pl.BlockSpec((1, tm, w), lambda b, i: (b, i, 0))
    return pl.pallas_call(
        _mixout_kernel,
        grid=(B, S // tm),
        in_specs=[tok(D),
                  pl.BlockSpec((1, N_MOD, D), lambda b, i: (b, 0, 0)),
                  _const_spec(g_l.shape),
                  _const_spec(lw["wg"].shape),
                  tok(N_HEADS * LANES), _const_spec(lw["wa"].shape),
                  tok(CONV_CH), _const_spec(lw["wb"].shape),
                  tok(W_BR), _const_spec(lw["wc"].shape),
                  tok(W_BR), _const_spec(lw["wd"].shape),
                  _const_spec(lw["wo"].shape)],
        out_specs=tok(D),
        out_shape=jax.ShapeDtypeStruct(x.shape, F32),
        compiler_params=_params("parallel", "parallel"),
        name="mixer_out",
    )(x, mod_l, g_l, lw["wg"], oa, lw["wa"], cb, lw["wb"], pc, lw["wc"], f, lw["wd"], lw["wo"])


def _rot_half_cols(w):
    half = w.shape[-1] // 2
    return jnp.concatenate([-w[..., half:], w[..., :half]], axis=-1)


def _tables(seq):
    pos = jnp.arange(seq, dtype=F32)
    inv = ROPE_THETA ** (-jnp.arange(0, QK_ROPE, 2, dtype=F32) / QK_ROPE)
    ang = pos[:, None] * inv[None, :]
    cos, sin = jnp.cos(ang), jnp.sin(ang)
    cc = jnp.concatenate([cos, cos], axis=-1)
    ss = jnp.concatenate([sin, sin], axis=-1)
    z64 = jnp.zeros((seq, QK_NOPE), F32)
    z32 = jnp.zeros((seq, LANES - QK_NOPE - QK_ROPE), F32)
    sm_scale = (QK_NOPE + QK_ROPE) ** -0.5 * math.log2(math.e)
    cosk = jnp.concatenate([z64, cc, z32], axis=-1)
    sink = jnp.concatenate([z64, ss, z32], axis=-1)
    cosq = jnp.concatenate([jnp.ones_like(z64), cc, z32], axis=-1) * sm_scale
    sinq = sink * sm_scale

    def cos_sin(m, period):
        th = m.astype(F32) * (2.0 * math.pi / period)
        return jnp.cos(th), jnp.sin(th)

    def dft(n):
        a = jnp.arange(n, dtype=jnp.int32)
        return cos_sin((a[:, None] * a[None, :]) % n, n)

    c_ch, s_ch = dft(GC)
    cs = (jnp.concatenate([c_ch, s_ch], axis=-1) * GC ** -0.5).astype(BF16)

    n1 = FFT_N1
    n2 = seq // n1
    c1, s1 = dft(n1)
    f1a = (jnp.concatenate([c1, -s1], axis=0) * n1 ** -0.5).astype(BF16)
    f1b = (jnp.concatenate([-s1, -c1], axis=0) * n1 ** -0.5).astype(BF16)
    k1 = jnp.arange(n1, dtype=jnp.int32)[:, None, None]
    k2 = jnp.arange(n2, dtype=jnp.int32)[None, :, None]
    t2 = jnp.arange(n2, dtype=jnp.int32)[None, None, :]
    m = ((k1 + n1 * k2) * t2) % seq
    c2, s2 = cos_sin(m, seq)
    g2c = (c2 * n2 ** -0.5).astype(BF16)
    g2s = (s2 * n2 ** -0.5).astype(BF16)
    return dict(cosq=cosq, sinq=sinq, cosk=cosk, sink=sink, cs=cs, f1a=f1a, f1b=f1b, g2c=g2c, g2s=g2s)


def _layer_weights(l, w_in, q_norm_g, w_uq, kv_norm_g, w_ukv, w_a, conv_w, conv_b, conv_ln_g,
                   conv_ln_b, w_b, pool_w, pool_scale, w_c, w_d, w_out):
    D = w_in.shape[1]
    wi = w_in[l]
    zeros = lambda n: jnp.zeros((D, n), F32)
    kr = wi[:, Q_LORA + KV_LORA:W_BR]
    place = lambda w: jnp.concatenate([zeros(QK_NOPE), w, zeros(LANES - QK_NOPE - QK_ROPE)], axis=-1)
    w1 = jnp.concatenate([
        wi[:, Q_LORA:Q_LORA + KV_LORA],
        wi[:, :Q_LORA], zeros(Q_PAD - Q_LORA),
        place(kr), place(_rot_half_cols(kr)),
        wi[:, W_BR:4 * W_BR]], axis=-1).astype(BF16)

    uq = w_uq[l].reshape(Q_LORA, N_HEADS, QK_NOPE + QK_ROPE)
    nope, rope = uq[..., :QK_NOPE], uq[..., QK_NOPE:]
    zq = lambda n: jnp.zeros((Q_LORA, N_HEADS, n), F32)
    rows = ((0, Q_PAD - Q_LORA), (0, 0))
    wq = jnp.pad(jnp.concatenate([nope, rope, zq(LANES - QK_NOPE - QK_ROPE)], axis=-1)
                 .reshape(Q_LORA, N_HEADS * LANES), rows).astype(BF16)
    wqr = jnp.pad(jnp.concatenate([zq(QK_NOPE), _rot_half_cols(rope), zq(LANES - QK_NOPE - QK_ROPE)], axis=-1)
                  .reshape(Q_LORA, N_HEADS * LANES), rows).astype(BF16)

    ukv = w_ukv[l].reshape(KV_LORA, N_HEADS, QK_NOPE + V_HEAD)
    zk = jnp.zeros((KV_LORA, N_HEADS, LANES - QK_NOPE), F32)
    zv = jnp.zeros((KV_LORA, N_HEADS, LANES - V_HEAD), F32)
    wkv = jnp.concatenate([
        jnp.concatenate([ukv[..., :QK_NOPE], zk], axis=-1).reshape(KV_LORA, N_HEADS * LANES),
        jnp.concatenate([ukv[..., QK_NOPE:], zv], axis=-1).reshape(KV_LORA, N_HEADS * LANES)],
        axis=-1).astype(BF16)

    wa = jnp.pad(w_a[l].reshape(N_HEADS, V_HEAD, D), ((0, 0), (0, LANES - V_HEAD), (0, 0)))
    wa = wa.reshape(N_HEADS * LANES, D).astype(BF16)
    gq = jnp.pad(q_norm_g[l], (0, Q_PAD - Q_LORA)).reshape(1, Q_PAD)
    return dict(
        w1=w1, wq=wq, wqr=wqr, wkv=wkv, gq=gq, gkv=kv_norm_g[l].reshape(1, KV_LORA),
        wg=wi[:, 4 * W_BR:].astype(BF16), wa=wa,
        wb=w_b[l].astype(BF16), wc=w_c[l].astype(BF16), wd=w_d[l].astype(BF16), wo=w_out[l].astype(BF16),
        conv_w=conv_w[l], conv_b=conv_b[l].reshape(1, CONV_CH),
        ln_g=conv_ln_g[l].reshape(1, CONV_CH), ln_b=conv_ln_b[l].reshape(1, CONV_CH),
        pool_w=pool_w[l].astype(BF16), pool_scale=pool_scale[l].reshape(1, W_BR))


def _tiles(seq):
    t = lambda n: min(n, seq)
    return dict(ffn=t(512), mix=t(512), tq=t(1024), tk=t(1024), local=t(512))


def kernel(x, c, ada_w, ada_b, norm_g, ffn1_w_in, ffn1_w_out, ffn2_w_in, ffn2_w_out, w_in, q_norm_g,
           w_uq, kv_norm_g, w_ukv, w_a, conv_w, conv_b, conv_ln_g, conv_ln_b, w_b, pool_w, pool_scale,
           w_c, w_d, w_out):
    B, S, D = x.shape
    L = ada_w.shape[0]
    assert D == D_MODEL and S % (FFT_N1 * 16) == 0
    ts = _tiles(S)
    tabs = _tables(S)
    mod = _mod_call(c, ada_w, ada_b).reshape(L, B, N_MOD, D)
    n2w = (S // FFT_N1) * W_BR
    for l in range(L):
        lw = _layer_weights(l, w_in, q_norm_g, w_uq, kv_norm_g, w_ukv, w_a, conv_w, conv_b, conv_ln_g,
                            conv_ln_b, w_b, pool_w, pool_scale, w_c, w_d, w_out)
        mod_l, g_l = mod[l], norm_g[l]
        x = _ffn_call(x, mod_l, g_l, ffn1_w_in[l].astype(BF16), ffn1_w_out[l].astype(BF16), 0, ts["ffn"])
        q, k, v, u, zc, p, qq = _mixin_call(x, mod_l, g_l, lw, tabs, ts["mix"])
        oa = _attn_call(q, k, v, ts["tq"], ts["tk"])
        cb, pc = _local_call(u, zc, lw, ts["local"])
        a = _fft1_call(p, qq, tabs, min(4096, n2w))
        f = _fft2_call(a, tabs, S, 8)
        x = _mixout_call(x, mod_l, g_l, lw, oa, cb, pc, f, ts["mix"])
        x = _ffn_call(x, mod_l, g_l, ffn2_w_in[l].astype(BF16), ffn2_w_out[l].astype(BF16), 2, ts["ffn"])
    return x
```

```python
import functools
import math

import jax
import jax.numpy as jnp
import numpy as np
from jax import lax
from jax.experimental import pallas as pl
from jax.experimental.pallas import tpu as pltpu

F32 = jnp.float32
BF16 = jnp.bfloat16

D_MODEL = 1024
D_FF = 2816
N_HEADS = 8
Q_LORA = 352
KV_LORA = 128
QK_NOPE = 64
QK_ROPE = 32
V_HEAD = 64
V_ROWS = 80
ROPE_THETA = 10000.0
CONV_CH = 256
CONV_WIDTH = 31
POOL_WINDOWS = (2, 4, 8, 16)
GC = 128
N_GROUPS = 4
W_BR = 512
N_MOD = 9
EPS = 1e-6

LANES = 128
SUBLANES = 8
Q_PAD = 384
HALO = 16
FFT_N1 = 128
VMEM_LIMIT = 56 * 1024 * 1024

C_CKV = 0
C_CQ = C_CKV + KV_LORA
C_KRP = C_CQ + Q_PAD
C_KRR = C_KRP + LANES
C_ZB = C_KRR + LANES
C_ZC = C_ZB + W_BR
C_ZD = C_ZC + W_BR
W1_COLS = C_ZD + W_BR


def _params(*sem):
    return pltpu.CompilerParams(dimension_semantics=sem, vmem_limit_bytes=VMEM_LIMIT)


def _const_spec(shape):
    zeros = (0,) * len(shape)
    return pl.BlockSpec(shape, lambda *_: zeros, pipeline_mode=pl.Buffered(1))


def _dot(a, b):
    return jnp.dot(a, b, preferred_element_type=F32)


def _rms(x):
    return x * lax.rsqrt(jnp.mean(x * x, axis=-1, keepdims=True) + EPS)


def _sigmoid(x):
    return 1.0 / (1.0 + jnp.exp(-x))


def _mod_kernel(c_ref, w_ref, b_ref, o_ref):
    c = c_ref[...]
    ca = c * _sigmoid(c)
    o_ref[0] = jnp.dot(ca, w_ref[0], preferred_element_type=F32,
                       precision=lax.Precision.HIGHEST) + b_ref[0]


def _mod_call(c, ada_w, ada_b):
    L, D, N = ada_w.shape
    B = c.shape[0]
    tn = 1152
    return pl.pallas_call(
        _mod_kernel,
        grid=(L, N // tn),
        in_specs=[pl.BlockSpec((B, D), lambda l, j: (0, 0)),
                  pl.BlockSpec((1, D, tn), lambda l, j: (l, 0, j)),
                  pl.BlockSpec((1, 1, tn), lambda l, j: (l, 0, j))],
        out_specs=pl.BlockSpec((1, B, tn), lambda l, j: (l, 0, j)),
        out_shape=jax.ShapeDtypeStruct((L, B, N), F32),
        compiler_params=_params("parallel", "parallel"),
        name="adaln_mod",
    )(c, ada_w, ada_b.reshape(L, 1, N))


def _ffn_kernel(x_ref, mod_ref, g_ref, win_ref, wout_ref, o_ref, a_scr, *, sub, chunk):
    x = x_ref[0]
    shift = mod_ref[0, 3 * sub:3 * sub + 1, :]
    scale = mod_ref[0, 3 * sub + 1:3 * sub + 2, :]
    gate = mod_ref[0, 3 * sub + 2:3 * sub + 3, :]
    g_in = g_ref[2 * sub:2 * sub + 1, :]
    g_out = g_ref[2 * sub + 1:2 * sub + 2, :]
    h = (_rms(x) * g_in) * (1.0 + scale) + shift
    hb = h.astype(BF16)
    for j in range(D_FF // chunk):
        gj = _dot(hb, win_ref[:, j * chunk:(j + 1) * chunk])
        uj = _dot(hb, win_ref[:, D_FF + j * chunk:D_FF + (j + 1) * chunk])
        a_scr[:, j * chunk:(j + 1) * chunk] = (gj * _sigmoid(gj) * uj).astype(BF16)
    y = _dot(a_scr[...], wout_ref[...])
    o_ref[0] = x + (0.5 * gate) * (_rms(y) * g_out)


def _ffn_call(x, mod_l, g_l, w_in, w_out, sub, tm):
    B, S, D = x.shape
    kern = functools.partial(_ffn_kernel, sub=sub, chunk=256)
    return pl.pallas_call(
        kern,
        grid=(B, S // tm),
        in_specs=[pl.BlockSpec((1, tm, D), lambda b, i: (b, i, 0)),
                  pl.BlockSpec((1, N_MOD, D), lambda b, i: (b, 0, 0)),
                  _const_spec(g_l.shape),
                  _const_spec(w_in.shape),
                  _const_spec(w_out.shape)],
        out_specs=pl.BlockSpec((1, tm, D), lambda b, i: (b, i, 0)),
        out_shape=jax.ShapeDtypeStruct(x.shape, F32),
        scratch_shapes=[pltpu.VMEM((tm, D_FF), BF16)],
        compiler_params=_params("parallel", "parallel"),
        name=f"ffn{sub}",
    )(x, mod_l, g_l, w_in, w_out)


def _mixin_kernel(x_ref, mod_ref, g_ref, w1_ref, gq_ref, gkv_ref, wq_ref, wqr_ref, wkv_ref,
                  cq_ref, sq_ref, ck_ref, sk_ref, cs_ref,
                  q_ref, k_ref, v_ref, u_ref, zc_ref, p_ref, qq_ref):
    x = x_ref[0]
    shift = mod_ref[0, 3:4, :]
    scale = mod_ref[0, 4:5, :]
    h = (_rms(x) * g_ref[2:3, :]) * (1.0 + scale) + shift
    hb = h.astype(BF16)
    z = _dot(hb, w1_ref[...])

    cq = z[:, C_CQ:C_CQ + Q_PAD]
    cqn = cq * lax.rsqrt(jnp.sum(cq * cq, axis=-1, keepdims=True) * (1.0 / Q_LORA) + EPS)
    cqn = (cqn * gq_ref[...]).astype(BF16)
    qa = _dot(cqn, wq_ref[...])
    qb = _dot(cqn, wqr_ref[...])
    cosq = cq_ref[...]
    sinq = sq_ref[...]
    for hd in range(N_HEADS):
        sl = slice(hd * LANES, (hd + 1) * LANES)
        q_ref[0, hd] = (qa[:, sl] * cosq + qb[:, sl] * sinq).T.astype(BF16)

    ckv = z[:, C_CKV:C_CKV + KV_LORA]
    ckvn = (_rms(ckv) * gkv_ref[...]).astype(BF16)
    kv = _dot(ckvn, wkv_ref[...])
    k_rope = z[:, C_KRP:C_KRP + LANES] * ck_ref[...] + z[:, C_KRR:C_KRR + LANES] * sk_ref[...]
    lane = lax.broadcasted_iota(jnp.int32, (1, LANES), 1)
    ones_col = (lane == V_HEAD).astype(F32)
    for hd in range(N_HEADS):
        k_ref[0, hd] = (kv[:, hd * LANES:(hd + 1) * LANES] + k_rope).astype(BF16)
        vt = (kv[:, (N_HEADS + hd) * LANES:(N_HEADS + hd + 1) * LANES] + ones_col).T
        v_ref[0, hd] = vt[0:V_ROWS, :].astype(BF16)

    a = z[:, C_ZB:C_ZB + CONV_CH]
    gt = z[:, C_ZB + CONV_CH:C_ZB + 2 * CONV_CH]
    u_ref[0] = a * _sigmoid(gt)

    zc_ref[0] = z[:, C_ZC:C_ZC + W_BR]

    for g in range(N_GROUPS):
        zd = z[:, C_ZD + g * GC:C_ZD + (g + 1) * GC].astype(BF16)
        pq = _dot(zd, cs_ref[...])
        p_ref[0, :, g * GC:(g + 1) * GC] = pq[:, :GC]
        qq_ref[0, :, g * GC:(g + 1) * GC] = pq[:, GC:]


def _mixin_call(x, mod_l, g_l, lw, tabs, tm):
    B, S, D = x.shape
    tok = lambda w: pl.BlockSpec((1, tm, w), lambda b, i: (b, i, 0))
    head = pl.BlockSpec((1, N_HEADS, tm, LANES), lambda b, i: (b, 0, i, 0))
    head_t = pl.BlockSpec((1, N_HEADS, LANES, tm), lambda b, i: (b, 0, 0, i))
    tab = pl.BlockSpec((tm, LANES), lambda b, i: (i, 0))
    hshape = jax.ShapeDtypeStruct((B, N_HEADS, S, LANES), BF16)
    hshape_t = jax.ShapeDtypeStruct((B, N_HEADS, LANES, S), BF16)
    return pl.pallas_call(
        _mixin_kernel,
        grid=(B, S // tm),
        in_specs=[tok(D),
                  pl.BlockSpec((1, N_MOD, D), lambda b, i: (b, 0, 0)),
                  _const_spec(g_l.shape),
                  _const_spec(lw["w1"].shape),
                  _const_spec(lw["gq"].shape),
                  _const_spec(lw["gkv"].shape),
                  _const_spec(lw["wq"].shape),
                  _const_spec(lw["wqr"].shape),
                  _const_spec(lw["wkv"].shape),
                  tab, tab, tab, tab,
                  _const_spec(tabs["cs"].shape)],
        out_specs=[head_t, head, pl.BlockSpec((1, N_HEADS, V_ROWS, tm), lambda b, i: (b, 0, 0, i)),
                   tok(CONV_CH), tok(W_BR), tok(W_BR), tok(W_BR)],
        out_shape=[hshape_t, hshape, jax.ShapeDtypeStruct((B, N_HEADS, V_ROWS, S), BF16),
                   jax.ShapeDtypeStruct((B, S, CONV_CH), F32),
                   jax.ShapeDtypeStruct((B, S, W_BR), F32),
                   jax.ShapeDtypeStruct((B, S, W_BR), F32),
                   jax.ShapeDtypeStruct((B, S, W_BR), F32)],
        compiler_params=_params("parallel", "parallel"),
        name="mixer_in",
    )(x, mod_l, g_l, lw["w1"], lw["gq"], lw["gkv"], lw["wq"], lw["wqr"], lw["wkv"],
      tabs["cosq"], tabs["sinq"], tabs["cosk"], tabs["sink"], tabs["cs"])


def _attn_kernel(qt_ref, k_ref, vt_ref, o_ref, m_scr, acc_scr, st_scr, mx_scr, *, tq, tk):
    seq = k_ref.shape[2]
    nq, nk = seq // tq, seq // tk
    assert nk % 2 == 0

    def scores(qoff, j, slot):
        st = _dot(k_ref[0, 0, j * tk:(j + 1) * tk, :], qt_ref[0, 0, :, pl.ds(qoff, tq)])
        st_scr[slot] = st
        mx_scr[slot] = jnp.max(st, axis=0, keepdims=True)

    scores(0, 0, 0)

    def q_tile(qi, carry):
        qoff = pl.multiple_of(qi * tq, tq)
        qoff_next = pl.multiple_of(jnp.minimum(qi + 1, nq - 1) * tq, tq)
        m_scr[...] = jnp.full(m_scr.shape, -jnp.inf, F32)
        acc_scr[...] = jnp.zeros(acc_scr.shape, F32)
        for j in range(nk):
            slot = j % 2
            if j + 1 < nk:
                scores(qoff, j + 1, 1 - slot)
            else:
                scores(qoff_next, 0, 1 - slot)
            m_prev = m_scr[...]
            m_new = jnp.maximum(m_prev, mx_scr[slot])
            alpha = jnp.exp2(m_prev - m_new)
            pt = jnp.exp2(st_scr[slot] - m_new).astype(BF16)
            pv = _dot(vt_ref[0, 0, :, j * tk:(j + 1) * tk], pt)
            acc_scr[0:V_ROWS, :] = alpha * acc_scr[0:V_ROWS, :] + pv
            m_scr[...] = m_new
        acc = acc_scr[...]
        denom = acc[V_HEAD:V_HEAD + 1, :]
        o_ref[0, pl.ds(qoff, tq), :] = (acc / denom).T.astype(o_ref.dtype)
        return carry

    lax.fori_loop(0, nq, q_tile, 0)


def _attn_call(qt, k, vt, tq, tk):
    B, H, S, _ = k.shape
    kern = functools.partial(_attn_kernel, tq=tq, tk=tk)
    return pl.pallas_call(
        kern,
        grid=(B, H),
        in_specs=[pl.BlockSpec((1, 1, LANES, S), lambda b, h: (b, h, 0, 0)),
                  pl.BlockSpec((1, 1, S, LANES), lambda b, h: (b, h, 0, 0)),
                  pl.BlockSpec((1, 1, V_ROWS, S), lambda b, h: (b, h, 0, 0))],
        out_specs=pl.BlockSpec((1, S, LANES), lambda b, h: (b, 0, h)),
        out_shape=jax.ShapeDtypeStruct((B, S, H * LANES), BF16),
        scratch_shapes=[pltpu.VMEM((1, tq), F32), pltpu.VMEM((LANES, tq), F32),
                        pltpu.VMEM((2, tk, tq), F32), pltpu.VMEM((2, 1, tq), F32)],
        compiler_params=_params("parallel", "arbitrary"),
        name="attention",
    )(qt, k, vt)


def _local_kernel(u_ref, up_ref, un_ref, z_ref, zp_ref, zn_ref, cw_ref, cb_ref, lg_ref, lb_ref,
                  pw_ref, ps_ref, oc_ref, op_ref, ubuf, zbuf, ush, *, ts, seq, rc, rp):
    i = pl.program_id(1)
    has_prev = i > 0
    has_next = i < pl.num_programs(1) - 1
    ubuf[0:HALO, :] = jnp.where(has_prev, up_ref[0], 0.0)
    ubuf[HALO:HALO + ts, :] = u_ref[0]
    ubuf[HALO + ts:, :] = jnp.where(has_next, un_ref[0], 0.0)
    zbuf[0:HALO, :] = jnp.where(has_prev, zp_ref[0], 0.0)
    zbuf[HALO:HALO + ts, :] = z_ref[0]
    zbuf[HALO + ts:, :] = jnp.where(has_next, zn_ref[0], 0.0)

    span = ts + 2 * HALO - SUBLANES
    for s in range(1, SUBLANES):
        ush[s - 1, 0:span, :] = ubuf[s:s + span, :]

    pad = CONV_WIDTH // 2
    for r in range(0, ts, rc):
        acc = jnp.zeros((rc, CONV_CH), F32)
        for kk in range(CONV_WIDTH):
            st = HALO + r + kk - pad
            s, base = st % SUBLANES, st - st % SUBLANES
            tap = ubuf[base:base + rc, :] if s == 0 else ush[s - 1, base:base + rc, :]
            acc = acc + tap * cw_ref[kk:kk + 1, :]
        y = acc + cb_ref[...]
        mu = jnp.mean(y, axis=-1, keepdims=True)
        yc = y - mu
        yn = yc * lax.rsqrt(jnp.mean(yc * yc, axis=-1, keepdims=True) + EPS)
        yn = yn * lg_ref[...] + lb_ref[...]
        oc_ref[0, r:r + rc, :] = (yn * _sigmoid(yn)).astype(oc_ref.dtype)

    for r in range(0, ts, rp):
        t = i * ts + r + lax.broadcasted_iota(jnp.int32, (rp, 1), 0)
        for g, w in enumerate(POOL_WINDOWS):
            lo = w // 2
            hi = w - 1 - lo
            cols = slice(g * GC, (g + 1) * GC)
            win = jnp.zeros((rp, GC), F32)
            for j in range(-lo, hi + 1):
                st = HALO + r + j
                win = win + zbuf[st:st + rp, cols]
            cnt = (jnp.minimum(t + hi + 1, seq) - jnp.maximum(t - lo, 0)).astype(F32)
            d = win / cnt - zbuf[HALO + r:HALO + r + rp, cols]
            yg = _dot(d.astype(BF16), pw_ref[g]) * ps_ref[:, cols]
            op_ref[0, r:r + rp, cols] = yg.astype(op_ref.dtype)


def _local_call(u, zc, lw, ts):
    B, S, _ = u.shape
    nh = ts // HALO
    last = S // HALO - 1
    cur = lambda w: pl.BlockSpec((1, ts, w), lambda b, i: (b, i, 0))
    prev = lambda w: pl.BlockSpec((1, HALO, w), lambda b, i: (b, jnp.maximum(i * nh - 1, 0), 0))
    nxt = lambda w: pl.BlockSpec((1, HALO, w), lambda b, i: (b, jnp.minimum((i + 1) * nh, last), 0))
    kern = functools.partial(_local_kernel, ts=ts, seq=S, rc=64, rp=min(ts, 256))
    return pl.pallas_call(
        kern,
        grid=(B, S // ts),
        in_specs=[cur(CONV_CH), prev(CONV_CH), nxt(CONV_CH), cur(W_BR), prev(W_BR), nxt(W_BR),
                  _const_spec(lw["conv_w"].shape), _const_spec(lw["conv_b"].shape),
                  _const_spec(lw["ln_g"].shape), _const_spec(lw["ln_b"].shape),
                  _const_spec(lw["pool_w"].shape), _const_spec(lw["pool_scale"].shape)],
        out_specs=[cur(CONV_CH), cur(W_BR)],
        out_shape=[jax.ShapeDtypeStruct((B, S, CONV_CH), BF16),
                   jax.ShapeDtypeStruct((B, S, W_BR), BF16)],
        scratch_shapes=[pltpu.VMEM((ts + 2 * HALO, CONV_CH), F32),
                        pltpu.VMEM((ts + 2 * HALO, W_BR), F32),
                        pltpu.VMEM((SUBLANES - 1, ts + 2 * HALO, CONV_CH), F32)],
        compiler_params=_params("parallel", "parallel"),
        name="conv_pool",
    )(u, u, u, zc, zc, zc, lw["conv_w"], lw["conv_b"], lw["ln_g"], lw["ln_b"],
      lw["pool_w"], lw["pool_scale"])


def _fft1_kernel(p_ref, q_ref, f_ref, o_ref):
    for j in range(p_ref.shape[2]):
        rhs = jnp.concatenate([p_ref[0, :, j, :], q_ref[0, :, j, :]], axis=0).astype(BF16)
        o_ref[0, :, j, :] = _dot(f_ref[...], rhs)


def _fft1_call(p, q, tabs):
    B, S, W = p.shape
    n2 = S // FFT_N1
    pv = p.reshape(B, FFT_N1, n2, W)
    qv = q.reshape(B, FFT_N1, n2, W)
    blk = pl.BlockSpec((1, FFT_N1, SUBLANES, W), lambda b, j: (b, 0, j, 0))
    return pl.pallas_call(
        _fft1_kernel,
        grid=(B, n2 // SUBLANES),
        in_specs=[blk, blk, _const_spec(tabs["f1"].shape)],
        out_specs=pl.BlockSpec((1, 2 * FFT_N1, SUBLANES, W), lambda b, j: (b, 0, j, 0)),
        out_shape=jax.ShapeDtypeStruct((B, 2 * FFT_N1, n2, W), F32),
        compiler_params=_params("parallel", "parallel"),
        name="fft_stage1",
    )(pv, qv, tabs["f1"])


def _fft2_kernel(a_ref, gc_ref, gs_ref, o_ref, *, kb, w):
    for j in range(kb):
        f = _dot(gc_ref[j], a_ref[0, 0, j].astype(BF16)) + _dot(gs_ref[j], a_ref[0, 1, j].astype(BF16))
        o_ref[0, :, j * w:(j + 1) * w] = f.astype(o_ref.dtype)


def _fft2_call(a, tabs, seq, kb):
    B = a.shape[0]
    n2 = seq // FFT_N1
    av = a.reshape(B, 2, FFT_N1, n2, W_BR)
    kern = functools.partial(_fft2_kernel, kb=kb, w=W_BR)
    out = pl.pallas_call(
        kern,
        grid=(B, FFT_N1 // kb),
        in_specs=[pl.BlockSpec((1, 2, kb, n2, W_BR), lambda b, j: (b, 0, j, 0, 0)),
                  pl.BlockSpec((kb, n2, n2), lambda b, j: (j, 0, 0)),
                  pl.BlockSpec((kb, n2, n2), lambda b, j: (j, 0, 0))],
        out_specs=pl.BlockSpec((1, n2, kb * W_BR), lambda b, j: (b, 0, j)),
        out_shape=jax.ShapeDtypeStruct((B, n2, FFT_N1 * W_BR), BF16),
        compiler_params=_params("parallel", "parallel"),
        name="fft_stage2",
    )(av, tabs["g2c"], tabs["g2s"])
    return out.reshape(B, seq, W_BR)


def _mixout_kernel(x_ref, mod_ref, g_ref, wg_ref, oa_ref, wa_ref, cb_ref, wb_ref, pc_ref, wc_ref,
                   f_ref, wd_ref, wo_ref, o_ref):
    x = x_ref[0]
    shift = mod_ref[0, 3:4, :]
    scale = mod_ref[0, 4:5, :]
    gate = mod_ref[0, 5:6, :]
    h = (_rms(x) * g_ref[2:3, :]) * (1.0 + scale) + shift
    hb = h.astype(BF16)
    d = x.shape[-1]
    merged = None
    for br, (b_ref, w_ref) in enumerate(((oa_ref, wa_ref), (cb_ref, wb_ref), (pc_ref, wc_ref), (f_ref, wd_ref))):
        gl = _sigmoid(_dot(hb, wg_ref[:, br * d:(br + 1) * d]))
        term = gl * _dot(b_ref[0], w_ref[...])
        merged = term if merged is None else merged + term
    y = _dot(merged.astype(BF16), wo_ref[...])
    o_ref[0] = x + gate * (_rms(y) * g_ref[3:4, :])


def _mixout_call(x, mod_l, g_l, lw, oa, cb, pc, f, tm):
    B, S, D = x.shape
    tok = lambda w: pl.BlockSpec((1, tm, w), lambda b, i: (b, i, 0))
    return pl.pallas_call(
        _mixout_kernel,
        grid=(B, S // tm),
        in_specs=[tok(D),
                  pl.BlockSpec((1, N_MOD, D), lambda b, i: (b, 0, 0)),
                  _const_spec(g_l.shape),
                  _const_spec(lw["wg"].shape),
                  tok(N_HEADS * LANES), _const_spec(lw["wa"].shape),
                  tok(CONV_CH), _const_spec(lw["wb"].shape),
                  tok(W_BR), _const_spec(lw["wc"].shape),
                  tok(W_BR), _const_spec(lw["wd"].shape),
                  _const_spec(lw["wo"].shape)],
        out_specs=tok(D),
        out_shape=jax.ShapeDtypeStruct(x.shape, F32),
        compiler_params=_params("parallel", "parallel"),
        name="mixer_out",
    )(x, mod_l, g_l, lw["wg"], oa, lw["wa"], cb, lw["wb"], pc, lw["wc"], f, lw["wd"], lw["wo"])


def _rot_half_cols(w):
    half = w.shape[-1] // 2
    return jnp.concatenate([-w[..., half:], w[..., :half]], axis=-1)


def _tables(seq):
    pos = jnp.arange(seq, dtype=F32)
    inv = ROPE_THETA ** (-jnp.arange(0, QK_ROPE, 2, dtype=F32) / QK_ROPE)
    ang = pos[:, None] * inv[None, :]
    cos, sin = jnp.cos(ang), jnp.sin(ang)
    cc = jnp.concatenate([cos, cos], axis=-1)
    ss = jnp.concatenate([sin, sin], axis=-1)
    z64 = jnp.zeros((seq, QK_NOPE), F32)
    z32 = jnp.zeros((seq, LANES - QK_NOPE - QK_ROPE), F32)
    sm_scale = (QK_NOPE + QK_ROPE) ** -0.5 * math.log2(math.e)
    cosk = jnp.concatenate([z64, cc, z32], axis=-1)
    sink = jnp.concatenate([z64, ss, z32], axis=-1)
    cosq = jnp.concatenate([jnp.ones_like(z64), cc, z32], axis=-1) * sm_scale
    sinq = sink * sm_scale

    def cos_sin(m, period):
        th = m.astype(F32) * (2.0 * math.pi / period)
        return jnp.cos(th), jnp.sin(th)

    def dft(n):
        a = jnp.arange(n, dtype=jnp.int32)
        return cos_sin((a[:, None] * a[None, :]) % n, n)

    c_ch, s_ch = dft(GC)
    cs = (jnp.concatenate([c_ch, s_ch], axis=-1) * GC ** -0.5).astype(BF16)

    n1 = FFT_N1
    n2 = seq // n1
    c1, s1 = dft(n1)
    f1 = (jnp.concatenate([jnp.concatenate([c1, -s1], axis=1),
                           jnp.concatenate([-s1, -c1], axis=1)], axis=0) * n1 ** -0.5).astype(BF16)
    k1 = jnp.arange(n1, dtype=jnp.int32)[:, None, None]
    k2 = jnp.arange(n2, dtype=jnp.int32)[None, :, None]
    t2 = jnp.arange(n2, dtype=jnp.int32)[None, None, :]
    m = ((k1 + n1 * k2) * t2) % seq
    c2, s2 = cos_sin(m, seq)
    g2c = (c2 * n2 ** -0.5).astype(BF16)
    g2s = (s2 * n2 ** -0.5).astype(BF16)
    return dict(cosq=cosq, sinq=sinq, cosk=cosk, sink=sink, cs=cs, f1=f1, g2c=g2c, g2s=g2s)


def _layer_weights(l, w_in, q_norm_g, w_uq, kv_norm_g, w_ukv, w_a, conv_w, conv_b, conv_ln_g,
                   conv_ln_b, w_b, pool_w, pool_scale, w_c, w_d, w_out):
    D = w_in.shape[1]
    wi = w_in[l]
    zeros = lambda n: jnp.zeros((D, n), F32)
    kr = wi[:, Q_LORA + KV_LORA:W_BR]
    place = lambda w: jnp.concatenate([zeros(QK_NOPE), w, zeros(LANES - QK_NOPE - QK_ROPE)], axis=-1)
    w1 = jnp.concatenate([
        wi[:, Q_LORA:Q_LORA + KV_LORA],
        wi[:, :Q_LORA], zeros(Q_PAD - Q_LORA),
        place(kr), place(_rot_half_cols(kr)),
        wi[:, W_BR:4 * W_BR]], axis=-1).astype(BF16)

    uq = w_uq[l].reshape(Q_LORA, N_HEADS, QK_NOPE + QK_ROPE)
    nope, rope = uq[..., :QK_NOPE], uq[..., QK_NOPE:]
    zq = lambda n: jnp.zeros((Q_LORA, N_HEADS, n), F32)
    rows = ((0, Q_PAD - Q_LORA), (0, 0))
    wq = jnp.pad(jnp.concatenate([nope, rope, zq(LANES - QK_NOPE - QK_ROPE)], axis=-1)
                 .reshape(Q_LORA, N_HEADS * LANES), rows).astype(BF16)
    wqr = jnp.pad(jnp.concatenate([zq(QK_NOPE), _rot_half_cols(rope), zq(LANES - QK_NOPE - QK_ROPE)], axis=-1)
                  .reshape(Q_LORA, N_HEADS * LANES), rows).astype(BF16)

    ukv = w_ukv[l].reshape(KV_LORA, N_HEADS, QK_NOPE + V_HEAD)
    zk = jnp.zeros((KV_LORA, N_HEADS, LANES - QK_NOPE), F32)
    zv = jnp.zeros((KV_LORA, N_HEADS, LANES - V_HEAD), F32)
    wkv = jnp.concatenate([
        jnp.concatenate([ukv[..., :QK_NOPE], zk], axis=-1).reshape(KV_LORA, N_HEADS * LANES),
        jnp.concatenate([ukv[..., QK_NOPE:], zv], axis=-1).reshape(KV_LORA, N_HEADS * LANES)],
        axis=-1).astype(BF16)

    wa = jnp.pad(w_a[l].reshape(N_HEADS, V_HEAD, D), ((0, 0), (0, LANES - V_HEAD), (0, 0)))
    wa = wa.reshape(N_HEADS * LANES, D).astype(BF16)
    gq = jnp.pad(q_norm_g[l], (0, Q_PAD - Q_LORA)).reshape(1, Q_PAD)
    return dict(
        w1=w1, wq=wq, wqr=wqr, wkv=wkv, gq=gq, gkv=kv_norm_g[l].reshape(1, KV_LORA),
        wg=wi[:, 4 * W_BR:].astype(BF16), wa=wa,
        wb=w_b[l].astype(BF16), wc=w_c[l].astype(BF16), wd=w_d[l].astype(BF16), wo=w_out[l].astype(BF16),
        conv_w=conv_w[l], conv_b=conv_b[l].reshape(1, CONV_CH),
        ln_g=conv_ln_g[l].reshape(1, CONV_CH), ln_b=conv_ln_b[l].reshape(1, CONV_CH),
        pool_w=pool_w[l].astype(BF16), pool_scale=pool_scale[l].reshape(1, W_BR))


def _tiles(seq):
    t = lambda n: min(n, seq)
    return dict(ffn=t(512), mix=t(512), tq=t(1024), tk=t(1024), local=t(512))


def kernel(x, c, ada_w, ada_b, norm_g, ffn1_w_in, ffn1_w_out, ffn2_w_in, ffn2_w_out, w_in, q_norm_g,
           w_uq, kv_norm_g, w_ukv, w_a, conv_w, conv_b, conv_ln_g, conv_ln_b, w_b, pool_w, pool_scale,
           w_c, w_d, w_out):
    B, S, D = x.shape
    L = ada_w.shape[0]
    assert D == D_MODEL and S % (FFT_N1 * 16) == 0
    ts = _tiles(S)
    tabs = _tables(S)
    mod = _mod_call(c, ada_w, ada_b).reshape(L, B, N_MOD, D)
    for l in range(L):
        lw = _layer_weights(l, w_in, q_norm_g, w_uq, kv_norm_g, w_ukv, w_a, conv_w, conv_b, conv_ln_g,
                            conv_ln_b, w_b, pool_w, pool_scale, w_c, w_d, w_out)
        mod_l, g_l = mod[l], norm_g[l]
        x = _ffn_call(x, mod_l, g_l, ffn1_w_in[l].astype(BF16), ffn1_w_out[l].astype(BF16), 0, ts["ffn"])
        q, k, v, u, zc, p, qq = _mixin_call(x, mod_l, g_l, lw, tabs, ts["mix"])
        oa = _attn_call(q, k, v, ts["tq"], ts["tk"])
        cb, pc = _local_call(u, zc, lw, ts["local"])
        a = _fft1_call(p, qq, tabs)
        f = _fft2_call(a, tabs, S, 8)
        x = _mixout_call(x, mod_l, g_l, lw, oa, cb, pc, f, ts["mix"])
        x = _ffn_call(x, mod_l, g_l, ffn2_w_in[l].astype(BF16), ffn2_w_out[l].astype(BF16), 2, ts["ffn"])
    return x
```

```python
import functools
import math

import jax
import jax.numpy as jnp
import numpy as np
from jax import lax
from jax.experimental import pallas as pl
from jax.experimental.pallas import tpu as pltpu

F32 = jnp.float32
BF16 = jnp.bfloat16

D_MODEL = 1024
D_FF = 2816
N_HEADS = 8
Q_LORA = 352
KV_LORA = 128
QK_NOPE = 64
QK_ROPE = 32
V_HEAD = 64
V_ROWS = 80
ROPE_THETA = 10000.0
CONV_CH = 256
CONV_WIDTH = 31
POOL_WINDOWS = (2, 4, 8, 16)
GC = 128
N_GROUPS = 4
W_BR = 512
N_MOD = 9
EPS = 1e-6

LANES = 128
SUBLANES = 8
Q_PAD = 384
HALO = 16
FFT_N1 = 128
VMEM_LIMIT = 56 * 1024 * 1024

C_CKV = 0
C_CQ = C_CKV + KV_LORA
C_KRP = C_CQ + Q_PAD
C_KRR = C_KRP + LANES
C_ZB = C_KRR + LANES
C_ZC = C_ZB + W_BR
C_ZD = C_ZC + W_BR
W1_COLS = C_ZD + W_BR


def _params(*sem):
    return pltpu.CompilerParams(dimension_semantics=sem, vmem_limit_bytes=VMEM_LIMIT)


def _const_spec(shape):
    zeros = (0,) * len(shape)
    return pl.BlockSpec(shape, lambda *_: zeros, pipeline_mode=pl.Buffered(1))


def _dot(a, b):
    return jnp.dot(a, b, preferred_element_type=F32)


def _rms(x):
    return x * lax.rsqrt(jnp.mean(x * x, axis=-1, keepdims=True) + EPS)


def _sigmoid(x):
    return 1.0 / (1.0 + jnp.exp(-x))


def _mod_kernel(c_ref, w_ref, b_ref, o_ref):
    c = c_ref[...]
    ca = c * _sigmoid(c)
    o_ref[0] = jnp.dot(ca, w_ref[0], preferred_element_type=F32,
                       precision=lax.Precision.HIGHEST) + b_ref[0]


def _mod_call(c, ada_w, ada_b):
    L, D, N = ada_w.shape
    B = c.shape[0]
    tn = 1152
    return pl.pallas_call(
        _mod_kernel,
        grid=(L, N // tn),
        in_specs=[pl.BlockSpec((B, D), lambda l, j: (0, 0)),
                  pl.BlockSpec((1, D, tn), lambda l, j: (l, 0, j)),
                  pl.BlockSpec((1, 1, tn), lambda l, j: (l, 0, j))],
        out_specs=pl.BlockSpec((1, B, tn), lambda l, j: (l, 0, j)),
        out_shape=jax.ShapeDtypeStruct((L, B, N), F32),
        compiler_params=_params("parallel", "parallel"),
        name="adaln_mod",
    )(c, ada_w, ada_b.reshape(L, 1, N))


def _ffn_kernel(x_ref, mod_ref, g_ref, win_ref, wout_ref, o_ref, a_scr, *, sub, chunk):
    x = x_ref[0]
    shift = mod_ref[0, 3 * sub:3 * sub + 1, :]
    scale = mod_ref[0, 3 * sub + 1:3 * sub + 2, :]
    gate = mod_ref[0, 3 * sub + 2:3 * sub + 3, :]
    g_in = g_ref[2 * sub:2 * sub + 1, :]
    g_out = g_ref[2 * sub + 1:2 * sub + 2, :]
    h = (_rms(x) * g_in) * (1.0 + scale) + shift
    hb = h.astype(BF16)
    for j in range(D_FF // chunk):
        gj = _dot(hb, win_ref[:, j * chunk:(j + 1) * chunk])
        uj = _dot(hb, win_ref[:, D_FF + j * chunk:D_FF + (j + 1) * chunk])
        a_scr[:, j * chunk:(j + 1) * chunk] = (gj * _sigmoid(gj) * uj).astype(BF16)
    y = _dot(a_scr[...], wout_ref[...])
    o_ref[0] = x + (0.5 * gate) * (_rms(y) * g_out)


def _ffn_call(x, mod_l, g_l, w_in, w_out, sub, tm):
    B, S, D = x.shape
    kern = functools.partial(_ffn_kernel, sub=sub, chunk=256)
    return pl.pallas_call(
        kern,
        grid=(B, S // tm),
        in_specs=[pl.BlockSpec((1, tm, D), lambda b, i: (b, i, 0)),
                  pl.BlockSpec((1, N_MOD, D), lambda b, i: (b, 0, 0)),
                  _const_spec(g_l.shape),
                  _const_spec(w_in.shape),
                  _const_spec(w_out.shape)],
        out_specs=pl.BlockSpec((1, tm, D), lambda b, i: (b, i, 0)),
        out_shape=jax.ShapeDtypeStruct(x.shape, F32),
        scratch_shapes=[pltpu.VMEM((tm, D_FF), BF16)],
        compiler_params=_params("parallel", "parallel"),
        name=f"ffn{sub}",
    )(x, mod_l, g_l, w_in, w_out)


def _mixin_kernel(x_ref, mod_ref, g_ref, w1_ref, gq_ref, gkv_ref, wq_ref, wqr_ref, wkv_ref,
                  cq_ref, sq_ref, ck_ref, sk_ref, cs_ref,
                  q_ref, k_ref, v_ref, u_ref, zc_ref, p_ref, qq_ref):
    x = x_ref[0]
    shift = mod_ref[0, 3:4, :]
    scale = mod_ref[0, 4:5, :]
    h = (_rms(x) * g_ref[2:3, :]) * (1.0 + scale) + shift
    hb = h.astype(BF16)
    z = _dot(hb, w1_ref[...])

    cq = z[:, C_CQ:C_CQ + Q_PAD]
    cqn = cq * lax.rsqrt(jnp.sum(cq * cq, axis=-1, keepdims=True) * (1.0 / Q_LORA) + EPS)
    cqn = (cqn * gq_ref[...]).astype(BF16)
    qa = _dot(cqn, wq_ref[...])
    qb = _dot(cqn, wqr_ref[...])
    cosq = cq_ref[...]
    sinq = sq_ref[...]
    for hd in range(N_HEADS):
        sl = slice(hd * LANES, (hd + 1) * LANES)
        q_ref[0, hd] = (qa[:, sl] * cosq + qb[:, sl] * sinq).T.astype(BF16)

    ckv = z[:, C_CKV:C_CKV + KV_LORA]
    ckvn = (_rms(ckv) * gkv_ref[...]).astype(BF16)
    kv = _dot(ckvn, wkv_ref[...])
    k_rope = z[:, C_KRP:C_KRP + LANES] * ck_ref[...] + z[:, C_KRR:C_KRR + LANES] * sk_ref[...]
    lane = lax.broadcasted_iota(jnp.int32, (1, LANES), 1)
    ones_col = (lane == V_HEAD).astype(F32)
    for hd in range(N_HEADS):
        k_ref[0, hd] = (kv[:, hd * LANES:(hd + 1) * LANES] + k_rope).astype(BF16)
        vt = (kv[:, (N_HEADS + hd) * LANES:(N_HEADS + hd + 1) * LANES] + ones_col).T
        v_ref[0, hd] = vt[0:V_ROWS, :].astype(BF16)

    a = z[:, C_ZB:C_ZB + CONV_CH]
    gt = z[:, C_ZB + CONV_CH:C_ZB + 2 * CONV_CH]
    u_ref[0] = a * _sigmoid(gt)

    zc_ref[0] = z[:, C_ZC:C_ZC + W_BR]

    for g in range(N_GROUPS):
        zd = z[:, C_ZD + g * GC:C_ZD + (g + 1) * GC].astype(BF16)
        pq = _dot(zd, cs_ref[...])
        p_ref[0, :, g * GC:(g + 1) * GC] = pq[:, :GC]
        qq_ref[0, :, g * GC:(g + 1) * GC] = pq[:, GC:]


def _mixin_call(x, mod_l, g_l, lw, tabs, tm):
    B, S, D = x.shape
    tok = lambda w: pl.BlockSpec((1, tm, w), lambda b, i: (b, i, 0))
    head = pl.BlockSpec((1, N_HEADS, tm, LANES), lambda b, i: (b, 0, i, 0))
    head_t = pl.BlockSpec((1, N_HEADS, LANES, tm), lambda b, i: (b, 0, 0, i))
    tab = pl.BlockSpec((tm, LANES), lambda b, i: (i, 0))
    hshape = jax.ShapeDtypeStruct((B, N_HEADS, S, LANES), BF16)
    hshape_t = jax.ShapeDtypeStruct((B, N_HEADS, LANES, S), BF16)
    return pl.pallas_call(
        _mixin_kernel,
        grid=(B, S // tm),
        in_specs=[tok(D),
                  pl.BlockSpec((1, N_MOD, D), lambda b, i: (b, 0, 0)),
                  _const_spec(g_l.shape),
                  _const_spec(lw["w1"].shape),
                  _const_spec(lw["gq"].shape),
                  _const_spec(lw["gkv"].shape),
                  _const_spec(lw["wq"].shape),
                  _const_spec(lw["wqr"].shape),
                  _const_spec(lw["wkv"].shape),
                  tab, tab, tab, tab,
                  _const_spec(tabs["cs"].shape)],
        out_specs=[head_t, head, pl.BlockSpec((1, N_HEADS, V_ROWS, tm), lambda b, i: (b, 0, 0, i)),
                   tok(CONV_CH), tok(W_BR), tok(W_BR), tok(W_BR)],
        out_shape=[hshape_t, hshape, jax.ShapeDtypeStruct((B, N_HEADS, V_ROWS, S), BF16),
                   jax.ShapeDtypeStruct((B, S, CONV_CH), F32),
                   jax.ShapeDtypeStruct((B, S, W_BR), F32),
                   jax.ShapeDtypeStruct((B, S, W_BR), F32),
                   jax.ShapeDtypeStruct((B, S, W_BR), F32)],
        compiler_params=_params("parallel", "parallel"),
        name="mixer_in",
    )(x, mod_l, g_l, lw["w1"], lw["gq"], lw["gkv"], lw["wq"], lw["wqr"], lw["wkv"],
      tabs["cosq"], tabs["sinq"], tabs["cosk"], tabs["sink"], tabs["cs"])


def _attn_kernel(qt_ref, k_ref, vt_ref, o_ref, m_scr, acc_scr, st_scr, mx_scr, *, tq, tk):
    seq = k_ref.shape[2]
    nq, nk = seq // tq, seq // tk
    assert nk % 2 == 0

    def scores(qoff, j, slot):
        st = _dot(k_ref[0, 0, j * tk:(j + 1) * tk, :], qt_ref[0, 0, :, pl.ds(qoff, tq)])
        st_scr[slot] = st
        mx_scr[slot] = jnp.max(st, axis=0, keepdims=True)

    scores(0, 0, 0)

    def q_tile(qi, carry):
        qoff = pl.multiple_of(qi * tq, tq)
        qoff_next = pl.multiple_of(jnp.minimum(qi + 1, nq - 1) * tq, tq)
        m_scr[...] = jnp.full(m_scr.shape, -jnp.inf, F32)
        acc_scr[...] = jnp.zeros(acc_scr.shape, F32)
        for j in range(nk):
            slot = j % 2
            if j + 1 < nk:
                scores(qoff, j + 1, 1 - slot)
            else:
                scores(qoff_next, 0, 1 - slot)
            m_prev = m_scr[...]
            m_new = jnp.maximum(m_prev, mx_scr[slot])
            alpha = jnp.exp2(m_prev - m_new)
            pt = jnp.exp2(st_scr[slot] - m_new).astype(BF16)
            pv = _dot(vt_ref[0, 0, :, j * tk:(j + 1) * tk], pt)
            acc_scr[0:V_ROWS, :] = alpha * acc_scr[0:V_ROWS, :] + pv
            m_scr[...] = m_new
        acc = acc_scr[...]
        denom = acc[V_HEAD:V_HEAD + 1, :]
        o_ref[0, pl.ds(qoff, tq), :] = (acc / denom).T.astype(o_ref.dtype)
        return carry

    lax.fori_loop(0, nq, q_tile, 0)


def _attn_call(qt, k, vt, tq, tk):
    B, H, S, _ = k.shape
    kern = functools.partial(_attn_kernel, tq=tq, tk=tk)
    return pl.pallas_call(
        kern,
        grid=(B, H),
        in_specs=[pl.BlockSpec((1, 1, LANES, S), lambda b, h: (b, h, 0, 0)),
                  pl.BlockSpec((1, 1, S, LANES), lambda b, h: (b, h, 0, 0)),
                  pl.BlockSpec((1, 1, V_ROWS, S), lambda b, h: (b, h, 0, 0))],
        out_specs=pl.BlockSpec((1, S, LANES), lambda b, h: (b, 0, h)),
        out_shape=jax.ShapeDtypeStruct((B, S, H * LANES), BF16),
        scratch_shapes=[pltpu.VMEM((1, tq), F32), pltpu.VMEM((LANES, tq), F32),
                        pltpu.VMEM((2, tk, tq), F32), pltpu.VMEM((2, 1, tq), F32)],
        compiler_params=_params("parallel", "arbitrary"),
        name="attention",
    )(qt, k, vt)


def _local_kernel(u_ref, up_ref, un_ref, z_ref, zp_ref, zn_ref, cw_ref, cb_ref, lg_ref, lb_ref,
                  pw_ref, ps_ref, oc_ref, op_ref, ubuf, zbuf, ush, *, ts, seq, rc, rp):
    i = pl.program_id(1)
    has_prev = i > 0
    has_next = i < pl.num_programs(1) - 1
    ubuf[0:HALO, :] = jnp.where(has_prev, up_ref[0], 0.0)
    ubuf[HALO:HALO + ts, :] = u_ref[0]
    ubuf[HALO + ts:, :] = jnp.where(has_next, un_ref[0], 0.0)
    zbuf[0:HALO, :] = jnp.where(has_prev, zp_ref[0], 0.0)
    zbuf[HALO:HALO + ts, :] = z_ref[0]
    zbuf[HALO + ts:, :] = jnp.where(has_next, zn_ref[0], 0.0)

    span = ts + 2 * HALO - SUBLANES
    for s in range(1, SUBLANES):
        ush[s - 1, 0:span, :] = ubuf[s:s + span, :]

    pad = CONV_WIDTH // 2
    for r in range(0, ts, rc):
        acc = jnp.zeros((rc, CONV_CH), F32)
        for kk in range(CONV_WIDTH):
            st = HALO + r + kk - pad
            s, base = st % SUBLANES, st - st % SUBLANES
            tap = ubuf[base:base + rc, :] if s == 0 else ush[s - 1, base:base + rc, :]
            acc = acc + tap * cw_ref[kk:kk + 1, :]
        y = acc + cb_ref[...]
        mu = jnp.mean(y, axis=-1, keepdims=True)
        yc = y - mu
        yn = yc * lax.rsqrt(jnp.mean(yc * yc, axis=-1, keepdims=True) + EPS)
        yn = yn * lg_ref[...] + lb_ref[...]
        oc_ref[0, r:r + rc, :] = (yn * _sigmoid(yn)).astype(oc_ref.dtype)

    for r in range(0, ts, rp):
        t = i * ts + r + lax.broadcasted_iota(jnp.int32, (rp, 1), 0)
        for g, w in enumerate(POOL_WINDOWS):
            lo = w // 2
            hi = w - 1 - lo
            cols = slice(g * GC, (g + 1) * GC)
            win = jnp.zeros((rp, GC), F32)
            for j in range(-lo, hi + 1):
                st = HALO + r + j
                win = win + zbuf[st:st + rp, cols]
            cnt = (jnp.minimum(t + hi + 1, seq) - jnp.maximum(t - lo, 0)).astype(F32)
            d = win / cnt - zbuf[HALO + r:HALO + r + rp, cols]
            yg = _dot(d.astype(BF16), pw_ref[g]) * ps_ref[:, cols]
            op_ref[0, r:r + rp, cols] = yg.astype(op_ref.dtype)


def _local_call(u, zc, lw, ts):
    B, S, _ = u.shape
    nh = ts // HALO
    last = S // HALO - 1
    cur = lambda w: pl.BlockSpec((1, ts, w), lambda b, i: (b, i, 0))
    prev = lambda w: pl.BlockSpec((1, HALO, w), lambda b, i: (b, jnp.maximum(i * nh - 1, 0), 0))
    nxt = lambda w: pl.BlockSpec((1, HALO, w), lambda b, i: (b, jnp.minimum((i + 1) * nh, last), 0))
    kern = functools.partial(_local_kernel, ts=ts, seq=S, rc=64, rp=min(ts, 256))
    return pl.pallas_call(
        kern,
        grid=(B, S // ts),
        in_specs=[cur(CONV_CH), prev(CONV_CH), nxt(CONV_CH), cur(W_BR), prev(W_BR), nxt(W_BR),
                  _const_spec(lw["conv_w"].shape), _const_spec(lw["conv_b"].shape),
                  _const_spec(lw["ln_g"].shape), _const_spec(lw["ln_b"].shape),
                  _const_spec(lw["pool_w"].shape), _const_spec(lw["pool_scale"].shape)],
        out_specs=[cur(CONV_CH), cur(W_BR)],
        out_shape=[jax.ShapeDtypeStruct((B, S, CONV_CH), BF16),
                   jax.ShapeDtypeStruct((B, S, W_BR), BF16)],
        scratch_shapes=[pltpu.VMEM((ts + 2 * HALO, CONV_CH), F32),
                        pltpu.VMEM((ts + 2 * HALO, W_BR), F32),
                        pltpu.VMEM((SUBLANES - 1, ts + 2 * HALO, CONV_CH), F32)],
        compiler_params=_params("parallel", "parallel"),
        name="conv_pool",
    )(u, u, u, zc, zc, zc, lw["conv_w"], lw["conv_b"], lw["ln_g"], lw["ln_b"],
      lw["pool_w"], lw["pool_scale"])


def _fft1_kernel(p_ref, q_ref, f_ref, o_ref):
    for j in range(p_ref.shape[2]):
        rhs = jnp.concatenate([p_ref[0, :, j, :], q_ref[0, :, j, :]], axis=0).astype(BF16)
        o_ref[0, :, j, :] = _dot(f_ref[...], rhs)


def _fft1_call(p, q, tabs):
    B, S, W = p.shape
    n2 = S // FFT_N1
    pv = p.reshape(B, FFT_N1, n2, W)
    qv = q.reshape(B, FFT_N1, n2, W)
    blk = pl.BlockSpec((1, FFT_N1, SUBLANES, W), lambda b, j: (b, 0, j, 0))
    return pl.pallas_call(
        _fft1_kernel,
        grid=(B, n2 // SUBLANES),
        in_specs=[blk, blk, _const_spec(tabs["f1"].shape)],
        out_specs=pl.BlockSpec((1, 2 * FFT_N1, SUBLANES, W), lambda b, j: (b, 0, j, 0)),
        out_shape=jax.ShapeDtypeStruct((B, 2 * FFT_N1, n2, W), F32),
        compiler_params=_params("parallel", "parallel"),
        name="fft_stage1",
    )(pv, qv, tabs["f1"])


def _fft2_kernel(a_ref, gc_ref, gs_ref, o_ref, *, kb):
    for j in range(kb):
        f = _dot(gc_ref[j], a_ref[0, 0, j].astype(BF16)) + _dot(gs_ref[j], a_ref[0, 1, j].astype(BF16))
        o_ref[0, :, j, :] = f


def _fft2_call(a, tabs, seq, kb):
    B = a.shape[0]
    n2 = seq // FFT_N1
    av = a.reshape(B, 2, FFT_N1, n2, W_BR)
    kern = functools.partial(_fft2_kernel, kb=kb)
    out = pl.pallas_call(
        kern,
        grid=(B, FFT_N1 // kb),
        in_specs=[pl.BlockSpec((1, 2, kb, n2, W_BR), lambda b, j: (b, 0, j, 0, 0)),
                  pl.BlockSpec((kb, n2, n2), lambda b, j: (j, 0, 0)),
                  pl.BlockSpec((kb, n2, n2), lambda b, j: (j, 0, 0))],
        out_specs=pl.BlockSpec((1, n2, kb, W_BR), lambda b, j: (b, 0, j, 0)),
        out_shape=jax.ShapeDtypeStruct((B, n2, FFT_N1, W_BR), F32),
        compiler_params=_params("parallel", "parallel"),
        name="fft_stage2",
    )(av, tabs["g2c"], tabs["g2s"])
    return out.reshape(B, seq, W_BR)


def _mixout_kernel(x_ref, mod_ref, g_ref, wg_ref, oa_ref, wa_ref, cb_ref, wb_ref, pc_ref, wc_ref,
                   f_ref, wd_ref, wo_ref, o_ref):
    x = x_ref[0]
    shift = mod_ref[0, 3:4, :]
    scale = mod_ref[0, 4:5, :]
    gate = mod_ref[0, 5:6, :]
    h = (_rms(x) * g_ref[2:3, :]) * (1.0 + scale) + shift
    hb = h.astype(BF16)
    d = x.shape[-1]
    merged = None
    for br, (b_ref, w_ref) in enumerate(((oa_ref, wa_ref), (cb_ref, wb_ref), (pc_ref, wc_ref), (f_ref, wd_ref))):
        gl = _sigmoid(_dot(hb, wg_ref[:, br * d:(br + 1) * d]))
        term = gl * _dot(b_ref[0].astype(BF16), w_ref[...])
        merged = term if merged is None else merged + term
    y = _dot(merged.astype(BF16), wo_ref[...])
    o_ref[0] = x + gate * (_rms(y) * g_ref[3:4, :])


def _mixout_call(x, mod_l, g_l, lw, oa, cb, pc, f, tm):
    B, S, D = x.shape
    tok = lambda w: pl.BlockSpec((1, tm, w), lambda b, i: (b, i, 0))
    return pl.pallas_call(
        _mixout_kernel,
        grid=(B, S // tm),
        in_specs=[tok(D),
                  pl.BlockSpec((1, N_MOD, D), lambda b, i: (b, 0, 0)),
                  _const_spec(g_l.shape),
                  _const_spec(lw["wg"].shape),
                  tok(N_HEADS * LANES), _const_spec(lw["wa"].shape),
                  tok(CONV_CH), _const_spec(lw["wb"].shape),
                  tok(W_BR), _const_spec(lw["wc"].shape),
                  tok(W_BR), _const_spec(lw["wd"].shape),
                  _const_spec(lw["wo"].shape)],
        out_specs=tok(D),
        out_shape=jax.ShapeDtypeStruct(x.shape, F32),
        compiler_params=_params("parallel", "parallel"),
        name="mixer_out",
    )(x, mod_l, g_l, lw["wg"], oa, lw["wa"], cb, lw["wb"], pc, lw["wc"], f, lw["wd"], lw["wo"])


def _rot_half_cols(w):
    half = w.shape[-1] // 2
    return jnp.concatenate([-w[..., half:], w[..., :half]], axis=-1)


def _tables(seq):
    pos = jnp.arange(seq, dtype=F32)
    inv = ROPE_THETA ** (-jnp.arange(0, QK_ROPE, 2, dtype=F32) / QK_ROPE)
    ang = pos[:, None] * inv[None, :]
    cos, sin = jnp.cos(ang), jnp.sin(ang)
    cc = jnp.concatenate([cos, cos], axis=-1)
    ss = jnp.concatenate([sin, sin], axis=-1)
    z64 = jnp.zeros((seq, QK_NOPE), F32)
    z32 = jnp.zeros((seq, LANES - QK_NOPE - QK_ROPE), F32)
    sm_scale = (QK_NOPE + QK_ROPE) ** -0.5 * math.log2(math.e)
    cosk = jnp.concatenate([z64, cc, z32], axis=-1)
    sink = jnp.concatenate([z64, ss, z32], axis=-1)
    cosq = jnp.concatenate([jnp.ones_like(z64), cc, z32], axis=-1) * sm_scale
    sinq = sink * sm_scale

    def cos_sin(m, period):
        th = m.astype(F32) * (2.0 * math.pi / period)
        return jnp.cos(th), jnp.sin(th)

    def dft(n):
        a = jnp.arange(n, dtype=jnp.int32)
        return cos_sin((a[:, None] * a[None, :]) % n, n)

    c_ch, s_ch = dft(GC)
    cs = (jnp.concatenate([c_ch, s_ch], axis=-1) * GC ** -0.5).astype(BF16)

    n1 = FFT_N1
    n2 = seq // n1
    c1, s1 = dft(n1)
    f1 = (jnp.concatenate([jnp.concatenate([c1, -s1], axis=1),
                           jnp.concatenate([-s1, -c1], axis=1)], axis=0) * n1 ** -0.5).astype(BF16)
    k1 = jnp.arange(n1, dtype=jnp.int32)[:, None, None]
    k2 = jnp.arange(n2, dtype=jnp.int32)[None, :, None]
    t2 = jnp.arange(n2, dtype=jnp.int32)[None, None, :]
    m = ((k1 + n1 * k2) * t2) % seq
    c2, s2 = cos_sin(m, seq)
    g2c = (c2 * n2 ** -0.5).astype(BF16)
    g2s = (s2 * n2 ** -0.5).astype(BF16)
    return dict(cosq=cosq, sinq=sinq, cosk=cosk, sink=sink, cs=cs, f1=f1, g2c=g2c, g2s=g2s)


def _layer_weights(l, w_in, q_norm_g, w_uq, kv_norm_g, w_ukv, w_a, conv_w, conv_b, conv_ln_g,
                   conv_ln_b, w_b, pool_w, pool_scale, w_c, w_d, w_out):
    D = w_in.shape[1]
    wi = w_in[l]
    zeros = lambda n: jnp.zeros((D, n), F32)
    kr = wi[:, Q_LORA + KV_LORA:W_BR]
    place = lambda w: jnp.concatenate([zeros(QK_NOPE), w, zeros(LANES - QK_NOPE - QK_ROPE)], axis=-1)
    w1 = jnp.concatenate([
        wi[:, Q_LORA:Q_LORA + KV_LORA],
        wi[:, :Q_LORA], zeros(Q_PAD - Q_LORA),
        place(kr), place(_rot_half_cols(kr)),
        wi[:, W_BR:4 * W_BR]], axis=-1).astype(BF16)

    uq = w_uq[l].reshape(Q_LORA, N_HEADS, QK_NOPE + QK_ROPE)
    nope, rope = uq[..., :QK_NOPE], uq[..., QK_NOPE:]
    zq = lambda n: jnp.zeros((Q_LORA, N_HEADS, n), F32)
    rows = ((0, Q_PAD - Q_LORA), (0, 0))
    wq = jnp.pad(jnp.concatenate([nope, rope, zq(LANES - QK_NOPE - QK_ROPE)], axis=-1)
                 .reshape(Q_LORA, N_HEADS * LANES), rows).astype(BF16)
    wqr = jnp.pad(jnp.concatenate([zq(QK_NOPE), _rot_half_cols(rope), zq(LANES - QK_NOPE - QK_ROPE)], axis=-1)
                  .reshape(Q_LORA, N_HEADS * LANES), rows).astype(BF16)

    ukv = w_ukv[l].reshape(KV_LORA, N_HEADS, QK_NOPE + V_HEAD)
    zk = jnp.zeros((KV_LORA, N_HEADS, LANES - QK_NOPE), F32)
    zv = jnp.zeros((KV_LORA, N_HEADS, LANES - V_HEAD), F32)
    wkv = jnp.concatenate([
        jnp.concatenate([ukv[..., :QK_NOPE], zk], axis=-1).reshape(KV_LORA, N_HEADS * LANES),
        jnp.concatenate([ukv[..., QK_NOPE:], zv], axis=-1).reshape(KV_LORA, N_HEADS * LANES)],
        axis=-1).astype(BF16)

    wa = jnp.pad(w_a[l].reshape(N_HEADS, V_HEAD, D), ((0, 0), (0, LANES - V_HEAD), (0, 0)))
    wa = wa.reshape(N_HEADS * LANES, D).astype(BF16)
    gq = jnp.pad(q_norm_g[l], (0, Q_PAD - Q_LORA)).reshape(1, Q_PAD)
    return dict(
        w1=w1, wq=wq, wqr=wqr, wkv=wkv, gq=gq, gkv=kv_norm_g[l].reshape(1, KV_LORA),
        wg=wi[:, 4 * W_BR:].astype(BF16), wa=wa,
        wb=w_b[l].astype(BF16), wc=w_c[l].astype(BF16), wd=w_d[l].astype(BF16), wo=w_out[l].astype(BF16),
        conv_w=conv_w[l], conv_b=conv_b[l].reshape(1, CONV_CH),
        ln_g=conv_ln_g[l].reshape(1, CONV_CH), ln_b=conv_ln_b[l].reshape(1, CONV_CH),
        pool_w=pool_w[l].astype(BF16), pool_scale=pool_scale[l].reshape(1, W_BR))


def _tiles(seq):
    t = lambda n: min(n, seq)
    return dict(ffn=t(512), mix=t(512), tq=t(1024), tk=t(1024), local=t(512))


def kernel(x, c, ada_w, ada_b, norm_g, ffn1_w_in, ffn1_w_out, ffn2_w_in, ffn2_w_out, w_in, q_norm_g,
           w_uq, kv_norm_g, w_ukv, w_a, conv_w, conv_b, conv_ln_g, conv_ln_b, w_b, pool_w, pool_scale,
           w_c, w_d, w_out):
    B, S, D = x.shape
    L = ada_w.shape[0]
    assert D == D_MODEL and S % (FFT_N1 * 16) == 0
    ts = _tiles(S)
    tabs = _tables(S)
    mod = _mod_call(c, ada_w, ada_b).reshape(L, B, N_MOD, D)
    for l in range(L):
        lw = _layer_weights(l, w_in, q_norm_g, w_uq, kv_norm_g, w_ukv, w_a, conv_w, conv_b, conv_ln_g,
                            conv_ln_b, w_b, pool_w, pool_scale, w_c, w_d, w_out)
        mod_l, g_l = mod[l], norm_g[l]
        x = _ffn_call(x, mod_l, g_l, ffn1_w_in[l].astype(BF16), ffn1_w_out[l].astype(BF16), 0, ts["ffn"])
        q, k, v, u, zc, p, qq = _mixin_call(x, mod_l, g_l, lw, tabs, ts["mix"])
        oa = _attn_call(q, k, v, ts["tq"], ts["tk"])
        cb, pc = _local_call(u, zc, lw, ts["local"])
        a = _fft1_call(p, qq, tabs)
        f = _fft2_call(a, tabs, S, 8)
        x = _mixout_call(x, mod_l, g_l, lw, oa, cb, pc, f, ts["mix"])
        x = _ffn_call(x, mod_l, g_l, ffn2_w_in[l].astype(BF16), ffn2_w_out[l].astype(BF16), 2, ts["ffn"])
    return x
```

```python
import functools
import math

import jax
import jax.numpy as jnp
import numpy as np
from jax import lax
from jax.experimental import pallas as pl
from jax.experimental.pallas import tpu as pltpu

F32 = jnp.float32
BF16 = jnp.bfloat16

D_MODEL = 1024
D_FF = 2816
N_HEADS = 8
Q_LORA = 352
KV_LORA = 128
QK_NOPE = 64
QK_ROPE = 32
V_HEAD = 64
V_ROWS = 80
ROPE_THETA = 10000.0
CONV_CH = 256
CONV_WIDTH = 31
POOL_WINDOWS = (2, 4, 8, 16)
GC = 128
N_GROUPS = 4
W_BR = 512
N_MOD = 9
EPS = 1e-6

LANES = 128
SUBLANES = 8
Q_PAD = 384
HALO = 16
FFT_N1 = 128
VMEM_LIMIT = 56 * 1024 * 1024

C_CKV = 0
C_CQ = C_CKV + KV_LORA
C_ZB = C_CQ + Q_PAD
C_ZC = C_ZB + W_BR
C_ZD = C_ZC + W_BR
W1_COLS = C_ZD + W_BR


def _params(*sem):
    return pltpu.CompilerParams(dimension_semantics=sem, vmem_limit_bytes=VMEM_LIMIT)


def _const_spec(shape):
    zeros = (0,) * len(shape)
    return pl.BlockSpec(shape, lambda *_: zeros, pipeline_mode=pl.Buffered(1))


def _dot(a, b):
    return jnp.dot(a, b, preferred_element_type=F32)


def _rms(x):
    return x * lax.rsqrt(jnp.mean(x * x, axis=-1, keepdims=True) + EPS)


def _sigmoid(x):
    return 1.0 / (1.0 + jnp.exp(-x))


def _mod_kernel(c_ref, w_ref, b_ref, o_ref):
    c = c_ref[...]
    ca = c * _sigmoid(c)
    o_ref[0] = jnp.dot(ca, w_ref[0], preferred_element_type=F32,
                       precision=lax.Precision.HIGHEST) + b_ref[0]


def _mod_call(c, ada_w, ada_b):
    L, D, N = ada_w.shape
    B = c.shape[0]
    tn = 1152
    return pl.pallas_call(
        _mod_kernel,
        grid=(L, N // tn),
        in_specs=[pl.BlockSpec((B, D), lambda l, j: (0, 0)),
                  pl.BlockSpec((1, D, tn), lambda l, j: (l, 0, j)),
                  pl.BlockSpec((1, 1, tn), lambda l, j: (l, 0, j))],
        out_specs=pl.BlockSpec((1, B, tn), lambda l, j: (l, 0, j)),
        out_shape=jax.ShapeDtypeStruct((L, B, N), F32),
        compiler_params=_params("parallel", "parallel"),
        name="adaln_mod",
    )(c, ada_w, ada_b.reshape(L, 1, N))


def _ffn_kernel(x_ref, mod_ref, g_ref, win_ref, wout_ref, o_ref, a_scr, *, sub, chunk):
    x = x_ref[0]
    shift = mod_ref[0, 3 * sub:3 * sub + 1, :]
    scale = mod_ref[0, 3 * sub + 1:3 * sub + 2, :]
    gate = mod_ref[0, 3 * sub + 2:3 * sub + 3, :]
    g_in = g_ref[2 * sub:2 * sub + 1, :]
    g_out = g_ref[2 * sub + 1:2 * sub + 2, :]
    h = (_rms(x) * g_in) * (1.0 + scale) + shift
    hb = h.astype(BF16)
    for j in range(D_FF // chunk):
        gj = _dot(hb, win_ref[:, j * chunk:(j + 1) * chunk])
        uj = _dot(hb, win_ref[:, D_FF + j * chunk:D_FF + (j + 1) * chunk])
        a_scr[:, j * chunk:(j + 1) * chunk] = (gj * _sigmoid(gj) * uj).astype(BF16)
    y = _dot(a_scr[...], wout_ref[...])
    o_ref[0] = x + (0.5 * gate) * (_rms(y) * g_out)


def _ffn_call(x, mod_l, g_l, w_in, w_out, sub, tm):
    B, S, D = x.shape
    kern = functools.partial(_ffn_kernel, sub=sub, chunk=256)
    return pl.pallas_call(
        kern,
        grid=(B, S // tm),
        in_specs=[pl.BlockSpec((1, tm, D), lambda b, i: (b, i, 0)),
                  pl.BlockSpec((1, N_MOD, D), lambda b, i: (b, 0, 0)),
                  _const_spec(g_l.shape),
                  _const_spec(w_in.shape),
                  _const_spec(w_out.shape)],
        out_specs=pl.BlockSpec((1, tm, D), lambda b, i: (b, i, 0)),
        out_shape=jax.ShapeDtypeStruct(x.shape, F32),
        scratch_shapes=[pltpu.VMEM((tm, D_FF), BF16)],
        compiler_params=_params("parallel", "parallel"),
        name=f"ffn{sub}",
    )(x, mod_l, g_l, w_in, w_out)


def _mixin_kernel(x_ref, mod_ref, g_ref, w1_ref, gq_ref, gkv_ref, wq_ref, wk_ref, wv_ref,
                  cr_ref, sr_ref, ck_ref, sk_ref, cs_ref,
                  q_ref, k_ref, v_ref, u_ref, zc_ref, p_ref, qq_ref, *, q_scale):
    x = x_ref[0]
    tm = x.shape[0]
    shift = mod_ref[0, 3:4, :]
    scale = mod_ref[0, 4:5, :]
    h = (_rms(x) * g_ref[2:3, :]) * (1.0 + scale) + shift
    hb = h.astype(BF16)
    z = _dot(hb, w1_ref[...])

    cq = z[:, C_CQ:C_CQ + Q_PAD]
    qlane = lax.broadcasted_iota(jnp.int32, (1, Q_PAD), 1)
    cqm = jnp.where(qlane < Q_LORA, cq, 0.0)
    cqn = cqm * lax.rsqrt(jnp.sum(cqm * cqm, axis=-1, keepdims=True) * (1.0 / Q_LORA) + EPS)
    cqn = (cqn * gq_ref[...]).astype(BF16)
    qq = _dot(cqn, wq_ref[...])
    nope = qq[:, :N_HEADS * QK_NOPE] * q_scale
    w_rope = N_HEADS * QK_ROPE
    roped = (qq[:, N_HEADS * QK_NOPE:N_HEADS * QK_NOPE + w_rope] * cr_ref[...]
             + qq[:, N_HEADS * QK_NOPE + w_rope:] * sr_ref[...])
    for blk in range(N_HEADS * QK_NOPE // LANES):
        nt = nope[:, blk * LANES:(blk + 1) * LANES].T.astype(BF16)
        for i in range(LANES // QK_NOPE):
            q_ref[0, blk * (LANES // QK_NOPE) + i, 0:QK_NOPE, :] = nt[i * QK_NOPE:(i + 1) * QK_NOPE]
    for blk in range(w_rope // LANES):
        rt = roped[:, blk * LANES:(blk + 1) * LANES].T.astype(BF16)
        for i in range(LANES // QK_ROPE):
            q_ref[0, blk * (LANES // QK_ROPE) + i, QK_NOPE:QK_NOPE + QK_ROPE, :] = rt[i * QK_ROPE:(i + 1) * QK_ROPE]
    zero_rows = jnp.zeros((LANES - QK_NOPE - QK_ROPE, tm), BF16)
    for hd in range(N_HEADS):
        q_ref[0, hd, QK_NOPE + QK_ROPE:, :] = zero_rows

    kr_at = Q_LORA - (Q_PAD - LANES)
    kr = pltpu.roll(cq[:, Q_PAD - LANES:], (QK_NOPE - kr_at) % LANES, 1)
    klane = lax.broadcasted_iota(jnp.int32, (1, LANES), 1)
    kr_rot = jnp.where(klane < QK_NOPE + QK_ROPE // 2,
                       -pltpu.roll(kr, LANES - QK_ROPE // 2, 1), pltpu.roll(kr, QK_ROPE // 2, 1))
    k_rope = kr * ck_ref[...] + kr_rot * sk_ref[...]

    ckv = z[:, C_CKV:C_CKV + KV_LORA]
    ckvn = (_rms(ckv) * gkv_ref[...]).astype(BF16)
    kk = _dot(ckvn, wk_ref[...])
    for hd in range(N_HEADS):
        k_ref[0, hd] = (kk[:, hd * LANES:(hd + 1) * LANES] + k_rope).astype(BF16)
    vv = _dot(ckvn, wv_ref[...])
    ones_rows = (lax.broadcasted_iota(jnp.int32, (V_ROWS - V_HEAD, tm), 0) == 0).astype(BF16)
    for blk in range(N_HEADS * V_HEAD // LANES):
        vt = vv[:, blk * LANES:(blk + 1) * LANES].T.astype(BF16)
        for i in range(LANES // V_HEAD):
            v_ref[0, blk * (LANES // V_HEAD) + i, 0:V_HEAD, :] = vt[i * V_HEAD:(i + 1) * V_HEAD]
    for hd in range(N_HEADS):
        v_ref[0, hd, V_HEAD:, :] = ones_rows

    a = z[:, C_ZB:C_ZB + CONV_CH]
    gt = z[:, C_ZB + CONV_CH:C_ZB + 2 * CONV_CH]
    u_ref[0] = a * _sigmoid(gt)

    zc_ref[0] = z[:, C_ZC:C_ZC + W_BR]

    for g in range(N_GROUPS):
        zd = z[:, C_ZD + g * GC:C_ZD + (g + 1) * GC].astype(BF16)
        pq = _dot(zd, cs_ref[...])
        p_ref[0, :, g * GC:(g + 1) * GC] = pq[:, :GC]
        qq_ref[0, :, g * GC:(g + 1) * GC] = pq[:, GC:]


def _mixin_call(x, mod_l, g_l, lw, tabs, tm):
    B, S, D = x.shape
    tok = lambda w: pl.BlockSpec((1, tm, w), lambda b, i: (b, i, 0))
    head = pl.BlockSpec((1, N_HEADS, tm, LANES), lambda b, i: (b, 0, i, 0))
    head_t = pl.BlockSpec((1, N_HEADS, LANES, tm), lambda b, i: (b, 0, 0, i))
    tab = lambda w: pl.BlockSpec((tm, w), lambda b, i: (i, 0))
    hshape = jax.ShapeDtypeStruct((B, N_HEADS, S, LANES), BF16)
    hshape_t = jax.ShapeDtypeStruct((B, N_HEADS, LANES, S), BF16)
    return pl.pallas_call(
        functools.partial(_mixin_kernel, q_scale=tabs["q_scale"]),
        grid=(B, S // tm),
        in_specs=[tok(D),
                  pl.BlockSpec((1, N_MOD, D), lambda b, i: (b, 0, 0)),
                  _const_spec(g_l.shape),
                  _const_spec(lw["w1"].shape),
                  _const_spec(lw["gq"].shape),
                  _const_spec(lw["gkv"].shape),
                  _const_spec(lw["wq"].shape),
                  _const_spec(lw["wk"].shape),
                  _const_spec(lw["wv"].shape),
                  tab(N_HEADS * QK_ROPE), tab(N_HEADS * QK_ROPE), tab(LANES), tab(LANES),
                  _const_spec(tabs["cs"].shape)],
        out_specs=[head_t, head, pl.BlockSpec((1, N_HEADS, V_ROWS, tm), lambda b, i: (b, 0, 0, i)),
                   tok(CONV_CH), tok(W_BR), tok(W_BR), tok(W_BR)],
        out_shape=[hshape_t, hshape, jax.ShapeDtypeStruct((B, N_HEADS, V_ROWS, S), BF16),
                   jax.ShapeDtypeStruct((B, S, CONV_CH), F32),
                   jax.ShapeDtypeStruct((B, S, W_BR), F32),
                   jax.ShapeDtypeStruct((B, S, W_BR), F32),
                   jax.ShapeDtypeStruct((B, S, W_BR), F32)],
        compiler_params=_params("parallel", "parallel"),
        name="mixer_in",
    )(x, mod_l, g_l, lw["w1"], lw["gq"], lw["gkv"], lw["wq"], lw["wk"], lw["wv"],
      tabs["cosr"], tabs["sinr"], tabs["cosk"], tabs["sink"], tabs["cs"])


def _attn_kernel(qt_ref, k_ref, vt_ref, o_ref, m_scr, acc_scr, st_scr, mx_scr, *, tq, tk):
    seq = k_ref.shape[2]
    nq, nk = seq // tq, seq // tk
    assert nk % 2 == 0

    def scores(qoff, j, slot):
        st = _dot(k_ref[0, 0, j * tk:(j + 1) * tk, :], qt_ref[0, 0, :, pl.ds(qoff, tq)])
        st_scr[slot] = st
        mx_scr[slot] = jnp.max(st, axis=0, keepdims=True)

    scores(0, 0, 0)

    def q_tile(qi, carry):
        qoff = pl.multiple_of(qi * tq, tq)
        qoff_next = pl.multiple_of(jnp.minimum(qi + 1, nq - 1) * tq, tq)
        m_scr[...] = jnp.full(m_scr.shape, -jnp.inf, F32)
        acc_scr[...] = jnp.zeros(acc_scr.shape, F32)
        for j in range(nk):
            slot = j % 2
            if j + 1 < nk:
                scores(qoff, j + 1, 1 - slot)
            else:
                scores(qoff_next, 0, 1 - slot)
            m_prev = m_scr[...]
            m_new = jnp.maximum(m_prev, mx_scr[slot])
            alpha = jnp.exp2(m_prev - m_new)
            pt = jnp.exp2(st_scr[slot] - m_new).astype(BF16)
            pv = _dot(vt_ref[0, 0, :, j * tk:(j + 1) * tk], pt)
            acc_scr[0:V_ROWS, :] = alpha * acc_scr[0:V_ROWS, :] + pv
            m_scr[...] = m_new
        acc = acc_scr[...]
        denom = acc[V_HEAD:V_HEAD + 1, :]
        o_ref[0, pl.ds(qoff, tq), :] = (acc / denom).T.astype(o_ref.dtype)
        return carry

    lax.fori_loop(0, nq, q_tile, 0)


def _attn_call(qt, k, vt, tq, tk):
    B, H, S, _ = k.shape
    kern = functools.partial(_attn_kernel, tq=tq, tk=tk)
    return pl.pallas_call(
        kern,
        grid=(B, H),
        in_specs=[pl.BlockSpec((1, 1, LANES, S), lambda b, h: (b, h, 0, 0)),
                  pl.BlockSpec((1, 1, S, LANES), lambda b, h: (b, h, 0, 0)),
                  pl.BlockSpec((1, 1, V_ROWS, S), lambda b, h: (b, h, 0, 0))],
        out_specs=pl.BlockSpec((1, S, LANES), lambda b, h: (b, 0, h)),
        out_shape=jax.ShapeDtypeStruct((B, S, H * LANES), BF16),
        scratch_shapes=[pltpu.VMEM((1, tq), F32), pltpu.VMEM((LANES, tq), F32),
                        pltpu.VMEM((2, tk, tq), F32), pltpu.VMEM((2, 1, tq), F32)],
        compiler_params=_params("parallel", "arbitrary"),
        name="attention",
    )(qt, k, vt)


def _local_kernel(u_ref, up_ref, un_ref, z_ref, zp_ref, zn_ref, cw_ref, cb_ref, lg_ref, lb_ref,
                  pw_ref, ps_ref, oc_ref, op_ref, ubuf, zbuf, ush, *, ts, seq, rc, rp):
    i = pl.program_id(1)
    has_prev = i > 0
    has_next = i < pl.num_programs(1) - 1
    ubuf[0:HALO, :] = jnp.where(has_prev, up_ref[0], 0.0)
    ubuf[HALO:HALO + ts, :] = u_ref[0]
    ubuf[HALO + ts:, :] = jnp.where(has_next, un_ref[0], 0.0)
    zbuf[0:HALO, :] = jnp.where(has_prev, zp_ref[0], 0.0)
    zbuf[HALO:HALO + ts, :] = z_ref[0]
    zbuf[HALO + ts:, :] = jnp.where(has_next, zn_ref[0], 0.0)

    span = ts + 2 * HALO - SUBLANES
    for s in range(1, SUBLANES):
        ush[s - 1, 0:span, :] = ubuf[s:s + span, :]

    pad = CONV_WIDTH // 2
    for r in range(0, ts, rc):
        acc = jnp.zeros((rc, CONV_CH), F32)
        for kk in range(CONV_WIDTH):
            st = HALO + r + kk - pad
            s, base = st % SUBLANES, st - st % SUBLANES
            tap = ubuf[base:base + rc, :] if s == 0 else ush[s - 1, base:base + rc, :]
            acc = acc + tap * cw_ref[kk:kk + 1, :]
        y = acc + cb_ref[...]
        mu = jnp.mean(y, axis=-1, keepdims=True)
        yc = y - mu
        yn = yc * lax.rsqrt(jnp.mean(yc * yc, axis=-1, keepdims=True) + EPS)
        yn = yn * lg_ref[...] + lb_ref[...]
        oc_ref[0, r:r + rc, :] = (yn * _sigmoid(yn)).astype(oc_ref.dtype)

    for r in range(0, ts, rp):
        t = i * ts + r + lax.broadcasted_iota(jnp.int32, (rp, 1), 0)
        for g, w in enumerate(POOL_WINDOWS):
            lo = w // 2
            hi = w - 1 - lo
            cols = slice(g * GC, (g + 1) * GC)
            win = jnp.zeros((rp, GC), F32)
            for j in range(-lo, hi + 1):
                st = HALO + r + j
                win = win + zbuf[st:st + rp, cols]
            cnt = (jnp.minimum(t + hi + 1, seq) - jnp.maximum(t - lo, 0)).astype(F32)
            d = win / cnt - zbuf[HALO + r:HALO + r + rp, cols]
            yg = _dot(d.astype(BF16), pw_ref[g]) * ps_ref[:, cols]
            op_ref[0, r:r + rp, cols] = yg.astype(op_ref.dtype)


def _local_call(u, zc, lw, ts):
    B, S, _ = u.shape
    nh = ts // HALO
    last = S // HALO - 1
    cur = lambda w: pl.BlockSpec((1, ts, w), lambda b, i: (b, i, 0))
    prev = lambda w: pl.BlockSpec((1, HALO, w), lambda b, i: (b, jnp.maximum(i * nh - 1, 0), 0))
    nxt = lambda w: pl.BlockSpec((1, HALO, w), lambda b, i: (b, jnp.minimum((i + 1) * nh, last), 0))
    kern = functools.partial(_local_kernel, ts=ts, seq=S, rc=64, rp=min(ts, 256))
    return pl.pallas_call(
        kern,
        grid=(B, S // ts),
        in_specs=[cur(CONV_CH), prev(CONV_CH), nxt(CONV_CH), cur(W_BR), prev(W_BR), nxt(W_BR),
                  _const_spec(lw["conv_w"].shape), _const_spec(lw["conv_b"].shape),
                  _const_spec(lw["ln_g"].shape), _const_spec(lw["ln_b"].shape),
                  _const_spec(lw["pool_w"].shape), _const_spec(lw["pool_scale"].shape)],
        out_specs=[cur(CONV_CH), cur(W_BR)],
        out_shape=[jax.ShapeDtypeStruct((B, S, CONV_CH), BF16),
                   jax.ShapeDtypeStruct((B, S, W_BR), BF16)],
        scratch_shapes=[pltpu.VMEM((ts + 2 * HALO, CONV_CH), F32),
                        pltpu.VMEM((ts + 2 * HALO, W_BR), F32),
                        pltpu.VMEM((SUBLANES - 1, ts + 2 * HALO, CONV_CH), F32)],
        compiler_params=_params("parallel", "parallel"),
        name="conv_pool",
    )(u, u, u, zc, zc, zc, lw["conv_w"], lw["conv_b"], lw["ln_g"], lw["ln_b"],
      lw["pool_w"], lw["pool_scale"])


def _fft1_kernel(p_ref, q_ref, f_ref, o_ref):
    for j in range(p_ref.shape[2]):
        rhs = jnp.concatenate([p_ref[0, :, j, :], q_ref[0, :, j, :]], axis=0).astype(BF16)
        o_ref[0, :, j, :] = _dot(f_ref[...], rhs)


def _fft1_call(p, q, tabs):
    B, S, W = p.shape
    n2 = S // FFT_N1
    pv = p.reshape(B, FFT_N1, n2, W)
    qv = q.reshape(B, FFT_N1, n2, W)
    blk = pl.BlockSpec((1, FFT_N1, SUBLANES, W), lambda b, j: (b, 0, j, 0))
    return pl.pallas_call(
        _fft1_kernel,
        grid=(B, n2 // SUBLANES),
        in_specs=[blk, blk, _const_spec(tabs["f1"].shape)],
        out_specs=pl.BlockSpec((1, 2 * FFT_N1, SUBLANES, W), lambda b, j: (b, 0, j, 0)),
        out_shape=jax.ShapeDtypeStruct((B, 2 * FFT_N1, n2, W), F32),
        compiler_params=_params("parallel", "parallel"),
        name="fft_stage1",
    )(pv, qv, tabs["f1"])


def _fft2_kernel(a_ref, gc_ref, gs_ref, o_ref, *, kb):
    for j in range(kb):
        f = _dot(gc_ref[j], a_ref[0, 0, j].astype(BF16)) + _dot(gs_ref[j], a_ref[0, 1, j].astype(BF16))
        o_ref[0, :, j, :] = f


def _fft2_call(a, tabs, seq, kb):
    B = a.shape[0]
    n2 = seq // FFT_N1
    av = a.reshape(B, 2, FFT_N1, n2, W_BR)
    kern = functools.partial(_fft2_kernel, kb=kb)
    out = pl.pallas_call(
        kern,
        grid=(B, FFT_N1 // kb),
        in_specs=[pl.BlockSpec((1, 2, kb, n2, W_BR), lambda b, j: (b, 0, j, 0, 0)),
                  pl.BlockSpec((kb, n2, n2), lambda b, j: (j, 0, 0)),
                  pl.BlockSpec((kb, n2, n2), lambda b, j: (j, 0, 0))],
        out_specs=pl.BlockSpec((1, n2, kb, W_BR), lambda b, j: (b, 0, j, 0)),
        out_shape=jax.ShapeDtypeStruct((B, n2, FFT_N1, W_BR), F32),
        compiler_params=_params("parallel", "parallel"),
        name="fft_stage2",
    )(av, tabs["g2c"], tabs["g2s"])
    return out.reshape(B, seq, W_BR)


def _mixout_kernel(x_ref, mod_ref, g_ref, wg_ref, oa_ref, wa_ref, cb_ref, wb_ref, pc_ref, wc_ref,
                   f_ref, wd_ref, wo_ref, o_ref):
    x = x_ref[0]
    shift = mod_ref[0, 3:4, :]
    scale = mod_ref[0, 4:5, :]
    gate = mod_ref[0, 5:6, :]
    h = (_rms(x) * g_ref[2:3, :]) * (1.0 + scale) + shift
    hb = h.astype(BF16)
    d = x.shape[-1]
    merged = None
    for br, (b_ref, w_ref) in enumerate(((oa_ref, wa_ref), (cb_ref, wb_ref), (pc_ref, wc_ref), (f_ref, wd_ref))):
        gl = _sigmoid(_dot(hb, wg_ref[:, br * d:(br + 1) * d]))
        term = gl * _dot(b_ref[0].astype(BF16), w_ref[...])
        merged = term if merged is None else merged + term
    y = _dot(merged.astype(BF16), wo_ref[...])
    o_ref[0] = x + gate * (_rms(y) * g_ref[3:4, :])


def _mixout_call(x, mod_l, g_l, lw, oa, cb, pc, f, tm):
    B, S, D = x.shape
    tok = lambda w: pl.BlockSpec((1, tm, w), lambda b, i: (b, i, 0))
    return pl.pallas_call(
        _mixout_kernel,
        grid=(B, S // tm),
        in_specs=[tok(D),
                  pl.BlockSpec((1, N_MOD, D), lambda b, i: (b, 0, 0)),
                  _const_spec(g_l.shape),
                  _const_spec(lw["wg"].shape),
                  tok(N_HEADS * LANES), _const_spec(lw["wa"].shape),
                  tok(CONV_CH), _const_spec(lw["wb"].shape),
                  tok(W_BR), _const_spec(lw["wc"].shape),
                  tok(W_BR), _const_spec(lw["wd"].shape),
                  _const_spec(lw["wo"].shape)],
        out_specs=tok(D),
        out_shape=jax.ShapeDtypeStruct(x.shape, F32),
        compiler_params=_params("parallel", "parallel"),
        name="mixer_out",
    )(x, mod_l, g_l, lw["wg"], oa, lw["wa"], cb, lw["wb"], pc, lw["wc"], f, lw["wd"], lw["wo"])


def _rot_half_cols(w):
    half = w.shape[-1] // 2
    return jnp.concatenate([-w[..., half:], w[..., :half]], axis=-1)


def _tables(seq):
    pos = jnp.arange(seq, dtype=F32)
    inv = ROPE_THETA ** (-jnp.arange(0, QK_ROPE, 2, dtype=F32) / QK_ROPE)
    ang = pos[:, None] * inv[None, :]
    cos, sin = jnp.cos(ang), jnp.sin(ang)
    cc = jnp.concatenate([cos, cos], axis=-1)
    ss = jnp.concatenate([sin, sin], axis=-1)
    z64 = jnp.zeros((seq, QK_NOPE), F32)
    z32 = jnp.zeros((seq, LANES - QK_NOPE - QK_ROPE), F32)
    sm_scale = (QK_NOPE + QK_ROPE) ** -0.5 * math.log2(math.e)
    cosk = jnp.concatenate([z64, cc, z32], axis=-1)
    sink = jnp.concatenate([z64, ss, z32], axis=-1)
    cosr = jnp.tile(cc, (1, N_HEADS)) * sm_scale
    sinr = jnp.tile(ss, (1, N_HEADS)) * sm_scale

    def cos_sin(m, period):
        th = m.astype(F32) * (2.0 * math.pi / period)
        return jnp.cos(th), jnp.sin(th)

    def dft(n):
        a = jnp.arange(n, dtype=jnp.int32)
        return cos_sin((a[:, None] * a[None, :]) % n, n)

    c_ch, s_ch = dft(GC)
    cs = (jnp.concatenate([c_ch, s_ch], axis=-1) * GC ** -0.5).astype(BF16)

    n1 = FFT_N1
    n2 = seq // n1
    c1, s1 = dft(n1)
    f1 = (jnp.concatenate([jnp.concatenate([c1, -s1], axis=1),
                           jnp.concatenate([-s1, -c1], axis=1)], axis=0) * n1 ** -0.5).astype(BF16)
    k1 = jnp.arange(n1, dtype=jnp.int32)[:, None, None]
    k2 = jnp.arange(n2, dtype=jnp.int32)[None, :, None]
    t2 = jnp.arange(n2, dtype=jnp.int32)[None, None, :]
    m = ((k1 + n1 * k2) * t2) % seq
    c2, s2 = cos_sin(m, seq)
    g2c = (c2 * n2 ** -0.5).astype(BF16)
    g2s = (s2 * n2 ** -0.5).astype(BF16)
    return dict(cosr=cosr, sinr=sinr, q_scale=sm_scale, cosk=cosk, sink=sink, cs=cs, f1=f1, g2c=g2c, g2s=g2s)


def _layer_weights(l, w_in, q_norm_g, w_uq, kv_norm_g, w_ukv, w_a, conv_w, conv_b, conv_ln_g,
                   conv_ln_b, w_b, pool_w, pool_scale, w_c, w_d, w_out):
    D = w_in.shape[1]
    wi = w_in[l]
    assert Q_PAD - Q_LORA == QK_ROPE
    w1 = jnp.concatenate([wi[:, Q_LORA:Q_LORA + KV_LORA], wi[:, :Q_LORA], wi[:, Q_LORA + KV_LORA:4 * W_BR]],
                         axis=-1).astype(BF16)

    uq = w_uq[l].reshape(Q_LORA, N_HEADS, QK_NOPE + QK_ROPE)
    nope, rope = uq[..., :QK_NOPE], uq[..., QK_NOPE:]
    wq = jnp.concatenate([nope.reshape(Q_LORA, N_HEADS * QK_NOPE), rope.reshape(Q_LORA, N_HEADS * QK_ROPE),
                          _rot_half_cols(rope).reshape(Q_LORA, N_HEADS * QK_ROPE)], axis=-1)
    wq = jnp.pad(wq, ((0, Q_PAD - Q_LORA), (0, 0))).astype(BF16)

    ukv = w_ukv[l].reshape(KV_LORA, N_HEADS, QK_NOPE + V_HEAD)
    zk = jnp.zeros((KV_LORA, N_HEADS, LANES - QK_NOPE), F32)
    wk = jnp.concatenate([ukv[..., :QK_NOPE], zk], axis=-1).reshape(KV_LORA, N_HEADS * LANES).astype(BF16)
    wv = ukv[..., QK_NOPE:].reshape(KV_LORA, N_HEADS * V_HEAD).astype(BF16)

    wa = jnp.pad(w_a[l].reshape(N_HEADS, V_HEAD, D), ((0, 0), (0, LANES - V_HEAD), (0, 0)))
    wa = wa.reshape(N_HEADS * LANES, D).astype(BF16)
    gq = jnp.pad(q_norm_g[l], (0, Q_PAD - Q_LORA)).reshape(1, Q_PAD)
    return dict(
        w1=w1, wq=wq, wk=wk, wv=wv, gq=gq, gkv=kv_norm_g[l].reshape(1, KV_LORA),
        wg=wi[:, 4 * W_BR:].astype(BF16), wa=wa,
        wb=w_b[l].astype(BF16), wc=w_c[l].astype(BF16), wd=w_d[l].astype(BF16), wo=w_out[l].astype(BF16),
        conv_w=conv_w[l], conv_b=conv_b[l].reshape(1, CONV_CH),
        ln_g=conv_ln_g[l].reshape(1, CONV_CH), ln_b=conv_ln_b[l].reshape(1, CONV_CH),
        pool_w=pool_w[l].astype(BF16), pool_scale=pool_scale[l].reshape(1, W_BR))


def _tiles(seq):
    t = lambda n: min(n, seq)
    return dict(ffn=t(512), mix=t(512), tq=t(1024), tk=t(1024), local=t(512))


def kernel(x, c, ada_w, ada_b, norm_g, ffn1_w_in, ffn1_w_out, ffn2_w_in, ffn2_w_out, w_in, q_norm_g,
           w_uq, kv_norm_g, w_ukv, w_a, conv_w, conv_b, conv_ln_g, conv_ln_b, w_b, pool_w, pool_scale,
           w_c, w_d, w_out):
    B, S, D = x.shape
    L = ada_w.shape[0]
    assert D == D_MODEL and S % (FFT_N1 * 16) == 0
    ts = _tiles(S)
    tabs = _tables(S)
    mod = _mod_call(c, ada_w, ada_b).reshape(L, B, N_MOD, D)
    for l in range(L):
        lw = _layer_weights(l, w_in, q_norm_g, w_uq, kv_norm_g, w_ukv, w_a, conv_w, conv_b, conv_ln_g,
                            conv_ln_b, w_b, pool_w, pool_scale, w_c, w_d, w_out)
        mod_l, g_l = mod[l], norm_g[l]
        x = _ffn_call(x, mod_l, g_l, ffn1_w_in[l].astype(BF16), ffn1_w_out[l].astype(BF16), 0, ts["ffn"])
        q, k, v, u, zc, p, qq = _mixin_call(x, mod_l, g_l, lw, tabs, ts["mix"])
        oa = _attn_call(q, k, v, ts["tq"], ts["tk"])
        cb, pc = _local_call(u, zc, lw, ts["local"])
        a = _fft1_call(p, qq, tabs)
        f = _fft2_call(a, tabs, S, 8)
        x = _mixout_call(x, mod_l, g_l, lw, oa, cb, pc, f, ts["mix"])
        x = _ffn_call(x, mod_l, g_l, ffn2_w_in[l].astype(BF16), ffn2_w_out[l].astype(BF16), 2, ts["ffn"])
    return x
```

```python
import functools
import math

import jax
import jax.numpy as jnp
import numpy as np
from jax import lax
from jax.experimental import pallas as pl
from jax.experimental.pallas import tpu as pltpu

F32 = jnp.float32
BF16 = jnp.bfloat16

D_MODEL = 1024
D_FF = 2816
N_HEADS = 8
Q_LORA = 352
KV_LORA = 128
QK_NOPE = 64
QK_ROPE = 32
V_HEAD = 64
QK_DIM = QK_NOPE + QK_ROPE
BF16_ROWS = 16
V_ROWS = 80
DENOM_MIN = 2.0 ** -80
ROPE_THETA = 10000.0
CONV_CH = 256
CONV_WIDTH = 31
POOL_WINDOWS = (2, 4, 8, 16)
GC = 128
N_GROUPS = 4
W_BR = 512
N_MOD = 9
EPS = 1e-6

LANES = 128
SUBLANES = 8
Q_PAD = 384
HALO = 16
FFT_N1 = 128
VMEM_LIMIT = 56 * 1024 * 1024

C_CKV = 0
C_CQ = C_CKV + KV_LORA
C_ZB = C_CQ + Q_PAD
C_ZC = C_ZB + W_BR
C_ZD = C_ZC + W_BR
W1_COLS = C_ZD + W_BR


def _params(*sem):
    return pltpu.CompilerParams(dimension_semantics=sem, vmem_limit_bytes=VMEM_LIMIT)


def _const_spec(shape):
    zeros = (0,) * len(shape)
    return pl.BlockSpec(shape, lambda *_: zeros, pipeline_mode=pl.Buffered(1))


def _dot(a, b):
    return jnp.dot(a, b, preferred_element_type=F32)


def _rms(x):
    return x * lax.rsqrt(jnp.mean(x * x, axis=-1, keepdims=True) + EPS)


def _sigmoid(x):
    return 1.0 / (1.0 + jnp.exp(-x))


def _mod_kernel(c_ref, w_ref, b_ref, o_ref):
    c = c_ref[...]
    ca = c * _sigmoid(c)
    o_ref[0] = jnp.dot(ca, w_ref[0], preferred_element_type=F32,
                       precision=lax.Precision.HIGHEST) + b_ref[0]


def _mod_call(c, ada_w, ada_b):
    L, D, N = ada_w.shape
    B = c.shape[0]
    tn = 1152
    return pl.pallas_call(
        _mod_kernel,
        grid=(L, N // tn),
        in_specs=[pl.BlockSpec((B, D), lambda l, j: (0, 0)),
                  pl.BlockSpec((1, D, tn), lambda l, j: (l, 0, j)),
                  pl.BlockSpec((1, 1, tn), lambda l, j: (l, 0, j))],
        out_specs=pl.BlockSpec((1, B, tn), lambda l, j: (l, 0, j)),
        out_shape=jax.ShapeDtypeStruct((L, B, N), F32),
        compiler_params=_params("parallel", "parallel"),
        name="adaln_mod",
    )(c, ada_w, ada_b.reshape(L, 1, N))


def _ffn_kernel(x_ref, mod_ref, g_ref, win_ref, wout_ref, o_ref, a_scr, *, sub, chunk):
    x = x_ref[0]
    shift = mod_ref[0, 3 * sub:3 * sub + 1, :]
    scale = mod_ref[0, 3 * sub + 1:3 * sub + 2, :]
    gate = mod_ref[0, 3 * sub + 2:3 * sub + 3, :]
    g_in = g_ref[2 * sub:2 * sub + 1, :]
    g_out = g_ref[2 * sub + 1:2 * sub + 2, :]
    h = (_rms(x) * g_in) * (1.0 + scale) + shift
    hb = h.astype(BF16)
    for j in range(D_FF // chunk):
        gj = _dot(hb, win_ref[:, j * chunk:(j + 1) * chunk])
        uj = _dot(hb, win_ref[:, D_FF + j * chunk:D_FF + (j + 1) * chunk])
        a_scr[:, j * chunk:(j + 1) * chunk] = (gj * _sigmoid(gj) * uj).astype(BF16)
    y = _dot(a_scr[...], wout_ref[...])
    o_ref[0] = x + (0.5 * gate) * (_rms(y) * g_out)


def _ffn_call(x, mod_l, g_l, w_in, w_out, sub, tm):
    B, S, D = x.shape
    kern = functools.partial(_ffn_kernel, sub=sub, chunk=256)
    return pl.pallas_call(
        kern,
        grid=(B, S // tm),
        in_specs=[pl.BlockSpec((1, tm, D), lambda b, i: (b, i, 0)),
                  pl.BlockSpec((1, N_MOD, D), lambda b, i: (b, 0, 0)),
                  _const_spec(g_l.shape),
                  _const_spec(w_in.shape),
                  _const_spec(w_out.shape)],
        out_specs=pl.BlockSpec((1, tm, D), lambda b, i: (b, i, 0)),
        out_shape=jax.ShapeDtypeStruct(x.shape, F32),
        scratch_shapes=[pltpu.VMEM((tm, D_FF), BF16)],
        compiler_params=_params("parallel", "parallel"),
        name=f"ffn{sub}",
    )(x, mod_l, g_l, w_in, w_out)


def _mixin_kernel(x_ref, mod_ref, g_ref, w1_ref, gq_ref, gkv_ref, wq_ref, wk_ref, wv_ref,
                  cr_ref, sr_ref, ck_ref, sk_ref, cs_ref,
                  q_ref, k_ref, v_ref, kn_ref, u_ref, zc_ref, p_ref, qq_ref, *, q_scale):
    x = x_ref[0]
    tm = x.shape[0]
    shift = mod_ref[0, 3:4, :]
    scale = mod_ref[0, 4:5, :]
    h = (_rms(x) * g_ref[2:3, :]) * (1.0 + scale) + shift
    hb = h.astype(BF16)
    z = _dot(hb, w1_ref[...])

    cq = z[:, C_CQ:C_CQ + Q_PAD]
    qlane = lax.broadcasted_iota(jnp.int32, (1, Q_PAD), 1)
    cqm = jnp.where(qlane < Q_LORA, cq, 0.0)
    cqn = cqm * lax.rsqrt(jnp.sum(cqm * cqm, axis=-1, keepdims=True) * (1.0 / Q_LORA) + EPS)
    cqn = (cqn * gq_ref[...]).astype(BF16)
    qq = _dot(cqn, wq_ref[...])
    nope = qq[:, :N_HEADS * QK_NOPE] * q_scale
    w_rope = N_HEADS * QK_ROPE
    roped = (qq[:, N_HEADS * QK_NOPE:N_HEADS * QK_NOPE + w_rope] * cr_ref[...]
             + qq[:, N_HEADS * QK_NOPE + w_rope:] * sr_ref[...])
    for blk in range(N_HEADS * QK_NOPE // LANES):
        nt = nope[:, blk * LANES:(blk + 1) * LANES].T.astype(BF16)
        for i in range(LANES // QK_NOPE):
            q_ref[0, blk * (LANES // QK_NOPE) + i, 0:QK_NOPE, :] = nt[i * QK_NOPE:(i + 1) * QK_NOPE]
    for blk in range(w_rope // LANES):
        rt = roped[:, blk * LANES:(blk + 1) * LANES].T.astype(BF16)
        for i in range(LANES // QK_ROPE):
            q_ref[0, blk * (LANES // QK_ROPE) + i, QK_NOPE:QK_NOPE + QK_ROPE, :] = rt[i * QK_ROPE:(i + 1) * QK_ROPE]
    zero_rows = jnp.zeros((LANES - QK_NOPE - QK_ROPE, tm), BF16)
    for hd in range(N_HEADS):
        q_ref[0, hd, QK_NOPE + QK_ROPE:, :] = zero_rows

    kr_at = Q_LORA - (Q_PAD - LANES)
    kr = pltpu.roll(cq[:, Q_PAD - LANES:], (QK_NOPE - kr_at) % LANES, 1)
    klane = lax.broadcasted_iota(jnp.int32, (1, LANES), 1)
    kr_rot = jnp.where(klane < QK_NOPE + QK_ROPE // 2,
                       -pltpu.roll(kr, LANES - QK_ROPE // 2, 1), pltpu.roll(kr, QK_ROPE // 2, 1))
    k_rope = kr * ck_ref[...] + kr_rot * sk_ref[...]

    ckv = z[:, C_CKV:C_CKV + KV_LORA]
    ckvn = (_rms(ckv) * gkv_ref[...]).astype(BF16)
    kk = _dot(ckvn, wk_ref[...])
    ones_lane = (klane == QK_DIM).astype(F32)
    for hd in range(N_HEADS):
        kh = kk[:, hd * LANES:(hd + 1) * LANES] + k_rope
        k_ref[0, hd] = (kh + ones_lane).astype(BF16)
        norm2 = jnp.max(jnp.sum(kh * kh, axis=-1, keepdims=True), axis=0, keepdims=True)
        kn_ref[0, hd] = jnp.broadcast_to(norm2, (SUBLANES, LANES))
    vv = _dot(ckvn, wv_ref[...])
    ones_rows = (lax.broadcasted_iota(jnp.int32, (V_ROWS - V_HEAD, tm), 0) == 0).astype(BF16)
    for blk in range(N_HEADS * V_HEAD // LANES):
        vt = vv[:, blk * LANES:(blk + 1) * LANES].T.astype(BF16)
        for i in range(LANES // V_HEAD):
            v_ref[0, blk * (LANES // V_HEAD) + i, 0:V_HEAD, :] = vt[i * V_HEAD:(i + 1) * V_HEAD]
    for hd in range(N_HEADS):
        v_ref[0, hd, V_HEAD:, :] = ones_rows

    a = z[:, C_ZB:C_ZB + CONV_CH]
    gt = z[:, C_ZB + CONV_CH:C_ZB + 2 * CONV_CH]
    u_ref[0] = a * _sigmoid(gt)

    zc_ref[0] = z[:, C_ZC:C_ZC + W_BR]

    for g in range(N_GROUPS):
        zd = z[:, C_ZD + g * GC:C_ZD + (g + 1) * GC].astype(BF16)
        pq = _dot(zd, cs_ref[...])
        p_ref[0, :, g * GC:(g + 1) * GC] = pq[:, :GC]
        qq_ref[0, :, g * GC:(g + 1) * GC] = pq[:, GC:]


def _mixin_call(x, mod_l, g_l, lw, tabs, tm):
    B, S, D = x.shape
    tok = lambda w: pl.BlockSpec((1, tm, w), lambda b, i: (b, i, 0))
    head = pl.BlockSpec((1, N_HEADS, tm, LANES), lambda b, i: (b, 0, i, 0))
    head_t = pl.BlockSpec((1, N_HEADS, LANES, tm), lambda b, i: (b, 0, 0, i))
    tab = lambda w: pl.BlockSpec((tm, w), lambda b, i: (i, 0))
    hshape = jax.ShapeDtypeStruct((B, N_HEADS, S, LANES), BF16)
    hshape_t = jax.ShapeDtypeStruct((B, N_HEADS, LANES, S), BF16)
    return pl.pallas_call(
        functools.partial(_mixin_kernel, q_scale=tabs["q_scale"]),
        grid=(B, S // tm),
        in_specs=[tok(D),
                  pl.BlockSpec((1, N_MOD, D), lambda b, i: (b, 0, 0)),
                  _const_spec(g_l.shape),
                  _const_spec(lw["w1"].shape),
                  _const_spec(lw["gq"].shape),
                  _const_spec(lw["gkv"].shape),
                  _const_spec(lw["wq"].shape),
                  _const_spec(lw["wk"].shape),
                  _const_spec(lw["wv"].shape),
                  tab(N_HEADS * QK_ROPE), tab(N_HEADS * QK_ROPE), tab(LANES), tab(LANES),
                  _const_spec(tabs["cs"].shape)],
        out_specs=[head_t, head, pl.BlockSpec((1, N_HEADS, V_ROWS, tm), lambda b, i: (b, 0, 0, i)),
                   pl.BlockSpec((1, N_HEADS, SUBLANES, LANES), lambda b, i: (b, 0, i, 0)),
                   tok(CONV_CH), tok(W_BR), tok(W_BR), tok(W_BR)],
        out_shape=[hshape_t, hshape, jax.ShapeDtypeStruct((B, N_HEADS, V_ROWS, S), BF16),
                   jax.ShapeDtypeStruct((B, N_HEADS, (S // tm) * SUBLANES, LANES), F32),
                   jax.ShapeDtypeStruct((B, S, CONV_CH), F32),
                   jax.ShapeDtypeStruct((B, S, W_BR), F32),
                   jax.ShapeDtypeStruct((B, S, W_BR), F32),
                   jax.ShapeDtypeStruct((B, S, W_BR), F32)],
        compiler_params=_params("parallel", "parallel"),
        name="mixer_in",
    )(x, mod_l, g_l, lw["w1"], lw["gq"], lw["gkv"], lw["wq"], lw["wk"], lw["wv"],
      tabs["cosr"], tabs["sinr"], tabs["cosk"], tabs["sink"], tabs["cs"])


def _attn_kernel(qt_ref, k_ref, vt_ref, kn_ref, o_ref, acc_scr, m_scr, *, tq, tk):
    seq = k_ref.shape[2]
    nq, nk = seq // tq, seq // tk
    acc_scr[...] = jnp.zeros(acc_scr.shape, F32)
    k_norm2 = jnp.max(kn_ref[0, 0])

    def exact_tile(qt):
        m_scr[...] = jnp.full(m_scr.shape, -jnp.inf, F32)
        acc_scr[0:V_ROWS, :] = jnp.zeros((V_ROWS, tq), F32)

        def chunk(j, carry):
            off = pl.multiple_of(j * tk, tk)
            st = _dot(k_ref[0, 0, pl.ds(off, tk), :], qt)
            m_prev = m_scr[...]
            m_new = jnp.maximum(m_prev, jnp.max(st, axis=0, keepdims=True))
            pt = jnp.exp2(st - m_new).astype(BF16)
            pv = _dot(vt_ref[0, 0, :, pl.ds(off, tk)], pt)
            acc_scr[0:V_ROWS, :] = jnp.exp2(m_prev - m_new) * acc_scr[0:V_ROWS, :] + pv
            m_scr[...] = m_new
            return carry

        lax.fori_loop(0, nk, chunk, 0)

    def q_tile(qi, carry):
        qoff = pl.multiple_of(qi * tq, tq)
        qt = qt_ref[0, 0, :, pl.ds(qoff, tq)]
        q32 = qt.astype(F32)
        bound = jnp.sqrt(jnp.sum(q32 * q32, axis=0, keepdims=True) * k_norm2)
        first = lax.broadcasted_iota(jnp.int32, (BF16_ROWS, tq), 0) == 0
        stab = jnp.where(first, -bound, 0.0).astype(BF16)
        qs = jnp.concatenate([qt[0:QK_DIM], stab, qt[QK_DIM + BF16_ROWS:]], axis=0)
        acc = None
        for j in range(nk):
            st = _dot(k_ref[0, 0, j * tk:(j + 1) * tk, :], qs)
            pv = _dot(vt_ref[0, 0, :, j * tk:(j + 1) * tk], jnp.exp2(st).astype(BF16))
            acc = pv if acc is None else acc + pv
        acc_scr[0:V_ROWS, :] = acc
        denom_ok = jnp.min(acc[V_HEAD:V_HEAD + 1, :]) >= DENOM_MIN

        @pl.when(jnp.logical_not(denom_ok))
        def _():
            exact_tile(qt)

        acc = acc_scr[...]
        denom = acc[V_HEAD:V_HEAD + 1, :]
        o_ref[0, pl.ds(qoff, tq), :] = (acc / denom).T.astype(o_ref.dtype)
        return carry

    lax.fori_loop(0, nq, q_tile, 0)


def _attn_call(qt, k, vt, kn, tq, tk):
    B, H, S, _ = k.shape
    kern = functools.partial(_attn_kernel, tq=tq, tk=tk)
    return pl.pallas_call(
        kern,
        grid=(B, H),
        in_specs=[pl.BlockSpec((1, 1, LANES, S), lambda b, h: (b, h, 0, 0)),
                  pl.BlockSpec((1, 1, S, LANES), lambda b, h: (b, h, 0, 0)),
                  pl.BlockSpec((1, 1, V_ROWS, S), lambda b, h: (b, h, 0, 0)),
                  pl.BlockSpec((1, 1) + kn.shape[2:], lambda b, h: (b, h, 0, 0))],
        out_specs=pl.BlockSpec((1, S, LANES), lambda b, h: (b, 0, h)),
        out_shape=jax.ShapeDtypeStruct((B, S, H * LANES), BF16),
        scratch_shapes=[pltpu.VMEM((LANES, tq), F32), pltpu.VMEM((1, tq), F32)],
        compiler_params=_params("parallel", "arbitrary"),
        name="attention",
    )(qt, k, vt, kn)


def _local_kernel(u_ref, up_ref, un_ref, z_ref, zp_ref, zn_ref, cw_ref, cb_ref, lg_ref, lb_ref,
                  pw_ref, ps_ref, oc_ref, op_ref, ubuf, zbuf, ush, *, ts, seq, rc, rp):
    i = pl.program_id(1)
    has_prev = i > 0
    has_next = i < pl.num_programs(1) - 1
    ubuf[0:HALO, :] = jnp.where(has_prev, up_ref[0], 0.0)
    ubuf[HALO:HALO + ts, :] = u_ref[0]
    ubuf[HALO + ts:, :] = jnp.where(has_next, un_ref[0], 0.0)
    zbuf[0:HALO, :] = jnp.where(has_prev, zp_ref[0], 0.0)
    zbuf[HALO:HALO + ts, :] = z_ref[0]
    zbuf[HALO + ts:, :] = jnp.where(has_next, zn_ref[0], 0.0)

    span = ts + 2 * HALO - SUBLANES
    for s in range(1, SUBLANES):
        ush[s - 1, 0:span, :] = ubuf[s:s + span, :]

    pad = CONV_WIDTH // 2
    for r in range(0, ts, rc):
        acc = jnp.zeros((rc, CONV_CH), F32)
        for kk in range(CONV_WIDTH):
            st = HALO + r + kk - pad
            s, base = st % SUBLANES, st - st % SUBLANES
            tap = ubuf[base:base + rc, :] if s == 0 else ush[s - 1, base:base + rc, :]
            acc = acc + tap * cw_ref[kk:kk + 1, :]
        y = acc + cb_ref[...]
        mu = jnp.mean(y, axis=-1, keepdims=True)
        yc = y - mu
        yn = yc * lax.rsqrt(jnp.mean(yc * yc, axis=-1, keepdims=True) + EPS)
        yn = yn * lg_ref[...] + lb_ref[...]
        oc_ref[0, r:r + rc, :] = (yn * _sigmoid(yn)).astype(oc_ref.dtype)

    for r in range(0, ts, rp):
        t = i * ts + r + lax.broadcasted_iota(jnp.int32, (rp, 1), 0)
        for g, w in enumerate(POOL_WINDOWS):
            lo = w // 2
            hi = w - 1 - lo
            cols = slice(g * GC, (g + 1) * GC)
            win = jnp.zeros((rp, GC), F32)
            for j in range(-lo, hi + 1):
                st = HALO + r + j
                win = win + zbuf[st:st + rp, cols]
            cnt = (jnp.minimum(t + hi + 1, seq) - jnp.maximum(t - lo, 0)).astype(F32)
            d = win / cnt - zbuf[HALO + r:HALO + r + rp, cols]
            yg = _dot(d.astype(BF16), pw_ref[g]) * ps_ref[:, cols]
            op_ref[0, r:r + rp, cols] = yg.astype(op_ref.dtype)


def _local_call(u, zc, lw, ts):
    B, S, _ = u.shape
    nh = ts // HALO
    last = S // HALO - 1
    cur = lambda w: pl.BlockSpec((1, ts, w), lambda b, i: (b, i, 0))
    prev = lambda w: pl.BlockSpec((1, HALO, w), lambda b, i: (b, jnp.maximum(i * nh - 1, 0), 0))
    nxt = lambda w: pl.BlockSpec((1, HALO, w), lambda b, i: (b, jnp.minimum((i + 1) * nh, last), 0))
    kern = functools.partial(_local_kernel, ts=ts, seq=S, rc=64, rp=min(ts, 256))
    return pl.pallas_call(
        kern,
        grid=(B, S // ts),
        in_specs=[cur(CONV_CH), prev(CONV_CH), nxt(CONV_CH), cur(W_BR), prev(W_BR), nxt(W_BR),
                  _const_spec(lw["conv_w"].shape), _const_spec(lw["conv_b"].shape),
                  _const_spec(lw["ln_g"].shape), _const_spec(lw["ln_b"].shape),
                  _const_spec(lw["pool_w"].shape), _const_spec(lw["pool_scale"].shape)],
        out_specs=[cur(CONV_CH), cur(W_BR)],
        out_shape=[jax.ShapeDtypeStruct((B, S, CONV_CH), BF16),
                   jax.ShapeDtypeStruct((B, S, W_BR), BF16)],
        scratch_shapes=[pltpu.VMEM((ts + 2 * HALO, CONV_CH), F32),
                        pltpu.VMEM((ts + 2 * HALO, W_BR), F32),
                        pltpu.VMEM((SUBLANES - 1, ts + 2 * HALO, CONV_CH), F32)],
        compiler_params=_params("parallel", "parallel"),
        name="conv_pool",
    )(u, u, u, zc, zc, zc, lw["conv_w"], lw["conv_b"], lw["ln_g"], lw["ln_b"],
      lw["pool_w"], lw["pool_scale"])


def _fft1_kernel(p_ref, q_ref, f_ref, o_ref):
    for j in range(p_ref.shape[2]):
        rhs = jnp.concatenate([p_ref[0, :, j, :], q_ref[0, :, j, :]], axis=0).astype(BF16)
        o_ref[0, :, j, :] = _dot(f_ref[...], rhs)


def _fft1_call(p, q, tabs):
    B, S, W = p.shape
    n2 = S // FFT_N1
    pv = p.reshape(B, FFT_N1, n2, W)
    qv = q.reshape(B, FFT_N1, n2, W)
    blk = pl.BlockSpec((1, FFT_N1, SUBLANES, W), lambda b, j: (b, 0, j, 0))
    return pl.pallas_call(
        _fft1_kernel,
        grid=(B, n2 // SUBLANES),
        in_specs=[blk, blk, _const_spec(tabs["f1"].shape)],
        out_specs=pl.BlockSpec((1, 2 * FFT_N1, SUBLANES, W), lambda b, j: (b, 0, j, 0)),
        out_shape=jax.ShapeDtypeStruct((B, 2 * FFT_N1, n2, W), F32),
        compiler_params=_params("parallel", "parallel"),
        name="fft_stage1",
    )(pv, qv, tabs["f1"])


def _fft2_kernel(a_ref, gc_ref, gs_ref, o_ref, *, kb):
    for j in range(kb):
        f = _dot(gc_ref[j], a_ref[0, 0, j].astype(BF16)) + _dot(gs_ref[j], a_ref[0, 1, j].astype(BF16))
        o_ref[0, :, j, :] = f


def _fft2_call(a, tabs, seq, kb):
    B = a.shape[0]
    n2 = seq // FFT_N1
    av = a.reshape(B, 2, FFT_N1, n2, W_BR)
    kern = functools.partial(_fft2_kernel, kb=kb)
    out = pl.pallas_call(
        kern,
        grid=(B, FFT_N1 // kb),
        in_specs=[pl.BlockSpec((1, 2, kb, n2, W_BR), lambda b, j: (b, 0, j, 0, 0)),
                  pl.BlockSpec((kb, n2, n2), lambda b, j: (j, 0, 0)),
                  pl.BlockSpec((kb, n2, n2), lambda b, j: (j, 0, 0))],
        out_specs=pl.BlockSpec((1, n2, kb, W_BR), lambda b, j: (b, 0, j, 0)),
        out_shape=jax.ShapeDtypeStruct((B, n2, FFT_N1, W_BR), F32),
        compiler_params=_params("parallel", "parallel"),
        name="fft_stage2",
    )(av, tabs["g2c"], tabs["g2s"])
    return out.reshape(B, seq, W_BR)


def _mixout_kernel(x_ref, mod_ref, g_ref, wg_ref, oa_ref, wa_ref, cb_ref, wb_ref, pc_ref, wc_ref,
                   f_ref, wd_ref, wo_ref, o_ref):
    x = x_ref[0]
    shift = mod_ref[0, 3:4, :]
    scale = mod_ref[0, 4:5, :]
    gate = mod_ref[0, 5:6, :]
    h = (_rms(x) * g_ref[2:3, :]) * (1.0 + scale) + shift
    hb = h.astype(BF16)
    d = x.shape[-1]
    merged = None
    for br, (b_ref, w_ref) in enumerate(((oa_ref, wa_ref), (cb_ref, wb_ref), (pc_ref, wc_ref), (f_ref, wd_ref))):
        gl = _sigmoid(_dot(hb, wg_ref[:, br * d:(br + 1) * d]))
        term = gl * _dot(b_ref[0].astype(BF16), w_ref[...])
        merged = term if merged is None else merged + term
    y = _dot(merged.astype(BF16), wo_ref[...])
    o_ref[0] = x + gate * (_rms(y) * g_ref[3:4, :])


def _mixout_call(x, mod_l, g_l, lw, oa, cb, pc, f, tm):
    B, S, D = x.shape
    tok = lambda w: pl.BlockSpec((1, tm, w), lambda b, i: (b, i, 0))
    return pl.pallas_call(
        _mixout_kernel,
        grid=(B, S // tm),
        in_specs=[tok(D),
                  pl.BlockSpec((1, N_MOD, D), lambda b, i: (b, 0, 0)),
                  _const_spec(g_l.shape),
                  _const_spec(lw["wg"].shape),
                  tok(N_HEADS * LANES), _const_spec(lw["wa"].shape),
                  tok(CONV_CH), _const_spec(lw["wb"].shape),
                  tok(W_BR), _const_spec(lw["wc"].shape),
                  tok(W_BR), _const_spec(lw["wd"].shape),
                  _const_spec(lw["wo"].shape)],
        out_specs=tok(D),
        out_shape=jax.ShapeDtypeStruct(x.shape, F32),
        compiler_params=_params("parallel", "parallel"),
        name="mixer_out",
    )(x, mod_l, g_l, lw["wg"], oa, lw["wa"], cb, lw["wb"], pc, lw["wc"], f, lw["wd"], lw["wo"])


def _rot_half_cols(w):
    half = w.shape[-1] // 2
    return jnp.concatenate([-w[..., half:], w[..., :half]], axis=-1)


def _tables(seq):
    pos = jnp.arange(seq, dtype=F32)
    inv = ROPE_THETA ** (-jnp.arange(0, QK_ROPE, 2, dtype=F32) / QK_ROPE)
    ang = pos[:, None] * inv[None, :]
    cos, sin = jnp.cos(ang), jnp.sin(ang)
    cc = jnp.concatenate([cos, cos], axis=-1)
    ss = jnp.concatenate([sin, sin], axis=-1)
    z64 = jnp.zeros((seq, QK_NOPE), F32)
    z32 = jnp.zeros((seq, LANES - QK_NOPE - QK_ROPE), F32)
    sm_scale = (QK_NOPE + QK_ROPE) ** -0.5 * math.log2(math.e)
    cosk = jnp.concatenate([z64, cc, z32], axis=-1)
    sink = jnp.concatenate([z64, ss, z32], axis=-1)
    cosr = jnp.tile(cc, (1, N_HEADS)) * sm_scale
    sinr = jnp.tile(ss, (1, N_HEADS)) * sm_scale

    def cos_sin(m, period):
        th = m.astype(F32) * (2.0 * math.pi / period)
        return jnp.cos(th), jnp.sin(th)

    def dft(n):
        a = jnp.arange(n, dtype=jnp.int32)
        return cos_sin((a[:, None] * a[None, :]) % n, n)

    c_ch, s_ch = dft(GC)
    cs = (jnp.concatenate([c_ch, s_ch], axis=-1) * GC ** -0.5).astype(BF16)

    n1 = FFT_N1
    n2 = seq // n1
    c1, s1 = dft(n1)
    f1 = (jnp.concatenate([jnp.concatenate([c1, -s1], axis=1),
                           jnp.concatenate([-s1, -c1], axis=1)], axis=0) * n1 ** -0.5).astype(BF16)
    k1 = jnp.arange(n1, dtype=jnp.int32)[:, None, None]
    k2 = jnp.arange(n2, dtype=jnp.int32)[None, :, None]
    t2 = jnp.arange(n2, dtype=jnp.int32)[None, None, :]
    m = ((k1 + n1 * k2) * t2) % seq
    c2, s2 = cos_sin(m, seq)
    g2c = (c2 * n2 ** -0.5).astype(BF16)
    g2s = (s2 * n2 ** -0.5).astype(BF16)
    return dict(cosr=cosr, sinr=sinr, q_scale=sm_scale, cosk=cosk, sink=sink, cs=cs, f1=f1, g2c=g2c, g2s=g2s)


def _layer_weights(l, w_in, q_norm_g, w_uq, kv_norm_g, w_ukv, w_a, conv_w, conv_b, conv_ln_g,
                   conv_ln_b, w_b, pool_w, pool_scale, w_c, w_d, w_out):
    D = w_in.shape[1]
    wi = w_in[l]
    assert Q_PAD - Q_LORA == QK_ROPE
    w1 = jnp.concatenate([wi[:, Q_LORA:Q_LORA + KV_LORA], wi[:, :Q_LORA], wi[:, Q_LORA + KV_LORA:4 * W_BR]],
                         axis=-1).astype(BF16)

    uq = w_uq[l].reshape(Q_LORA, N_HEADS, QK_NOPE + QK_ROPE)
    nope, rope = uq[..., :QK_NOPE], uq[..., QK_NOPE:]
    wq = jnp.concatenate([nope.reshape(Q_LORA, N_HEADS * QK_NOPE), rope.reshape(Q_LORA, N_HEADS * QK_ROPE),
                          _rot_half_cols(rope).reshape(Q_LORA, N_HEADS * QK_ROPE)], axis=-1)
    wq = jnp.pad(wq, ((0, Q_PAD - Q_LORA), (0, 0))).astype(BF16)

    ukv = w_ukv[l].reshape(KV_LORA, N_HEADS, QK_NOPE + V_HEAD)
    zk = jnp.zeros((KV_LORA, N_HEADS, LANES - QK_NOPE), F32)
    wk = jnp.concatenate([ukv[..., :QK_NOPE], zk], axis=-1).reshape(KV_LORA, N_HEADS * LANES).astype(BF16)
    wv = ukv[..., QK_NOPE:].reshape(KV_LORA, N_HEADS * V_HEAD).astype(BF16)

    wa = jnp.pad(w_a[l].reshape(N_HEADS, V_HEAD, D), ((0, 0), (0, LANES - V_HEAD), (0, 0)))
    wa = wa.reshape(N_HEADS * LANES, D).astype(BF16)
    gq = jnp.pad(q_norm_g[l], (0, Q_PAD - Q_LORA)).reshape(1, Q_PAD)
    return dict(
        w1=w1, wq=wq, wk=wk, wv=wv, gq=gq, gkv=kv_norm_g[l].reshape(1, KV_LORA),
        wg=wi[:, 4 * W_BR:].astype(BF16), wa=wa,
        wb=w_b[l].astype(BF16), wc=w_c[l].astype(BF16), wd=w_d[l].astype(BF16), wo=w_out[l].astype(BF16),
        conv_w=conv_w[l], conv_b=conv_b[l].reshape(1, CONV_CH),
        ln_g=conv_ln_g[l].reshape(1, CONV_CH), ln_b=conv_ln_b[l].reshape(1, CONV_CH),
        pool_w=pool_w[l].astype(BF16), pool_scale=pool_scale[l].reshape(1, W_BR))


def _tiles(seq):
    t = lambda n: min(n, seq)
    return dict(ffn=t(512), mix=t(512), tq=t(1024), tk=t(1024), local=t(512))


def kernel(x, c, ada_w, ada_b, norm_g, ffn1_w_in, ffn1_w_out, ffn2_w_in, ffn2_w_out, w_in, q_norm_g,
           w_uq, kv_norm_g, w_ukv, w_a, conv_w, conv_b, conv_ln_g, conv_ln_b, w_b, pool_w, pool_scale,
           w_c, w_d, w_out):
    B, S, D = x.shape
    L = ada_w.shape[0]
    assert D == D_MODEL and S % (FFT_N1 * 16) == 0
    ts = _tiles(S)
    tabs = _tables(S)
    mod = _mod_call(c, ada_w, ada_b).reshape(L, B, N_MOD, D)
    for l in range(L):
        lw = _layer_weights(l, w_in, q_norm_g, w_uq, kv_norm_g, w_ukv, w_a, conv_w, conv_b, conv_ln_g,
                            conv_ln_b, w_b, pool_w, pool_scale, w_c, w_d, w_out)
        mod_l, g_l = mod[l], norm_g[l]
        x = _ffn_call(x, mod_l, g_l, ffn1_w_in[l].astype(BF16), ffn1_w_out[l].astype(BF16), 0, ts["ffn"])
        q, k, v, kn, u, zc, p, qq = _mixin_call(x, mod_l, g_l, lw, tabs, ts["mix"])
        oa = _attn_call(q, k, v, kn, ts["tq"], ts["tk"])
        cb, pc = _local_call(u, zc, lw, ts["local"])
        a = _fft1_call(p, qq, tabs)
        f = _fft2_call(a, tabs, S, 8)
        x = _mixout_call(x, mod_l, g_l, lw, oa, cb, pc, f, ts["mix"])
        x = _ffn_call(x, mod_l, g_l, ffn2_w_in[l].astype(BF16), ffn2_w_out[l].astype(BF16), 2, ts["ffn"])
    return x
```

```python
import functools
import math

import jax
import jax.numpy as jnp
import numpy as np
from jax import lax
from jax.experimental import pallas as pl
from jax.experimental.pallas import tpu as pltpu

F32 = jnp.float32
BF16 = jnp.bfloat16

D_MODEL = 1024
D_FF = 2816
N_HEADS = 8
Q_LORA = 352
KV_LORA = 128
QK_NOPE = 64
QK_ROPE = 32
V_HEAD = 64
QK_DIM = QK_NOPE + QK_ROPE
BF16_ROWS = 16
V_ROWS = 80
DENOM_MIN = 2.0 ** -80
ROPE_THETA = 10000.0
CONV_CH = 256
CONV_WIDTH = 31
POOL_WINDOWS = (2, 4, 8, 16)
GC = 128
N_GROUPS = 4
W_BR = 512
N_MOD = 9
EPS = 1e-6

LANES = 128
SUBLANES = 8
Q_PAD = 384
HALO = 16
FFT_N1 = 128
VMEM_LIMIT = 56 * 1024 * 1024

C_CKV = 0
C_CQ = C_CKV + KV_LORA
C_ZB = C_CQ + Q_PAD
C_ZC = C_ZB + W_BR
C_ZD = C_ZC + W_BR
W1_COLS = C_ZD + W_BR


def _params(*sem):
    return pltpu.CompilerParams(dimension_semantics=sem, vmem_limit_bytes=VMEM_LIMIT)


def _const_spec(shape):
    zeros = (0,) * len(shape)
    return pl.BlockSpec(shape, lambda *_: zeros, pipeline_mode=pl.Buffered(1))


def _dot(a, b):
    return jnp.dot(a, b, preferred_element_type=F32)


def _rms(x):
    return x * lax.rsqrt(jnp.mean(x * x, axis=-1, keepdims=True) + EPS)


def _sigmoid(x):
    return 1.0 / (1.0 + jnp.exp(-x))


def _mod_kernel(c_ref, w_ref, b_ref, o_ref):
    c = c_ref[...]
    ca = c * _sigmoid(c)
    o_ref[0] = jnp.dot(ca, w_ref[0], preferred_element_type=F32,
                       precision=lax.Precision.HIGHEST) + b_ref[0]


def _mod_call(c, ada_w, ada_b):
    L, D, N = ada_w.shape
    B = c.shape[0]
    tn = 1152
    return pl.pallas_call(
        _mod_kernel,
        grid=(L, N // tn),
        in_specs=[pl.BlockSpec((B, D), lambda l, j: (0, 0)),
                  pl.BlockSpec((1, D, tn), lambda l, j: (l, 0, j)),
                  pl.BlockSpec((1, 1, tn), lambda l, j: (l, 0, j))],
        out_specs=pl.BlockSpec((1, B, tn), lambda l, j: (l, 0, j)),
        out_shape=jax.ShapeDtypeStruct((L, B, N), F32),
        compiler_params=_params("parallel", "parallel"),
        name="adaln_mod",
    )(c, ada_w, ada_b.reshape(L, 1, N))


def _ffn_kernel(x_ref, mod_ref, g_ref, win_ref, wout_ref, o_ref, a_scr, *, sub, chunk, parts):
    shift = mod_ref[0, 3 * sub:3 * sub + 1, :]
    scale = mod_ref[0, 3 * sub + 1:3 * sub + 2, :]
    gate = mod_ref[0, 3 * sub + 2:3 * sub + 3, :]
    g_in = g_ref[2 * sub:2 * sub + 1, :]
    g_out = g_ref[2 * sub + 1:2 * sub + 2, :]
    rows = x_ref.shape[1] // parts
    hbs = []
    for p in range(parts):
        x = x_ref[0, p * rows:(p + 1) * rows, :]
        hbs.append(((_rms(x) * g_in) * (1.0 + scale) + shift).astype(BF16))
    for p in range(parts):
        for j in range(D_FF // chunk):
            gj = _dot(hbs[p], win_ref[:, j * chunk:(j + 1) * chunk])
            uj = _dot(hbs[p], win_ref[:, D_FF + j * chunk:D_FF + (j + 1) * chunk])
            a_scr[p, :, j * chunk:(j + 1) * chunk] = (gj * _sigmoid(gj) * uj).astype(BF16)
    for p in range(parts):
        y = _dot(a_scr[p], wout_ref[...])
        x = x_ref[0, p * rows:(p + 1) * rows, :]
        o_ref[0, p * rows:(p + 1) * rows, :] = x + (0.5 * gate) * (_rms(y) * g_out)


def _ffn_call(x, mod_l, g_l, w_in, w_out, sub, tm):
    B, S, D = x.shape
    parts = 4 if tm % 1024 == 0 else 1
    kern = functools.partial(_ffn_kernel, sub=sub, chunk=256, parts=parts)
    return pl.pallas_call(
        kern,
        grid=(B, S // tm),
        in_specs=[pl.BlockSpec((1, tm, D), lambda b, i: (b, i, 0)),
                  pl.BlockSpec((1, N_MOD, D), lambda b, i: (b, 0, 0)),
                  _const_spec(g_l.shape),
                  _const_spec(w_in.shape),
                  _const_spec(w_out.shape)],
        out_specs=pl.BlockSpec((1, tm, D), lambda b, i: (b, i, 0)),
        out_shape=jax.ShapeDtypeStruct(x.shape, F32),
        scratch_shapes=[pltpu.VMEM((parts, tm // parts, D_FF), BF16)],
        compiler_params=_params("parallel", "parallel"),
        name=f"ffn{sub}",
    )(x, mod_l, g_l, w_in, w_out)


def _mixin_kernel(x_ref, mod_ref, g_ref, w1_ref, gq_ref, gkv_ref, wq_ref, wk_ref, wv_ref,
                  cr_ref, sr_ref, ck_ref, sk_ref, cs_ref,
                  q_ref, k_ref, v_ref, kn_ref, u_ref, zc_ref, p_ref, qq_ref, *, q_scale):
    x = x_ref[0]
    tm = x.shape[0]
    shift = mod_ref[0, 3:4, :]
    scale = mod_ref[0, 4:5, :]
    h = (_rms(x) * g_ref[2:3, :]) * (1.0 + scale) + shift
    hb = h.astype(BF16)
    z = _dot(hb, w1_ref[...])

    cq = z[:, C_CQ:C_CQ + Q_PAD]
    qlane = lax.broadcasted_iota(jnp.int32, (1, Q_PAD), 1)
    cqm = jnp.where(qlane < Q_LORA, cq, 0.0)
    cqn = cqm * lax.rsqrt(jnp.sum(cqm * cqm, axis=-1, keepdims=True) * (1.0 / Q_LORA) + EPS)
    cqn = (cqn * gq_ref[...]).astype(BF16)
    qq = _dot(cqn, wq_ref[...])
    nope = qq[:, :N_HEADS * QK_NOPE] * q_scale
    w_rope = N_HEADS * QK_ROPE
    roped = (qq[:, N_HEADS * QK_NOPE:N_HEADS * QK_NOPE + w_rope] * cr_ref[...]
             + qq[:, N_HEADS * QK_NOPE + w_rope:] * sr_ref[...])
    for blk in range(N_HEADS * QK_NOPE // LANES):
        nt = nope[:, blk * LANES:(blk + 1) * LANES].T.astype(BF16)
        for i in range(LANES // QK_NOPE):
            q_ref[0, blk * (LANES // QK_NOPE) + i, 0:QK_NOPE, :] = nt[i * QK_NOPE:(i + 1) * QK_NOPE]
    for blk in range(w_rope // LANES):
        rt = roped[:, blk * LANES:(blk + 1) * LANES].T.astype(BF16)
        for i in range(LANES // QK_ROPE):
            q_ref[0, blk * (LANES // QK_ROPE) + i, QK_NOPE:QK_NOPE + QK_ROPE, :] = rt[i * QK_ROPE:(i + 1) * QK_ROPE]
    zero_rows = jnp.zeros((LANES - QK_NOPE - QK_ROPE, tm), BF16)
    for hd in range(N_HEADS):
        q_ref[0, hd, QK_NOPE + QK_ROPE:, :] = zero_rows

    kr_at = Q_LORA - (Q_PAD - LANES)
    kr = pltpu.roll(cq[:, Q_PAD - LANES:], (QK_NOPE - kr_at) % LANES, 1)
    klane = lax.broadcasted_iota(jnp.int32, (1, LANES), 1)
    kr_rot = jnp.where(klane < QK_NOPE + QK_ROPE // 2,
                       -pltpu.roll(kr, LANES - QK_ROPE // 2, 1), pltpu.roll(kr, QK_ROPE // 2, 1))
    k_rope = kr * ck_ref[...] + kr_rot * sk_ref[...]

    ckv = z[:, C_CKV:C_CKV + KV_LORA]
    ckvn = (_rms(ckv) * gkv_ref[...]).astype(BF16)
    kk = _dot(ckvn, wk_ref[...])
    ones_lane = (klane == QK_DIM).astype(F32)
    for hd in range(N_HEADS):
        kh = kk[:, hd * LANES:(hd + 1) * LANES] + k_rope
        k_ref[0, hd] = (kh + ones_lane).astype(BF16)
        norm2 = jnp.max(jnp.sum(kh * kh, axis=-1, keepdims=True), axis=0, keepdims=True)
        kn_ref[0, hd] = jnp.broadcast_to(norm2, (SUBLANES, LANES))
    vv = _dot(ckvn, wv_ref[...])
    ones_rows = (lax.broadcasted_iota(jnp.int32, (V_ROWS - V_HEAD, tm), 0) == 0).astype(BF16)
    for blk in range(N_HEADS * V_HEAD // LANES):
        vt = vv[:, blk * LANES:(blk + 1) * LANES].T.astype(BF16)
        for i in range(LANES // V_HEAD):
            v_ref[0, blk * (LANES // V_HEAD) + i, 0:V_HEAD, :] = vt[i * V_HEAD:(i + 1) * V_HEAD]
    for hd in range(N_HEADS):
        v_ref[0, hd, V_HEAD:, :] = ones_rows

    a = z[:, C_ZB:C_ZB + CONV_CH]
    gt = z[:, C_ZB + CONV_CH:C_ZB + 2 * CONV_CH]
    u_ref[0] = a * _sigmoid(gt)

    zc_ref[0] = z[:, C_ZC:C_ZC + W_BR]

    for g in range(N_GROUPS):
        zd = z[:, C_ZD + g * GC:C_ZD + (g + 1) * GC].astype(BF16)
        pq = _dot(zd, cs_ref[...])
        p_ref[0, :, g * GC:(g + 1) * GC] = pq[:, :GC]
        qq_ref[0, :, g * GC:(g + 1) * GC] = pq[:, GC:]


def _mixin_call(x, mod_l, g_l, lw, tabs, tm):
    B, S, D = x.shape
    tok = lambda w: pl.BlockSpec((1, tm, w), lambda b, i: (b, i, 0))
    head = pl.BlockSpec((1, N_HEADS, tm, LANES), lambda b, i: (b, 0, i, 0))
    head_t = pl.BlockSpec((1, N_HEADS, LANES, tm), lambda b, i: (b, 0, 0, i))
    tab = lambda w: pl.BlockSpec((tm, w), lambda b, i: (i, 0))
    hshape = jax.ShapeDtypeStruct((B, N_HEADS, S, LANES), BF16)
    hshape_t = jax.ShapeDtypeStruct((B, N_HEADS, LANES, S), BF16)
    return pl.pallas_call(
        functools.partial(_mixin_kernel, q_scale=tabs["q_scale"]),
        grid=(B, S // tm),
        in_specs=[tok(D),
                  pl.BlockSpec((1, N_MOD, D), lambda b, i: (b, 0, 0)),
                  _const_spec(g_l.shape),
                  _const_spec(lw["w1"].shape),
                  _const_spec(lw["gq"].shape),
                  _const_spec(lw["gkv"].shape),
                  _const_spec(lw["wq"].shape),
                  _const_spec(lw["wk"].shape),
                  _const_spec(lw["wv"].shape),
                  tab(N_HEADS * QK_ROPE), tab(N_HEADS * QK_ROPE), tab(LANES), tab(LANES),
                  _const_spec(tabs["cs"].shape)],
        out_specs=[head_t, head, pl.BlockSpec((1, N_HEADS, V_ROWS, tm), lambda b, i: (b, 0, 0, i)),
                   pl.BlockSpec((1, N_HEADS, SUBLANES, LANES), lambda b, i: (b, 0, i, 0)),
                   tok(CONV_CH), tok(W_BR), tok(W_BR), tok(W_BR)],
        out_shape=[hshape_t, hshape, jax.ShapeDtypeStruct((B, N_HEADS, V_ROWS, S), BF16),
                   jax.ShapeDtypeStruct((B, N_HEADS, (S // tm) * SUBLANES, LANES), F32),
                   jax.ShapeDtypeStruct((B, S, CONV_CH), F32),
                   jax.ShapeDtypeStruct((B, S, W_BR), F32),
                   jax.ShapeDtypeStruct((B, S, W_BR), F32),
                   jax.ShapeDtypeStruct((B, S, W_BR), F32)],
        compiler_params=_params("parallel", "parallel"),
        name="mixer_in",
    )(x, mod_l, g_l, lw["w1"], lw["gq"], lw["gkv"], lw["wq"], lw["wk"], lw["wv"],
      tabs["cosr"], tabs["sinr"], tabs["cosk"], tabs["sink"], tabs["cs"])


def _attn_kernel(qt_ref, k_ref, vt_ref, kn_ref, o_ref, acc_scr, m_scr, *, tq, tk):
    seq = k_ref.shape[2]
    nq, nk = seq // tq, seq // tk
    acc_scr[...] = jnp.zeros(acc_scr.shape, F32)
    k_norm2 = jnp.max(kn_ref[0, 0])

    def exact_tile(t, qoff):
        qt = qt_ref[0, 0, :, pl.ds(qoff, tq)]
        m_scr[...] = jnp.full(m_scr.shape, -jnp.inf, F32)
        acc_scr[t, 0:V_ROWS, :] = jnp.zeros((V_ROWS, tq), F32)

        def chunk(j, carry):
            off = pl.multiple_of(j * tk, tk)
            st = _dot(k_ref[0, 0, pl.ds(off, tk), :], qt)
            m_prev = m_scr[...]
            m_new = jnp.maximum(m_prev, jnp.max(st, axis=0, keepdims=True))
            pt = jnp.exp2(st - m_new).astype(BF16)
            pv = _dot(vt_ref[0, 0, :, pl.ds(off, tk)], pt)
            acc_scr[t, 0:V_ROWS, :] = jnp.exp2(m_prev - m_new) * acc_scr[t, 0:V_ROWS, :] + pv
            m_scr[...] = m_new
            return carry

        lax.fori_loop(0, nk, chunk, 0)

    def bounded_tile(t, qoff):
        qt = qt_ref[0, 0, :, pl.ds(qoff, tq)]
        q32 = qt.astype(F32)
        bound = jnp.sqrt(jnp.sum(q32 * q32, axis=0, keepdims=True) * k_norm2)
        first = lax.broadcasted_iota(jnp.int32, (BF16_ROWS, tq), 0) == 0
        stab = jnp.where(first, -bound, 0.0).astype(BF16)
        qs = jnp.concatenate([qt[0:QK_DIM], stab, qt[QK_DIM + BF16_ROWS:]], axis=0)
        acc = None
        for j in range(nk):
            st = _dot(k_ref[0, 0, j * tk:(j + 1) * tk, :], qs)
            pv = _dot(vt_ref[0, 0, :, j * tk:(j + 1) * tk], jnp.exp2(st).astype(BF16))
            acc = pv if acc is None else acc + pv
        acc_scr[t, 0:V_ROWS, :] = acc
        return jnp.min(acc[V_HEAD:V_HEAD + 1, :]) >= DENOM_MIN

    group = acc_scr.shape[0]

    def q_group(gi, carry):
        offs = [pl.multiple_of((gi * group + t) * tq, tq) for t in range(group)]
        ok = None
        for t in range(group):
            ok_t = bounded_tile(t, offs[t])
            ok = ok_t if ok is None else jnp.logical_and(ok, ok_t)

        @pl.when(jnp.logical_not(ok))
        def _():
            for t in range(group):
                exact_tile(t, offs[t])

        for t in range(group):
            acc = acc_scr[t]
            denom = acc[V_HEAD:V_HEAD + 1, :]
            o_ref[0, pl.ds(offs[t], tq), :] = (acc / denom).T.astype(o_ref.dtype)
        return carry

    lax.fori_loop(0, nq // group, q_group, 0)


def _attn_call(qt, k, vt, kn, tq, tk):
    B, H, S, _ = k.shape
    kern = functools.partial(_attn_kernel, tq=tq, tk=tk)
    return pl.pallas_call(
        kern,
        grid=(B, H),
        in_specs=[pl.BlockSpec((1, 1, LANES, S), lambda b, h: (b, h, 0, 0)),
                  pl.BlockSpec((1, 1, S, LANES), lambda b, h: (b, h, 0, 0)),
                  pl.BlockSpec((1, 1, V_ROWS, S), lambda b, h: (b, h, 0, 0)),
                  pl.BlockSpec((1, 1) + kn.shape[2:], lambda b, h: (b, h, 0, 0))],
        out_specs=pl.BlockSpec((1, S, LANES), lambda b, h: (b, 0, h)),
        out_shape=jax.ShapeDtypeStruct((B, S, H * LANES), BF16),
        scratch_shapes=[pltpu.VMEM((2 if (S // tq) % 2 == 0 else 1, LANES, tq), F32), pltpu.VMEM((1, tq), F32)],
        compiler_params=_params("parallel", "arbitrary"),
        name="attention",
    )(qt, k, vt, kn)


def _local_kernel(u_ref, up_ref, un_ref, z_ref, zp_ref, zn_ref, cw_ref, cb_ref, lg_ref, lb_ref,
                  pw_ref, ps_ref, oc_ref, op_ref, ubuf, zbuf, ush, *, ts, seq, rc, rp):
    i = pl.program_id(1)
    has_prev = i > 0
    has_next = i < pl.num_programs(1) - 1
    ubuf[0:HALO, :] = jnp.where(has_prev, up_ref[0], 0.0)
    ubuf[HALO:HALO + ts, :] = u_ref[0]
    ubuf[HALO + ts:, :] = jnp.where(has_next, un_ref[0], 0.0)
    zbuf[0:HALO, :] = jnp.where(has_prev, zp_ref[0], 0.0)
    zbuf[HALO:HALO + ts, :] = z_ref[0]
    zbuf[HALO + ts:, :] = jnp.where(has_next, zn_ref[0], 0.0)

    span = ts + 2 * HALO - SUBLANES
    for s in range(1, SUBLANES):
        ush[s - 1, 0:span, :] = ubuf[s:s + span, :]

    pad = CONV_WIDTH // 2
    for r in range(0, ts, rc):
        acc = jnp.zeros((rc, CONV_CH), F32)
        for kk in range(CONV_WIDTH):
            st = HALO + r + kk - pad
            s, base = st % SUBLANES, st - st % SUBLANES
            tap = ubuf[base:base + rc, :] if s == 0 else ush[s - 1, base:base + rc, :]
            acc = acc + tap * cw_ref[kk:kk + 1, :]
        y = acc + cb_ref[...]
        mu = jnp.mean(y, axis=-1, keepdims=True)
        yc = y - mu
        yn = yc * lax.rsqrt(jnp.mean(yc * yc, axis=-1, keepdims=True) + EPS)
        yn = yn * lg_ref[...] + lb_ref[...]
        oc_ref[0, r:r + rc, :] = (yn * _sigmoid(yn)).astype(oc_ref.dtype)

    for r in range(0, ts, rp):
        t = i * ts + r + lax.broadcasted_iota(jnp.int32, (rp, 1), 0)
        for g, w in enumerate(POOL_WINDOWS):
            lo = w // 2
            hi = w - 1 - lo
            cols = slice(g * GC, (g + 1) * GC)
            win = jnp.zeros((rp, GC), F32)
            for j in range(-lo, hi + 1):
                st = HALO + r + j
                win = win + zbuf[st:st + rp, cols]
            cnt = (jnp.minimum(t + hi + 1, seq) - jnp.maximum(t - lo, 0)).astype(F32)
            d = win / cnt - zbuf[HALO + r:HALO + r + rp, cols]
            yg = _dot(d.astype(BF16), pw_ref[g]) * ps_ref[:, cols]
            op_ref[0, r:r + rp, cols] = yg.astype(op_ref.dtype)


def _local_call(u, zc, lw, ts):
    B, S, _ = u.shape
    nh = ts // HALO
    last = S // HALO - 1
    cur = lambda w: pl.BlockSpec((1, ts, w), lambda b, i: (b, i, 0))
    prev = lambda w: pl.BlockSpec((1, HALO, w), lambda b, i: (b, jnp.maximum(i * nh - 1, 0), 0))
    nxt = lambda w: pl.BlockSpec((1, HALO, w), lambda b, i: (b, jnp.minimum((i + 1) * nh, last), 0))
    kern = functools.partial(_local_kernel, ts=ts, seq=S, rc=64, rp=min(ts, 256))
    return pl.pallas_call(
        kern,
        grid=(B, S // ts),
        in_specs=[cur(CONV_CH), prev(CONV_CH), nxt(CONV_CH), cur(W_BR), prev(W_BR), nxt(W_BR),
                  _const_spec(lw["conv_w"].shape), _const_spec(lw["conv_b"].shape),
                  _const_spec(lw["ln_g"].shape), _const_spec(lw["ln_b"].shape),
                  _const_spec(lw["pool_w"].shape), _const_spec(lw["pool_scale"].shape)],
        out_specs=[cur(CONV_CH), cur(W_BR)],
        out_shape=[jax.ShapeDtypeStruct((B, S, CONV_CH), BF16),
                   jax.ShapeDtypeStruct((B, S, W_BR), BF16)],
        scratch_shapes=[pltpu.VMEM((ts + 2 * HALO, CONV_CH), F32),
                        pltpu.VMEM((ts + 2 * HALO, W_BR), F32),
                        pltpu.VMEM((SUBLANES - 1, ts + 2 * HALO, CONV_CH), F32)],
        compiler_params=_params("parallel", "parallel"),
        name="conv_pool",
    )(u, u, u, zc, zc, zc, lw["conv_w"], lw["conv_b"], lw["ln_g"], lw["ln_b"],
      lw["pool_w"], lw["pool_scale"])


def _fft1_kernel(p_ref, q_ref, f_ref, o_ref):
    for j in range(p_ref.shape[2]):
        rhs = jnp.concatenate([p_ref[0, :, j, :], q_ref[0, :, j, :]], axis=0).astype(BF16)
        o_ref[0, :, j, :] = _dot(f_ref[...], rhs)


def _fft1_call(p, q, tabs):
    B, S, W = p.shape
    n2 = S // FFT_N1
    pv = p.reshape(B, FFT_N1, n2, W)
    qv = q.reshape(B, FFT_N1, n2, W)
    blk = pl.BlockSpec((1, FFT_N1, SUBLANES, W), lambda b, j: (b, 0, j, 0))
    return pl.pallas_call(
        _fft1_kernel,
        grid=(B, n2 // SUBLANES),
        in_specs=[blk, blk, _const_spec(tabs["f1"].shape)],
        out_specs=pl.BlockSpec((1, 2 * FFT_N1, SUBLANES, W), lambda b, j: (b, 0, j, 0)),
        out_shape=jax.ShapeDtypeStruct((B, 2 * FFT_N1, n2, W), F32),
        compiler_params=_params("parallel", "parallel"),
        name="fft_stage1",
    )(pv, qv, tabs["f1"])


def _fft2_kernel(a_ref, gc_ref, gs_ref, o_ref, *, kb):
    for j in range(kb):
        f = _dot(gc_ref[j], a_ref[0, 0, j].astype(BF16)) + _dot(gs_ref[j], a_ref[0, 1, j].astype(BF16))
        o_ref[0, :, j, :] = f


def _fft2_call(a, tabs, seq, kb):
    B = a.shape[0]
    n2 = seq // FFT_N1
    av = a.reshape(B, 2, FFT_N1, n2, W_BR)
    kern = functools.partial(_fft2_kernel, kb=kb)
    out = pl.pallas_call(
        kern,
        grid=(B, FFT_N1 // kb),
        in_specs=[pl.BlockSpec((1, 2, kb, n2, W_BR), lambda b, j: (b, 0, j, 0, 0)),
                  pl.BlockSpec((kb, n2, n2), lambda b, j: (j, 0, 0)),
                  pl.BlockSpec((kb, n2, n2), lambda b, j: (j, 0, 0))],
        out_specs=pl.BlockSpec((1, n2, kb, W_BR), lambda b, j: (b, 0, j, 0)),
        out_shape=jax.ShapeDtypeStruct((B, n2, FFT_N1, W_BR), F32),
        compiler_params=_params("parallel", "parallel"),
        name="fft_stage2",
    )(av, tabs["g2c"], tabs["g2s"])
    return out.reshape(B, seq, W_BR)


def _mixout_kernel(x_ref, mod_ref, g_ref, wg_ref, oa_ref, wa_ref, cb_ref, wb_ref, pc_ref, wc_ref,
                   f_ref, wd_ref, wo_ref, o_ref, *, parts):
    shift = mod_ref[0, 3:4, :]
    scale = mod_ref[0, 4:5, :]
    gate = mod_ref[0, 5:6, :]
    d = x_ref.shape[-1]
    rows = x_ref.shape[1] // parts
    hbs = []
    for p in range(parts):
        x = x_ref[0, p * rows:(p + 1) * rows, :]
        hbs.append(((_rms(x) * g_ref[2:3, :]) * (1.0 + scale) + shift).astype(BF16))
    ys = []
    for p in range(parts):
        sl = slice(p * rows, (p + 1) * rows)
        merged = None
        for br, (b_ref, w_ref) in enumerate(((oa_ref, wa_ref), (cb_ref, wb_ref), (pc_ref, wc_ref), (f_ref, wd_ref))):
            gl = _sigmoid(_dot(hbs[p], wg_ref[:, br * d:(br + 1) * d]))
            term = gl * _dot(b_ref[0, sl, :].astype(BF16), w_ref[...])
            merged = term if merged is None else merged + term
        ys.append(_dot(merged.astype(BF16), wo_ref[...]))
    for p in range(parts):
        sl = slice(p * rows, (p + 1) * rows)
        o_ref[0, sl, :] = x_ref[0, sl, :] + gate * (_rms(ys[p]) * g_ref[3:4, :])


def _mixout_call(x, mod_l, g_l, lw, oa, cb, pc, f, tm):
    B, S, D = x.shape
    tok = lambda w: pl.BlockSpec((1, tm, w), lambda b, i: (b, i, 0))
    return pl.pallas_call(
        functools.partial(_mixout_kernel, parts=2 if tm % 512 == 0 else 1),
        grid=(B, S // tm),
        in_specs=[tok(D),
                  pl.BlockSpec((1, N_MOD, D), lambda b, i: (b, 0, 0)),
                  _const_spec(g_l.shape),
                  _const_spec(lw["wg"].shape),
                  tok(N_HEADS * LANES), _const_spec(lw["wa"].shape),
                  tok(CONV_CH), _const_spec(lw["wb"].shape),
                  tok(W_BR), _const_spec(lw["wc"].shape),
                  tok(W_BR), _const_spec(lw["wd"].shape),
                  _const_spec(lw["wo"].shape)],
        out_specs=tok(D),
        out_shape=jax.ShapeDtypeStruct(x.shape, F32),
        compiler_params=_params("parallel", "parallel"),
        name="mixer_out",
    )(x, mod_l, g_l, lw["wg"], oa, lw["wa"], cb, lw["wb"], pc, lw["wc"], f, lw["wd"], lw["wo"])


def _rot_half_cols(w):
    half = w.shape[-1] // 2
    return jnp.concatenate([-w[..., half:], w[..., :half]], axis=-1)


def _tables(seq):
    pos = jnp.arange(seq, dtype=F32)
    inv = ROPE_THETA ** (-jnp.arange(0, QK_ROPE, 2, dtype=F32) / QK_ROPE)
    ang = pos[:, None] * inv[None, :]
    cos, sin = jnp.cos(ang), jnp.sin(ang)
    cc = jnp.concatenate([cos, cos], axis=-1)
    ss = jnp.concatenate([sin, sin], axis=-1)
    z64 = jnp.zeros((seq, QK_NOPE), F32)
    z32 = jnp.zeros((seq, LANES - QK_NOPE - QK_ROPE), F32)
    sm_scale = (QK_NOPE + QK_ROPE) ** -0.5 * math.log2(math.e)
    cosk = jnp.concatenate([z64, cc, z32], axis=-1)
    sink = jnp.concatenate([z64, ss, z32], axis=-1)
    cosr = jnp.tile(cc, (1, N_HEADS)) * sm_scale
    sinr = jnp.tile(ss, (1, N_HEADS)) * sm_scale

    def cos_sin(m, period):
        th = m.astype(F32) * (2.0 * math.pi / period)
        return jnp.cos(th), jnp.sin(th)

    def dft(n):
        a = jnp.arange(n, dtype=jnp.int32)
        return cos_sin((a[:, None] * a[None, :]) % n, n)

    c_ch, s_ch = dft(GC)
    cs = (jnp.concatenate([c_ch, s_ch], axis=-1) * GC ** -0.5).astype(BF16)

    n1 = FFT_N1
    n2 = seq // n1
    c1, s1 = dft(n1)
    f1 = (jnp.concatenate([jnp.concatenate([c1, -s1], axis=1),
                           jnp.concatenate([-s1, -c1], axis=1)], axis=0) * n1 ** -0.5).astype(BF16)
    k1 = jnp.arange(n1, dtype=jnp.int32)[:, None, None]
    k2 = jnp.arange(n2, dtype=jnp.int32)[None, :, None]
    t2 = jnp.arange(n2, dtype=jnp.int32)[None, None, :]
    m = ((k1 + n1 * k2) * t2) % seq
    c2, s2 = cos_sin(m, seq)
    g2c = (c2 * n2 ** -0.5).astype(BF16)
    g2s = (s2 * n2 ** -0.5).astype(BF16)
    return dict(cosr=cosr, sinr=sinr, q_scale=sm_scale, cosk=cosk, sink=sink, cs=cs, f1=f1, g2c=g2c, g2s=g2s)


def _layer_weights(l, w_in, q_norm_g, w_uq, kv_norm_g, w_ukv, w_a, conv_w, conv_b, conv_ln_g,
                   conv_ln_b, w_b, pool_w, pool_scale, w_c, w_d, w_out):
    D = w_in.shape[1]
    wi = w_in[l]
    assert Q_PAD - Q_LORA == QK_ROPE
    w1 = jnp.concatenate([wi[:, Q_LORA:Q_LORA + KV_LORA], wi[:, :Q_LORA], wi[:, Q_LORA + KV_LORA:4 * W_BR]],
                         axis=-1).astype(BF16)

    uq = w_uq[l].reshape(Q_LORA, N_HEADS, QK_NOPE + QK_ROPE)
    nope, rope = uq[..., :QK_NOPE], uq[..., QK_NOPE:]
    wq = jnp.concatenate([nope.reshape(Q_LORA, N_HEADS * QK_NOPE), rope.reshape(Q_LORA, N_HEADS * QK_ROPE),
                          _rot_half_cols(rope).reshape(Q_LORA, N_HEADS * QK_ROPE)], axis=-1)
    wq = jnp.pad(wq, ((0, Q_PAD - Q_LORA), (0, 0))).astype(BF16)

    ukv = w_ukv[l].reshape(KV_LORA, N_HEADS, QK_NOPE + V_HEAD)
    zk = jnp.zeros((KV_LORA, N_HEADS, LANES - QK_NOPE), F32)
    wk = jnp.concatenate([ukv[..., :QK_NOPE], zk], axis=-1).reshape(KV_LORA, N_HEADS * LANES).astype(BF16)
    wv = ukv[..., QK_NOPE:].reshape(KV_LORA, N_HEADS * V_HEAD).astype(BF16)

    wa = jnp.pad(w_a[l].reshape(N_HEADS, V_HEAD, D), ((0, 0), (0, LANES - V_HEAD), (0, 0)))
    wa = wa.reshape(N_HEADS * LANES, D).astype(BF16)
    gq = jnp.pad(q_norm_g[l], (0, Q_PAD - Q_LORA)).reshape(1, Q_PAD)
    return dict(
        w1=w1, wq=wq, wk=wk, wv=wv, gq=gq, gkv=kv_norm_g[l].reshape(1, KV_LORA),
        wg=wi[:, 4 * W_BR:].astype(BF16), wa=wa,
        wb=w_b[l].astype(BF16), wc=w_c[l].astype(BF16), wd=w_d[l].astype(BF16), wo=w_out[l].astype(BF16),
        conv_w=conv_w[l], conv_b=conv_b[l].reshape(1, CONV_CH),
        ln_g=conv_ln_g[l].reshape(1, CONV_CH), ln_b=conv_ln_b[l].reshape(1, CONV_CH),
        pool_w=pool_w[l].astype(BF16), pool_scale=pool_scale[l].reshape(1, W_BR))


def _tiles(seq):
    t = lambda n: min(n, seq)
    return dict(ffn=t(1024), mix=t(512), mixout=t(512), tq=t(1024), tk=t(1024), local=t(512))


def kernel(x, c, ada_w, ada_b, norm_g, ffn1_w_in, ffn1_w_out, ffn2_w_in, ffn2_w_out, w_in, q_norm_g,
           w_uq, kv_norm_g, w_ukv, w_a, conv_w, conv_b, conv_ln_g, conv_ln_b, w_b, pool_w, pool_scale,
           w_c, w_d, w_out):
    B, S, D = x.shape
    L = ada_w.shape[0]
    assert D == D_MODEL and S % (FFT_N1 * 16) == 0
    ts = _tiles(S)
    tabs = _tables(S)
    mod = _mod_call(c, ada_w, ada_b).reshape(L, B, N_MOD, D)
    for l in range(L):
        lw = _layer_weights(l, w_in, q_norm_g, w_uq, kv_norm_g, w_ukv, w_a, conv_w, conv_b, conv_ln_g,
                            conv_ln_b, w_b, pool_w, pool_scale, w_c, w_d, w_out)
        mod_l, g_l = mod[l], norm_g[l]
        x = _ffn_call(x, mod_l, g_l, ffn1_w_in[l].astype(BF16), ffn1_w_out[l].astype(BF16), 0, ts["ffn"])
        q, k, v, kn, u, zc, p, qq = _mixin_call(x, mod_l, g_l, lw, tabs, ts["mix"])
        oa = _attn_call(q, k, v, kn, ts["tq"], ts["tk"])
        cb, pc = _local_call(u, zc, lw, ts["local"])
        a = _fft1_call(p, qq, tabs)
        f = _fft2_call(a, tabs, S, 8)
        x = _mixout_call(x, mod_l, g_l, lw, oa, cb, pc, f, ts["mixout"])
        x = _ffn_call(x, mod_l, g_l, ffn2_w_in[l].astype(BF16), ffn2_w_out[l].astype(BF16), 2, ts["ffn"])
    return x
```

```python
import functools
import math

import jax
import jax.numpy as jnp
import numpy as np
from jax import lax
from jax.experimental import pallas as pl
from jax.experimental.pallas import tpu as pltpu

F32 = jnp.float32
BF16 = jnp.bfloat16

D_MODEL = 1024
D_FF = 2816
N_HEADS = 8
Q_LORA = 352
KV_LORA = 128
QK_NOPE = 64
QK_ROPE = 32
V_HEAD = 64
QK_DIM = QK_NOPE + QK_ROPE
BF16_ROWS = 16
V_ROWS = 80
DENOM_MIN = 2.0 ** -80
ROPE_THETA = 10000.0
CONV_CH = 256
CONV_WIDTH = 31
POOL_WINDOWS = (2, 4, 8, 16)
GC = 128
N_GROUPS = 4
W_BR = 512
N_MOD = 9
EPS = 1e-6

LANES = 128
SUBLANES = 8
Q_PAD = 384
HALO = 16
FFT_N1 = 128
VMEM_LIMIT = 56 * 1024 * 1024

C_CKV = 0
C_CQ = C_CKV + KV_LORA
C_ZB = C_CQ + Q_PAD
C_ZC = C_ZB + W_BR
C_ZD = C_ZC + W_BR
W1_COLS = C_ZD + W_BR


def _params(*sem):
    return pltpu.CompilerParams(dimension_semantics=sem, vmem_limit_bytes=VMEM_LIMIT)


def _const_spec(shape):
    zeros = (0,) * len(shape)
    return pl.BlockSpec(shape, lambda *_: zeros, pipeline_mode=pl.Buffered(1))


def _dot(a, b):
    return jnp.dot(a, b, preferred_element_type=F32)


def _rms(x):
    return x * lax.rsqrt(jnp.mean(x * x, axis=-1, keepdims=True) + EPS)


def _sigmoid(x):
    return 1.0 / (1.0 + jnp.exp(-x))


def _mod_kernel(c_ref, w_ref, b_ref, o_ref):
    c = c_ref[...]
    ca = c * _sigmoid(c)
    o_ref[0] = jnp.dot(ca, w_ref[0], preferred_element_type=F32,
                       precision=lax.Precision.HIGHEST) + b_ref[0]


def _mod_call(c, ada_w, ada_b):
    L, D, N = ada_w.shape
    B = c.shape[0]
    tn = 1152
    return pl.pallas_call(
        _mod_kernel,
        grid=(L, N // tn),
        in_specs=[pl.BlockSpec((B, D), lambda l, j: (0, 0)),
                  pl.BlockSpec((1, D, tn), lambda l, j: (l, 0, j)),
                  pl.BlockSpec((1, 1, tn), lambda l, j: (l, 0, j))],
        out_specs=pl.BlockSpec((1, B, tn), lambda l, j: (l, 0, j)),
        out_shape=jax.ShapeDtypeStruct((L, B, N), F32),
        compiler_params=_params("parallel", "parallel"),
        name="adaln_mod",
    )(c, ada_w, ada_b.reshape(L, 1, N))


def _ffn_kernel(x_ref, mod_ref, g_ref, win_ref, wout_ref, o_ref, a_scr, *, sub, chunk, parts):
    shift = mod_ref[0, 3 * sub:3 * sub + 1, :]
    scale = mod_ref[0, 3 * sub + 1:3 * sub + 2, :]
    gate = mod_ref[0, 3 * sub + 2:3 * sub + 3, :]
    g_in = g_ref[2 * sub:2 * sub + 1, :]
    g_out = g_ref[2 * sub + 1:2 * sub + 2, :]
    rows = x_ref.shape[1] // parts
    hbs = []
    for p in range(parts):
        x = x_ref[0, p * rows:(p + 1) * rows, :]
        hbs.append(((_rms(x) * g_in) * (1.0 + scale) + shift).astype(BF16))
    for p in range(parts):
        for j in range(D_FF // chunk):
            gj = _dot(hbs[p], win_ref[:, j * chunk:(j + 1) * chunk])
            uj = _dot(hbs[p], win_ref[:, D_FF + j * chunk:D_FF + (j + 1) * chunk])
            a_scr[p, :, j * chunk:(j + 1) * chunk] = (gj * _sigmoid(gj) * uj).astype(BF16)
    for p in range(parts):
        y = _dot(a_scr[p], wout_ref[...])
        x = x_ref[0, p * rows:(p + 1) * rows, :]
        o_ref[0, p * rows:(p + 1) * rows, :] = x + (0.5 * gate) * (_rms(y) * g_out)


def _ffn_call(x, mod_l, g_l, w_in, w_out, sub, tm):
    B, S, D = x.shape
    parts = 4 if tm % 1024 == 0 else 1
    kern = functools.partial(_ffn_kernel, sub=sub, chunk=256, parts=parts)
    return pl.pallas_call(
        kern,
        grid=(B, S // tm),
        in_specs=[pl.BlockSpec((1, tm, D), lambda b, i: (b, i, 0)),
                  pl.BlockSpec((1, N_MOD, D), lambda b, i: (b, 0, 0)),
                  _const_spec(g_l.shape),
                  _const_spec(w_in.shape),
                  _const_spec(w_out.shape)],
        out_specs=pl.BlockSpec((1, tm, D), lambda b, i: (b, i, 0)),
        out_shape=jax.ShapeDtypeStruct(x.shape, F32),
        scratch_shapes=[pltpu.VMEM((parts, tm // parts, D_FF), BF16)],
        compiler_params=_params("parallel", "parallel"),
        name=f"ffn{sub}",
    )(x, mod_l, g_l, w_in, w_out)


def _mixin_kernel(x_ref, mod_ref, g_ref, w1_ref, gq_ref, gkv_ref, wq_ref, wk_ref, wv_ref,
                  cr_ref, sr_ref, ck_ref, sk_ref, cs_ref,
                  q_ref, k_ref, v_ref, kn_ref, u_ref, zc_ref, p_ref, qq_ref, *, q_scale):
    x = x_ref[0]
    tm = x.shape[0]
    shift = mod_ref[0, 3:4, :]
    scale = mod_ref[0, 4:5, :]
    h = (_rms(x) * g_ref[2:3, :]) * (1.0 + scale) + shift
    hb = h.astype(BF16)
    z = _dot(hb, w1_ref[...])

    cq = z[:, C_CQ:C_CQ + Q_PAD]
    qlane = lax.broadcasted_iota(jnp.int32, (1, Q_PAD), 1)
    cqm = jnp.where(qlane < Q_LORA, cq, 0.0)
    cqn = cqm * lax.rsqrt(jnp.sum(cqm * cqm, axis=-1, keepdims=True) * (1.0 / Q_LORA) + EPS)
    cqn = (cqn * gq_ref[...]).astype(BF16)
    qq = _dot(cqn, wq_ref[...])
    nope = qq[:, :N_HEADS * QK_NOPE] * q_scale
    w_rope = N_HEADS * QK_ROPE
    roped = (qq[:, N_HEADS * QK_NOPE:N_HEADS * QK_NOPE + w_rope] * cr_ref[...]
             + qq[:, N_HEADS * QK_NOPE + w_rope:] * sr_ref[...])
    for blk in range(N_HEADS * QK_NOPE // LANES):
        nt = nope[:, blk * LANES:(blk + 1) * LANES].T.astype(BF16)
        for i in range(LANES // QK_NOPE):
            q_ref[0, blk * (LANES // QK_NOPE) + i, 0:QK_NOPE, :] = nt[i * QK_NOPE:(i + 1) * QK_NOPE]
    for blk in range(w_rope // LANES):
        rt = roped[:, blk * LANES:(blk + 1) * LANES].T.astype(BF16)
        for i in range(LANES // QK_ROPE):
            q_ref[0, blk * (LANES // QK_ROPE) + i, QK_NOPE:QK_NOPE + QK_ROPE, :] = rt[i * QK_ROPE:(i + 1) * QK_ROPE]
    zero_rows = jnp.zeros((LANES - QK_NOPE - QK_ROPE, tm), BF16)
    for hd in range(N_HEADS):
        q_ref[0, hd, QK_NOPE + QK_ROPE:, :] = zero_rows

    kr_at = Q_LORA - (Q_PAD - LANES)
    kr = pltpu.roll(cq[:, Q_PAD - LANES:], (QK_NOPE - kr_at) % LANES, 1)
    klane = lax.broadcasted_iota(jnp.int32, (1, LANES), 1)
    kr_rot = jnp.where(klane < QK_NOPE + QK_ROPE // 2,
                       -pltpu.roll(kr, LANES - QK_ROPE // 2, 1), pltpu.roll(kr, QK_ROPE // 2, 1))
    k_rope = kr * ck_ref[...] + kr_rot * sk_ref[...]

    ckv = z[:, C_CKV:C_CKV + KV_LORA]
    ckvn = (_rms(ckv) * gkv_ref[...]).astype(BF16)
    kk = _dot(ckvn, wk_ref[...])
    ones_lane = (klane == QK_DIM).astype(F32)
    for hd in range(N_HEADS):
        kh = kk[:, hd * LANES:(hd + 1) * LANES] + k_rope
        k_ref[0, hd] = (kh + ones_lane).astype(BF16)
        norm2 = jnp.max(jnp.sum(kh * kh, axis=-1, keepdims=True), axis=0, keepdims=True)
        kn_ref[0, hd] = jnp.broadcast_to(norm2, (SUBLANES, LANES))
    vv = _dot(ckvn, wv_ref[...])
    ones_rows = (lax.broadcasted_iota(jnp.int32, (V_ROWS - V_HEAD, tm), 0) == 0).astype(BF16)
    for blk in range(N_HEADS * V_HEAD // LANES):
        vt = vv[:, blk * LANES:(blk + 1) * LANES].T.astype(BF16)
        for i in range(LANES // V_HEAD):
            v_ref[0, blk * (LANES // V_HEAD) + i, 0:V_HEAD, :] = vt[i * V_HEAD:(i + 1) * V_HEAD]
    for hd in range(N_HEADS):
        v_ref[0, hd, V_HEAD:, :] = ones_rows

    a = z[:, C_ZB:C_ZB + CONV_CH]
    gt = z[:, C_ZB + CONV_CH:C_ZB + 2 * CONV_CH]
    u_ref[0] = a * _sigmoid(gt)

    zc_ref[0] = z[:, C_ZC:C_ZC + W_BR]

    for g in range(N_GROUPS):
        zd = z[:, C_ZD + g * GC:C_ZD + (g + 1) * GC].astype(BF16)
        pq = _dot(zd, cs_ref[...])
        p_ref[0, :, g * GC:(g + 1) * GC] = pq[:, :GC]
        qq_ref[0, :, g * GC:(g + 1) * GC] = pq[:, GC:]


def _mixin_call(x, mod_l, g_l, lw, tabs, tm):
    B, S, D = x.shape
    tok = lambda w: pl.BlockSpec((1, tm, w), lambda b, i: (b, i, 0))
    head = pl.BlockSpec((1, N_HEADS, tm, LANES), lambda b, i: (b, 0, i, 0))
    head_t = pl.BlockSpec((1, N_HEADS, LANES, tm), lambda b, i: (b, 0, 0, i))
    tab = lambda w: pl.BlockSpec((tm, w), lambda b, i: (i, 0))
    hshape = jax.ShapeDtypeStruct((B, N_HEADS, S, LANES), BF16)
    hshape_t = jax.ShapeDtypeStruct((B, N_HEADS, LANES, S), BF16)
    return pl.pallas_call(
        functools.partial(_mixin_kernel, q_scale=tabs["q_scale"]),
        grid=(B, S // tm),
        in_specs=[tok(D),
                  pl.BlockSpec((1, N_MOD, D), lambda b, i: (b, 0, 0)),
                  _const_spec(g_l.shape),
                  _const_spec(lw["w1"].shape),
                  _const_spec(lw["gq"].shape),
                  _const_spec(lw["gkv"].shape),
                  _const_spec(lw["wq"].shape),
                  _const_spec(lw["wk"].shape),
                  _const_spec(lw["wv"].shape),
                  tab(N_HEADS * QK_ROPE), tab(N_HEADS * QK_ROPE), tab(LANES), tab(LANES),
                  _const_spec(tabs["cs"].shape)],
        out_specs=[head_t, head, pl.BlockSpec((1, N_HEADS, V_ROWS, tm), lambda b, i: (b, 0, 0, i)),
                   pl.BlockSpec((1, N_HEADS, SUBLANES, LANES), lambda b, i: (b, 0, i, 0)),
                   tok(CONV_CH), tok(W_BR), tok(W_BR), tok(W_BR)],
        out_shape=[hshape_t, hshape, jax.ShapeDtypeStruct((B, N_HEADS, V_ROWS, S), BF16),
                   jax.ShapeDtypeStruct((B, N_HEADS, (S // tm) * SUBLANES, LANES), F32),
                   jax.ShapeDtypeStruct((B, S, CONV_CH), F32),
                   jax.ShapeDtypeStruct((B, S, W_BR), F32),
                   jax.ShapeDtypeStruct((B, S, W_BR), F32),
                   jax.ShapeDtypeStruct((B, S, W_BR), F32)],
        compiler_params=_params("parallel", "parallel"),
        name="mixer_in",
    )(x, mod_l, g_l, lw["w1"], lw["gq"], lw["gkv"], lw["wq"], lw["wk"], lw["wv"],
      tabs["cosr"], tabs["sinr"], tabs["cosk"], tabs["sink"], tabs["cs"])


def _attn_kernel(qt_ref, k_ref, vt_ref, kn_ref, o_ref, acc_scr, m_scr, *, tq, tk):
    seq = k_ref.shape[2]
    nq, nk = seq // tq, seq // tk
    k_norm2 = jnp.max(kn_ref[0, 0])

    def exact_tile(t, qoff):
        qt = qt_ref[0, 0, :, pl.ds(qoff, tq)]
        m_scr[...] = jnp.full(m_scr.shape, -jnp.inf, F32)
        acc_scr[t, 0:V_ROWS, :] = jnp.zeros((V_ROWS, tq), F32)

        def chunk(j, carry):
            off = pl.multiple_of(j * tk, tk)
            st = _dot(k_ref[0, 0, pl.ds(off, tk), :], qt)
            m_prev = m_scr[...]
            m_new = jnp.maximum(m_prev, jnp.max(st, axis=0, keepdims=True))
            pt = jnp.exp2(st - m_new).astype(BF16)
            pv = _dot(vt_ref[0, 0, :, pl.ds(off, tk)], pt)
            acc_scr[t, 0:V_ROWS, :] = jnp.exp2(m_prev - m_new) * acc_scr[t, 0:V_ROWS, :] + pv
            m_scr[...] = m_new
            return carry

        lax.fori_loop(0, nk, chunk, 0)

    def bounded_tile(t, qoff):
        qt = qt_ref[0, 0, :, pl.ds(qoff, tq)]
        q32 = qt.astype(F32)
        bound = jnp.sqrt(jnp.sum(q32 * q32, axis=0, keepdims=True) * k_norm2)
        first = lax.broadcasted_iota(jnp.int32, (BF16_ROWS, tq), 0) == 0
        stab = jnp.where(first, -bound, 0.0).astype(BF16)
        qs = jnp.concatenate([qt[0:QK_DIM], stab, qt[QK_DIM + BF16_ROWS:]], axis=0)
        acc = None
        for j in range(nk):
            st = _dot(k_ref[0, 0, j * tk:(j + 1) * tk, :], qs)
            pv = _dot(vt_ref[0, 0, :, j * tk:(j + 1) * tk], jnp.exp2(st).astype(BF16))
            acc = pv if acc is None else acc + pv
        acc_scr[t, 0:V_ROWS, :] = acc
        return jnp.min(acc[V_HEAD:V_HEAD + 1, :]) >= DENOM_MIN

    group = acc_scr.shape[0]

    def q_group(gi, carry):
        offs = [pl.multiple_of((gi * group + t) * tq, tq) for t in range(group)]
        ok = None
        for t in range(group):
            ok_t = bounded_tile(t, offs[t])
            ok = ok_t if ok is None else jnp.logical_and(ok, ok_t)

        @pl.when(jnp.logical_not(ok))
        def _():
            for t in range(group):
                exact_tile(t, offs[t])

        for t in range(group):
            denom = acc_scr[t, V_HEAD:V_HEAD + 1, :]
            o_ref[0, :, pl.ds(offs[t], tq)] = (acc_scr[t, 0:V_HEAD, :] / denom).astype(o_ref.dtype)
        return carry

    lax.fori_loop(0, nq // group, q_group, 0)


def _attn_call(qt, k, vt, kn, tq, tk):
    B, H, S, _ = k.shape
    kern = functools.partial(_attn_kernel, tq=tq, tk=tk)
    return pl.pallas_call(
        kern,
        grid=(B, H),
        in_specs=[pl.BlockSpec((1, 1, LANES, S), lambda b, h: (b, h, 0, 0)),
                  pl.BlockSpec((1, 1, S, LANES), lambda b, h: (b, h, 0, 0)),
                  pl.BlockSpec((1, 1, V_ROWS, S), lambda b, h: (b, h, 0, 0)),
                  pl.BlockSpec((1, 1) + kn.shape[2:], lambda b, h: (b, h, 0, 0))],
        out_specs=pl.BlockSpec((1, V_HEAD, S), lambda b, h: (b, h, 0)),
        out_shape=jax.ShapeDtypeStruct((B, H * V_HEAD, S), BF16),
        scratch_shapes=[pltpu.VMEM((2 if (S // tq) % 2 == 0 else 1, V_ROWS, tq), F32), pltpu.VMEM((1, tq), F32)],
        compiler_params=_params("parallel", "arbitrary"),
        name="attention",
    )(qt, k, vt, kn)


def _local_kernel(u_ref, up_ref, un_ref, z_ref, zp_ref, zn_ref, cw_ref, cb_ref, lg_ref, lb_ref,
                  pw_ref, ps_ref, oc_ref, op_ref, ubuf, zbuf, ush, *, ts, seq, rc, rp):
    i = pl.program_id(1)
    has_prev = i > 0
    has_next = i < pl.num_programs(1) - 1
    ubuf[0:HALO, :] = jnp.where(has_prev, up_ref[0], 0.0)
    ubuf[HALO:HALO + ts, :] = u_ref[0]
    ubuf[HALO + ts:, :] = jnp.where(has_next, un_ref[0], 0.0)
    zbuf[0:HALO, :] = jnp.where(has_prev, zp_ref[0], 0.0)
    zbuf[HALO:HALO + ts, :] = z_ref[0]
    zbuf[HALO + ts:, :] = jnp.where(has_next, zn_ref[0], 0.0)

    span = ts + 2 * HALO - SUBLANES
    for s in range(1, SUBLANES):
        ush[s - 1, 0:span, :] = ubuf[s:s + span, :]

    pad = CONV_WIDTH // 2
    for r in range(0, ts, rc):
        acc = jnp.zeros((rc, CONV_CH), F32)
        for kk in range(CONV_WIDTH):
            st = HALO + r + kk - pad
            s, base = st % SUBLANES, st - st % SUBLANES
            tap = ubuf[base:base + rc, :] if s == 0 else ush[s - 1, base:base + rc, :]
            acc = acc + tap * cw_ref[kk:kk + 1, :]
        y = acc + cb_ref[...]
        mu = jnp.mean(y, axis=-1, keepdims=True)
        yc = y - mu
        yn = yc * lax.rsqrt(jnp.mean(yc * yc, axis=-1, keepdims=True) + EPS)
        yn = yn * lg_ref[...] + lb_ref[...]
        oc_ref[0, r:r + rc, :] = (yn * _sigmoid(yn)).astype(oc_ref.dtype)

    for r in range(0, ts, rp):
        t = i * ts + r + lax.broadcasted_iota(jnp.int32, (rp, 1), 0)
        for g, w in enumerate(POOL_WINDOWS):
            lo = w // 2
            hi = w - 1 - lo
            cols = slice(g * GC, (g + 1) * GC)
            win = jnp.zeros((rp, GC), F32)
            for j in range(-lo, hi + 1):
                st = HALO + r + j
                win = win + zbuf[st:st + rp, cols]
            cnt = (jnp.minimum(t + hi + 1, seq) - jnp.maximum(t - lo, 0)).astype(F32)
            d = win / cnt - zbuf[HALO + r:HALO + r + rp, cols]
            yg = _dot(d.astype(BF16), pw_ref[g]) * ps_ref[:, cols]
            op_ref[0, r:r + rp, cols] = yg.astype(op_ref.dtype)


def _local_call(u, zc, lw, ts):
    B, S, _ = u.shape
    nh = ts // HALO
    last = S // HALO - 1
    cur = lambda w: pl.BlockSpec((1, ts, w), lambda b, i: (b, i, 0))
    prev = lambda w: pl.BlockSpec((1, HALO, w), lambda b, i: (b, jnp.maximum(i * nh - 1, 0), 0))
    nxt = lambda w: pl.BlockSpec((1, HALO, w), lambda b, i: (b, jnp.minimum((i + 1) * nh, last), 0))
    kern = functools.partial(_local_kernel, ts=ts, seq=S, rc=64, rp=min(ts, 256))
    return pl.pallas_call(
        kern,
        grid=(B, S // ts),
        in_specs=[cur(CONV_CH), prev(CONV_CH), nxt(CONV_CH), cur(W_BR), prev(W_BR), nxt(W_BR),
                  _const_spec(lw["conv_w"].shape), _const_spec(lw["conv_b"].shape),
                  _const_spec(lw["ln_g"].shape), _const_spec(lw["ln_b"].shape),
                  _const_spec(lw["pool_w"].shape), _const_spec(lw["pool_scale"].shape)],
        out_specs=[cur(CONV_CH), cur(W_BR)],
        out_shape=[jax.ShapeDtypeStruct((B, S, CONV_CH), BF16),
                   jax.ShapeDtypeStruct((B, S, W_BR), BF16)],
        scratch_shapes=[pltpu.VMEM((ts + 2 * HALO, CONV_CH), F32),
                        pltpu.VMEM((ts + 2 * HALO, W_BR), F32),
                        pltpu.VMEM((SUBLANES - 1, ts + 2 * HALO, CONV_CH), F32)],
        compiler_params=_params("parallel", "parallel"),
        name="conv_pool",
    )(u, u, u, zc, zc, zc, lw["conv_w"], lw["conv_b"], lw["ln_g"], lw["ln_b"],
      lw["pool_w"], lw["pool_scale"])


def _fft1_kernel(p_ref, q_ref, f_ref, o_ref):
    for j in range(p_ref.shape[2]):
        rhs = jnp.concatenate([p_ref[0, :, j, :], q_ref[0, :, j, :]], axis=0).astype(BF16)
        o_ref[0, :, j, :] = _dot(f_ref[...], rhs)


def _fft1_call(p, q, tabs):
    B, S, W = p.shape
    n2 = S // FFT_N1
    pv = p.reshape(B, FFT_N1, n2, W)
    qv = q.reshape(B, FFT_N1, n2, W)
    blk = pl.BlockSpec((1, FFT_N1, SUBLANES, W), lambda b, j: (b, 0, j, 0))
    return pl.pallas_call(
        _fft1_kernel,
        grid=(B, n2 // SUBLANES),
        in_specs=[blk, blk, _const_spec(tabs["f1"].shape)],
        out_specs=pl.BlockSpec((1, 2 * FFT_N1, SUBLANES, W), lambda b, j: (b, 0, j, 0)),
        out_shape=jax.ShapeDtypeStruct((B, 2 * FFT_N1, n2, W), F32),
        compiler_params=_params("parallel", "parallel"),
        name="fft_stage1",
    )(pv, qv, tabs["f1"])


def _fft2_kernel(a_ref, gc_ref, gs_ref, o_ref, *, kb):
    for j in range(kb):
        f = _dot(gc_ref[j], a_ref[0, 0, j].astype(BF16)) + _dot(gs_ref[j], a_ref[0, 1, j].astype(BF16))
        o_ref[0, :, j, :] = f


def _fft2_call(a, tabs, seq, kb):
    B = a.shape[0]
    n2 = seq // FFT_N1
    av = a.reshape(B, 2, FFT_N1, n2, W_BR)
    kern = functools.partial(_fft2_kernel, kb=kb)
    out = pl.pallas_call(
        kern,
        grid=(B, FFT_N1 // kb),
        in_specs=[pl.BlockSpec((1, 2, kb, n2, W_BR), lambda b, j: (b, 0, j, 0, 0)),
                  pl.BlockSpec((kb, n2, n2), lambda b, j: (j, 0, 0)),
                  pl.BlockSpec((kb, n2, n2), lambda b, j: (j, 0, 0))],
        out_specs=pl.BlockSpec((1, n2, kb, W_BR), lambda b, j: (b, 0, j, 0)),
        out_shape=jax.ShapeDtypeStruct((B, n2, FFT_N1, W_BR), F32),
        compiler_params=_params("parallel", "parallel"),
        name="fft_stage2",
    )(av, tabs["g2c"], tabs["g2s"])
    return out.reshape(B, seq, W_BR)


def _mixout_kernel(x_ref, mod_ref, g_ref, wg_ref, oa_ref, wa_ref, cb_ref, wb_ref, pc_ref, wc_ref,
                   f_ref, wd_ref, wo_ref, o_ref, *, parts):
    shift = mod_ref[0, 3:4, :]
    scale = mod_ref[0, 4:5, :]
    gate = mod_ref[0, 5:6, :]
    d = x_ref.shape[-1]
    rows = x_ref.shape[1] // parts
    hbs = []
    for p in range(parts):
        x = x_ref[0, p * rows:(p + 1) * rows, :]
        hbs.append(((_rms(x) * g_ref[2:3, :]) * (1.0 + scale) + shift).astype(BF16))
    ys = []
    for p in range(parts):
        sl = slice(p * rows, (p + 1) * rows)
        merged = None
        for br, (b_ref, w_ref) in enumerate(((oa_ref, wa_ref), (cb_ref, wb_ref), (pc_ref, wc_ref), (f_ref, wd_ref))):
            gl = _sigmoid(_dot(hbs[p], wg_ref[:, br * d:(br + 1) * d]))
            if br == 0:
                yb = lax.dot_general(b_ref[0, :, sl], w_ref[...], (((0,), (0,)), ((), ())),
                                     preferred_element_type=F32)
            else:
                yb = _dot(b_ref[0, sl, :].astype(BF16), w_ref[...])
            term = gl * yb
            merged = term if merged is None else merged + term
        ys.append(_dot(merged.astype(BF16), wo_ref[...]))
    for p in range(parts):
        sl = slice(p * rows, (p + 1) * rows)
        o_ref[0, sl, :] = x_ref[0, sl, :] + gate * (_rms(ys[p]) * g_ref[3:4, :])


def _mixout_call(x, mod_l, g_l, lw, oa, cb, pc, f, tm):
    B, S, D = x.shape
    tok = lambda w: pl.BlockSpec((1, tm, w), lambda b, i: (b, i, 0))
    return pl.pallas_call(
        functools.partial(_mixout_kernel, parts=2 if tm % 512 == 0 else 1),
        grid=(B, S // tm),
        in_specs=[tok(D),
                  pl.BlockSpec((1, N_MOD, D), lambda b, i: (b, 0, 0)),
                  _const_spec(g_l.shape),
                  _const_spec(lw["wg"].shape),
                  pl.BlockSpec((1, N_HEADS * V_HEAD, tm), lambda b, i: (b, 0, i)), _const_spec(lw["wa"].shape),
                  tok(CONV_CH), _const_spec(lw["wb"].shape),
                  tok(W_BR), _const_spec(lw["wc"].shape),
                  tok(W_BR), _const_spec(lw["wd"].shape),
                  _const_spec(lw["wo"].shape)],
        out_specs=tok(D),
        out_shape=jax.ShapeDtypeStruct(x.shape, F32),
        compiler_params=_params("parallel", "parallel"),
        name="mixer_out",
    )(x, mod_l, g_l, lw["wg"], oa, lw["wa"], cb, lw["wb"], pc, lw["wc"], f, lw["wd"], lw["wo"])


def _rot_half_cols(w):
    half = w.shape[-1] // 2
    return jnp.concatenate([-w[..., half:], w[..., :half]], axis=-1)


def _tables(seq):
    pos = jnp.arange(seq, dtype=F32)
    inv = ROPE_THETA ** (-jnp.arange(0, QK_ROPE, 2, dtype=F32) / QK_ROPE)
    ang = pos[:, None] * inv[None, :]
    cos, sin = jnp.cos(ang), jnp.sin(ang)
    cc = jnp.concatenate([cos, cos], axis=-1)
    ss = jnp.concatenate([sin, sin], axis=-1)
    z64 = jnp.zeros((seq, QK_NOPE), F32)
    z32 = jnp.zeros((seq, LANES - QK_NOPE - QK_ROPE), F32)
    sm_scale = (QK_NOPE + QK_ROPE) ** -0.5 * math.log2(math.e)
    cosk = jnp.concatenate([z64, cc, z32], axis=-1)
    sink = jnp.concatenate([z64, ss, z32], axis=-1)
    cosr = jnp.tile(cc, (1, N_HEADS)) * sm_scale
    sinr = jnp.tile(ss, (1, N_HEADS)) * sm_scale

    def cos_sin(m, period):
        th = m.astype(F32) * (2.0 * math.pi / period)
        return jnp.cos(th), jnp.sin(th)

    def dft(n):
        a = jnp.arange(n, dtype=jnp.int32)
        return cos_sin((a[:, None] * a[None, :]) % n, n)

    c_ch, s_ch = dft(GC)
    cs = (jnp.concatenate([c_ch, s_ch], axis=-1) * GC ** -0.5).astype(BF16)

    n1 = FFT_N1
    n2 = seq // n1
    c1, s1 = dft(n1)
    f1 = (jnp.concatenate([jnp.concatenate([c1, -s1], axis=1),
                           jnp.concatenate([-s1, -c1], axis=1)], axis=0) * n1 ** -0.5).astype(BF16)
    k1 = jnp.arange(n1, dtype=jnp.int32)[:, None, None]
    k2 = jnp.arange(n2, dtype=jnp.int32)[None, :, None]
    t2 = jnp.arange(n2, dtype=jnp.int32)[None, None, :]
    m = ((k1 + n1 * k2) * t2) % seq
    c2, s2 = cos_sin(m, seq)
    g2c = (c2 * n2 ** -0.5).astype(BF16)
    g2s = (s2 * n2 ** -0.5).astype(BF16)
    return dict(cosr=cosr, sinr=sinr, q_scale=sm_scale, cosk=cosk, sink=sink, cs=cs, f1=f1, g2c=g2c, g2s=g2s)


def _layer_weights(l, w_in, q_norm_g, w_uq, kv_norm_g, w_ukv, w_a, conv_w, conv_b, conv_ln_g,
                   conv_ln_b, w_b, pool_w, pool_scale, w_c, w_d, w_out):
    D = w_in.shape[1]
    wi = w_in[l]
    assert Q_PAD - Q_LORA == QK_ROPE
    w1 = jnp.concatenate([wi[:, Q_LORA:Q_LORA + KV_LORA], wi[:, :Q_LORA], wi[:, Q_LORA + KV_LORA:4 * W_BR]],
                         axis=-1).astype(BF16)

    uq = w_uq[l].reshape(Q_LORA, N_HEADS, QK_NOPE + QK_ROPE)
    nope, rope = uq[..., :QK_NOPE], uq[..., QK_NOPE:]
    wq = jnp.concatenate([nope.reshape(Q_LORA, N_HEADS * QK_NOPE), rope.reshape(Q_LORA, N_HEADS * QK_ROPE),
                          _rot_half_cols(rope).reshape(Q_LORA, N_HEADS * QK_ROPE)], axis=-1)
    wq = jnp.pad(wq, ((0, Q_PAD - Q_LORA), (0, 0))).astype(BF16)

    ukv = w_ukv[l].reshape(KV_LORA, N_HEADS, QK_NOPE + V_HEAD)
    zk = jnp.zeros((KV_LORA, N_HEADS, LANES - QK_NOPE), F32)
    wk = jnp.concatenate([ukv[..., :QK_NOPE], zk], axis=-1).reshape(KV_LORA, N_HEADS * LANES).astype(BF16)
    wv = ukv[..., QK_NOPE:].reshape(KV_LORA, N_HEADS * V_HEAD).astype(BF16)

    gq = jnp.pad(q_norm_g[l], (0, Q_PAD - Q_LORA)).reshape(1, Q_PAD)
    return dict(
        w1=w1, wq=wq, wk=wk, wv=wv, gq=gq, gkv=kv_norm_g[l].reshape(1, KV_LORA),
        wg=wi[:, 4 * W_BR:].astype(BF16), wa=w_a[l].astype(BF16),
        wb=w_b[l].astype(BF16), wc=w_c[l].astype(BF16), wd=w_d[l].astype(BF16), wo=w_out[l].astype(BF16),
        conv_w=conv_w[l], conv_b=conv_b[l].reshape(1, CONV_CH),
        ln_g=conv_ln_g[l].reshape(1, CONV_CH), ln_b=conv_ln_b[l].reshape(1, CONV_CH),
        pool_w=pool_w[l].astype(BF16), pool_scale=pool_scale[l].reshape(1, W_BR))


def _tiles(seq):
    t = lambda n: min(n, seq)
    return dict(ffn=t(1024), mix=t(512), mixout=t(512), tq=t(1024), tk=t(1024), local=t(512))


def kernel(x, c, ada_w, ada_b, norm_g, ffn1_w_in, ffn1_w_out, ffn2_w_in, ffn2_w_out, w_in, q_norm_g,
           w_uq, kv_norm_g, w_ukv, w_a, conv_w, conv_b, conv_ln_g, conv_ln_b, w_b, pool_w, pool_scale,
           w_c, w_d, w_out):
    B, S, D = x.shape
    L = ada_w.shape[0]
    assert D == D_MODEL and S % (FFT_N1 * 16) == 0
    ts = _tiles(S)
    tabs = _tables(S)
    mod = _mod_call(c, ada_w, ada_b).reshape(L, B, N_MOD, D)
    for l in range(L):
        lw = _layer_weights(l, w_in, q_norm_g, w_uq, kv_norm_g, w_ukv, w_a, conv_w, conv_b, conv_ln_g,
                            conv_ln_b, w_b, pool_w, pool_scale, w_c, w_d, w_out)
        mod_l, g_l = mod[l], norm_g[l]
        x = _ffn_call(x, mod_l, g_l, ffn1_w_in[l].astype(BF16), ffn1_w_out[l].astype(BF16), 0, ts["ffn"])
        q, k, v, kn, u, zc, p, qq = _mixin_call(x, mod_l, g_l, lw, tabs, ts["mix"])
        oa = _attn_call(q, k, v, kn, ts["tq"], ts["tk"])
        cb, pc = _local_call(u, zc, lw, ts["local"])
        a = _fft1_call(p, qq, tabs)
        f = _fft2_call(a, tabs, S, 8)
        x = _mixout_call(x, mod_l, g_l, lw, oa, cb, pc, f, ts["mixout"])
        x = _ffn_call(x, mod_l, g_l, ffn2_w_in[l].astype(BF16), ffn2_w_out[l].astype(BF16), 2, ts["ffn"])
    return x
```

```python
import functools
import math

import jax
import jax.numpy as jnp
import numpy as np
from jax import lax
from jax.experimental import pallas as pl
from jax.experimental.pallas import tpu as pltpu

F32 = jnp.float32
BF16 = jnp.bfloat16

D_MODEL = 1024
D_FF = 2816
N_HEADS = 8
Q_LORA = 352
KV_LORA = 128
QK_NOPE = 64
QK_ROPE = 32
V_HEAD = 64
QK_DIM = QK_NOPE + QK_ROPE
BF16_ROWS = 16
V_ROWS = 80
DENOM_MIN = 2.0 ** -80
ROPE_THETA = 10000.0
CONV_CH = 256
CONV_WIDTH = 31
POOL_WINDOWS = (2, 4, 8, 16)
GC = 128
N_GROUPS = 4
W_BR = 512
N_MOD = 9
EPS = 1e-6

LANES = 128
SUBLANES = 8
Q_PAD = 384
HALO = 16
FFT_N1 = 128
VMEM_LIMIT = 56 * 1024 * 1024

C_CKV = 0
C_CQ = C_CKV + KV_LORA
C_ZB = C_CQ + Q_PAD
C_ZC = C_ZB + W_BR
C_ZD = C_ZC + W_BR
W1_COLS = C_ZD + W_BR


def _params(*sem):
    return pltpu.CompilerParams(dimension_semantics=sem, vmem_limit_bytes=VMEM_LIMIT)


def _const_spec(shape):
    zeros = (0,) * len(shape)
    return pl.BlockSpec(shape, lambda *_: zeros, pipeline_mode=pl.Buffered(1))


def _dot(a, b):
    return jnp.dot(a, b, preferred_element_type=F32)


def _rms(x):
    return x * lax.rsqrt(jnp.mean(x * x, axis=-1, keepdims=True) + EPS)


def _sigmoid(x):
    return 1.0 / (1.0 + jnp.exp(-x))


def _mod_kernel(c_ref, w_ref, b_ref, o_ref):
    c = c_ref[...]
    ca = c * _sigmoid(c)
    o_ref[0] = jnp.dot(ca, w_ref[0], preferred_element_type=F32,
                       precision=lax.Precision.HIGHEST) + b_ref[0]


def _mod_call(c, ada_w, ada_b):
    L, D, N = ada_w.shape
    B = c.shape[0]
    tn = 1152
    return pl.pallas_call(
        _mod_kernel,
        grid=(L, N // tn),
        in_specs=[pl.BlockSpec((B, D), lambda l, j: (0, 0)),
                  pl.BlockSpec((1, D, tn), lambda l, j: (l, 0, j)),
                  pl.BlockSpec((1, 1, tn), lambda l, j: (l, 0, j))],
        out_specs=pl.BlockSpec((1, B, tn), lambda l, j: (l, 0, j)),
        out_shape=jax.ShapeDtypeStruct((L, B, N), F32),
        compiler_params=_params("parallel", "parallel"),
        name="adaln_mod",
    )(c, ada_w, ada_b.reshape(L, 1, N))


def _ffn_kernel(x_ref, mod_ref, g_ref, win_ref, wout_ref, o_ref, a_scr, *, sub, chunk, parts):
    shift = mod_ref[0, 3 * sub:3 * sub + 1, :]
    scale = mod_ref[0, 3 * sub + 1:3 * sub + 2, :]
    gate = mod_ref[0, 3 * sub + 2:3 * sub + 3, :]
    g_in = g_ref[2 * sub:2 * sub + 1, :]
    g_out = g_ref[2 * sub + 1:2 * sub + 2, :]
    rows = x_ref.shape[1] // parts
    hbs = []
    for p in range(parts):
        x = x_ref[0, p * rows:(p + 1) * rows, :]
        hbs.append(((_rms(x) * g_in) * (1.0 + scale) + shift).astype(BF16))
    for p in range(parts):
        for j in range(D_FF // chunk):
            gj = _dot(hbs[p], win_ref[:, j * chunk:(j + 1) * chunk])
            uj = _dot(hbs[p], win_ref[:, D_FF + j * chunk:D_FF + (j + 1) * chunk])
            a_scr[p, :, j * chunk:(j + 1) * chunk] = (gj * _sigmoid(gj) * uj).astype(BF16)
    for p in range(parts):
        y = _dot(a_scr[p], wout_ref[...])
        x = x_ref[0, p * rows:(p + 1) * rows, :]
        o_ref[0, p * rows:(p + 1) * rows, :] = x + (0.5 * gate) * (_rms(y) * g_out)


def _ffn_call(x, mod_l, g_l, w_in, w_out, sub, tm):
    B, S, D = x.shape
    parts = 4 if tm % 1024 == 0 else 1
    kern = functools.partial(_ffn_kernel, sub=sub, chunk=256, parts=parts)
    return pl.pallas_call(
        kern,
        grid=(B, S // tm),
        in_specs=[pl.BlockSpec((1, tm, D), lambda b, i: (b, i, 0)),
                  pl.BlockSpec((1, N_MOD, D), lambda b, i: (b, 0, 0)),
                  _const_spec(g_l.shape),
                  _const_spec(w_in.shape),
                  _const_spec(w_out.shape)],
        out_specs=pl.BlockSpec((1, tm, D), lambda b, i: (b, i, 0)),
        out_shape=jax.ShapeDtypeStruct(x.shape, F32),
        scratch_shapes=[pltpu.VMEM((parts, tm // parts, D_FF), BF16)],
        compiler_params=_params("parallel", "parallel"),
        name=f"ffn{sub}",
    )(x, mod_l, g_l, w_in, w_out)


def _mixin_kernel(x_ref, mod_ref, g_ref, w1_ref, gq_ref, gkv_ref, wq_ref, wk_ref, wv_ref,
                  cr_ref, sr_ref, ck_ref, sk_ref, cs_ref,
                  q_ref, k_ref, v_ref, kn_ref, u_ref, zc_ref, p_ref, qq_ref, *, q_scale):
    x = x_ref[0]
    tm = x.shape[0]
    shift = mod_ref[0, 3:4, :]
    scale = mod_ref[0, 4:5, :]
    h = (_rms(x) * g_ref[2:3, :]) * (1.0 + scale) + shift
    hb = h.astype(BF16)
    z = _dot(hb, w1_ref[...])

    cq = z[:, C_CQ:C_CQ + Q_PAD]
    qlane = lax.broadcasted_iota(jnp.int32, (1, Q_PAD), 1)
    cqm = jnp.where(qlane < Q_LORA, cq, 0.0)
    cqn = cqm * lax.rsqrt(jnp.sum(cqm * cqm, axis=-1, keepdims=True) * (1.0 / Q_LORA) + EPS)
    cqn = (cqn * gq_ref[...]).astype(BF16)
    qq = _dot(cqn, wq_ref[...])
    nope = qq[:, :N_HEADS * QK_NOPE] * q_scale
    w_rope = N_HEADS * QK_ROPE
    roped = (qq[:, N_HEADS * QK_NOPE:N_HEADS * QK_NOPE + w_rope] * cr_ref[...]
             + qq[:, N_HEADS * QK_NOPE + w_rope:] * sr_ref[...])
    for blk in range(N_HEADS * QK_NOPE // LANES):
        nt = nope[:, blk * LANES:(blk + 1) * LANES].T.astype(BF16)
        for i in range(LANES // QK_NOPE):
            q_ref[0, blk * (LANES // QK_NOPE) + i, 0:QK_NOPE, :] = nt[i * QK_NOPE:(i + 1) * QK_NOPE]
    for blk in range(w_rope // LANES):
        rt = roped[:, blk * LANES:(blk + 1) * LANES].T.astype(BF16)
        for i in range(LANES // QK_ROPE):
            q_ref[0, blk * (LANES // QK_ROPE) + i, QK_NOPE:QK_NOPE + QK_ROPE, :] = rt[i * QK_ROPE:(i + 1) * QK_ROPE]
    zero_rows = jnp.zeros((LANES - QK_NOPE - QK_ROPE, tm), BF16)
    for hd in range(N_HEADS):
        q_ref[0, hd, QK_NOPE + QK_ROPE:, :] = zero_rows

    kr_at = Q_LORA - (Q_PAD - LANES)
    kr = pltpu.roll(cq[:, Q_PAD - LANES:], (QK_NOPE - kr_at) % LANES, 1)
    klane = lax.broadcasted_iota(jnp.int32, (1, LANES), 1)
    kr_rot = jnp.where(klane < QK_NOPE + QK_ROPE // 2,
                       -pltpu.roll(kr, LANES - QK_ROPE // 2, 1), pltpu.roll(kr, QK_ROPE // 2, 1))
    k_rope = kr * ck_ref[...] + kr_rot * sk_ref[...]

    ckv = z[:, C_CKV:C_CKV + KV_LORA]
    ckvn = (_rms(ckv) * gkv_ref[...]).astype(BF16)
    kk = _dot(ckvn, wk_ref[...])
    ones_lane = (klane == QK_DIM).astype(F32)
    for hd in range(N_HEADS):
        kh = kk[:, hd * LANES:(hd + 1) * LANES] + k_rope
        k_ref[0, hd] = (kh + ones_lane).astype(BF16)
        norm2 = jnp.max(jnp.sum(kh * kh, axis=-1, keepdims=True), axis=0, keepdims=True)
        kn_ref[0, hd] = jnp.broadcast_to(norm2, (SUBLANES, LANES))
    vv = _dot(ckvn, wv_ref[...])
    ones_rows = (lax.broadcasted_iota(jnp.int32, (V_ROWS - V_HEAD, tm), 0) == 0).astype(BF16)
    for blk in range(N_HEADS * V_HEAD // LANES):
        vt = vv[:, blk * LANES:(blk + 1) * LANES].T.astype(BF16)
        for i in range(LANES // V_HEAD):
            v_ref[0, blk * (LANES // V_HEAD) + i, 0:V_HEAD, :] = vt[i * V_HEAD:(i + 1) * V_HEAD]
    for hd in range(N_HEADS):
        v_ref[0, hd, V_HEAD:, :] = ones_rows

    a = z[:, C_ZB:C_ZB + CONV_CH]
    gt = z[:, C_ZB + CONV_CH:C_ZB + 2 * CONV_CH]
    u_ref[0] = a * _sigmoid(gt)

    zc_ref[0] = z[:, C_ZC:C_ZC + W_BR]

    for g in range(N_GROUPS):
        zd = z[:, C_ZD + g * GC:C_ZD + (g + 1) * GC].astype(BF16)
        pq = _dot(zd, cs_ref[...])
        p_ref[0, :, g * GC:(g + 1) * GC] = pq[:, :GC]
        qq_ref[0, :, g * GC:(g + 1) * GC] = pq[:, GC:]


def _mixin_call(x, mod_l, g_l, lw, tabs, tm):
    B, S, D = x.shape
    tok = lambda w: pl.BlockSpec((1, tm, w), lambda b, i: (b, i, 0))
    head = pl.BlockSpec((1, N_HEADS, tm, LANES), lambda b, i: (b, 0, i, 0))
    head_t = pl.BlockSpec((1, N_HEADS, LANES, tm), lambda b, i: (b, 0, 0, i))
    tab = lambda w: pl.BlockSpec((tm, w), lambda b, i: (i, 0))
    hshape = jax.ShapeDtypeStruct((B, N_HEADS, S, LANES), BF16)
    hshape_t = jax.ShapeDtypeStruct((B, N_HEADS, LANES, S), BF16)
    return pl.pallas_call(
        functools.partial(_mixin_kernel, q_scale=tabs["q_scale"]),
        grid=(B, S // tm),
        in_specs=[tok(D),
                  pl.BlockSpec((1, N_MOD, D), lambda b, i: (b, 0, 0)),
                  _const_spec(g_l.shape),
                  _const_spec(lw["w1"].shape),
                  _const_spec(lw["gq"].shape),
                  _const_spec(lw["gkv"].shape),
                  _const_spec(lw["wq"].shape),
                  _const_spec(lw["wk"].shape),
                  _const_spec(lw["wv"].shape),
                  tab(N_HEADS * QK_ROPE), tab(N_HEADS * QK_ROPE), tab(LANES), tab(LANES),
                  _const_spec(tabs["cs"].shape)],
        out_specs=[head_t, head, pl.BlockSpec((1, N_HEADS, V_ROWS, tm), lambda b, i: (b, 0, 0, i)),
                   pl.BlockSpec((1, N_HEADS, SUBLANES, LANES), lambda b, i: (b, 0, i, 0)),
                   tok(CONV_CH), tok(W_BR), tok(W_BR), tok(W_BR)],
        out_shape=[hshape_t, hshape, jax.ShapeDtypeStruct((B, N_HEADS, V_ROWS, S), BF16),
                   jax.ShapeDtypeStruct((B, N_HEADS, (S // tm) * SUBLANES, LANES), F32),
                   jax.ShapeDtypeStruct((B, S, CONV_CH), F32),
                   jax.ShapeDtypeStruct((B, S, W_BR), F32),
                   jax.ShapeDtypeStruct((B, S, W_BR), F32),
                   jax.ShapeDtypeStruct((B, S, W_BR), F32)],
        compiler_params=_params("parallel", "parallel"),
        name="mixer_in",
    )(x, mod_l, g_l, lw["w1"], lw["gq"], lw["gkv"], lw["wq"], lw["wk"], lw["wv"],
      tabs["cosr"], tabs["sinr"], tabs["cosk"], tabs["sink"], tabs["cs"])


def _attn_kernel(qt_ref, k_ref, vt_ref, kn_ref, o_ref, acc_scr, m_scr, *, tq, tk):
    seq = k_ref.shape[2]
    nq, nk = seq // tq, seq // tk
    k_norm2 = jnp.max(kn_ref[0, 0])

    def exact_tile(t, qoff):
        qt = qt_ref[0, 0, :, pl.ds(qoff, tq)]
        m_scr[...] = jnp.full(m_scr.shape, -jnp.inf, F32)
        acc_scr[t, 0:V_ROWS, :] = jnp.zeros((V_ROWS, tq), F32)

        def chunk(j, carry):
            off = pl.multiple_of(j * tk, tk)
            st = _dot(k_ref[0, 0, pl.ds(off, tk), :], qt)
            m_prev = m_scr[...]
            m_new = jnp.maximum(m_prev, jnp.max(st, axis=0, keepdims=True))
            pt = jnp.exp2(st - m_new).astype(BF16)
            pv = _dot(vt_ref[0, 0, :, pl.ds(off, tk)], pt)
            acc_scr[t, 0:V_ROWS, :] = jnp.exp2(m_prev - m_new) * acc_scr[t, 0:V_ROWS, :] + pv
            m_scr[...] = m_new
            return carry

        lax.fori_loop(0, nk, chunk, 0)

    def bounded_tile(t, qoff):
        qt = qt_ref[0, 0, :, pl.ds(qoff, tq)]
        q32 = qt.astype(F32)
        bound = jnp.sqrt(jnp.sum(q32 * q32, axis=0, keepdims=True) * k_norm2)
        first = lax.broadcasted_iota(jnp.int32, (BF16_ROWS, tq), 0) == 0
        stab = jnp.where(first, -bound, 0.0).astype(BF16)
        qs = jnp.concatenate([qt[0:QK_DIM], stab, qt[QK_DIM + BF16_ROWS:]], axis=0)
        acc = None
        for j in range(nk):
            st = _dot(k_ref[0, 0, j * tk:(j + 1) * tk, :], qs)
            pv = _dot(vt_ref[0, 0, :, j * tk:(j + 1) * tk], jnp.exp2(st).astype(BF16))
            acc = pv if acc is None else acc + pv
        acc_scr[t, 0:V_ROWS, :] = acc
        return jnp.min(acc[V_HEAD:V_HEAD + 1, :]) >= DENOM_MIN

    group = acc_scr.shape[0]

    def q_group(gi, carry):
        offs = [pl.multiple_of((gi * group + t) * tq, tq) for t in range(group)]
        ok = None
        for t in range(group):
            ok_t = bounded_tile(t, offs[t])
            ok = ok_t if ok is None else jnp.logical_and(ok, ok_t)

        @pl.when(jnp.logical_not(ok))
        def _():
            for t in range(group):
                exact_tile(t, offs[t])

        for t in range(group):
            denom = acc_scr[t, V_HEAD:V_HEAD + 1, :]
            o_ref[0, :, pl.ds(offs[t], tq)] = (acc_scr[t, 0:V_HEAD, :] / denom).astype(o_ref.dtype)
        return carry

    lax.fori_loop(0, nq // group, q_group, 0)


def _attn_call(qt, k, vt, kn, tq, tk):
    B, H, S, _ = k.shape
    kern = functools.partial(_attn_kernel, tq=tq, tk=tk)
    return pl.pallas_call(
        kern,
        grid=(B, H),
        in_specs=[pl.BlockSpec((1, 1, LANES, S), lambda b, h: (b, h, 0, 0)),
                  pl.BlockSpec((1, 1, S, LANES), lambda b, h: (b, h, 0, 0)),
                  pl.BlockSpec((1, 1, V_ROWS, S), lambda b, h: (b, h, 0, 0)),
                  pl.BlockSpec((1, 1) + kn.shape[2:], lambda b, h: (b, h, 0, 0))],
        out_specs=pl.BlockSpec((1, V_HEAD, S), lambda b, h: (b, h, 0)),
        out_shape=jax.ShapeDtypeStruct((B, H * V_HEAD, S), BF16),
        scratch_shapes=[pltpu.VMEM((2 if (S // tq) % 2 == 0 else 1, V_ROWS, tq), F32), pltpu.VMEM((1, tq), F32)],
        compiler_params=_params("parallel", "arbitrary"),
        name="attention",
    )(qt, k, vt, kn)


def _local_kernel(u_ref, up_ref, un_ref, z_ref, zp_ref, zn_ref, cw_ref, cb_ref, lg_ref, lb_ref,
                  pw_ref, ps_ref, oc_ref, op_ref, ubuf, zbuf, ush, *, ts, seq, rc, rp):
    i = pl.program_id(1)
    has_prev = i > 0
    has_next = i < pl.num_programs(1) - 1
    ubuf[0:HALO, :] = jnp.where(has_prev, up_ref[0], 0.0)
    ubuf[HALO:HALO + ts, :] = u_ref[0]
    ubuf[HALO + ts:, :] = jnp.where(has_next, un_ref[0], 0.0)
    zbuf[0:HALO, :] = jnp.where(has_prev, zp_ref[0], 0.0)
    zbuf[HALO:HALO + ts, :] = z_ref[0]
    zbuf[HALO + ts:, :] = jnp.where(has_next, zn_ref[0], 0.0)

    span = ts + 2 * HALO - SUBLANES
    for s in range(1, SUBLANES):
        ush[s - 1, 0:span, :] = ubuf[s:s + span, :]

    pad = CONV_WIDTH // 2
    for r in range(0, ts, rc):
        acc = jnp.zeros((rc, CONV_CH), F32)
        for kk in range(CONV_WIDTH):
            st = HALO + r + kk - pad
            s, base = st % SUBLANES, st - st % SUBLANES
            tap = ubuf[base:base + rc, :] if s == 0 else ush[s - 1, base:base + rc, :]
            acc = acc + tap * cw_ref[kk:kk + 1, :]
        y = acc + cb_ref[...]
        mu = jnp.mean(y, axis=-1, keepdims=True)
        yc = y - mu
        yn = yc * lax.rsqrt(jnp.mean(yc * yc, axis=-1, keepdims=True) + EPS)
        yn = yn * lg_ref[...] + lb_ref[...]
        oc_ref[0, r:r + rc, :] = (yn * _sigmoid(yn)).astype(oc_ref.dtype)

    for r in range(0, ts, rp):
        t = i * ts + r + lax.broadcasted_iota(jnp.int32, (rp, 1), 0)
        for g, w in enumerate(POOL_WINDOWS):
            lo = w // 2
            hi = w - 1 - lo
            cols = slice(g * GC, (g + 1) * GC)
            win = jnp.zeros((rp, GC), F32)
            for j in range(-lo, hi + 1):
                st = HALO + r + j
                win = win + zbuf[st:st + rp, cols]
            cnt = (jnp.minimum(t + hi + 1, seq) - jnp.maximum(t - lo, 0)).astype(F32)
            d = win / cnt - zbuf[HALO + r:HALO + r + rp, cols]
            yg = _dot(d.astype(BF16), pw_ref[g]) * ps_ref[:, cols]
            op_ref[0, r:r + rp, cols] = yg.astype(op_ref.dtype)


def _local_call(u, zc, lw, ts):
    B, S, _ = u.shape
    nh = ts // HALO
    last = S // HALO - 1
    cur = lambda w: pl.BlockSpec((1, ts, w), lambda b, i: (b, i, 0))
    prev = lambda w: pl.BlockSpec((1, HALO, w), lambda b, i: (b, jnp.maximum(i * nh - 1, 0), 0))
    nxt = lambda w: pl.BlockSpec((1, HALO, w), lambda b, i: (b, jnp.minimum((i + 1) * nh, last), 0))
    kern = functools.partial(_local_kernel, ts=ts, seq=S, rc=64, rp=min(ts, 256))
    return pl.pallas_call(
        kern,
        grid=(B, S // ts),
        in_specs=[cur(CONV_CH), prev(CONV_CH), nxt(CONV_CH), cur(W_BR), prev(W_BR), nxt(W_BR),
                  _const_spec(lw["conv_w"].shape), _const_spec(lw["conv_b"].shape),
                  _const_spec(lw["ln_g"].shape), _const_spec(lw["ln_b"].shape),
                  _const_spec(lw["pool_w"].shape), _const_spec(lw["pool_scale"].shape)],
        out_specs=[cur(CONV_CH), cur(W_BR)],
        out_shape=[jax.ShapeDtypeStruct((B, S, CONV_CH), BF16),
                   jax.ShapeDtypeStruct((B, S, W_BR), BF16)],
        scratch_shapes=[pltpu.VMEM((ts + 2 * HALO, CONV_CH), F32),
                        pltpu.VMEM((ts + 2 * HALO, W_BR), F32),
                        pltpu.VMEM((SUBLANES - 1, ts + 2 * HALO, CONV_CH), F32)],
        compiler_params=_params("parallel", "parallel"),
        name="conv_pool",
    )(u, u, u, zc, zc, zc, lw["conv_w"], lw["conv_b"], lw["ln_g"], lw["ln_b"],
      lw["pool_w"], lw["pool_scale"])


def _fft1_kernel(p_ref, q_ref, f_ref, o_ref):
    for j in range(p_ref.shape[2]):
        rhs = jnp.concatenate([p_ref[0, :, j, :], q_ref[0, :, j, :]], axis=0).astype(BF16)
        o_ref[0, :, j, :] = _dot(f_ref[...], rhs)


def _fft1_call(p, q, tabs):
    B, S, W = p.shape
    n2 = S // FFT_N1
    pv = p.reshape(B, FFT_N1, n2, W)
    qv = q.reshape(B, FFT_N1, n2, W)
    blk = pl.BlockSpec((1, FFT_N1, SUBLANES, W), lambda b, j: (b, 0, j, 0))
    return pl.pallas_call(
        _fft1_kernel,
        grid=(B, n2 // SUBLANES),
        in_specs=[blk, blk, _const_spec(tabs["f1"].shape)],
        out_specs=pl.BlockSpec((1, 2 * FFT_N1, SUBLANES, W), lambda b, j: (b, 0, j, 0)),
        out_shape=jax.ShapeDtypeStruct((B, 2 * FFT_N1, n2, W), F32),
        compiler_params=_params("parallel", "parallel"),
        name="fft_stage1",
    )(pv, qv, tabs["f1"])


def _fft2_kernel(a_ref, gc_ref, gs_ref, o_ref, *, kb):
    for j in range(kb):
        f = _dot(gc_ref[j], a_ref[0, 0, j].astype(BF16)) + _dot(gs_ref[j], a_ref[0, 1, j].astype(BF16))
        o_ref[0, :, j, :] = f


def _fft2_call(a, tabs, seq, kb):
    B = a.shape[0]
    n2 = seq // FFT_N1
    av = a.reshape(B, 2, FFT_N1, n2, W_BR)
    kern = functools.partial(_fft2_kernel, kb=kb)
    out = pl.pallas_call(
        kern,
        grid=(B, FFT_N1 // kb),
        in_specs=[pl.BlockSpec((1, 2, kb, n2, W_BR), lambda b, j: (b, 0, j, 0, 0)),
                  pl.BlockSpec((kb, n2, n2), lambda b, j: (j, 0, 0)),
                  pl.BlockSpec((kb, n2, n2), lambda b, j: (j, 0, 0))],
        out_specs=pl.BlockSpec((1, n2, kb, W_BR), lambda b, j: (b, 0, j, 0)),
        out_shape=jax.ShapeDtypeStruct((B, n2, FFT_N1, W_BR), F32),
        compiler_params=_params("parallel", "parallel"),
        name="fft_stage2",
    )(av, tabs["g2c"], tabs["g2s"])
    return out.reshape(B, seq, W_BR)


def _mixout_kernel(x_ref, mod_ref, g_ref, wg_ref, oa_ref, wa_ref, cb_ref, wb_ref, pc_ref, wc_ref,
                   f_ref, wd_ref, wo_ref, o_ref, *, parts):
    shift = mod_ref[0, 3:4, :]
    scale = mod_ref[0, 4:5, :]
    gate = mod_ref[0, 5:6, :]
    d = x_ref.shape[-1]
    rows = x_ref.shape[1] // parts
    hbs = []
    for p in range(parts):
        x = x_ref[0, p * rows:(p + 1) * rows, :]
        hbs.append(((_rms(x) * g_ref[2:3, :]) * (1.0 + scale) + shift).astype(BF16))
    ys = []
    for p in range(parts):
        sl = slice(p * rows, (p + 1) * rows)
        merged = None
        for br, (b_ref, w_ref) in enumerate(((oa_ref, wa_ref), (cb_ref, wb_ref), (pc_ref, wc_ref), (f_ref, wd_ref))):
            gl = _sigmoid(_dot(hbs[p], wg_ref[:, br * d:(br + 1) * d]))
            if br == 0:
                yb = lax.dot_general(b_ref[0, :, sl], w_ref[...], (((0,), (0,)), ((), ())),
                                     preferred_element_type=F32)
            else:
                yb = _dot(b_ref[0, sl, :].astype(BF16), w_ref[...])
            term = gl * yb
            merged = term if merged is None else merged + term
        ys.append(_dot(merged.astype(BF16), wo_ref[...]))
    for p in range(parts):
        sl = slice(p * rows, (p + 1) * rows)
        o_ref[0, sl, :] = x_ref[0, sl, :] + gate * (_rms(ys[p]) * g_ref[3:4, :])


def _mixout_call(x, mod_l, g_l, lw, oa, cb, pc, f, tm):
    B, S, D = x.shape
    tok = lambda w: pl.BlockSpec((1, tm, w), lambda b, i: (b, i, 0))
    return pl.pallas_call(
        functools.partial(_mixout_kernel, parts=2 if tm % 512 == 0 else 1),
        grid=(B, S // tm),
        in_specs=[tok(D),
                  pl.BlockSpec((1, N_MOD, D), lambda b, i: (b, 0, 0)),
                  _const_spec(g_l.shape),
                  _const_spec(lw["wg"].shape),
                  pl.BlockSpec((1, N_HEADS * V_HEAD, tm), lambda b, i: (b, 0, i)), _const_spec(lw["wa"].shape),
                  tok(CONV_CH), _const_spec(lw["wb"].shape),
                  tok(W_BR), _const_spec(lw["wc"].shape),
                  tok(W_BR), _const_spec(lw["wd"].shape),
                  _const_spec(lw["wo"].shape)],
        out_specs=tok(D),
        out_shape=jax.ShapeDtypeStruct(x.shape, F32),
        compiler_params=_params("parallel", "parallel"),
        name="mixer_out",
    )(x, mod_l, g_l, lw["wg"], oa, lw["wa"], cb, lw["wb"], pc, lw["wc"], f, lw["wd"], lw["wo"])


def _rot_half_cols(w):
    half = w.shape[-1] // 2
    return jnp.concatenate([-w[..., half:], w[..., :half]], axis=-1)


def _tables(seq):
    f32 = np.float32
    pos = np.arange(seq, dtype=f32)
    inv = (f32(ROPE_THETA) ** (-np.arange(0, QK_ROPE, 2, dtype=f32) / f32(QK_ROPE))).astype(f32)
    ang = pos[:, None] * inv[None, :]
    cos, sin = np.cos(ang).astype(f32), np.sin(ang).astype(f32)
    cc = np.concatenate([cos, cos], axis=-1)
    ss = np.concatenate([sin, sin], axis=-1)
    z64 = np.zeros((seq, QK_NOPE), f32)
    z32 = np.zeros((seq, LANES - QK_NOPE - QK_ROPE), f32)
    sm_scale = QK_DIM ** -0.5 * math.log2(math.e)
    cosk = np.concatenate([z64, cc, z32], axis=-1)
    sink = np.concatenate([z64, ss, z32], axis=-1)
    cosr = np.tile(cc, (1, N_HEADS)) * f32(sm_scale)
    sinr = np.tile(ss, (1, N_HEADS)) * f32(sm_scale)

    def cos_sin(m, period):
        th = m.astype(np.float64) * (2.0 * math.pi / period)
        return np.cos(th), np.sin(th)

    def dft(n):
        a = np.arange(n, dtype=np.int64)
        return cos_sin((a[:, None] * a[None, :]) % n, n)

    bf = lambda a: a.astype(f32).astype(BF16)
    c_ch, s_ch = dft(GC)
    cs = bf(np.concatenate([c_ch, s_ch], axis=-1) * GC ** -0.5)

    n1 = FFT_N1
    n2 = seq // n1
    c1, s1 = dft(n1)
    f1 = bf(np.concatenate([np.concatenate([c1, -s1], axis=1), np.concatenate([-s1, -c1], axis=1)], axis=0)
            * n1 ** -0.5)
    k1 = np.arange(n1, dtype=np.int64)[:, None, None]
    k2 = np.arange(n2, dtype=np.int64)[None, :, None]
    t2 = np.arange(n2, dtype=np.int64)[None, None, :]
    c2, s2 = cos_sin(((k1 + n1 * k2) * t2) % seq, seq)
    g2c = bf(c2 * n2 ** -0.5)
    g2s = bf(s2 * n2 ** -0.5)
    return dict(cosr=cosr, sinr=sinr, q_scale=sm_scale, cosk=cosk, sink=sink, cs=cs, f1=f1, g2c=g2c, g2s=g2s)


def _layer_weights(l, w_in, q_norm_g, w_uq, kv_norm_g, w_ukv, w_a, conv_w, conv_b, conv_ln_g,
                   conv_ln_b, w_b, pool_w, pool_scale, w_c, w_d, w_out):
    D = w_in.shape[1]
    wi = w_in[l]
    assert Q_PAD - Q_LORA == QK_ROPE
    w1 = jnp.concatenate([wi[:, Q_LORA:Q_LORA + KV_LORA], wi[:, :Q_LORA], wi[:, Q_LORA + KV_LORA:4 * W_BR]],
                         axis=-1).astype(BF16)

    uq = w_uq[l].reshape(Q_LORA, N_HEADS, QK_NOPE + QK_ROPE)
    nope, rope = uq[..., :QK_NOPE], uq[..., QK_NOPE:]
    wq = jnp.concatenate([nope.reshape(Q_LORA, N_HEADS * QK_NOPE), rope.reshape(Q_LORA, N_HEADS * QK_ROPE),
                          _rot_half_cols(rope).reshape(Q_LORA, N_HEADS * QK_ROPE)], axis=-1)
    wq = jnp.pad(wq, ((0, Q_PAD - Q_LORA), (0, 0))).astype(BF16)

    ukv = w_ukv[l].reshape(KV_LORA, N_HEADS, QK_NOPE + V_HEAD)
    zk = jnp.zeros((KV_LORA, N_HEADS, LANES - QK_NOPE), F32)
    wk = jnp.concatenate([ukv[..., :QK_NOPE], zk], axis=-1).reshape(KV_LORA, N_HEADS * LANES).astype(BF16)
    wv = ukv[..., QK_NOPE:].reshape(KV_LORA, N_HEADS * V_HEAD).astype(BF16)

    gq = jnp.pad(q_norm_g[l], (0, Q_PAD - Q_LORA)).reshape(1, Q_PAD)
    return dict(
        w1=w1, wq=wq, wk=wk, wv=wv, gq=gq, gkv=kv_norm_g[l].reshape(1, KV_LORA),
        wg=wi[:, 4 * W_BR:].astype(BF16), wa=w_a[l].astype(BF16),
        wb=w_b[l].astype(BF16), wc=w_c[l].astype(BF16), wd=w_d[l].astype(BF16), wo=w_out[l].astype(BF16),
        conv_w=conv_w[l], conv_b=conv_b[l].reshape(1, CONV_CH),
        ln_g=conv_ln_g[l].reshape(1, CONV_CH), ln_b=conv_ln_b[l].reshape(1, CONV_CH),
        pool_w=pool_w[l].astype(BF16), pool_scale=pool_scale[l].reshape(1, W_BR))


def _tiles(seq):
    t = lambda n: min(n, seq)
    return dict(ffn=t(1024), mix=t(512), mixout=t(512), tq=t(1024), tk=t(1024), local=t(512))


def kernel(x, c, ada_w, ada_b, norm_g, ffn1_w_in, ffn1_w_out, ffn2_w_in, ffn2_w_out, w_in, q_norm_g,
           w_uq, kv_norm_g, w_ukv, w_a, conv_w, conv_b, conv_ln_g, conv_ln_b, w_b, pool_w, pool_scale,
           w_c, w_d, w_out):
    B, S, D = x.shape
    L = ada_w.shape[0]
    assert D == D_MODEL and S % (FFT_N1 * 16) == 0
    ts = _tiles(S)
    tabs = _tables(S)
    mod = _mod_call(c, ada_w, ada_b).reshape(L, B, N_MOD, D)
    for l in range(L):
        lw = _layer_weights(l, w_in, q_norm_g, w_uq, kv_norm_g, w_ukv, w_a, conv_w, conv_b, conv_ln_g,
                            conv_ln_b, w_b, pool_w, pool_scale, w_c, w_d, w_out)
        mod_l, g_l = mod[l], norm_g[l]
        x = _ffn_call(x, mod_l, g_l, ffn1_w_in[l].astype(BF16), ffn1_w_out[l].astype(BF16), 0, ts["ffn"])
        q, k, v, kn, u, zc, p, qq = _mixin_call(x, mod_l, g_l, lw, tabs, ts["mix"])
        oa = _attn_call(q, k, v, kn, ts["tq"], ts["tk"])
        cb, pc = _local_call(u, zc, lw, ts["local"])
        a = _fft1_call(p, qq, tabs)
        f = _fft2_call(a, tabs, S, 8)
        x = _mixout_call(x, mod_l, g_l, lw, oa, cb, pc, f, ts["mixout"])
        x = _ffn_call(x, mod_l, g_l, ffn2_w_in[l].astype(BF16), ffn2_w_out[l].astype(BF16), 2, ts["ffn"])
    return x
```

```python
import functools
import math

import jax
import jax.numpy as jnp
import numpy as np
from jax import lax
from jax.experimental import pallas as pl
from jax.experimental.pallas import tpu as pltpu

F32 = jnp.float32
BF16 = jnp.bfloat16

D_MODEL = 1024
D_FF = 2816
N_HEADS = 8
Q_LORA = 352
KV_LORA = 128
QK_NOPE = 64
QK_ROPE = 32
V_HEAD = 64
QK_DIM = QK_NOPE + QK_ROPE
BF16_ROWS = 16
V_ROWS = 80
DENOM_MIN = 2.0 ** -80
ROPE_THETA = 10000.0
CONV_CH = 256
CONV_WIDTH = 31
POOL_WINDOWS = (2, 4, 8, 16)
GC = 128
N_GROUPS = 4
W_BR = 512
N_MOD = 9
EPS = 1e-6

LANES = 128
SUBLANES = 8
Q_PAD = 384
HALO = 16
FFT_N1 = 128
VMEM_LIMIT = 56 * 1024 * 1024

C_CKV = 0
C_CQ = C_CKV + KV_LORA
C_ZB = C_CQ + Q_PAD
C_ZC = C_ZB + W_BR
C_ZD = C_ZC + W_BR
W1_COLS = C_ZD + W_BR


def _params(*sem):
    return pltpu.CompilerParams(dimension_semantics=sem, vmem_limit_bytes=VMEM_LIMIT)


def _const_spec(shape):
    zeros = (0,) * len(shape)
    return pl.BlockSpec(shape, lambda *_: zeros, pipeline_mode=pl.Buffered(1))


def _dot(a, b):
    return jnp.dot(a, b, preferred_element_type=F32)


def _rms(x):
    return x * lax.rsqrt(jnp.mean(x * x, axis=-1, keepdims=True) + EPS)


def _sigmoid(x):
    return 1.0 / (1.0 + jnp.exp(-x))


def _mod_kernel(c_ref, w_ref, b_ref, o_ref):
    c = c_ref[...]
    ca = c * _sigmoid(c)
    o_ref[0] = jnp.dot(ca, w_ref[0], preferred_element_type=F32,
                       precision=lax.Precision.HIGHEST) + b_ref[0]


def _mod_call(c, ada_w, ada_b):
    L, D, N = ada_w.shape
    B = c.shape[0]
    tn = 1152
    return pl.pallas_call(
        _mod_kernel,
        grid=(L, N // tn),
        in_specs=[pl.BlockSpec((B, D), lambda l, j: (0, 0)),
                  pl.BlockSpec((1, D, tn), lambda l, j: (l, 0, j)),
                  pl.BlockSpec((1, 1, tn), lambda l, j: (l, 0, j))],
        out_specs=pl.BlockSpec((1, B, tn), lambda l, j: (l, 0, j)),
        out_shape=jax.ShapeDtypeStruct((L, B, N), F32),
        compiler_params=_params("parallel", "parallel"),
        name="adaln_mod",
    )(c, ada_w, ada_b.reshape(L, 1, N))


def _ffn_kernel(x_ref, mod_ref, g_ref, win_ref, wout_ref, o_ref, a_scr, *, sub, chunk, parts):
    shift = mod_ref[0, 3 * sub:3 * sub + 1, :]
    scale = mod_ref[0, 3 * sub + 1:3 * sub + 2, :]
    gate = mod_ref[0, 3 * sub + 2:3 * sub + 3, :]
    g_in = g_ref[2 * sub:2 * sub + 1, :]
    g_out = g_ref[2 * sub + 1:2 * sub + 2, :]
    rows = x_ref.shape[1] // parts
    hbs = []
    for p in range(parts):
        x = x_ref[0, p * rows:(p + 1) * rows, :]
        hbs.append(((_rms(x) * g_in) * (1.0 + scale) + shift).astype(BF16))
    for p in range(parts):
        for j in range(D_FF // chunk):
            gj = _dot(hbs[p], win_ref[:, j * chunk:(j + 1) * chunk])
            uj = _dot(hbs[p], win_ref[:, D_FF + j * chunk:D_FF + (j + 1) * chunk])
            a_scr[p, :, j * chunk:(j + 1) * chunk] = (gj * _sigmoid(gj) * uj).astype(BF16)
    for p in range(parts):
        y = _dot(a_scr[p], wout_ref[...])
        x = x_ref[0, p * rows:(p + 1) * rows, :]
        o_ref[0, p * rows:(p + 1) * rows, :] = x + (0.5 * gate) * (_rms(y) * g_out)


def _ffn_call(x, mod_l, g_l, w_in, w_out, sub, tm):
    B, S, D = x.shape
    parts = 4 if tm % 1024 == 0 else 1
    kern = functools.partial(_ffn_kernel, sub=sub, chunk=256, parts=parts)
    return pl.pallas_call(
        kern,
        grid=(B, S // tm),
        in_specs=[pl.BlockSpec((1, tm, D), lambda b, i: (b, i, 0)),
                  pl.BlockSpec((1, N_MOD, D), lambda b, i: (b, 0, 0)),
                  _const_spec(g_l.shape),
                  _const_spec(w_in.shape),
                  _const_spec(w_out.shape)],
        out_specs=pl.BlockSpec((1, tm, D), lambda b, i: (b, i, 0)),
        out_shape=jax.ShapeDtypeStruct(x.shape, F32),
        scratch_shapes=[pltpu.VMEM((parts, tm // parts, D_FF), BF16)],
        compiler_params=_params("parallel", "parallel"),
        name=f"ffn{sub}",
    )(x, mod_l, g_l, w_in, w_out)


def _mixin_kernel(x_ref, mod_ref, g_ref, w1_ref, gq_ref, gkv_ref, wq_ref, wk_ref, wv_ref,
                  cr_ref, sr_ref, ck_ref, sk_ref, cs_ref,
                  q_ref, k_ref, v_ref, kn_ref, u_ref, zc_ref, p_ref, qq_ref, *, q_scale):
    x = x_ref[0]
    tm = x.shape[0]
    shift = mod_ref[0, 3:4, :]
    scale = mod_ref[0, 4:5, :]
    h = (_rms(x) * g_ref[2:3, :]) * (1.0 + scale) + shift
    hb = h.astype(BF16)
    z = _dot(hb, w1_ref[...])

    cq = z[:, C_CQ:C_CQ + Q_PAD]
    qlane = lax.broadcasted_iota(jnp.int32, (1, Q_PAD), 1)
    cqm = jnp.where(qlane < Q_LORA, cq, 0.0)
    cqn = cqm * lax.rsqrt(jnp.sum(cqm * cqm, axis=-1, keepdims=True) * (1.0 / Q_LORA) + EPS)
    cqn = (cqn * gq_ref[...]).astype(BF16)
    qq = _dot(cqn, wq_ref[...])
    nope = qq[:, :N_HEADS * QK_NOPE] * q_scale
    w_rope = N_HEADS * QK_ROPE
    roped = (qq[:, N_HEADS * QK_NOPE:N_HEADS * QK_NOPE + w_rope] * cr_ref[...]
             + qq[:, N_HEADS * QK_NOPE + w_rope:] * sr_ref[...])
    for blk in range(N_HEADS * QK_NOPE // LANES):
        nt = nope[:, blk * LANES:(blk + 1) * LANES].T.astype(BF16)
        for i in range(LANES // QK_NOPE):
            q_ref[0, blk * (LANES // QK_NOPE) + i, 0:QK_NOPE, :] = nt[i * QK_NOPE:(i + 1) * QK_NOPE]
    for blk in range(w_rope // LANES):
        rt = roped[:, blk * LANES:(blk + 1) * LANES].T.astype(BF16)
        for i in range(LANES // QK_ROPE):
            q_ref[0, blk * (LANES // QK_ROPE) + i, QK_NOPE:QK_NOPE + QK_ROPE, :] = rt[i * QK_ROPE:(i + 1) * QK_ROPE]
    zero_rows = jnp.zeros((LANES - QK_NOPE - QK_ROPE, tm), BF16)
    for hd in range(N_HEADS):
        q_ref[0, hd, QK_NOPE + QK_ROPE:, :] = zero_rows

    kr_at = Q_LORA - (Q_PAD - LANES)
    kr = pltpu.roll(cq[:, Q_PAD - LANES:], (QK_NOPE - kr_at) % LANES, 1)
    klane = lax.broadcasted_iota(jnp.int32, (1, LANES), 1)
    kr_rot = jnp.where(klane < QK_NOPE + QK_ROPE // 2,
                       -pltpu.roll(kr, LANES - QK_ROPE // 2, 1), pltpu.roll(kr, QK_ROPE // 2, 1))
    k_rope = kr * ck_ref[...] + kr_rot * sk_ref[...]

    ckv = z[:, C_CKV:C_CKV + KV_LORA]
    ckvn = (_rms(ckv) * gkv_ref[...]).astype(BF16)
    kk = _dot(ckvn, wk_ref[...])
    ones_lane = (klane == QK_DIM).astype(F32)
    for hd in range(N_HEADS):
        kh = kk[:, hd * LANES:(hd + 1) * LANES] + k_rope
        k_ref[0, hd] = (kh + ones_lane).astype(BF16)
        norm2 = jnp.max(jnp.sum(kh * kh, axis=-1, keepdims=True), axis=0, keepdims=True)
        kn_ref[0, hd] = jnp.broadcast_to(norm2, (SUBLANES, LANES))
    vv = _dot(ckvn, wv_ref[...])
    ones_rows = (lax.broadcasted_iota(jnp.int32, (V_ROWS - V_HEAD, tm), 0) == 0).astype(BF16)
    for blk in range(N_HEADS * V_HEAD // LANES):
        vt = vv[:, blk * LANES:(blk + 1) * LANES].T.astype(BF16)
        for i in range(LANES // V_HEAD):
            v_ref[0, blk * (LANES // V_HEAD) + i, 0:V_HEAD, :] = vt[i * V_HEAD:(i + 1) * V_HEAD]
    for hd in range(N_HEADS):
        v_ref[0, hd, V_HEAD:, :] = ones_rows

    a = z[:, C_ZB:C_ZB + CONV_CH]
    gt = z[:, C_ZB + CONV_CH:C_ZB + 2 * CONV_CH]
    u_ref[0] = a * _sigmoid(gt)

    zc_ref[0] = z[:, C_ZC:C_ZC + W_BR]

    for g in range(N_GROUPS):
        zd = z[:, C_ZD + g * GC:C_ZD + (g + 1) * GC].astype(BF16)
        pq = _dot(zd, cs_ref[...])
        p_ref[0, :, g * GC:(g + 1) * GC] = pq[:, :GC]
        qq_ref[0, :, g * GC:(g + 1) * GC] = pq[:, GC:]


def _mixin_call(x, mod_l, g_l, lw, tabs, tm):
    B, S, D = x.shape
    tok = lambda w: pl.BlockSpec((1, tm, w), lambda b, i: (b, i, 0))
    head = pl.BlockSpec((1, N_HEADS, tm, LANES), lambda b, i: (b, 0, i, 0))
    head_t = pl.BlockSpec((1, N_HEADS, LANES, tm), lambda b, i: (b, 0, 0, i))
    tab = lambda w: pl.BlockSpec((tm, w), lambda b, i: (i, 0))
    hshape = jax.ShapeDtypeStruct((B, N_HEADS, S, LANES), BF16)
    hshape_t = jax.ShapeDtypeStruct((B, N_HEADS, LANES, S), BF16)
    return pl.pallas_call(
        functools.partial(_mixin_kernel, q_scale=tabs["q_scale"]),
        grid=(B, S // tm),
        in_specs=[tok(D),
                  pl.BlockSpec((1, N_MOD, D), lambda b, i: (b, 0, 0)),
                  _const_spec(g_l.shape),
                  _const_spec(lw["w1"].shape),
                  _const_spec(lw["gq"].shape),
                  _const_spec(lw["gkv"].shape),
                  _const_spec(lw["wq"].shape),
                  _const_spec(lw["wk"].shape),
                  _const_spec(lw["wv"].shape),
                  tab(N_HEADS * QK_ROPE), tab(N_HEADS * QK_ROPE), tab(LANES), tab(LANES),
                  _const_spec(tabs["cs"].shape)],
        out_specs=[head_t, head, pl.BlockSpec((1, N_HEADS, V_ROWS, tm), lambda b, i: (b, 0, 0, i)),
                   pl.BlockSpec((1, N_HEADS, SUBLANES, LANES), lambda b, i: (b, 0, i, 0)),
                   tok(CONV_CH), tok(W_BR), tok(W_BR), tok(W_BR)],
        out_shape=[hshape_t, hshape, jax.ShapeDtypeStruct((B, N_HEADS, V_ROWS, S), BF16),
                   jax.ShapeDtypeStruct((B, N_HEADS, (S // tm) * SUBLANES, LANES), F32),
                   jax.ShapeDtypeStruct((B, S, CONV_CH), F32),
                   jax.ShapeDtypeStruct((B, S, W_BR), F32),
                   jax.ShapeDtypeStruct((B, S, W_BR), F32),
                   jax.ShapeDtypeStruct((B, S, W_BR), F32)],
        compiler_params=_params("parallel", "parallel"),
        name="mixer_in",
    )(x, mod_l, g_l, lw["w1"], lw["gq"], lw["gkv"], lw["wq"], lw["wk"], lw["wv"],
      tabs["cosr"], tabs["sinr"], tabs["cosk"], tabs["sink"], tabs["cs"])


def _attn_kernel(qt_ref, k_ref, vt_ref, kn_ref, o_ref, acc_scr, m_scr, *, tq, tk):
    seq = k_ref.shape[2]
    nq, nk = seq // tq, seq // tk
    k_norm2 = jnp.max(kn_ref[0, 0])

    def exact_tile(t, qoff):
        qt = qt_ref[0, 0, :, pl.ds(qoff, tq)]
        m_scr[...] = jnp.full(m_scr.shape, -jnp.inf, F32)
        acc_scr[t, 0:V_ROWS, :] = jnp.zeros((V_ROWS, tq), F32)

        def chunk(j, carry):
            off = pl.multiple_of(j * tk, tk)
            st = _dot(k_ref[0, 0, pl.ds(off, tk), :], qt)
            m_prev = m_scr[...]
            m_new = jnp.maximum(m_prev, jnp.max(st, axis=0, keepdims=True))
            pt = jnp.exp2(st - m_new).astype(BF16)
            pv = _dot(vt_ref[0, 0, :, pl.ds(off, tk)], pt)
            acc_scr[t, 0:V_ROWS, :] = jnp.exp2(m_prev - m_new) * acc_scr[t, 0:V_ROWS, :] + pv
            m_scr[...] = m_new
            return carry

        lax.fori_loop(0, nk, chunk, 0)

    def bounded_tile(t, qoff):
        qt = qt_ref[0, 0, :, pl.ds(qoff, tq)]
        q32 = qt.astype(F32)
        bound = jnp.sqrt(jnp.sum(q32 * q32, axis=0, keepdims=True) * k_norm2)
        first = lax.broadcasted_iota(jnp.int32, (BF16_ROWS, tq), 0) == 0
        stab = jnp.where(first, -bound, 0.0).astype(BF16)
        qs = jnp.concatenate([qt[0:QK_DIM], stab, qt[QK_DIM + BF16_ROWS:]], axis=0)
        acc = None
        for j in range(nk):
            st = _dot(k_ref[0, 0, j * tk:(j + 1) * tk, :], qs)
            pv = _dot(vt_ref[0, 0, :, j * tk:(j + 1) * tk], jnp.exp2(st).astype(BF16))
            acc = pv if acc is None else acc + pv
        acc_scr[t, 0:V_ROWS, :] = acc
        return jnp.min(acc[V_HEAD:V_HEAD + 1, :]) >= DENOM_MIN

    group = acc_scr.shape[0]

    def q_group(gi, carry):
        offs = [pl.multiple_of((gi * group + t) * tq, tq) for t in range(group)]
        ok = None
        for t in range(group):
            ok_t = bounded_tile(t, offs[t])
            ok = ok_t if ok is None else jnp.logical_and(ok, ok_t)

        @pl.when(jnp.logical_not(ok))
        def _():
            for t in range(group):
                exact_tile(t, offs[t])

        for t in range(group):
            denom = acc_scr[t, V_HEAD:V_HEAD + 1, :]
            o_ref[0, :, pl.ds(offs[t], tq)] = (acc_scr[t, 0:V_HEAD, :] / denom).astype(o_ref.dtype)
        return carry

    lax.fori_loop(0, nq // group, q_group, 0)


def _attn_call(qt, k, vt, kn, tq, tk):
    B, H, S, _ = k.shape
    kern = functools.partial(_attn_kernel, tq=tq, tk=tk)
    return pl.pallas_call(
        kern,
        grid=(B, H),
        in_specs=[pl.BlockSpec((1, 1, LANES, S), lambda b, h: (b, h, 0, 0)),
                  pl.BlockSpec((1, 1, S, LANES), lambda b, h: (b, h, 0, 0)),
                  pl.BlockSpec((1, 1, V_ROWS, S), lambda b, h: (b, h, 0, 0)),
                  pl.BlockSpec((1, 1) + kn.shape[2:], lambda b, h: (b, h, 0, 0))],
        out_specs=pl.BlockSpec((1, V_HEAD, S), lambda b, h: (b, h, 0)),
        out_shape=jax.ShapeDtypeStruct((B, H * V_HEAD, S), BF16),
        scratch_shapes=[pltpu.VMEM((2 if (S // tq) % 2 == 0 else 1, V_ROWS, tq), F32), pltpu.VMEM((1, tq), F32)],
        compiler_params=_params("parallel", "arbitrary"),
        name="attention",
    )(qt, k, vt, kn)


def _conv_pool_tile(u_ref, up_ref, un_ref, z_ref, zp_ref, zn_ref, cw_ref, cb_ref, lg_ref, lb_ref,
                    pw_ref, ps_ref, oc_ref, op_ref, ubuf, zbuf, ush, *, ts, seq, rc, rp):
    i = pl.program_id(1)
    has_prev = i > 0
    has_next = i < pl.num_programs(1) - 1
    ubuf[0:HALO, :] = jnp.where(has_prev, up_ref[0], 0.0)
    ubuf[HALO:HALO + ts, :] = u_ref[0]
    ubuf[HALO + ts:, :] = jnp.where(has_next, un_ref[0], 0.0)
    zbuf[0:HALO, :] = jnp.where(has_prev, zp_ref[0], 0.0)
    zbuf[HALO:HALO + ts, :] = z_ref[0]
    zbuf[HALO + ts:, :] = jnp.where(has_next, zn_ref[0], 0.0)

    span = ts + 2 * HALO - SUBLANES
    for s in range(1, SUBLANES):
        ush[s - 1, 0:span, :] = ubuf[s:s + span, :]

    pad = CONV_WIDTH // 2
    for r in range(0, ts, rc):
        acc = jnp.zeros((rc, CONV_CH), F32)
        for kk in range(CONV_WIDTH):
            st = HALO + r + kk - pad
            s, base = st % SUBLANES, st - st % SUBLANES
            tap = ubuf[base:base + rc, :] if s == 0 else ush[s - 1, base:base + rc, :]
            acc = acc + tap * cw_ref[kk:kk + 1, :]
        y = acc + cb_ref[...]
        mu = jnp.mean(y, axis=-1, keepdims=True)
        yc = y - mu
        yn = yc * lax.rsqrt(jnp.mean(yc * yc, axis=-1, keepdims=True) + EPS)
        yn = yn * lg_ref[...] + lb_ref[...]
        oc_ref[r:r + rc, :] = (yn * _sigmoid(yn)).astype(oc_ref.dtype)

    for r in range(0, ts, rp):
        t = i * ts + r + lax.broadcasted_iota(jnp.int32, (rp, 1), 0)
        for g, w in enumerate(POOL_WINDOWS):
            lo = w // 2
            hi = w - 1 - lo
            cols = slice(g * GC, (g + 1) * GC)
            win = jnp.zeros((rp, GC), F32)
            for j in range(-lo, hi + 1):
                st = HALO + r + j
                win = win + zbuf[st:st + rp, cols]
            cnt = (jnp.minimum(t + hi + 1, seq) - jnp.maximum(t - lo, 0)).astype(F32)
            d = win / cnt - zbuf[HALO + r:HALO + r + rp, cols]
            yg = _dot(d.astype(BF16), pw_ref[g]) * ps_ref[:, cols]
            op_ref[r:r + rp, cols] = yg.astype(op_ref.dtype)


def _fft1_kernel(p_ref, q_ref, f_ref, o_ref):
    for j in range(p_ref.shape[2]):
        rhs = jnp.concatenate([p_ref[0, :, j, :], q_ref[0, :, j, :]], axis=0).astype(BF16)
        o_ref[0, :, j, :] = _dot(f_ref[...], rhs)


def _fft1_call(p, q, tabs):
    B, S, W = p.shape
    n2 = S // FFT_N1
    pv = p.reshape(B, FFT_N1, n2, W)
    qv = q.reshape(B, FFT_N1, n2, W)
    blk = pl.BlockSpec((1, FFT_N1, SUBLANES, W), lambda b, j: (b, 0, j, 0))
    return pl.pallas_call(
        _fft1_kernel,
        grid=(B, n2 // SUBLANES),
        in_specs=[blk, blk, _const_spec(tabs["f1"].shape)],
        out_specs=pl.BlockSpec((1, 2 * FFT_N1, SUBLANES, W), lambda b, j: (b, 0, j, 0)),
        out_shape=jax.ShapeDtypeStruct((B, 2 * FFT_N1, n2, W), F32),
        compiler_params=_params("parallel", "parallel"),
        name="fft_stage1",
    )(pv, qv, tabs["f1"])


def _fft2_kernel(a_ref, gc_ref, gs_ref, o_ref, *, kb):
    for j in range(kb):
        f = _dot(gc_ref[j], a_ref[0, 0, j].astype(BF16)) + _dot(gs_ref[j], a_ref[0, 1, j].astype(BF16))
        o_ref[0, :, j, :] = f


def _fft2_call(a, tabs, seq, kb):
    B = a.shape[0]
    n2 = seq // FFT_N1
    av = a.reshape(B, 2, FFT_N1, n2, W_BR)
    kern = functools.partial(_fft2_kernel, kb=kb)
    out = pl.pallas_call(
        kern,
        grid=(B, FFT_N1 // kb),
        in_specs=[pl.BlockSpec((1, 2, kb, n2, W_BR), lambda b, j: (b, 0, j, 0, 0)),
                  pl.BlockSpec((kb, n2, n2), lambda b, j: (j, 0, 0)),
                  pl.BlockSpec((kb, n2, n2), lambda b, j: (j, 0, 0))],
        out_specs=pl.BlockSpec((1, n2, kb, W_BR), lambda b, j: (b, 0, j, 0)),
        out_shape=jax.ShapeDtypeStruct((B, n2, FFT_N1, W_BR), F32),
        compiler_params=_params("parallel", "parallel"),
        name="fft_stage2",
    )(av, tabs["g2c"], tabs["g2s"])
    return out.reshape(B, seq, W_BR)


def _mixout_kernel(x_ref, mod_ref, g_ref, wg_ref, oa_ref, wa_ref, wb_ref, wc_ref, f_ref, wd_ref, wo_ref,
                   u_ref, up_ref, un_ref, z_ref, zp_ref, zn_ref, cw_ref, cbias_ref, lg_ref, lb_ref, pw_ref, ps_ref,
                   o_ref, cb_ref, pc_ref, ubuf, zbuf, ush, *, parts, seq):
    _conv_pool_tile(u_ref, up_ref, un_ref, z_ref, zp_ref, zn_ref, cw_ref, cbias_ref, lg_ref, lb_ref, pw_ref, ps_ref,
                    cb_ref, pc_ref, ubuf, zbuf, ush, ts=x_ref.shape[1], seq=seq, rc=64, rp=min(x_ref.shape[1], 256))
    shift = mod_ref[0, 3:4, :]
    scale = mod_ref[0, 4:5, :]
    gate = mod_ref[0, 5:6, :]
    d = x_ref.shape[-1]
    rows = x_ref.shape[1] // parts
    hbs = []
    for p in range(parts):
        x = x_ref[0, p * rows:(p + 1) * rows, :]
        hbs.append(((_rms(x) * g_ref[2:3, :]) * (1.0 + scale) + shift).astype(BF16))
    ys = []
    for p in range(parts):
        sl = slice(p * rows, (p + 1) * rows)
        merged = None
        for br, (b_ref, w_ref) in enumerate(((oa_ref, wa_ref), (cb_ref, wb_ref), (pc_ref, wc_ref), (f_ref, wd_ref))):
            gl = _sigmoid(_dot(hbs[p], wg_ref[:, br * d:(br + 1) * d]))
            if br == 0:
                yb = lax.dot_general(b_ref[0, :, sl], w_ref[...], (((0,), (0,)), ((), ())),
                                     preferred_element_type=F32)
            elif br == 3:
                yb = _dot(b_ref[0, sl, :].astype(BF16), w_ref[...])
            else:
                yb = _dot(b_ref[sl, :], w_ref[...])
            term = gl * yb
            merged = term if merged is None else merged + term
        ys.append(_dot(merged.astype(BF16), wo_ref[...]))
    for p in range(parts):
        sl = slice(p * rows, (p + 1) * rows)
        o_ref[0, sl, :] = x_ref[0, sl, :] + gate * (_rms(ys[p]) * g_ref[3:4, :])


def _mixout_call(x, mod_l, g_l, lw, oa, u, zc, f, tm):
    B, S, D = x.shape
    nh = tm // HALO
    last = S // HALO - 1
    tok = lambda w: pl.BlockSpec((1, tm, w), lambda b, i: (b, i, 0))
    prev = lambda w: pl.BlockSpec((1, HALO, w), lambda b, i: (b, jnp.maximum(i * nh - 1, 0), 0))
    nxt = lambda w: pl.BlockSpec((1, HALO, w), lambda b, i: (b, jnp.minimum((i + 1) * nh, last), 0))
    return pl.pallas_call(
        functools.partial(_mixout_kernel, parts=2 if tm % 512 == 0 else 1, seq=S),
        grid=(B, S // tm),
        in_specs=[tok(D),
                  pl.BlockSpec((1, N_MOD, D), lambda b, i: (b, 0, 0)),
                  _const_spec(g_l.shape),
                  _const_spec(lw["wg"].shape),
                  pl.BlockSpec((1, N_HEADS * V_HEAD, tm), lambda b, i: (b, 0, i)), _const_spec(lw["wa"].shape),
                  _const_spec(lw["wb"].shape), _const_spec(lw["wc"].shape),
                  tok(W_BR), _const_spec(lw["wd"].shape),
                  _const_spec(lw["wo"].shape),
                  tok(CONV_CH), prev(CONV_CH), nxt(CONV_CH), tok(W_BR), prev(W_BR), nxt(W_BR),
                  _const_spec(lw["conv_w"].shape), _const_spec(lw["conv_b"].shape),
                  _const_spec(lw["ln_g"].shape), _const_spec(lw["ln_b"].shape),
                  _const_spec(lw["pool_w"].shape), _const_spec(lw["pool_scale"].shape)],
        out_specs=tok(D),
        out_shape=jax.ShapeDtypeStruct(x.shape, F32),
        scratch_shapes=[pltpu.VMEM((tm, CONV_CH), BF16), pltpu.VMEM((tm, W_BR), BF16),
                        pltpu.VMEM((tm + 2 * HALO, CONV_CH), F32),
                        pltpu.VMEM((tm + 2 * HALO, W_BR), F32),
                        pltpu.VMEM((SUBLANES - 1, tm + 2 * HALO, CONV_CH), F32)],
        compiler_params=_params("parallel", "parallel"),
        name="mixer_out",
    )(x, mod_l, g_l, lw["wg"], oa, lw["wa"], lw["wb"], lw["wc"], f, lw["wd"], lw["wo"],
      u, u, u, zc, zc, zc, lw["conv_w"], lw["conv_b"], lw["ln_g"], lw["ln_b"], lw["pool_w"], lw["pool_scale"])


def _rot_half_cols(w):
    half = w.shape[-1] // 2
    return jnp.concatenate([-w[..., half:], w[..., :half]], axis=-1)


def _tables(seq):
    f32 = np.float32
    pos = np.arange(seq, dtype=f32)
    inv = (f32(ROPE_THETA) ** (-np.arange(0, QK_ROPE, 2, dtype=f32) / f32(QK_ROPE))).astype(f32)
    ang = pos[:, None] * inv[None, :]
    cos, sin = np.cos(ang).astype(f32), np.sin(ang).astype(f32)
    cc = np.concatenate([cos, cos], axis=-1)
    ss = np.concatenate([sin, sin], axis=-1)
    z64 = np.zeros((seq, QK_NOPE), f32)
    z32 = np.zeros((seq, LANES - QK_NOPE - QK_ROPE), f32)
    sm_scale = QK_DIM ** -0.5 * math.log2(math.e)
    cosk = np.concatenate([z64, cc, z32], axis=-1)
    sink = np.concatenate([z64, ss, z32], axis=-1)
    cosr = np.tile(cc, (1, N_HEADS)) * f32(sm_scale)
    sinr = np.tile(ss, (1, N_HEADS)) * f32(sm_scale)

    def cos_sin(m, period):
        th = m.astype(np.float64) * (2.0 * math.pi / period)
        return np.cos(th), np.sin(th)

    def dft(n):
        a = np.arange(n, dtype=np.int64)
        return cos_sin((a[:, None] * a[None, :]) % n, n)

    bf = lambda a: a.astype(f32).astype(BF16)
    c_ch, s_ch = dft(GC)
    cs = bf(np.concatenate([c_ch, s_ch], axis=-1) * GC ** -0.5)

    n1 = FFT_N1
    n2 = seq // n1
    c1, s1 = dft(n1)
    f1 = bf(np.concatenate([np.concatenate([c1, -s1], axis=1), np.concatenate([-s1, -c1], axis=1)], axis=0)
            * n1 ** -0.5)
    k1 = np.arange(n1, dtype=np.int64)[:, None, None]
    k2 = np.arange(n2, dtype=np.int64)[None, :, None]
    t2 = np.arange(n2, dtype=np.int64)[None, None, :]
    c2, s2 = cos_sin(((k1 + n1 * k2) * t2) % seq, seq)
    g2c = bf(c2 * n2 ** -0.5)
    g2s = bf(s2 * n2 ** -0.5)
    return dict(cosr=cosr, sinr=sinr, q_scale=sm_scale, cosk=cosk, sink=sink, cs=cs, f1=f1, g2c=g2c, g2s=g2s)


def _layer_weights(l, w_in, q_norm_g, w_uq, kv_norm_g, w_ukv, w_a, conv_w, conv_b, conv_ln_g,
                   conv_ln_b, w_b, pool_w, pool_scale, w_c, w_d, w_out):
    D = w_in.shape[1]
    wi = w_in[l]
    assert Q_PAD - Q_LORA == QK_ROPE
    w1 = jnp.concatenate([wi[:, Q_LORA:Q_LORA + KV_LORA], wi[:, :Q_LORA], wi[:, Q_LORA + KV_LORA:4 * W_BR]],
                         axis=-1).astype(BF16)

    uq = w_uq[l].reshape(Q_LORA, N_HEADS, QK_NOPE + QK_ROPE)
    nope, rope = uq[..., :QK_NOPE], uq[..., QK_NOPE:]
    wq = jnp.concatenate([nope.reshape(Q_LORA, N_HEADS * QK_NOPE), rope.reshape(Q_LORA, N_HEADS * QK_ROPE),
                          _rot_half_cols(rope).reshape(Q_LORA, N_HEADS * QK_ROPE)], axis=-1)
    wq = jnp.pad(wq, ((0, Q_PAD - Q_LORA), (0, 0))).astype(BF16)

    ukv = w_ukv[l].reshape(KV_LORA, N_HEADS, QK_NOPE + V_HEAD)
    zk = jnp.zeros((KV_LORA, N_HEADS, LANES - QK_NOPE), F32)
    wk = jnp.concatenate([ukv[..., :QK_NOPE], zk], axis=-1).reshape(KV_LORA, N_HEADS * LANES).astype(BF16)
    wv = ukv[..., QK_NOPE:].reshape(KV_LORA, N_HEADS * V_HEAD).astype(BF16)

    gq = jnp.pad(q_norm_g[l], (0, Q_PAD - Q_LORA)).reshape(1, Q_PAD)
    return dict(
        w1=w1, wq=wq, wk=wk, wv=wv, gq=gq, gkv=kv_norm_g[l].reshape(1, KV_LORA),
        wg=wi[:, 4 * W_BR:].astype(BF16), wa=w_a[l].astype(BF16),
        wb=w_b[l].astype(BF16), wc=w_c[l].astype(BF16), wd=w_d[l].astype(BF16), wo=w_out[l].astype(BF16),
        conv_w=conv_w[l], conv_b=conv_b[l].reshape(1, CONV_CH),
        ln_g=conv_ln_g[l].reshape(1, CONV_CH), ln_b=conv_ln_b[l].reshape(1, CONV_CH),
        pool_w=pool_w[l].astype(BF16), pool_scale=pool_scale[l].reshape(1, W_BR))


def _tiles(seq):
    t = lambda n: min(n, seq)
    return dict(ffn=t(1024), mix=t(512), mixout=t(512), tq=t(1024), tk=t(1024))


def kernel(x, c, ada_w, ada_b, norm_g, ffn1_w_in, ffn1_w_out, ffn2_w_in, ffn2_w_out, w_in, q_norm_g,
           w_uq, kv_norm_g, w_ukv, w_a, conv_w, conv_b, conv_ln_g, conv_ln_b, w_b, pool_w, pool_scale,
           w_c, w_d, w_out):
    B, S, D = x.shape
    L = ada_w.shape[0]
    assert D == D_MODEL and S % (FFT_N1 * 16) == 0
    ts = _tiles(S)
    tabs = _tables(S)
    mod = _mod_call(c, ada_w, ada_b).reshape(L, B, N_MOD, D)
    for l in range(L):
        lw = _layer_weights(l, w_in, q_norm_g, w_uq, kv_norm_g, w_ukv, w_a, conv_w, conv_b, conv_ln_g,
                            conv_ln_b, w_b, pool_w, pool_scale, w_c, w_d, w_out)
        mod_l, g_l = mod[l], norm_g[l]
        x = _ffn_call(x, mod_l, g_l, ffn1_w_in[l].astype(BF16), ffn1_w_out[l].astype(BF16), 0, ts["ffn"])
        q, k, v, kn, u, zc, p, qq = _mixin_call(x, mod_l, g_l, lw, tabs, ts["mix"])
        oa = _attn_call(q, k, v, kn, ts["tq"], ts["tk"])
        a = _fft1_call(p, qq, tabs)
        f = _fft2_call(a, tabs, S, 8)
        x = _mixout_call(x, mod_l, g_l, lw, oa, u, zc, f, ts["mixout"])
        x = _ffn_call(x, mod_l, g_l, ffn2_w_in[l].astype(BF16), ffn2_w_out[l].astype(BF16), 2, ts["ffn"])
    return x
```

```python
import functools
import math

import jax
import jax.numpy as jnp
import numpy as np
from jax import lax
from jax.experimental import pallas as pl
from jax.experimental.pallas import tpu as pltpu

F32 = jnp.float32
BF16 = jnp.bfloat16

D_MODEL = 1024
D_FF = 2816
N_HEADS = 8
Q_LORA = 352
KV_LORA = 128
QK_NOPE = 64
QK_ROPE = 32
V_HEAD = 64
QK_DIM = QK_NOPE + QK_ROPE
BF16_ROWS = 16
V_ROWS = 80
DENOM_MIN = 2.0 ** -80
ROPE_THETA = 10000.0
CONV_CH = 256
CONV_WIDTH = 31
POOL_WINDOWS = (2, 4, 8, 16)
GC = 128
N_GROUPS = 4
W_BR = 512
N_MOD = 9
EPS = 1e-6

LANES = 128
SUBLANES = 8
Q_PAD = 384
HALO = 16
FFT_N1 = 128
VMEM_LIMIT = 56 * 1024 * 1024

C_CKV = 0
C_CQ = C_CKV + KV_LORA
C_ZB = C_CQ + Q_PAD
C_ZC = C_ZB + W_BR
C_ZD = C_ZC + W_BR
W1_COLS = C_ZD + W_BR


def _params(*sem):
    return pltpu.CompilerParams(dimension_semantics=sem, vmem_limit_bytes=VMEM_LIMIT)


def _const_spec(shape):
    zeros = (0,) * len(shape)
    return pl.BlockSpec(shape, lambda *_: zeros, pipeline_mode=pl.Buffered(1))


def _dot(a, b):
    return jnp.dot(a, b, preferred_element_type=F32)


def _rms(x):
    return x * lax.rsqrt(jnp.mean(x * x, axis=-1, keepdims=True) + EPS)


def _sigmoid(x):
    return 1.0 / (1.0 + jnp.exp(-x))


def _mod_kernel(c_ref, w_ref, b_ref, o_ref):
    c = c_ref[...]
    ca = c * _sigmoid(c)
    o_ref[0] = jnp.dot(ca, w_ref[0], preferred_element_type=F32,
                       precision=lax.Precision.HIGHEST) + b_ref[0]


def _mod_call(c, ada_w, ada_b):
    L, D, N = ada_w.shape
    B = c.shape[0]
    tn = 1152
    return pl.pallas_call(
        _mod_kernel,
        grid=(L, N // tn),
        in_specs=[pl.BlockSpec((B, D), lambda l, j: (0, 0)),
                  pl.BlockSpec((1, D, tn), lambda l, j: (l, 0, j)),
                  pl.BlockSpec((1, 1, tn), lambda l, j: (l, 0, j))],
        out_specs=pl.BlockSpec((1, B, tn), lambda l, j: (l, 0, j)),
        out_shape=jax.ShapeDtypeStruct((L, B, N), F32),
        compiler_params=_params("parallel", "parallel"),
        name="adaln_mod",
    )(c, ada_w, ada_b.reshape(L, 1, N))


def _ffn_kernel(x_ref, mod_ref, g_ref, win_ref, wout_ref, o_ref, a_scr, *, sub, chunk, parts):
    shift = mod_ref[0, 3 * sub:3 * sub + 1, :]
    scale = mod_ref[0, 3 * sub + 1:3 * sub + 2, :]
    gate = mod_ref[0, 3 * sub + 2:3 * sub + 3, :]
    g_in = g_ref[2 * sub:2 * sub + 1, :]
    g_out = g_ref[2 * sub + 1:2 * sub + 2, :]
    rows = x_ref.shape[1] // parts
    hbs = []
    for p in range(parts):
        x = x_ref[0, p * rows:(p + 1) * rows, :]
        hbs.append(((_rms(x) * g_in) * (1.0 + scale) + shift).astype(BF16))
    for p in range(parts):
        for j in range(D_FF // chunk):
            gj = _dot(hbs[p], win_ref[:, j * chunk:(j + 1) * chunk])
            uj = _dot(hbs[p], win_ref[:, D_FF + j * chunk:D_FF + (j + 1) * chunk])
            a_scr[p, :, j * chunk:(j + 1) * chunk] = (gj * _sigmoid(gj) * uj).astype(BF16)
    for p in range(parts):
        y = _dot(a_scr[p], wout_ref[...])
        x = x_ref[0, p * rows:(p + 1) * rows, :]
        o_ref[0, p * rows:(p + 1) * rows, :] = x + (0.5 * gate) * (_rms(y) * g_out)


def _ffn_call(x, mod_l, g_l, w_in, w_out, sub, tm):
    B, S, D = x.shape
    parts = 4 if tm % 1024 == 0 else 1
    kern = functools.partial(_ffn_kernel, sub=sub, chunk=256, parts=parts)
    return pl.pallas_call(
        kern,
        grid=(B, S // tm),
        in_specs=[pl.BlockSpec((1, tm, D), lambda b, i: (b, i, 0)),
                  pl.BlockSpec((1, N_MOD, D), lambda b, i: (b, 0, 0)),
                  _const_spec(g_l.shape),
                  _const_spec(w_in.shape),
                  _const_spec(w_out.shape)],
        out_specs=pl.BlockSpec((1, tm, D), lambda b, i: (b, i, 0)),
        out_shape=jax.ShapeDtypeStruct(x.shape, F32),
        scratch_shapes=[pltpu.VMEM((parts, tm // parts, D_FF), BF16)],
        compiler_params=_params("parallel", "parallel"),
        name=f"ffn{sub}",
    )(x, mod_l, g_l, w_in, w_out)


def _mixin_kernel(x_ref, mod_ref, g_ref, w1_ref, gq_ref, gkv_ref, wq_ref, wk_ref, wv_ref,
                  cr_ref, sr_ref, ck_ref, sk_ref, cs_ref,
                  q_ref, k_ref, v_ref, kn_ref, u_ref, zc_ref, p_ref, qq_ref, *, q_scale):
    x = x_ref[0]
    tm = x.shape[0]
    shift = mod_ref[0, 3:4, :]
    scale = mod_ref[0, 4:5, :]
    h = (_rms(x) * g_ref[2:3, :]) * (1.0 + scale) + shift
    hb = h.astype(BF16)
    z = _dot(hb, w1_ref[...])

    cq = z[:, C_CQ:C_CQ + Q_PAD]
    qlane = lax.broadcasted_iota(jnp.int32, (1, Q_PAD), 1)
    cqm = jnp.where(qlane < Q_LORA, cq, 0.0)
    cqn = cqm * lax.rsqrt(jnp.sum(cqm * cqm, axis=-1, keepdims=True) * (1.0 / Q_LORA) + EPS)
    cqn = (cqn * gq_ref[...]).astype(BF16)
    qq = _dot(cqn, wq_ref[...])
    nope = qq[:, :N_HEADS * QK_NOPE] * q_scale
    w_rope = N_HEADS * QK_ROPE
    roped = (qq[:, N_HEADS * QK_NOPE:N_HEADS * QK_NOPE + w_rope] * cr_ref[...]
             + qq[:, N_HEADS * QK_NOPE + w_rope:] * sr_ref[...])
    for blk in range(N_HEADS * QK_NOPE // LANES):
        nt = nope[:, blk * LANES:(blk + 1) * LANES].T.astype(BF16)
        for i in range(LANES // QK_NOPE):
            q_ref[0, blk * (LANES // QK_NOPE) + i, 0:QK_NOPE, :] = nt[i * QK_NOPE:(i + 1) * QK_NOPE]
    for blk in range(w_rope // LANES):
        rt = roped[:, blk * LANES:(blk + 1) * LANES].T.astype(BF16)
        for i in range(LANES // QK_ROPE):
            q_ref[0, blk * (LANES // QK_ROPE) + i, QK_NOPE:QK_NOPE + QK_ROPE, :] = rt[i * QK_ROPE:(i + 1) * QK_ROPE]
    zero_rows = jnp.zeros((LANES - QK_NOPE - QK_ROPE, tm), BF16)
    for hd in range(N_HEADS):
        q_ref[0, hd, QK_NOPE + QK_ROPE:, :] = zero_rows

    kr_at = Q_LORA - (Q_PAD - LANES)
    kr = pltpu.roll(cq[:, Q_PAD - LANES:], (QK_NOPE - kr_at) % LANES, 1)
    klane = lax.broadcasted_iota(jnp.int32, (1, LANES), 1)
    kr_rot = jnp.where(klane < QK_NOPE + QK_ROPE // 2,
                       -pltpu.roll(kr, LANES - QK_ROPE // 2, 1), pltpu.roll(kr, QK_ROPE // 2, 1))
    k_rope = kr * ck_ref[...] + kr_rot * sk_ref[...]

    ckv = z[:, C_CKV:C_CKV + KV_LORA]
    ckvn = (_rms(ckv) * gkv_ref[...]).astype(BF16)
    kk = _dot(ckvn, wk_ref[...])
    ones_lane = (klane == QK_DIM).astype(F32)
    for hd in range(N_HEADS):
        kh = kk[:, hd * LANES:(hd + 1) * LANES] + k_rope
        k_ref[0, hd] = (kh + ones_lane).astype(BF16)
        norm2 = jnp.max(jnp.sum(kh * kh, axis=-1, keepdims=True), axis=0, keepdims=True)
        kn_ref[0, hd] = jnp.broadcast_to(norm2, (SUBLANES, LANES))
    vv = _dot(ckvn, wv_ref[...])
    ones_rows = (lax.broadcasted_iota(jnp.int32, (V_ROWS - V_HEAD, tm), 0) == 0).astype(BF16)
    for blk in range(N_HEADS * V_HEAD // LANES):
        vt = vv[:, blk * LANES:(blk + 1) * LANES].T.astype(BF16)
        for i in range(LANES // V_HEAD):
            v_ref[0, blk * (LANES // V_HEAD) + i, 0:V_HEAD, :] = vt[i * V_HEAD:(i + 1) * V_HEAD]
    for hd in range(N_HEADS):
        v_ref[0, hd, V_HEAD:, :] = ones_rows

    a = z[:, C_ZB:C_ZB + CONV_CH]
    gt = z[:, C_ZB + CONV_CH:C_ZB + 2 * CONV_CH]
    u_ref[0] = a * _sigmoid(gt)

    zc_ref[0] = z[:, C_ZC:C_ZC + W_BR]

    for g in range(N_GROUPS):
        zd = z[:, C_ZD + g * GC:C_ZD + (g + 1) * GC].astype(BF16)
        pq = _dot(zd, cs_ref[...])
        p_ref[0, :, g * GC:(g + 1) * GC] = pq[:, :GC]
        qq_ref[0, :, g * GC:(g + 1) * GC] = pq[:, GC:]


def _mixin_call(x, mod_l, g_l, lw, tabs, tm):
    B, S, D = x.shape
    tok = lambda w: pl.BlockSpec((1, tm, w), lambda b, i: (b, i, 0))
    head = pl.BlockSpec((1, N_HEADS, tm, LANES), lambda b, i: (b, 0, i, 0))
    head_t = pl.BlockSpec((1, N_HEADS, LANES, tm), lambda b, i: (b, 0, 0, i))
    tab = lambda w: pl.BlockSpec((tm, w), lambda b, i: (i, 0))
    hshape = jax.ShapeDtypeStruct((B, N_HEADS, S, LANES), BF16)
    hshape_t = jax.ShapeDtypeStruct((B, N_HEADS, LANES, S), BF16)
    return pl.pallas_call(
        functools.partial(_mixin_kernel, q_scale=tabs["q_scale"]),
        grid=(B, S // tm),
        in_specs=[tok(D),
                  pl.BlockSpec((1, N_MOD, D), lambda b, i: (b, 0, 0)),
                  _const_spec(g_l.shape),
                  _const_spec(lw["w1"].shape),
                  _const_spec(lw["gq"].shape),
                  _const_spec(lw["gkv"].shape),
                  _const_spec(lw["wq"].shape),
                  _const_spec(lw["wk"].shape),
                  _const_spec(lw["wv"].shape),
                  tab(N_HEADS * QK_ROPE), tab(N_HEADS * QK_ROPE), tab(LANES), tab(LANES),
                  _const_spec(tabs["cs"].shape)],
        out_specs=[head_t, head, pl.BlockSpec((1, N_HEADS, V_ROWS, tm), lambda b, i: (b, 0, 0, i)),
                   pl.BlockSpec((1, N_HEADS, SUBLANES, LANES), lambda b, i: (b, 0, i, 0)),
                   tok(CONV_CH), tok(W_BR), tok(W_BR), tok(W_BR)],
        out_shape=[hshape_t, hshape, jax.ShapeDtypeStruct((B, N_HEADS, V_ROWS, S), BF16),
                   jax.ShapeDtypeStruct((B, N_HEADS, (S // tm) * SUBLANES, LANES), F32),
                   jax.ShapeDtypeStruct((B, S, CONV_CH), F32),
                   jax.ShapeDtypeStruct((B, S, W_BR), F32),
                   jax.ShapeDtypeStruct((B, S, W_BR), F32),
                   jax.ShapeDtypeStruct((B, S, W_BR), F32)],
        compiler_params=_params("parallel", "parallel"),
        name="mixer_in",
    )(x, mod_l, g_l, lw["w1"], lw["gq"], lw["gkv"], lw["wq"], lw["wk"], lw["wv"],
      tabs["cosr"], tabs["sinr"], tabs["cosk"], tabs["sink"], tabs["cs"])


def _attn_kernel(qt_ref, k_ref, vt_ref, kn_ref, o_ref, acc_scr, m_scr, *, tq, tk):
    seq = k_ref.shape[2]
    nq, nk = seq // tq, seq // tk
    k_norm2 = jnp.max(kn_ref[0, 0])

    def exact_tile(t, qoff):
        qt = qt_ref[0, 0, :, pl.ds(qoff, tq)]
        m_scr[...] = jnp.full(m_scr.shape, -jnp.inf, F32)
        acc_scr[t, 0:V_ROWS, :] = jnp.zeros((V_ROWS, tq), F32)

        def chunk(j, carry):
            off = pl.multiple_of(j * tk, tk)
            st = _dot(k_ref[0, 0, pl.ds(off, tk), :], qt)
            m_prev = m_scr[...]
            m_new = jnp.maximum(m_prev, jnp.max(st, axis=0, keepdims=True))
            pt = jnp.exp2(st - m_new).astype(BF16)
            pv = _dot(vt_ref[0, 0, :, pl.ds(off, tk)], pt)
            acc_scr[t, 0:V_ROWS, :] = jnp.exp2(m_prev - m_new) * acc_scr[t, 0:V_ROWS, :] + pv
            m_scr[...] = m_new
            return carry

        lax.fori_loop(0, nk, chunk, 0)

    def bounded_tile(t, qoff):
        qt = qt_ref[0, 0, :, pl.ds(qoff, tq)]
        q32 = qt.astype(F32)
        bound = jnp.sqrt(jnp.sum(q32 * q32, axis=0, keepdims=True) * k_norm2)
        first = lax.broadcasted_iota(jnp.int32, (BF16_ROWS, tq), 0) == 0
        stab = jnp.where(first, -bound, 0.0).astype(BF16)
        qs = jnp.concatenate([qt[0:QK_DIM], stab, qt[QK_DIM + BF16_ROWS:]], axis=0)
        acc = None
        for j in range(nk):
            st = _dot(k_ref[0, 0, j * tk:(j + 1) * tk, :], qs)
            pv = _dot(vt_ref[0, 0, :, j * tk:(j + 1) * tk], jnp.exp2(st).astype(BF16))
            acc = pv if acc is None else acc + pv
        acc_scr[t, 0:V_ROWS, :] = acc
        return jnp.min(acc[V_HEAD:V_HEAD + 1, :]) >= DENOM_MIN

    group = acc_scr.shape[0]

    def q_group(gi, carry):
        offs = [pl.multiple_of((gi * group + t) * tq, tq) for t in range(group)]
        ok = None
        for t in range(group):
            ok_t = bounded_tile(t, offs[t])
            ok = ok_t if ok is None else jnp.logical_and(ok, ok_t)

        @pl.when(jnp.logical_not(ok))
        def _():
            for t in range(group):
                exact_tile(t, offs[t])

        for t in range(group):
            denom = acc_scr[t, V_HEAD:V_HEAD + 1, :]
            o_ref[0, :, pl.ds(offs[t], tq)] = (acc_scr[t, 0:V_HEAD, :] / denom).astype(o_ref.dtype)
        return carry

    lax.fori_loop(0, nq // group, q_group, 0)


def _attn_call(qt, k, vt, kn, tq, tk):
    B, H, S, _ = k.shape
    kern = functools.partial(_attn_kernel, tq=tq, tk=tk)
    return pl.pallas_call(
        kern,
        grid=(B, H),
        in_specs=[pl.BlockSpec((1, 1, LANES, S), lambda b, h: (b, h, 0, 0)),
                  pl.BlockSpec((1, 1, S, LANES), lambda b, h: (b, h, 0, 0)),
                  pl.BlockSpec((1, 1, V_ROWS, S), lambda b, h: (b, h, 0, 0)),
                  pl.BlockSpec((1, 1) + kn.shape[2:], lambda b, h: (b, h, 0, 0))],
        out_specs=pl.BlockSpec((1, V_HEAD, S), lambda b, h: (b, h, 0)),
        out_shape=jax.ShapeDtypeStruct((B, H * V_HEAD, S), BF16),
        scratch_shapes=[pltpu.VMEM((2 if (S // tq) % 2 == 0 else 1, V_ROWS, tq), F32), pltpu.VMEM((1, tq), F32)],
        compiler_params=_params("parallel", "arbitrary"),
        name="attention",
    )(qt, k, vt, kn)


def _local_kernel(u_ref, up_ref, un_ref, z_ref, zp_ref, zn_ref, cw_ref, cb_ref, lg_ref, lb_ref,
                  pw_ref, ps_ref, oc_ref, op_ref, ubuf, zbuf, ush, sbuf, *, ts, seq, rc, rp):
    i = pl.program_id(1)
    has_prev = i > 0
    has_next = i < pl.num_programs(1) - 1
    ubuf[0:HALO, :] = jnp.where(has_prev, up_ref[0], 0.0)
    ubuf[HALO:HALO + ts, :] = u_ref[0]
    ubuf[HALO + ts:, :] = jnp.where(has_next, un_ref[0], 0.0)
    zbuf[0:HALO, :] = jnp.where(has_prev, zp_ref[0], 0.0)
    zbuf[HALO:HALO + ts, :] = z_ref[0]
    zbuf[HALO + ts:2 * HALO + ts, :] = jnp.where(has_next, zn_ref[0], 0.0)
    zbuf[2 * HALO + ts:, :] = jnp.zeros((SUBLANES, W_BR), F32)

    span = ts + 2 * HALO - SUBLANES
    for s in range(1, SUBLANES):
        ush[s - 1, 0:span, :] = ubuf[s:s + span, :]

    pad = CONV_WIDTH // 2
    for r in range(0, ts, rc):
        acc = jnp.zeros((rc, CONV_CH), F32)
        for kk in range(CONV_WIDTH):
            st = HALO + r + kk - pad
            s, base = st % SUBLANES, st - st % SUBLANES
            tap = ubuf[base:base + rc, :] if s == 0 else ush[s - 1, base:base + rc, :]
            acc = acc + tap * cw_ref[kk:kk + 1, :]
        y = acc + cb_ref[...]
        mu = jnp.mean(y, axis=-1, keepdims=True)
        yc = y - mu
        yn = yc * lax.rsqrt(jnp.mean(yc * yc, axis=-1, keepdims=True) + EPS)
        yn = yn * lg_ref[...] + lb_ref[...]
        oc_ref[0, r:r + rc, :] = (yn * _sigmoid(yn)).astype(oc_ref.dtype)

    assert all(w == 2 ** (g + 1) for g, w in enumerate(POOL_WINDOWS))
    rows_m = ts + 2 * HALO
    for m in range(1, N_GROUPS):
        step = 2 ** (m - 1)
        src = zbuf if m == 1 else sbuf.at[m - 2]
        lanes = slice(m * GC, N_GROUPS * GC)
        for c in range(0, rows_m, LANES):
            n = min(LANES, rows_m - c)
            sbuf[m - 1, c:c + n, lanes] = src[c:c + n, lanes] + src[c + step:c + step + n, lanes]
        rows_m -= SUBLANES

    for r in range(0, ts, rp):
        t = i * ts + r + lax.broadcasted_iota(jnp.int32, (rp, 1), 0)
        for g, w in enumerate(POOL_WINDOWS):
            lo = w // 2
            hi = w - 1 - lo
            cols = slice(g * GC, (g + 1) * GC)
            half = zbuf if g == 0 else sbuf.at[g - 1]
            win = half[HALO + r - lo:HALO + r - lo + rp, cols] + half[HALO + r:HALO + r + rp, cols]
            cnt = (jnp.minimum(t + hi + 1, seq) - jnp.maximum(t - lo, 0)).astype(F32)
            d = win / cnt - zbuf[HALO + r:HALO + r + rp, cols]
            yg = _dot(d.astype(BF16), pw_ref[g]) * ps_ref[:, cols]
            op_ref[0, r:r + rp, cols] = yg.astype(op_ref.dtype)


def _local_call(u, zc, lw, ts):
    B, S, _ = u.shape
    nh = ts // HALO
    last = S // HALO - 1
    cur = lambda w: pl.BlockSpec((1, ts, w), lambda b, i: (b, i, 0))
    prev = lambda w: pl.BlockSpec((1, HALO, w), lambda b, i: (b, jnp.maximum(i * nh - 1, 0), 0))
    nxt = lambda w: pl.BlockSpec((1, HALO, w), lambda b, i: (b, jnp.minimum((i + 1) * nh, last), 0))
    kern = functools.partial(_local_kernel, ts=ts, seq=S, rc=64, rp=min(ts, 256))
    return pl.pallas_call(
        kern,
        grid=(B, S // ts),
        in_specs=[cur(CONV_CH), prev(CONV_CH), nxt(CONV_CH), cur(W_BR), prev(W_BR), nxt(W_BR),
                  _const_spec(lw["conv_w"].shape), _const_spec(lw["conv_b"].shape),
                  _const_spec(lw["ln_g"].shape), _const_spec(lw["ln_b"].shape),
                  _const_spec(lw["pool_w"].shape), _const_spec(lw["pool_scale"].shape)],
        out_specs=[cur(CONV_CH), cur(W_BR)],
        out_shape=[jax.ShapeDtypeStruct((B, S, CONV_CH), BF16),
                   jax.ShapeDtypeStruct((B, S, W_BR), BF16)],
        scratch_shapes=[pltpu.VMEM((ts + 2 * HALO, CONV_CH), F32),
                        pltpu.VMEM((ts + 2 * HALO + SUBLANES, W_BR), F32),
                        pltpu.VMEM((SUBLANES - 1, ts + 2 * HALO, CONV_CH), F32),
                        pltpu.VMEM((N_GROUPS - 1, ts + 2 * HALO, W_BR), F32)],
        compiler_params=_params("parallel", "parallel"),
        name="conv_pool",
    )(u, u, u, zc, zc, zc, lw["conv_w"], lw["conv_b"], lw["ln_g"], lw["ln_b"],
      lw["pool_w"], lw["pool_scale"])


def _fft1_kernel(p_ref, q_ref, f_ref, o_ref):
    for j in range(p_ref.shape[2]):
        rhs = jnp.concatenate([p_ref[0, :, j, :], q_ref[0, :, j, :]], axis=0).astype(BF16)
        o_ref[0, :, j, :] = _dot(f_ref[...], rhs)


def _fft1_call(p, q, tabs):
    B, S, W = p.shape
    n2 = S // FFT_N1
    pv = p.reshape(B, FFT_N1, n2, W)
    qv = q.reshape(B, FFT_N1, n2, W)
    blk = pl.BlockSpec((1, FFT_N1, SUBLANES, W), lambda b, j: (b, 0, j, 0))
    return pl.pallas_call(
        _fft1_kernel,
        grid=(B, n2 // SUBLANES),
        in_specs=[blk, blk, _const_spec(tabs["f1"].shape)],
        out_specs=pl.BlockSpec((1, 2 * FFT_N1, SUBLANES, W), lambda b, j: (b, 0, j, 0)),
        out_shape=jax.ShapeDtypeStruct((B, 2 * FFT_N1, n2, W), F32),
        compiler_params=_params("parallel", "parallel"),
        name="fft_stage1",
    )(pv, qv, tabs["f1"])


def _fft2_kernel(a_ref, gc_ref, gs_ref, o_ref, *, kb):
    for j in range(kb):
        f = _dot(gc_ref[j], a_ref[0, 0, j].astype(BF16)) + _dot(gs_ref[j], a_ref[0, 1, j].astype(BF16))
        o_ref[0, :, j, :] = f


def _fft2_call(a, tabs, seq, kb):
    B = a.shape[0]
    n2 = seq // FFT_N1
    av = a.reshape(B, 2, FFT_N1, n2, W_BR)
    kern = functools.partial(_fft2_kernel, kb=kb)
    out = pl.pallas_call(
        kern,
        grid=(B, FFT_N1 // kb),
        in_specs=[pl.BlockSpec((1, 2, kb, n2, W_BR), lambda b, j: (b, 0, j, 0, 0)),
                  pl.BlockSpec((kb, n2, n2), lambda b, j: (j, 0, 0)),
                  pl.BlockSpec((kb, n2, n2), lambda b, j: (j, 0, 0))],
        out_specs=pl.BlockSpec((1, n2, kb, W_BR), lambda b, j: (b, 0, j, 0)),
        out_shape=jax.ShapeDtypeStruct((B, n2, FFT_N1, W_BR), F32),
        compiler_params=_params("parallel", "parallel"),
        name="fft_stage2",
    )(av, tabs["g2c"], tabs["g2s"])
    return out.reshape(B, seq, W_BR)


def _mixout_kernel(x_ref, mod_ref, g_ref, wg_ref, oa_ref, wa_ref, cb_ref, wb_ref, pc_ref, wc_ref,
                   f_ref, wd_ref, wo_ref, o_ref, *, parts):
    shift = mod_ref[0, 3:4, :]
    scale = mod_ref[0, 4:5, :]
    gate = mod_ref[0, 5:6, :]
    d = x_ref.shape[-1]
    rows = x_ref.shape[1] // parts
    hbs = []
    for p in range(parts):
        x = x_ref[0, p * rows:(p + 1) * rows, :]
        hbs.append(((_rms(x) * g_ref[2:3, :]) * (1.0 + scale) + shift).astype(BF16))
    ys = []
    for p in range(parts):
        sl = slice(p * rows, (p + 1) * rows)
        merged = None
        for br, (b_ref, w_ref) in enumerate(((oa_ref, wa_ref), (cb_ref, wb_ref), (pc_ref, wc_ref), (f_ref, wd_ref))):
            gl = _sigmoid(_dot(hbs[p], wg_ref[:, br * d:(br + 1) * d]))
            if br == 0:
                yb = lax.dot_general(b_ref[0, :, sl], w_ref[...], (((0,), (0,)), ((), ())),
                                     preferred_element_type=F32)
            else:
                yb = _dot(b_ref[0, sl, :].astype(BF16), w_ref[...])
            term = gl * yb
            merged = term if merged is None else merged + term
        ys.append(_dot(merged.astype(BF16), wo_ref[...]))
    for p in range(parts):
        sl = slice(p * rows, (p + 1) * rows)
        o_ref[0, sl, :] = x_ref[0, sl, :] + gate * (_rms(ys[p]) * g_ref[3:4, :])


def _mixout_call(x, mod_l, g_l, lw, oa, cb, pc, f, tm):
    B, S, D = x.shape
    tok = lambda w: pl.BlockSpec((1, tm, w), lambda b, i: (b, i, 0))
    return pl.pallas_call(
        functools.partial(_mixout_kernel, parts=2 if tm % 512 == 0 else 1),
        grid=(B, S // tm),
        in_specs=[tok(D),
                  pl.BlockSpec((1, N_MOD, D), lambda b, i: (b, 0, 0)),
                  _const_spec(g_l.shape),
                  _const_spec(lw["wg"].shape),
                  pl.BlockSpec((1, N_HEADS * V_HEAD, tm), lambda b, i: (b, 0, i)), _const_spec(lw["wa"].shape),
                  tok(CONV_CH), _const_spec(lw["wb"].shape),
                  tok(W_BR), _const_spec(lw["wc"].shape),
                  tok(W_BR), _const_spec(lw["wd"].shape),
                  _const_spec(lw["wo"].shape)],
        out_specs=tok(D),
        out_shape=jax.ShapeDtypeStruct(x.shape, F32),
        compiler_params=_params("parallel", "parallel"),
        name="mixer_out",
    )(x, mod_l, g_l, lw["wg"], oa, lw["wa"], cb, lw["wb"], pc, lw["wc"], f, lw["wd"], lw["wo"])


def _rot_half_cols(w):
    half = w.shape[-1] // 2
    return jnp.concatenate([-w[..., half:], w[..., :half]], axis=-1)


def _tables(seq):
    f32 = np.float32
    pos = np.arange(seq, dtype=f32)
    inv = (f32(ROPE_THETA) ** (-np.arange(0, QK_ROPE, 2, dtype=f32) / f32(QK_ROPE))).astype(f32)
    ang = pos[:, None] * inv[None, :]
    cos, sin = np.cos(ang).astype(f32), np.sin(ang).astype(f32)
    cc = np.concatenate([cos, cos], axis=-1)
    ss = np.concatenate([sin, sin], axis=-1)
    z64 = np.zeros((seq, QK_NOPE), f32)
    z32 = np.zeros((seq, LANES - QK_NOPE - QK_ROPE), f32)
    sm_scale = QK_DIM ** -0.5 * math.log2(math.e)
    cosk = np.concatenate([z64, cc, z32], axis=-1)
    sink = np.concatenate([z64, ss, z32], axis=-1)
    cosr = np.tile(cc, (1, N_HEADS)) * f32(sm_scale)
    sinr = np.tile(ss, (1, N_HEADS)) * f32(sm_scale)

    def cos_sin(m, period):
        th = m.astype(np.float64) * (2.0 * math.pi / period)
        return np.cos(th), np.sin(th)

    def dft(n):
        a = np.arange(n, dtype=np.int64)
        return cos_sin((a[:, None] * a[None, :]) % n, n)

    bf = lambda a: a.astype(f32).astype(BF16)
    c_ch, s_ch = dft(GC)
    cs = bf(np.concatenate([c_ch, s_ch], axis=-1) * GC ** -0.5)

    n1 = FFT_N1
    n2 = seq // n1
    c1, s1 = dft(n1)
    f1 = bf(np.concatenate([np.concatenate([c1, -s1], axis=1), np.concatenate([-s1, -c1], axis=1)], axis=0)
            * n1 ** -0.5)
    k1 = np.arange(n1, dtype=np.int64)[:, None, None]
    k2 = np.arange(n2, dtype=np.int64)[None, :, None]
    t2 = np.arange(n2, dtype=np.int64)[None, None, :]
    c2, s2 = cos_sin(((k1 + n1 * k2) * t2) % seq, seq)
    g2c = bf(c2 * n2 ** -0.5)
    g2s = bf(s2 * n2 ** -0.5)
    return dict(cosr=cosr, sinr=sinr, q_scale=sm_scale, cosk=cosk, sink=sink, cs=cs, f1=f1, g2c=g2c, g2s=g2s)


def _layer_weights(l, w_in, q_norm_g, w_uq, kv_norm_g, w_ukv, w_a, conv_w, conv_b, conv_ln_g,
                   conv_ln_b, w_b, pool_w, pool_scale, w_c, w_d, w_out):
    D = w_in.shape[1]
    wi = w_in[l]
    assert Q_PAD - Q_LORA == QK_ROPE
    w1 = jnp.concatenate([wi[:, Q_LORA:Q_LORA + KV_LORA], wi[:, :Q_LORA], wi[:, Q_LORA + KV_LORA:4 * W_BR]],
                         axis=-1).astype(BF16)

    uq = w_uq[l].reshape(Q_LORA, N_HEADS, QK_NOPE + QK_ROPE)
    nope, rope = uq[..., :QK_NOPE], uq[..., QK_NOPE:]
    wq = jnp.concatenate([nope.reshape(Q_LORA, N_HEADS * QK_NOPE), rope.reshape(Q_LORA, N_HEADS * QK_ROPE),
                          _rot_half_cols(rope).reshape(Q_LORA, N_HEADS * QK_ROPE)], axis=-1)
    wq = jnp.pad(wq, ((0, Q_PAD - Q_LORA), (0, 0))).astype(BF16)

    ukv = w_ukv[l].reshape(KV_LORA, N_HEADS, QK_NOPE + V_HEAD)
    zk = jnp.zeros((KV_LORA, N_HEADS, LANES - QK_NOPE), F32)
    wk = jnp.concatenate([ukv[..., :QK_NOPE], zk], axis=-1).reshape(KV_LORA, N_HEADS * LANES).astype(BF16)
    wv = ukv[..., QK_NOPE:].reshape(KV_LORA, N_HEADS * V_HEAD).astype(BF16)

    gq = jnp.pad(q_norm_g[l], (0, Q_PAD - Q_LORA)).reshape(1, Q_PAD)
    return dict(
        w1=w1, wq=wq, wk=wk, wv=wv, gq=gq, gkv=kv_norm_g[l].reshape(1, KV_LORA),
        wg=wi[:, 4 * W_BR:].astype(BF16), wa=w_a[l].astype(BF16),
        wb=w_b[l].astype(BF16), wc=w_c[l].astype(BF16), wd=w_d[l].astype(BF16), wo=w_out[l].astype(BF16),
        conv_w=conv_w[l], conv_b=conv_b[l].reshape(1, CONV_CH),
        ln_g=conv_ln_g[l].reshape(1, CONV_CH), ln_b=conv_ln_b[l].reshape(1, CONV_CH),
        pool_w=pool_w[l].astype(BF16), pool_scale=pool_scale[l].reshape(1, W_BR))


def _tiles(seq):
    t = lambda n: min(n, seq)
    return dict(ffn=t(1024), mix=t(512), mixout=t(512), tq=t(1024), tk=t(1024), local=t(512))


def kernel(x, c, ada_w, ada_b, norm_g, ffn1_w_in, ffn1_w_out, ffn2_w_in, ffn2_w_out, w_in, q_norm_g,
           w_uq, kv_norm_g, w_ukv, w_a, conv_w, conv_b, conv_ln_g, conv_ln_b, w_b, pool_w, pool_scale,
           w_c, w_d, w_out):
    B, S, D = x.shape
    L = ada_w.shape[0]
    assert D == D_MODEL and S % (FFT_N1 * 16) == 0
    ts = _tiles(S)
    tabs = _tables(S)
    mod = _mod_call(c, ada_w, ada_b).reshape(L, B, N_MOD, D)
    for l in range(L):
        lw = _layer_weights(l, w_in, q_norm_g, w_uq, kv_norm_g, w_ukv, w_a, conv_w, conv_b, conv_ln_g,
                            conv_ln_b, w_b, pool_w, pool_scale, w_c, w_d, w_out)
        mod_l, g_l = mod[l], norm_g[l]
        x = _ffn_call(x, mod_l, g_l, ffn1_w_in[l].astype(BF16), ffn1_w_out[l].astype(BF16), 0, ts["ffn"])
        q, k, v, kn, u, zc, p, qq = _mixin_call(x, mod_l, g_l, lw, tabs, ts["mix"])
        oa = _attn_call(q, k, v, kn, ts["tq"], ts["tk"])
        cb, pc = _local_call(u, zc, lw, ts["local"])
        a = _fft1_call(p, qq, tabs)
        f = _fft2_call(a, tabs, S, 8)
        x = _mixout_call(x, mod_l, g_l, lw, oa, cb, pc, f, ts["mixout"])
        x = _ffn_call(x, mod_l, g_l, ffn2_w_in[l].astype(BF16), ffn2_w_out[l].astype(BF16), 2, ts["ffn"])
    return x
```

```python
import functools
import math

import jax
import jax.numpy as jnp
import numpy as np
from jax import lax
from jax.experimental import pallas as pl
from jax.experimental.pallas import tpu as pltpu

F32 = jnp.float32
BF16 = jnp.bfloat16

D_MODEL = 1024
D_FF = 2816
N_HEADS = 8
Q_LORA = 352
KV_LORA = 128
QK_NOPE = 64
QK_ROPE = 32
V_HEAD = 64
QK_DIM = QK_NOPE + QK_ROPE
BF16_ROWS = 16
V_ROWS = 80
DENOM_MIN = 2.0 ** -80
ROPE_THETA = 10000.0
CONV_CH = 256
CONV_WIDTH = 31
POOL_WINDOWS = (2, 4, 8, 16)
GC = 128
N_GROUPS = 4
W_BR = 512
N_MOD = 9
EPS = 1e-6

LANES = 128
SUBLANES = 8
Q_PAD = 384
HALO = 16
FFT_N1 = 128
VMEM_LIMIT = 56 * 1024 * 1024

C_CKV = 0
C_CQ = C_CKV + KV_LORA
C_ZB = C_CQ + Q_PAD
C_ZC = C_ZB + W_BR
C_ZD = C_ZC + W_BR
W1_COLS = C_ZD + W_BR


def _params(*sem):
    return pltpu.CompilerParams(dimension_semantics=sem, vmem_limit_bytes=VMEM_LIMIT)


def _const_spec(shape):
    zeros = (0,) * len(shape)
    return pl.BlockSpec(shape, lambda *_: zeros, pipeline_mode=pl.Buffered(1))


def _dot(a, b):
    return jnp.dot(a, b, preferred_element_type=F32)


def _rms(x):
    return x * lax.rsqrt(jnp.mean(x * x, axis=-1, keepdims=True) + EPS)


def _sigmoid(x):
    return 1.0 / (1.0 + jnp.exp(-x))


def _mod_kernel(c_ref, w_ref, b_ref, o_ref):
    c = c_ref[...]
    ca = c * _sigmoid(c)
    o_ref[0] = jnp.dot(ca, w_ref[0], preferred_element_type=F32,
                       precision=lax.Precision.HIGHEST) + b_ref[0]


def _mod_call(c, ada_w, ada_b):
    L, D, N = ada_w.shape
    B = c.shape[0]
    tn = 1152
    return pl.pallas_call(
        _mod_kernel,
        grid=(L, N // tn),
        in_specs=[pl.BlockSpec((B, D), lambda l, j: (0, 0)),
                  pl.BlockSpec((1, D, tn), lambda l, j: (l, 0, j)),
                  pl.BlockSpec((1, 1, tn), lambda l, j: (l, 0, j))],
        out_specs=pl.BlockSpec((1, B, tn), lambda l, j: (l, 0, j)),
        out_shape=jax.ShapeDtypeStruct((L, B, N), F32),
        compiler_params=_params("parallel", "parallel"),
        name="adaln_mod",
    )(c, ada_w, ada_b.reshape(L, 1, N))


def _ffn_kernel(x_ref, mod_ref, g_ref, win_ref, wout_ref, o_ref, a_scr, *, sub, chunk, parts):
    shift = mod_ref[0, 3 * sub:3 * sub + 1, :]
    scale = mod_ref[0, 3 * sub + 1:3 * sub + 2, :]
    gate = mod_ref[0, 3 * sub + 2:3 * sub + 3, :]
    g_in = g_ref[2 * sub:2 * sub + 1, :]
    g_out = g_ref[2 * sub + 1:2 * sub + 2, :]
    rows = x_ref.shape[1] // parts
    hbs = []
    for p in range(parts):
        x = x_ref[0, p * rows:(p + 1) * rows, :]
        hbs.append(((_rms(x) * g_in) * (1.0 + scale) + shift).astype(BF16))
    for p in range(parts):
        for j in range(D_FF // chunk):
            gj = _dot(hbs[p], win_ref[0, :, j * chunk:(j + 1) * chunk])
            uj = _dot(hbs[p], win_ref[0, :, D_FF + j * chunk:D_FF + (j + 1) * chunk])
            a_scr[p, :, j * chunk:(j + 1) * chunk] = (gj * _sigmoid(gj) * uj).astype(BF16)
    for p in range(parts):
        y = _dot(a_scr[p], wout_ref[0])
        x = x_ref[0, p * rows:(p + 1) * rows, :]
        o_ref[0, p * rows:(p + 1) * rows, :] = x + (0.5 * gate) * (_rms(y) * g_out)


def _cast_kernel(w_ref, o_ref):
    o_ref[...] = w_ref[...].astype(o_ref.dtype)


def _to_bf16(w, rows):
    L, R, C = w.shape
    blk = pl.BlockSpec((1, rows, C), lambda l, i: (l, i, 0))
    return pl.pallas_call(
        _cast_kernel,
        grid=(L, R // rows),
        in_specs=[blk],
        out_specs=blk,
        out_shape=jax.ShapeDtypeStruct(w.shape, BF16),
        compiler_params=_params("parallel", "parallel"),
        name="weight_cast",
    )(w)


def _layer_spec(w, l):
    return pl.BlockSpec((1,) + w.shape[1:], lambda *_: (l,) + (0,) * (w.ndim - 1), pipeline_mode=pl.Buffered(1))


def _ffn_call(x, mod_l, g_l, w_in, w_out, l, sub, tm):
    B, S, D = x.shape
    parts = 4 if tm % 1024 == 0 else 1
    kern = functools.partial(_ffn_kernel, sub=sub, chunk=256, parts=parts)
    return pl.pallas_call(
        kern,
        grid=(B, S // tm),
        in_specs=[pl.BlockSpec((1, tm, D), lambda b, i: (b, i, 0)),
                  pl.BlockSpec((1, N_MOD, D), lambda b, i: (b, 0, 0)),
                  _const_spec(g_l.shape),
                  _layer_spec(w_in, l),
                  _layer_spec(w_out, l)],
        out_specs=pl.BlockSpec((1, tm, D), lambda b, i: (b, i, 0)),
        out_shape=jax.ShapeDtypeStruct(x.shape, F32),
        scratch_shapes=[pltpu.VMEM((parts, tm // parts, D_FF), BF16)],
        compiler_params=_params("parallel", "parallel"),
        name=f"ffn{sub}",
    )(x, mod_l, g_l, w_in, w_out)


def _mixin_kernel(x_ref, mod_ref, g_ref, w1_ref, gq_ref, gkv_ref, wq_ref, wk_ref, wv_ref,
                  cr_ref, sr_ref, ck_ref, sk_ref, cs_ref,
                  q_ref, k_ref, v_ref, kn_ref, u_ref, zc_ref, p_ref, qq_ref, *, q_scale):
    x = x_ref[0]
    tm = x.shape[0]
    shift = mod_ref[0, 3:4, :]
    scale = mod_ref[0, 4:5, :]
    h = (_rms(x) * g_ref[2:3, :]) * (1.0 + scale) + shift
    hb = h.astype(BF16)
    z = _dot(hb, w1_ref[...])

    cq = z[:, C_CQ:C_CQ + Q_PAD]
    qlane = lax.broadcasted_iota(jnp.int32, (1, Q_PAD), 1)
    cqm = jnp.where(qlane < Q_LORA, cq, 0.0)
    cqn = cqm * lax.rsqrt(jnp.sum(cqm * cqm, axis=-1, keepdims=True) * (1.0 / Q_LORA) + EPS)
    cqn = (cqn * gq_ref[...]).astype(BF16)
    qq = _dot(cqn, wq_ref[...])
    nope = qq[:, :N_HEADS * QK_NOPE] * q_scale
    w_rope = N_HEADS * QK_ROPE
    roped = (qq[:, N_HEADS * QK_NOPE:N_HEADS * QK_NOPE + w_rope] * cr_ref[...]
             + qq[:, N_HEADS * QK_NOPE + w_rope:] * sr_ref[...])
    for blk in range(N_HEADS * QK_NOPE // LANES):
        nt = nope[:, blk * LANES:(blk + 1) * LANES].T.astype(BF16)
        for i in range(LANES // QK_NOPE):
            q_ref[0, blk * (LANES // QK_NOPE) + i, 0:QK_NOPE, :] = nt[i * QK_NOPE:(i + 1) * QK_NOPE]
    for blk in range(w_rope // LANES):
        rt = roped[:, blk * LANES:(blk + 1) * LANES].T.astype(BF16)
        for i in range(LANES // QK_ROPE):
            q_ref[0, blk * (LANES // QK_ROPE) + i, QK_NOPE:QK_NOPE + QK_ROPE, :] = rt[i * QK_ROPE:(i + 1) * QK_ROPE]
    zero_rows = jnp.zeros((LANES - QK_NOPE - QK_ROPE, tm), BF16)
    for hd in range(N_HEADS):
        q_ref[0, hd, QK_NOPE + QK_ROPE:, :] = zero_rows

    kr_at = Q_LORA - (Q_PAD - LANES)
    kr = pltpu.roll(cq[:, Q_PAD - LANES:], (QK_NOPE - kr_at) % LANES, 1)
    klane = lax.broadcasted_iota(jnp.int32, (1, LANES), 1)
    kr_rot = jnp.where(klane < QK_NOPE + QK_ROPE // 2,
                       -pltpu.roll(kr, LANES - QK_ROPE // 2, 1), pltpu.roll(kr, QK_ROPE // 2, 1))
    k_rope = kr * ck_ref[...] + kr_rot * sk_ref[...]

    ckv = z[:, C_CKV:C_CKV + KV_LORA]
    ckvn = (_rms(ckv) * gkv_ref[...]).astype(BF16)
    kk = _dot(ckvn, wk_ref[...])
    ones_lane = (klane == QK_DIM).astype(F32)
    for hd in range(N_HEADS):
        kh = kk[:, hd * LANES:(hd + 1) * LANES] + k_rope
        k_ref[0, hd] = (kh + ones_lane).astype(BF16)
        norm2 = jnp.max(jnp.sum(kh * kh, axis=-1, keepdims=True), axis=0, keepdims=True)
        kn_ref[0, hd] = jnp.broadcast_to(norm2, (SUBLANES, LANES))
    vv = _dot(ckvn, wv_ref[...])
    ones_rows = (lax.broadcasted_iota(jnp.int32, (V_ROWS - V_HEAD, tm), 0) == 0).astype(BF16)
    for blk in range(N_HEADS * V_HEAD // LANES):
        vt = vv[:, blk * LANES:(blk + 1) * LANES].T.astype(BF16)
        for i in range(LANES // V_HEAD):
            v_ref[0, blk * (LANES // V_HEAD) + i, 0:V_HEAD, :] = vt[i * V_HEAD:(i + 1) * V_HEAD]
    for hd in range(N_HEADS):
        v_ref[0, hd, V_HEAD:, :] = ones_rows

    a = z[:, C_ZB:C_ZB + CONV_CH]
    gt = z[:, C_ZB + CONV_CH:C_ZB + 2 * CONV_CH]
    u_ref[0] = a * _sigmoid(gt)

    zc_ref[0] = z[:, C_ZC:C_ZC + W_BR]

    for g in range(N_GROUPS):
        zd = z[:, C_ZD + g * GC:C_ZD + (g + 1) * GC].astype(BF16)
        pq = _dot(zd, cs_ref[...])
        p_ref[0, :, g * GC:(g + 1) * GC] = pq[:, :GC]
        qq_ref[0, :, g * GC:(g + 1) * GC] = pq[:, GC:]


def _mixin_call(x, mod_l, g_l, lw, tabs, tm):
    B, S, D = x.shape
    tok = lambda w: pl.BlockSpec((1, tm, w), lambda b, i: (b, i, 0))
    head = pl.BlockSpec((1, N_HEADS, tm, LANES), lambda b, i: (b, 0, i, 0))
    head_t = pl.BlockSpec((1, N_HEADS, LANES, tm), lambda b, i: (b, 0, 0, i))
    tab = lambda w: pl.BlockSpec((tm, w), lambda b, i: (i, 0))
    hshape = jax.ShapeDtypeStruct((B, N_HEADS, S, LANES), BF16)
    hshape_t = jax.ShapeDtypeStruct((B, N_HEADS, LANES, S), BF16)
    return pl.pallas_call(
        functools.partial(_mixin_kernel, q_scale=tabs["q_scale"]),
        grid=(B, S // tm),
        in_specs=[tok(D),
                  pl.BlockSpec((1, N_MOD, D), lambda b, i: (b, 0, 0)),
                  _const_spec(g_l.shape),
                  _const_spec(lw["w1"].shape),
                  _const_spec(lw["gq"].shape),
                  _const_spec(lw["gkv"].shape),
                  _const_spec(lw["wq"].shape),
                  _const_spec(lw["wk"].shape),
                  _const_spec(lw["wv"].shape),
                  tab(N_HEADS * QK_ROPE), tab(N_HEADS * QK_ROPE), tab(LANES), tab(LANES),
                  _const_spec(tabs["cs"].shape)],
        out_specs=[head_t, head, pl.BlockSpec((1, N_HEADS, V_ROWS, tm), lambda b, i: (b, 0, 0, i)),
                   pl.BlockSpec((1, N_HEADS, SUBLANES, LANES), lambda b, i: (b, 0, i, 0)),
                   tok(CONV_CH), tok(W_BR), tok(W_BR), tok(W_BR)],
        out_shape=[hshape_t, hshape, jax.ShapeDtypeStruct((B, N_HEADS, V_ROWS, S), BF16),
                   jax.ShapeDtypeStruct((B, N_HEADS, (S // tm) * SUBLANES, LANES), F32),
                   jax.ShapeDtypeStruct((B, S, CONV_CH), F32),
                   jax.ShapeDtypeStruct((B, S, W_BR), F32),
                   jax.ShapeDtypeStruct((B, S, W_BR), F32),
                   jax.ShapeDtypeStruct((B, S, W_BR), F32)],
        compiler_params=_params("parallel", "parallel"),
        name="mixer_in",
    )(x, mod_l, g_l, lw["w1"], lw["gq"], lw["gkv"], lw["wq"], lw["wk"], lw["wv"],
      tabs["cosr"], tabs["sinr"], tabs["cosk"], tabs["sink"], tabs["cs"])


def _attn_kernel(qt_ref, k_ref, vt_ref, kn_ref, o_ref, acc_scr, m_scr, *, tq, tk):
    seq = k_ref.shape[2]
    nq, nk = seq // tq, seq // tk
    k_norm2 = jnp.max(kn_ref[0, 0])

    def exact_tile(t, qoff):
        qt = qt_ref[0, 0, :, pl.ds(qoff, tq)]
        m_scr[...] = jnp.full(m_scr.shape, -jnp.inf, F32)
        acc_scr[t, 0:V_ROWS, :] = jnp.zeros((V_ROWS, tq), F32)

        def chunk(j, carry):
            off = pl.multiple_of(j * tk, tk)
            st = _dot(k_ref[0, 0, pl.ds(off, tk), :], qt)
            m_prev = m_scr[...]
            m_new = jnp.maximum(m_prev, jnp.max(st, axis=0, keepdims=True))
            pt = jnp.exp2(st - m_new).astype(BF16)
            pv = _dot(vt_ref[0, 0, :, pl.ds(off, tk)], pt)
            acc_scr[t, 0:V_ROWS, :] = jnp.exp2(m_prev - m_new) * acc_scr[t, 0:V_ROWS, :] + pv
            m_scr[...] = m_new
            return carry

        lax.fori_loop(0, nk, chunk, 0)

    def bounded_tile(t, qoff):
        qt = qt_ref[0, 0, :, pl.ds(qoff, tq)]
        q32 = qt.astype(F32)
        bound = jnp.sqrt(jnp.sum(q32 * q32, axis=0, keepdims=True) * k_norm2)
        first = lax.broadcasted_iota(jnp.int32, (BF16_ROWS, tq), 0) == 0
        stab = jnp.where(first, -bound, 0.0).astype(BF16)
        qs = jnp.concatenate([qt[0:QK_DIM], stab, qt[QK_DIM + BF16_ROWS:]], axis=0)
        acc = None
        for j in range(nk):
            st = _dot(k_ref[0, 0, j * tk:(j + 1) * tk, :], qs)
            pv = _dot(vt_ref[0, 0, :, j * tk:(j + 1) * tk], jnp.exp2(st).astype(BF16))
            acc = pv if acc is None else acc + pv
        acc_scr[t, 0:V_ROWS, :] = acc
        return jnp.min(acc[V_HEAD:V_HEAD + 1, :]) >= DENOM_MIN

    group = acc_scr.shape[0]

    def q_group(gi, carry):
        offs = [pl.multiple_of((gi * group + t) * tq, tq) for t in range(group)]
        ok = None
        for t in range(group):
            ok_t = bounded_tile(t, offs[t])
            ok = ok_t if ok is None else jnp.logical_and(ok, ok_t)

        @pl.when(jnp.logical_not(ok))
        def _():
            for t in range(group):
                exact_tile(t, offs[t])

        for t in range(group):
            denom = acc_scr[t, V_HEAD:V_HEAD + 1, :]
            o_ref[0, :, pl.ds(offs[t], tq)] = (acc_scr[t, 0:V_HEAD, :] / denom).astype(o_ref.dtype)
        return carry

    lax.fori_loop(0, nq // group, q_group, 0)


def _attn_call(qt, k, vt, kn, tq, tk):
    B, H, S, _ = k.shape
    kern = functools.partial(_attn_kernel, tq=tq, tk=tk)
    return pl.pallas_call(
        kern,
        grid=(B, H),
        in_specs=[pl.BlockSpec((1, 1, LANES, S), lambda b, h: (b, h, 0, 0)),
                  pl.BlockSpec((1, 1, S, LANES), lambda b, h: (b, h, 0, 0)),
                  pl.BlockSpec((1, 1, V_ROWS, S), lambda b, h: (b, h, 0, 0)),
                  pl.BlockSpec((1, 1) + kn.shape[2:], lambda b, h: (b, h, 0, 0))],
        out_specs=pl.BlockSpec((1, V_HEAD, S), lambda b, h: (b, h, 0)),
        out_shape=jax.ShapeDtypeStruct((B, H * V_HEAD, S), BF16),
        scratch_shapes=[pltpu.VMEM((2 if (S // tq) % 2 == 0 else 1, V_ROWS, tq), F32), pltpu.VMEM((1, tq), F32)],
        compiler_params=_params("parallel", "arbitrary"),
        name="attention",
    )(qt, k, vt, kn)


def _local_kernel(u_ref, up_ref, un_ref, z_ref, zp_ref, zn_ref, cw_ref, cb_ref, lg_ref, lb_ref,
                  pw_ref, ps_ref, oc_ref, op_ref, ubuf, zbuf, ush, sbuf, *, ts, seq, rc, rp):
    i = pl.program_id(1)
    has_prev = i > 0
    has_next = i < pl.num_programs(1) - 1
    ubuf[0:HALO, :] = jnp.where(has_prev, up_ref[0], 0.0)
    ubuf[HALO:HALO + ts, :] = u_ref[0]
    ubuf[HALO + ts:, :] = jnp.where(has_next, un_ref[0], 0.0)
    zbuf[0:HALO, :] = jnp.where(has_prev, zp_ref[0], 0.0)
    zbuf[HALO:HALO + ts, :] = z_ref[0]
    zbuf[HALO + ts:2 * HALO + ts, :] = jnp.where(has_next, zn_ref[0], 0.0)
    zbuf[2 * HALO + ts:, :] = jnp.zeros((SUBLANES, W_BR), F32)

    span = ts + 2 * HALO - SUBLANES
    for s in range(1, SUBLANES):
        ush[s - 1, 0:span, :] = ubuf[s:s + span, :]

    pad = CONV_WIDTH // 2
    for r in range(0, ts, rc):
        acc = jnp.zeros((rc, CONV_CH), F32)
        for kk in range(CONV_WIDTH):
            st = HALO + r + kk - pad
            s, base = st % SUBLANES, st - st % SUBLANES
            tap = ubuf[base:base + rc, :] if s == 0 else ush[s - 1, base:base + rc, :]
            acc = acc + tap * cw_ref[kk:kk + 1, :]
        y = acc + cb_ref[...]
        mu = jnp.mean(y, axis=-1, keepdims=True)
        yc = y - mu
        yn = yc * lax.rsqrt(jnp.mean(yc * yc, axis=-1, keepdims=True) + EPS)
        yn = yn * lg_ref[...] + lb_ref[...]
        oc_ref[0, r:r + rc, :] = (yn * _sigmoid(yn)).astype(oc_ref.dtype)

    assert all(w == 2 ** (g + 1) for g, w in enumerate(POOL_WINDOWS))
    rows_m = ts + 2 * HALO
    for m in range(1, N_GROUPS):
        step = 2 ** (m - 1)
        src = zbuf if m == 1 else sbuf.at[m - 2]
        lanes = slice(m * GC, N_GROUPS * GC)
        for c in range(0, rows_m, LANES):
            n = min(LANES, rows_m - c)
            sbuf[m - 1, c:c + n, lanes] = src[c:c + n, lanes] + src[c + step:c + step + n, lanes]
        rows_m -= SUBLANES

    for r in range(0, ts, rp):
        t = i * ts + r + lax.broadcasted_iota(jnp.int32, (rp, 1), 0)
        for g, w in enumerate(POOL_WINDOWS):
            lo = w // 2
            hi = w - 1 - lo
            cols = slice(g * GC, (g + 1) * GC)
            half = zbuf if g == 0 else sbuf.at[g - 1]
            win = half[HALO + r - lo:HALO + r - lo + rp, cols] + half[HALO + r:HALO + r + rp, cols]
            cnt = (jnp.minimum(t + hi + 1, seq) - jnp.maximum(t - lo, 0)).astype(F32)
            d = win / cnt - zbuf[HALO + r:HALO + r + rp, cols]
            yg = _dot(d.astype(BF16), pw_ref[g]) * ps_ref[:, cols]
            op_ref[0, r:r + rp, cols] = yg.astype(op_ref.dtype)


def _local_call(u, zc, lw, ts):
    B, S, _ = u.shape
    nh = ts // HALO
    last = S // HALO - 1
    cur = lambda w: pl.BlockSpec((1, ts, w), lambda b, i: (b, i, 0))
    prev = lambda w: pl.BlockSpec((1, HALO, w), lambda b, i: (b, jnp.maximum(i * nh - 1, 0), 0))
    nxt = lambda w: pl.BlockSpec((1, HALO, w), lambda b, i: (b, jnp.minimum((i + 1) * nh, last), 0))
    kern = functools.partial(_local_kernel, ts=ts, seq=S, rc=64, rp=min(ts, 256))
    return pl.pallas_call(
        kern,
        grid=(B, S // ts),
        in_specs=[cur(CONV_CH), prev(CONV_CH), nxt(CONV_CH), cur(W_BR), prev(W_BR), nxt(W_BR),
                  _const_spec(lw["conv_w"].shape), _const_spec(lw["conv_b"].shape),
                  _const_spec(lw["ln_g"].shape), _const_spec(lw["ln_b"].shape),
                  _const_spec(lw["pool_w"].shape), _const_spec(lw["pool_scale"].shape)],
        out_specs=[cur(CONV_CH), cur(W_BR)],
        out_shape=[jax.ShapeDtypeStruct((B, S, CONV_CH), BF16),
                   jax.ShapeDtypeStruct((B, S, W_BR), BF16)],
        scratch_shapes=[pltpu.VMEM((ts + 2 * HALO, CONV_CH), F32),
                        pltpu.VMEM((ts + 2 * HALO + SUBLANES, W_BR), F32),
                        pltpu.VMEM((SUBLANES - 1, ts + 2 * HALO, CONV_CH), F32),
                        pltpu.VMEM((N_GROUPS - 1, ts + 2 * HALO, W_BR), F32)],
        compiler_params=_params("parallel", "parallel"),
        name="conv_pool",
    )(u, u, u, zc, zc, zc, lw["conv_w"], lw["conv_b"], lw["ln_g"], lw["ln_b"],
      lw["pool_w"], lw["pool_scale"])


def _fft1_kernel(p_ref, q_ref, f_ref, o_ref):
    for j in range(p_ref.shape[2]):
        rhs = jnp.concatenate([p_ref[0, :, j, :], q_ref[0, :, j, :]], axis=0).astype(BF16)
        o_ref[0, :, j, :] = _dot(f_ref[...], rhs)


def _fft1_call(p, q, tabs):
    B, S, W = p.shape
    n2 = S // FFT_N1
    pv = p.reshape(B, FFT_N1, n2, W)
    qv = q.reshape(B, FFT_N1, n2, W)
    blk = pl.BlockSpec((1, FFT_N1, SUBLANES, W), lambda b, j: (b, 0, j, 0))
    return pl.pallas_call(
        _fft1_kernel,
        grid=(B, n2 // SUBLANES),
        in_specs=[blk, blk, _const_spec(tabs["f1"].shape)],
        out_specs=pl.BlockSpec((1, 2 * FFT_N1, SUBLANES, W), lambda b, j: (b, 0, j, 0)),
        out_shape=jax.ShapeDtypeStruct((B, 2 * FFT_N1, n2, W), F32),
        compiler_params=_params("parallel", "parallel"),
        name="fft_stage1",
    )(pv, qv, tabs["f1"])


def _fft2_kernel(a_ref, gc_ref, gs_ref, o_ref, *, kb):
    for j in range(kb):
        f = _dot(gc_ref[j], a_ref[0, 0, j].astype(BF16)) + _dot(gs_ref[j], a_ref[0, 1, j].astype(BF16))
        o_ref[0, :, j, :] = f


def _fft2_call(a, tabs, seq, kb):
    B = a.shape[0]
    n2 = seq // FFT_N1
    av = a.reshape(B, 2, FFT_N1, n2, W_BR)
    kern = functools.partial(_fft2_kernel, kb=kb)
    out = pl.pallas_call(
        kern,
        grid=(B, FFT_N1 // kb),
        in_specs=[pl.BlockSpec((1, 2, kb, n2, W_BR), lambda b, j: (b, 0, j, 0, 0)),
                  pl.BlockSpec((kb, n2, n2), lambda b, j: (j, 0, 0)),
                  pl.BlockSpec((kb, n2, n2), lambda b, j: (j, 0, 0))],
        out_specs=pl.BlockSpec((1, n2, kb, W_BR), lambda b, j: (b, 0, j, 0)),
        out_shape=jax.ShapeDtypeStruct((B, n2, FFT_N1, W_BR), F32),
        compiler_params=_params("parallel", "parallel"),
        name="fft_stage2",
    )(av, tabs["g2c"], tabs["g2s"])
    return out.reshape(B, seq, W_BR)


def _mixout_kernel(x_ref, mod_ref, g_ref, wg_ref, oa_ref, wa_ref, cb_ref, wb_ref, pc_ref, wc_ref,
                   f_ref, wd_ref, wo_ref, o_ref, *, parts):
    shift = mod_ref[0, 3:4, :]
    scale = mod_ref[0, 4:5, :]
    gate = mod_ref[0, 5:6, :]
    d = x_ref.shape[-1]
    rows = x_ref.shape[1] // parts
    hbs = []
    for p in range(parts):
        x = x_ref[0, p * rows:(p + 1) * rows, :]
        hbs.append(((_rms(x) * g_ref[2:3, :]) * (1.0 + scale) + shift).astype(BF16))
    ys = []
    for p in range(parts):
        sl = slice(p * rows, (p + 1) * rows)
        merged = None
        for br, (b_ref, w_ref) in enumerate(((oa_ref, wa_ref), (cb_ref, wb_ref), (pc_ref, wc_ref), (f_ref, wd_ref))):
            gl = _sigmoid(_dot(hbs[p], wg_ref[:, br * d:(br + 1) * d]))
            if br == 0:
                yb = lax.dot_general(b_ref[0, :, sl], w_ref[...], (((0,), (0,)), ((), ())),
                                     preferred_element_type=F32)
            else:
                yb = _dot(b_ref[0, sl, :].astype(BF16), w_ref[...])
            term = gl * yb
            merged = term if merged is None else merged + term
        ys.append(_dot(merged.astype(BF16), wo_ref[...]))
    for p in range(parts):
        sl = slice(p * rows, (p + 1) * rows)
        o_ref[0, sl, :] = x_ref[0, sl, :] + gate * (_rms(ys[p]) * g_ref[3:4, :])


def _mixout_call(x, mod_l, g_l, lw, oa, cb, pc, f, tm):
    B, S, D = x.shape
    tok = lambda w: pl.BlockSpec((1, tm, w), lambda b, i: (b, i, 0))
    return pl.pallas_call(
        functools.partial(_mixout_kernel, parts=2 if tm % 512 == 0 else 1),
        grid=(B, S // tm),
        in_specs=[tok(D),
                  pl.BlockSpec((1, N_MOD, D), lambda b, i: (b, 0, 0)),
                  _const_spec(g_l.shape),
                  _const_spec(lw["wg"].shape),
                  pl.BlockSpec((1, N_HEADS * V_HEAD, tm), lambda b, i: (b, 0, i)), _const_spec(lw["wa"].shape),
                  tok(CONV_CH), _const_spec(lw["wb"].shape),
                  tok(W_BR), _const_spec(lw["wc"].shape),
                  tok(W_BR), _const_spec(lw["wd"].shape),
                  _const_spec(lw["wo"].shape)],
        out_specs=tok(D),
        out_shape=jax.ShapeDtypeStruct(x.shape, F32),
        compiler_params=_params("parallel", "parallel"),
        name="mixer_out",
    )(x, mod_l, g_l, lw["wg"], oa, lw["wa"], cb, lw["wb"], pc, lw["wc"], f, lw["wd"], lw["wo"])


def _rot_half_cols(w):
    half = w.shape[-1] // 2
    return jnp.concatenate([-w[..., half:], w[..., :half]], axis=-1)


def _tables(seq):
    f32 = np.float32
    pos = np.arange(seq, dtype=f32)
    inv = (f32(ROPE_THETA) ** (-np.arange(0, QK_ROPE, 2, dtype=f32) / f32(QK_ROPE))).astype(f32)
    ang = pos[:, None] * inv[None, :]
    cos, sin = np.cos(ang).astype(f32), np.sin(ang).astype(f32)
    cc = np.concatenate([cos, cos], axis=-1)
    ss = np.concatenate([sin, sin], axis=-1)
    z64 = np.zeros((seq, QK_NOPE), f32)
    z32 = np.zeros((seq, LANES - QK_NOPE - QK_ROPE), f32)
    sm_scale = QK_DIM ** -0.5 * math.log2(math.e)
    cosk = np.concatenate([z64, cc, z32], axis=-1)
    sink = np.concatenate([z64, ss, z32], axis=-1)
    cosr = np.tile(cc, (1, N_HEADS)) * f32(sm_scale)
    sinr = np.tile(ss, (1, N_HEADS)) * f32(sm_scale)

    def cos_sin(m, period):
        th = m.astype(np.float64) * (2.0 * math.pi / period)
        return np.cos(th), np.sin(th)

    def dft(n):
        a = np.arange(n, dtype=np.int64)
        return cos_sin((a[:, None] * a[None, :]) % n, n)

    bf = lambda a: a.astype(f32).astype(BF16)
    c_ch, s_ch = dft(GC)
    cs = bf(np.concatenate([c_ch, s_ch], axis=-1) * GC ** -0.5)

    n1 = FFT_N1
    n2 = seq // n1
    c1, s1 = dft(n1)
    f1 = bf(np.concatenate([np.concatenate([c1, -s1], axis=1), np.concatenate([-s1, -c1], axis=1)], axis=0)
            * n1 ** -0.5)
    k1 = np.arange(n1, dtype=np.int64)[:, None, None]
    k2 = np.arange(n2, dtype=np.int64)[None, :, None]
    t2 = np.arange(n2, dtype=np.int64)[None, None, :]
    c2, s2 = cos_sin(((k1 + n1 * k2) * t2) % seq, seq)
    g2c = bf(c2 * n2 ** -0.5)
    g2s = bf(s2 * n2 ** -0.5)
    return dict(cosr=cosr, sinr=sinr, q_scale=sm_scale, cosk=cosk, sink=sink, cs=cs, f1=f1, g2c=g2c, g2s=g2s)


def _layer_weights(l, w_in, q_norm_g, w_uq, kv_norm_g, w_ukv, w_a, conv_w, conv_b, conv_ln_g,
                   conv_ln_b, w_b, pool_w, pool_scale, w_c, w_d, w_out):
    D = w_in.shape[1]
    wi = w_in[l]
    assert Q_PAD - Q_LORA == QK_ROPE
    w1 = jnp.concatenate([wi[:, Q_LORA:Q_LORA + KV_LORA], wi[:, :Q_LORA], wi[:, Q_LORA + KV_LORA:4 * W_BR]],
                         axis=-1).astype(BF16)

    uq = w_uq[l].reshape(Q_LORA, N_HEADS, QK_NOPE + QK_ROPE)
    nope, rope = uq[..., :QK_NOPE], uq[..., QK_NOPE:]
    wq = jnp.concatenate([nope.reshape(Q_LORA, N_HEADS * QK_NOPE), rope.reshape(Q_LORA, N_HEADS * QK_ROPE),
                          _rot_half_cols(rope).reshape(Q_LORA, N_HEADS * QK_ROPE)], axis=-1)
    wq = jnp.pad(wq, ((0, Q_PAD - Q_LORA), (0, 0))).astype(BF16)

    ukv = w_ukv[l].reshape(KV_LORA, N_HEADS, QK_NOPE + V_HEAD)
    zk = jnp.zeros((KV_LORA, N_HEADS, LANES - QK_NOPE), F32)
    wk = jnp.concatenate([ukv[..., :QK_NOPE], zk], axis=-1).reshape(KV_LORA, N_HEADS * LANES).astype(BF16)
    wv = ukv[..., QK_NOPE:].reshape(KV_LORA, N_HEADS * V_HEAD).astype(BF16)

    gq = jnp.pad(q_norm_g[l], (0, Q_PAD - Q_LORA)).reshape(1, Q_PAD)
    return dict(
        w1=w1, wq=wq, wk=wk, wv=wv, gq=gq, gkv=kv_norm_g[l].reshape(1, KV_LORA),
        wg=wi[:, 4 * W_BR:].astype(BF16), wa=w_a[l].astype(BF16),
        wb=w_b[l].astype(BF16), wc=w_c[l].astype(BF16), wd=w_d[l].astype(BF16), wo=w_out[l].astype(BF16),
        conv_w=conv_w[l], conv_b=conv_b[l].reshape(1, CONV_CH),
        ln_g=conv_ln_g[l].reshape(1, CONV_CH), ln_b=conv_ln_b[l].reshape(1, CONV_CH),
        pool_w=pool_w[l].astype(BF16), pool_scale=pool_scale[l].reshape(1, W_BR))


def _tiles(seq):
    t = lambda n: min(n, seq)
    return dict(ffn=t(1024), mix=t(512), mixout=t(512), tq=t(1024), tk=t(1024), local=t(512))


def kernel(x, c, ada_w, ada_b, norm_g, ffn1_w_in, ffn1_w_out, ffn2_w_in, ffn2_w_out, w_in, q_norm_g,
           w_uq, kv_norm_g, w_ukv, w_a, conv_w, conv_b, conv_ln_g, conv_ln_b, w_b, pool_w, pool_scale,
           w_c, w_d, w_out):
    B, S, D = x.shape
    L = ada_w.shape[0]
    assert D == D_MODEL and S % (FFT_N1 * 16) == 0
    ts = _tiles(S)
    tabs = _tables(S)
    mod = _mod_call(c, ada_w, ada_b).reshape(L, B, N_MOD, D)
    ffn_w = [(_to_bf16(wi, 256), _to_bf16(wo, 704)) for wi, wo in ((ffn1_w_in, ffn1_w_out), (ffn2_w_in, ffn2_w_out))]
    for l in range(L):
        lw = _layer_weights(l, w_in, q_norm_g, w_uq, kv_norm_g, w_ukv, w_a, conv_w, conv_b, conv_ln_g,
                            conv_ln_b, w_b, pool_w, pool_scale, w_c, w_d, w_out)
        mod_l, g_l = mod[l], norm_g[l]
        x = _ffn_call(x, mod_l, g_l, *ffn_w[0], l, 0, ts["ffn"])
        q, k, v, kn, u, zc, p, qq = _mixin_call(x, mod_l, g_l, lw, tabs, ts["mix"])
        oa = _attn_call(q, k, v, kn, ts["tq"], ts["tk"])
        cb, pc = _local_call(u, zc, lw, ts["local"])
        a = _fft1_call(p, qq, tabs)
        f = _fft2_call(a, tabs, S, 8)
        x = _mixout_call(x, mod_l, g_l, lw, oa, cb, pc, f, ts["mixout"])
        x = _ffn_call(x, mod_l, g_l, *ffn_w[1], l, 2, ts["ffn"])
    return x
```

```python
import functools
import math

import jax
import jax.numpy as jnp
import numpy as np
from jax import lax
from jax.experimental import pallas as pl
from jax.experimental.pallas import tpu as pltpu

F32 = jnp.float32
BF16 = jnp.bfloat16

D_MODEL = 1024
D_FF = 2816
N_HEADS = 8
Q_LORA = 352
KV_LORA = 128
QK_NOPE = 64
QK_ROPE = 32
V_HEAD = 64
QK_DIM = QK_NOPE + QK_ROPE
BF16_ROWS = 16
V_ROWS = 80
DENOM_MIN = 2.0 ** -80
ROPE_THETA = 10000.0
CONV_CH = 256
CONV_WIDTH = 31
POOL_WINDOWS = (2, 4, 8, 16)
GC = 128
N_GROUPS = 4
W_BR = 512
N_MOD = 9
EPS = 1e-6

LANES = 128
SUBLANES = 8
Q_PAD = 384
HALO = 16
FFT_N1 = 128
VMEM_LIMIT = 56 * 1024 * 1024

C_CKV = 0
C_CQ = C_CKV + KV_LORA
C_ZB = C_CQ + Q_PAD
C_ZC = C_ZB + W_BR
C_ZD = C_ZC + W_BR
W1_COLS = C_ZD + W_BR


def _params(*sem):
    return pltpu.CompilerParams(dimension_semantics=sem, vmem_limit_bytes=VMEM_LIMIT)


def _const_spec(shape):
    zeros = (0,) * len(shape)
    return pl.BlockSpec(shape, lambda *_: zeros, pipeline_mode=pl.Buffered(1))


def _dot(a, b):
    return jnp.dot(a, b, preferred_element_type=F32)


def _rms(x):
    return x * lax.rsqrt(jnp.mean(x * x, axis=-1, keepdims=True) + EPS)


def _sigmoid(x):
    return 1.0 / (1.0 + jnp.exp(-x))


def _mod_kernel(c_ref, w_ref, b_ref, o_ref):
    c = c_ref[...]
    ca = c * _sigmoid(c)
    o_ref[0] = jnp.dot(ca, w_ref[0], preferred_element_type=F32,
                       precision=lax.Precision.HIGHEST) + b_ref[0]


def _mod_call(c, ada_w, ada_b):
    L, D, N = ada_w.shape
    B = c.shape[0]
    tn = 1152
    return pl.pallas_call(
        _mod_kernel,
        grid=(L, N // tn),
        in_specs=[pl.BlockSpec((B, D), lambda l, j: (0, 0)),
                  pl.BlockSpec((1, D, tn), lambda l, j: (l, 0, j)),
                  pl.BlockSpec((1, 1, tn), lambda l, j: (l, 0, j))],
        out_specs=pl.BlockSpec((1, B, tn), lambda l, j: (l, 0, j)),
        out_shape=jax.ShapeDtypeStruct((L, B, N), F32),
        compiler_params=_params("parallel", "parallel"),
        name="adaln_mod",
    )(c, ada_w, ada_b.reshape(L, 1, N))


def _ffn_kernel(x_ref, mod_ref, g_ref, win_ref, wout_ref, o_ref, a_scr, *, sub, chunk, parts):
    shift = mod_ref[0, 3 * sub:3 * sub + 1, :]
    scale = mod_ref[0, 3 * sub + 1:3 * sub + 2, :]
    gate = mod_ref[0, 3 * sub + 2:3 * sub + 3, :]
    g_in = g_ref[2 * sub:2 * sub + 1, :]
    g_out = g_ref[2 * sub + 1:2 * sub + 2, :]
    rows = x_ref.shape[1] // parts
    hbs = []
    for p in range(parts):
        x = x_ref[0, p * rows:(p + 1) * rows, :]
        hbs.append(((_rms(x) * g_in) * (1.0 + scale) + shift).astype(BF16))
    for p in range(parts):
        for j in range(D_FF // chunk):
            gj = _dot(hbs[p], win_ref[0, :, j * chunk:(j + 1) * chunk])
            uj = _dot(hbs[p], win_ref[0, :, D_FF + j * chunk:D_FF + (j + 1) * chunk])
            a_scr[p, :, j * chunk:(j + 1) * chunk] = (gj * _sigmoid(gj) * uj).astype(BF16)
    for p in range(parts):
        y = _dot(a_scr[p], wout_ref[0])
        x = x_ref[0, p * rows:(p + 1) * rows, :]
        o_ref[0, p * rows:(p + 1) * rows, :] = x + (0.5 * gate) * (_rms(y) * g_out)


def _cast_kernel(w_ref, o_ref):
    o_ref[...] = w_ref[...].astype(o_ref.dtype)


def _to_bf16(w):
    L, R, C = w.shape
    rows = R // 4
    assert R % 4 == 0 and rows % BF16_ROWS == 0
    blk = pl.BlockSpec((1, rows, C), lambda l, i: (l, i, 0))
    return pl.pallas_call(
        _cast_kernel,
        grid=(L, R // rows),
        in_specs=[blk],
        out_specs=blk,
        out_shape=jax.ShapeDtypeStruct(w.shape, BF16),
        compiler_params=_params("parallel", "parallel"),
        name="weight_cast",
    )(w)


def _layer_spec(w, l):
    return pl.BlockSpec((1,) + w.shape[1:], lambda *_: (l,) + (0,) * (w.ndim - 1), pipeline_mode=pl.Buffered(1))


def _ffn_call(x, mod_l, g_l, w_in, w_out, l, sub, tm):
    B, S, D = x.shape
    parts = 4 if tm % 1024 == 0 else 1
    kern = functools.partial(_ffn_kernel, sub=sub, chunk=256, parts=parts)
    return pl.pallas_call(
        kern,
        grid=(B, S // tm),
        in_specs=[pl.BlockSpec((1, tm, D), lambda b, i: (b, i, 0)),
                  pl.BlockSpec((1, N_MOD, D), lambda b, i: (b, 0, 0)),
                  _const_spec(g_l.shape),
                  _layer_spec(w_in, l),
                  _layer_spec(w_out, l)],
        out_specs=pl.BlockSpec((1, tm, D), lambda b, i: (b, i, 0)),
        out_shape=jax.ShapeDtypeStruct(x.shape, F32),
        scratch_shapes=[pltpu.VMEM((parts, tm // parts, D_FF), BF16)],
        compiler_params=_params("parallel", "parallel"),
        name=f"ffn{sub}",
    )(x, mod_l, g_l, w_in, w_out)


def _mixin_kernel(x_ref, mod_ref, g_ref, w1_ref, gq_ref, gkv_ref, wq_ref, wk_ref, wv_ref,
                  cr_ref, sr_ref, ck_ref, sk_ref, cs_ref,
                  q_ref, k_ref, v_ref, kn_ref, u_ref, zc_ref, p_ref, qq_ref, *, q_scale):
    x = x_ref[0]
    tm = x.shape[0]
    shift = mod_ref[0, 3:4, :]
    scale = mod_ref[0, 4:5, :]
    h = (_rms(x) * g_ref[2:3, :]) * (1.0 + scale) + shift
    hb = h.astype(BF16)
    z = _dot(hb, w1_ref[...])

    cq = z[:, C_CQ:C_CQ + Q_PAD]
    qlane = lax.broadcasted_iota(jnp.int32, (1, Q_PAD), 1)
    cqm = jnp.where(qlane < Q_LORA, cq, 0.0)
    cqn = cqm * lax.rsqrt(jnp.sum(cqm * cqm, axis=-1, keepdims=True) * (1.0 / Q_LORA) + EPS)
    cqn = (cqn * gq_ref[...]).astype(BF16)
    qq = _dot(cqn, wq_ref[...])
    nope = qq[:, :N_HEADS * QK_NOPE] * q_scale
    w_rope = N_HEADS * QK_ROPE
    roped = (qq[:, N_HEADS * QK_NOPE:N_HEADS * QK_NOPE + w_rope] * cr_ref[...]
             + qq[:, N_HEADS * QK_NOPE + w_rope:] * sr_ref[...])
    for blk in range(N_HEADS * QK_NOPE // LANES):
        nt = nope[:, blk * LANES:(blk + 1) * LANES].T.astype(BF16)
        for i in range(LANES // QK_NOPE):
            q_ref[0, blk * (LANES // QK_NOPE) + i, 0:QK_NOPE, :] = nt[i * QK_NOPE:(i + 1) * QK_NOPE]
    for blk in range(w_rope // LANES):
        rt = roped[:, blk * LANES:(blk + 1) * LANES].T.astype(BF16)
        for i in range(LANES // QK_ROPE):
            q_ref[0, blk * (LANES // QK_ROPE) + i, QK_NOPE:QK_NOPE + QK_ROPE, :] = rt[i * QK_ROPE:(i + 1) * QK_ROPE]
    zero_rows = jnp.zeros((LANES - QK_NOPE - QK_ROPE, tm), BF16)
    for hd in range(N_HEADS):
        q_ref[0, hd, QK_NOPE + QK_ROPE:, :] = zero_rows

    kr_at = Q_LORA - (Q_PAD - LANES)
    kr = pltpu.roll(cq[:, Q_PAD - LANES:], (QK_NOPE - kr_at) % LANES, 1)
    klane = lax.broadcasted_iota(jnp.int32, (1, LANES), 1)
    kr_rot = jnp.where(klane < QK_NOPE + QK_ROPE // 2,
                       -pltpu.roll(kr, LANES - QK_ROPE // 2, 1), pltpu.roll(kr, QK_ROPE // 2, 1))
    k_rope = kr * ck_ref[...] + kr_rot * sk_ref[...]

    ckv = z[:, C_CKV:C_CKV + KV_LORA]
    ckvn = (_rms(ckv) * gkv_ref[...]).astype(BF16)
    kk = _dot(ckvn, wk_ref[...])
    ones_lane = (klane == QK_DIM).astype(F32)
    for hd in range(N_HEADS):
        kh = kk[:, hd * LANES:(hd + 1) * LANES] + k_rope
        k_ref[0, hd] = (kh + ones_lane).astype(BF16)
        norm2 = jnp.max(jnp.sum(kh * kh, axis=-1, keepdims=True), axis=0, keepdims=True)
        kn_ref[0, hd] = jnp.broadcast_to(norm2, (SUBLANES, LANES))
    vv = _dot(ckvn, wv_ref[...])
    ones_rows = (lax.broadcasted_iota(jnp.int32, (V_ROWS - V_HEAD, tm), 0) == 0).astype(BF16)
    for blk in range(N_HEADS * V_HEAD // LANES):
        vt = vv[:, blk * LANES:(blk + 1) * LANES].T.astype(BF16)
        for i in range(LANES // V_HEAD):
            v_ref[0, blk * (LANES // V_HEAD) + i, 0:V_HEAD, :] = vt[i * V_HEAD:(i + 1) * V_HEAD]
    for hd in range(N_HEADS):
        v_ref[0, hd, V_HEAD:, :] = ones_rows

    a = z[:, C_ZB:C_ZB + CONV_CH]
    gt = z[:, C_ZB + CONV_CH:C_ZB + 2 * CONV_CH]
    u_ref[0] = a * _sigmoid(gt)

    zc_ref[0] = z[:, C_ZC:C_ZC + W_BR]

    for g in range(N_GROUPS):
        zd = z[:, C_ZD + g * GC:C_ZD + (g + 1) * GC].astype(BF16)
        pq = _dot(zd, cs_ref[...])
        p_ref[0, :, g * GC:(g + 1) * GC] = pq[:, :GC]
        qq_ref[0, :, g * GC:(g + 1) * GC] = pq[:, GC:]


def _mixin_call(x, mod_l, g_l, lw, tabs, tm):
    B, S, D = x.shape
    tok = lambda w: pl.BlockSpec((1, tm, w), lambda b, i: (b, i, 0))
    head = pl.BlockSpec((1, N_HEADS, tm, LANES), lambda b, i: (b, 0, i, 0))
    head_t = pl.BlockSpec((1, N_HEADS, LANES, tm), lambda b, i: (b, 0, 0, i))
    tab = lambda w: pl.BlockSpec((tm, w), lambda b, i: (i, 0))
    hshape = jax.ShapeDtypeStruct((B, N_HEADS, S, LANES), BF16)
    hshape_t = jax.ShapeDtypeStruct((B, N_HEADS, LANES, S), BF16)
    return pl.pallas_call(
        functools.partial(_mixin_kernel, q_scale=tabs["q_scale"]),
        grid=(B, S // tm),
        in_specs=[tok(D),
                  pl.BlockSpec((1, N_MOD, D), lambda b, i: (b, 0, 0)),
                  _const_spec(g_l.shape),
                  _const_spec(lw["w1"].shape),
                  _const_spec(lw["gq"].shape),
                  _const_spec(lw["gkv"].shape),
                  _const_spec(lw["wq"].shape),
                  _const_spec(lw["wk"].shape),
                  _const_spec(lw["wv"].shape),
                  tab(N_HEADS * QK_ROPE), tab(N_HEADS * QK_ROPE), tab(LANES), tab(LANES),
                  _const_spec(tabs["cs"].shape)],
        out_specs=[head_t, head, pl.BlockSpec((1, N_HEADS, V_ROWS, tm), lambda b, i: (b, 0, 0, i)),
                   pl.BlockSpec((1, N_HEADS, SUBLANES, LANES), lambda b, i: (b, 0, i, 0)),
                   tok(CONV_CH), tok(W_BR), tok(W_BR), tok(W_BR)],
        out_shape=[hshape_t, hshape, jax.ShapeDtypeStruct((B, N_HEADS, V_ROWS, S), BF16),
                   jax.ShapeDtypeStruct((B, N_HEADS, (S // tm) * SUBLANES, LANES), F32),
                   jax.ShapeDtypeStruct((B, S, CONV_CH), F32),
                   jax.ShapeDtypeStruct((B, S, W_BR), F32),
                   jax.ShapeDtypeStruct((B, S, W_BR), F32),
                   jax.ShapeDtypeStruct((B, S, W_BR), F32)],
        compiler_params=_params("parallel", "parallel"),
        name="mixer_in",
    )(x, mod_l, g_l, lw["w1"], lw["gq"], lw["gkv"], lw["wq"], lw["wk"], lw["wv"],
      tabs["cosr"], tabs["sinr"], tabs["cosk"], tabs["sink"], tabs["cs"])


def _attn_kernel(qt_ref, k_ref, vt_ref, kn_ref, o_ref, acc_scr, m_scr, *, tq, tk):
    seq = k_ref.shape[2]
    nq, nk = seq // tq, seq // tk
    k_norm2 = jnp.max(kn_ref[0, 0])

    def exact_tile(t, qoff):
        qt = qt_ref[0, 0, :, pl.ds(qoff, tq)]
        m_scr[...] = jnp.full(m_scr.shape, -jnp.inf, F32)
        acc_scr[t, 0:V_ROWS, :] = jnp.zeros((V_ROWS, tq), F32)

        def chunk(j, carry):
            off = pl.multiple_of(j * tk, tk)
            st = _dot(k_ref[0, 0, pl.ds(off, tk), :], qt)
            m_prev = m_scr[...]
            m_new = jnp.maximum(m_prev, jnp.max(st, axis=0, keepdims=True))
            pt = jnp.exp2(st - m_new).astype(BF16)
            pv = _dot(vt_ref[0, 0, :, pl.ds(off, tk)], pt)
            acc_scr[t, 0:V_ROWS, :] = jnp.exp2(m_prev - m_new) * acc_scr[t, 0:V_ROWS, :] + pv
            m_scr[...] = m_new
            return carry

        lax.fori_loop(0, nk, chunk, 0)

    def bounded_tile(t, qoff):
        qt = qt_ref[0, 0, :, pl.ds(qoff, tq)]
        q32 = qt.astype(F32)
        bound = jnp.sqrt(jnp.sum(q32 * q32, axis=0, keepdims=True) * k_norm2)
        first = lax.broadcasted_iota(jnp.int32, (BF16_ROWS, tq), 0) == 0
        stab = jnp.where(first, -bound, 0.0).astype(BF16)
        qs = jnp.concatenate([qt[0:QK_DIM], stab, qt[QK_DIM + BF16_ROWS:]], axis=0)
        acc = None
        for j in range(nk):
            st = _dot(k_ref[0, 0, j * tk:(j + 1) * tk, :], qs)
            pv = _dot(vt_ref[0, 0, :, j * tk:(j + 1) * tk], jnp.exp2(st).astype(BF16))
            acc = pv if acc is None else acc + pv
        acc_scr[t, 0:V_ROWS, :] = acc
        return jnp.min(acc[V_HEAD:V_HEAD + 1, :]) >= DENOM_MIN

    group = acc_scr.shape[0]

    def q_group(gi, carry):
        offs = [pl.multiple_of((gi * group + t) * tq, tq) for t in range(group)]
        ok = None
        for t in range(group):
            ok_t = bounded_tile(t, offs[t])
            ok = ok_t if ok is None else jnp.logical_and(ok, ok_t)

        @pl.when(jnp.logical_not(ok))
        def _():
            for t in range(group):
                exact_tile(t, offs[t])

        for t in range(group):
            denom = acc_scr[t, V_HEAD:V_HEAD + 1, :]
            o_ref[0, :, pl.ds(offs[t], tq)] = (acc_scr[t, 0:V_HEAD, :] / denom).astype(o_ref.dtype)
        return carry

    lax.fori_loop(0, nq // group, q_group, 0)


def _attn_call(qt, k, vt, kn, tq, tk):
    B, H, S, _ = k.shape
    kern = functools.partial(_attn_kernel, tq=tq, tk=tk)
    return pl.pallas_call(
        kern,
        grid=(B, H),
        in_specs=[pl.BlockSpec((1, 1, LANES, S), lambda b, h: (b, h, 0, 0)),
                  pl.BlockSpec((1, 1, S, LANES), lambda b, h: (b, h, 0, 0)),
                  pl.BlockSpec((1, 1, V_ROWS, S), lambda b, h: (b, h, 0, 0)),
                  pl.BlockSpec((1, 1) + kn.shape[2:], lambda b, h: (b, h, 0, 0))],
        out_specs=pl.BlockSpec((1, V_HEAD, S), lambda b, h: (b, h, 0)),
        out_shape=jax.ShapeDtypeStruct((B, H * V_HEAD, S), BF16),
        scratch_shapes=[pltpu.VMEM((2 if (S // tq) % 2 == 0 else 1, V_ROWS, tq), F32), pltpu.VMEM((1, tq), F32)],
        compiler_params=_params("parallel", "arbitrary"),
        name="attention",
    )(qt, k, vt, kn)


def _local_kernel(u_ref, up_ref, un_ref, z_ref, zp_ref, zn_ref, cw_ref, cb_ref, lg_ref, lb_ref,
                  pw_ref, ps_ref, oc_ref, op_ref, ubuf, zbuf, ush, sbuf, *, ts, seq, rc, rp):
    i = pl.program_id(1)
    has_prev = i > 0
    has_next = i < pl.num_programs(1) - 1
    ubuf[0:HALO, :] = jnp.where(has_prev, up_ref[0], 0.0)
    ubuf[HALO:HALO + ts, :] = u_ref[0]
    ubuf[HALO + ts:, :] = jnp.where(has_next, un_ref[0], 0.0)
    zbuf[0:HALO, :] = jnp.where(has_prev, zp_ref[0], 0.0)
    zbuf[HALO:HALO + ts, :] = z_ref[0]
    zbuf[HALO + ts:2 * HALO + ts, :] = jnp.where(has_next, zn_ref[0], 0.0)
    zbuf[2 * HALO + ts:, :] = jnp.zeros((SUBLANES, W_BR), F32)

    span = ts + 2 * HALO - SUBLANES
    for s in range(1, SUBLANES):
        ush[s - 1, 0:span, :] = ubuf[s:s + span, :]

    pad = CONV_WIDTH // 2
    for r in range(0, ts, rc):
        acc = jnp.zeros((rc, CONV_CH), F32)
        for kk in range(CONV_WIDTH):
            st = HALO + r + kk - pad
            s, base = st % SUBLANES, st - st % SUBLANES
            tap = ubuf[base:base + rc, :] if s == 0 else ush[s - 1, base:base + rc, :]
            acc = acc + tap * cw_ref[kk:kk + 1, :]
        y = acc + cb_ref[...]
        mu = jnp.mean(y, axis=-1, keepdims=True)
        yc = y - mu
        yn = yc * lax.rsqrt(jnp.mean(yc * yc, axis=-1, keepdims=True) + EPS)
        yn = yn * lg_ref[...] + lb_ref[...]
        oc_ref[0, r:r + rc, :] = (yn * _sigmoid(yn)).astype(oc_ref.dtype)

    assert all(w == 2 ** (g + 1) for g, w in enumerate(POOL_WINDOWS))
    rows_m = ts + 2 * HALO
    for m in range(1, N_GROUPS):
        step = 2 ** (m - 1)
        src = zbuf if m == 1 else sbuf.at[m - 2]
        lanes = slice(m * GC, N_GROUPS * GC)
        for c in range(0, rows_m, LANES):
            n = min(LANES, rows_m - c)
            sbuf[m - 1, c:c + n, lanes] = src[c:c + n, lanes] + src[c + step:c + step + n, lanes]
        rows_m -= SUBLANES

    for r in range(0, ts, rp):
        t = i * ts + r + lax.broadcasted_iota(jnp.int32, (rp, 1), 0)
        for g, w in enumerate(POOL_WINDOWS):
            lo = w // 2
            hi = w - 1 - lo
            cols = slice(g * GC, (g + 1) * GC)
            half = zbuf if g == 0 else sbuf.at[g - 1]
            win = half[HALO + r - lo:HALO + r - lo + rp, cols] + half[HALO + r:HALO + r + rp, cols]
            cnt = (jnp.minimum(t + hi + 1, seq) - jnp.maximum(t - lo, 0)).astype(F32)
            d = win / cnt - zbuf[HALO + r:HALO + r + rp, cols]
            yg = _dot(d.astype(BF16), pw_ref[g]) * ps_ref[:, cols]
            op_ref[0, r:r + rp, cols] = yg.astype(op_ref.dtype)


def _local_call(u, zc, lw, ts):
    B, S, _ = u.shape
    nh = ts // HALO
    last = S // HALO - 1
    cur = lambda w: pl.BlockSpec((1, ts, w), lambda b, i: (b, i, 0))
    prev = lambda w: pl.BlockSpec((1, HALO, w), lambda b, i: (b, jnp.maximum(i * nh - 1, 0), 0))
    nxt = lambda w: pl.BlockSpec((1, HALO, w), lambda b, i: (b, jnp.minimum((i + 1) * nh, last), 0))
    kern = functools.partial(_local_kernel, ts=ts, seq=S, rc=64, rp=min(ts, 256))
    return pl.pallas_call(
        kern,
        grid=(B, S // ts),
        in_specs=[cur(CONV_CH), prev(CONV_CH), nxt(CONV_CH), cur(W_BR), prev(W_BR), nxt(W_BR),
                  _const_spec(lw["conv_w"].shape), _const_spec(lw["conv_b"].shape),
                  _const_spec(lw["ln_g"].shape), _const_spec(lw["ln_b"].shape),
                  _const_spec(lw["pool_w"].shape), _const_spec(lw["pool_scale"].shape)],
        out_specs=[cur(CONV_CH), cur(W_BR)],
        out_shape=[jax.ShapeDtypeStruct((B, S, CONV_CH), BF16),
                   jax.ShapeDtypeStruct((B, S, W_BR), BF16)],
        scratch_shapes=[pltpu.VMEM((ts + 2 * HALO, CONV_CH), F32),
                        pltpu.VMEM((ts + 2 * HALO + SUBLANES, W_BR), F32),
                        pltpu.VMEM((SUBLANES - 1, ts + 2 * HALO, CONV_CH), F32),
                        pltpu.VMEM((N_GROUPS - 1, ts + 2 * HALO, W_BR), F32)],
        compiler_params=_params("parallel", "parallel"),
        name="conv_pool",
    )(u, u, u, zc, zc, zc, lw["conv_w"], lw["conv_b"], lw["ln_g"], lw["ln_b"],
      lw["pool_w"], lw["pool_scale"])


def _fft1_kernel(p_ref, q_ref, f_ref, o_ref):
    for j in range(p_ref.shape[2]):
        rhs = jnp.concatenate([p_ref[0, :, j, :], q_ref[0, :, j, :]], axis=0).astype(BF16)
        o_ref[0, :, j, :] = _dot(f_ref[...], rhs)


def _fft1_call(p, q, tabs):
    B, S, W = p.shape
    n2 = S // FFT_N1
    pv = p.reshape(B, FFT_N1, n2, W)
    qv = q.reshape(B, FFT_N1, n2, W)
    blk = pl.BlockSpec((1, FFT_N1, SUBLANES, W), lambda b, j: (b, 0, j, 0))
    return pl.pallas_call(
        _fft1_kernel,
        grid=(B, n2 // SUBLANES),
        in_specs=[blk, blk, _const_spec(tabs["f1"].shape)],
        out_specs=pl.BlockSpec((1, 2 * FFT_N1, SUBLANES, W), lambda b, j: (b, 0, j, 0)),
        out_shape=jax.ShapeDtypeStruct((B, 2 * FFT_N1, n2, W), F32),
        compiler_params=_params("parallel", "parallel"),
        name="fft_stage1",
    )(pv, qv, tabs["f1"])


def _fft2_kernel(a_ref, gc_ref, gs_ref, o_ref, *, kb):
    for j in range(kb):
        f = _dot(gc_ref[j], a_ref[0, 0, j].astype(BF16)) + _dot(gs_ref[j], a_ref[0, 1, j].astype(BF16))
        o_ref[0, :, j, :] = f


def _fft2_call(a, tabs, seq, kb):
    B = a.shape[0]
    n2 = seq // FFT_N1
    av = a.reshape(B, 2, FFT_N1, n2, W_BR)
    kern = functools.partial(_fft2_kernel, kb=kb)
    out = pl.pallas_call(
        kern,
        grid=(B, FFT_N1 // kb),
        in_specs=[pl.BlockSpec((1, 2, kb, n2, W_BR), lambda b, j: (b, 0, j, 0, 0)),
                  pl.BlockSpec((kb, n2, n2), lambda b, j: (j, 0, 0)),
                  pl.BlockSpec((kb, n2, n2), lambda b, j: (j, 0, 0))],
        out_specs=pl.BlockSpec((1, n2, kb, W_BR), lambda b, j: (b, 0, j, 0)),
        out_shape=jax.ShapeDtypeStruct((B, n2, FFT_N1, W_BR), F32),
        compiler_params=_params("parallel", "parallel"),
        name="fft_stage2",
    )(av, tabs["g2c"], tabs["g2s"])
    return out.reshape(B, seq, W_BR)


def _mixout_kernel(x_ref, mod_ref, g_ref, wg_ref, oa_ref, wa_ref, cb_ref, wb_ref, pc_ref, wc_ref,
                   f_ref, wd_ref, wo_ref, o_ref, *, parts):
    shift = mod_ref[0, 3:4, :]
    scale = mod_ref[0, 4:5, :]
    gate = mod_ref[0, 5:6, :]
    d = x_ref.shape[-1]
    rows = x_ref.shape[1] // parts
    hbs = []
    for p in range(parts):
        x = x_ref[0, p * rows:(p + 1) * rows, :]
        hbs.append(((_rms(x) * g_ref[2:3, :]) * (1.0 + scale) + shift).astype(BF16))
    ys = []
    for p in range(parts):
        sl = slice(p * rows, (p + 1) * rows)
        merged = None
        for br, (b_ref, w_ref) in enumerate(((oa_ref, wa_ref), (cb_ref, wb_ref), (pc_ref, wc_ref), (f_ref, wd_ref))):
            gl = _sigmoid(_dot(hbs[p], wg_ref[:, br * d:(br + 1) * d]))
            if br == 0:
                yb = lax.dot_general(b_ref[0, :, sl], w_ref[...], (((0,), (0,)), ((), ())),
                                     preferred_element_type=F32)
            else:
                yb = _dot(b_ref[0, sl, :].astype(BF16), w_ref[...])
            term = gl * yb
            merged = term if merged is None else merged + term
        ys.append(_dot(merged.astype(BF16), wo_ref[...]))
    for p in range(parts):
        sl = slice(p * rows, (p + 1) * rows)
        o_ref[0, sl, :] = x_ref[0, sl, :] + gate * (_rms(ys[p]) * g_ref[3:4, :])


def _mixout_call(x, mod_l, g_l, lw, oa, cb, pc, f, tm):
    B, S, D = x.shape
    tok = lambda w: pl.BlockSpec((1, tm, w), lambda b, i: (b, i, 0))
    return pl.pallas_call(
        functools.partial(_mixout_kernel, parts=2 if tm % 512 == 0 else 1),
        grid=(B, S // tm),
        in_specs=[tok(D),
                  pl.BlockSpec((1, N_MOD, D), lambda b, i: (b, 0, 0)),
                  _const_spec(g_l.shape),
                  _const_spec(lw["wg"].shape),
                  pl.BlockSpec((1, N_HEADS * V_HEAD, tm), lambda b, i: (b, 0, i)), _const_spec(lw["wa"].shape),
                  tok(CONV_CH), _const_spec(lw["wb"].shape),
                  tok(W_BR), _const_spec(lw["wc"].shape),
                  tok(W_BR), _const_spec(lw["wd"].shape),
                  _const_spec(lw["wo"].shape)],
        out_specs=tok(D),
        out_shape=jax.ShapeDtypeStruct(x.shape, F32),
        compiler_params=_params("parallel", "parallel"),
        name="mixer_out",
    )(x, mod_l, g_l, lw["wg"], oa, lw["wa"], cb, lw["wb"], pc, lw["wc"], f, lw["wd"], lw["wo"])


def _rot_half_cols(w):
    half = w.shape[-1] // 2
    return jnp.concatenate([-w[..., half:], w[..., :half]], axis=-1)


def _tables(seq):
    f32 = np.float32
    pos = np.arange(seq, dtype=f32)
    inv = (f32(ROPE_THETA) ** (-np.arange(0, QK_ROPE, 2, dtype=f32) / f32(QK_ROPE))).astype(f32)
    ang = pos[:, None] * inv[None, :]
    cos, sin = np.cos(ang).astype(f32), np.sin(ang).astype(f32)
    cc = np.concatenate([cos, cos], axis=-1)
    ss = np.concatenate([sin, sin], axis=-1)
    z64 = np.zeros((seq, QK_NOPE), f32)
    z32 = np.zeros((seq, LANES - QK_NOPE - QK_ROPE), f32)
    sm_scale = QK_DIM ** -0.5 * math.log2(math.e)
    cosk = np.concatenate([z64, cc, z32], axis=-1)
    sink = np.concatenate([z64, ss, z32], axis=-1)
    cosr = np.tile(cc, (1, N_HEADS)) * f32(sm_scale)
    sinr = np.tile(ss, (1, N_HEADS)) * f32(sm_scale)

    def cos_sin(m, period):
        th = m.astype(np.float64) * (2.0 * math.pi / period)
        return np.cos(th), np.sin(th)

    def dft(n):
        a = np.arange(n, dtype=np.int64)
        return cos_sin((a[:, None] * a[None, :]) % n, n)

    bf = lambda a: a.astype(f32).astype(BF16)
    c_ch, s_ch = dft(GC)
    cs = bf(np.concatenate([c_ch, s_ch], axis=-1) * GC ** -0.5)

    n1 = FFT_N1
    n2 = seq // n1
    c1, s1 = dft(n1)
    f1 = bf(np.concatenate([np.concatenate([c1, -s1], axis=1), np.concatenate([-s1, -c1], axis=1)], axis=0)
            * n1 ** -0.5)
    k1 = np.arange(n1, dtype=np.int64)[:, None, None]
    k2 = np.arange(n2, dtype=np.int64)[None, :, None]
    t2 = np.arange(n2, dtype=np.int64)[None, None, :]
    c2, s2 = cos_sin(((k1 + n1 * k2) * t2) % seq, seq)
    g2c = bf(c2 * n2 ** -0.5)
    g2s = bf(s2 * n2 ** -0.5)
    return dict(cosr=cosr, sinr=sinr, q_scale=sm_scale, cosk=cosk, sink=sink, cs=cs, f1=f1, g2c=g2c, g2s=g2s)


def _layer_weights(l, w_in, q_norm_g, w_uq, kv_norm_g, w_ukv, w_a, conv_w, conv_b, conv_ln_g,
                   conv_ln_b, w_b, pool_w, pool_scale, w_c, w_d, w_out):
    D = w_in.shape[1]
    wi = w_in[l]
    assert Q_PAD - Q_LORA == QK_ROPE
    w1 = jnp.concatenate([wi[:, Q_LORA:Q_LORA + KV_LORA], wi[:, :Q_LORA], wi[:, Q_LORA + KV_LORA:4 * W_BR]],
                         axis=-1).astype(BF16)

    uq = w_uq[l].reshape(Q_LORA, N_HEADS, QK_NOPE + QK_ROPE)
    nope, rope = uq[..., :QK_NOPE], uq[..., QK_NOPE:]
    wq = jnp.concatenate([nope.reshape(Q_LORA, N_HEADS * QK_NOPE), rope.reshape(Q_LORA, N_HEADS * QK_ROPE),
                          _rot_half_cols(rope).reshape(Q_LORA, N_HEADS * QK_ROPE)], axis=-1)
    wq = jnp.pad(wq, ((0, Q_PAD - Q_LORA), (0, 0))).astype(BF16)

    ukv = w_ukv[l].reshape(KV_LORA, N_HEADS, QK_NOPE + V_HEAD)
    zk = jnp.zeros((KV_LORA, N_HEADS, LANES - QK_NOPE), F32)
    wk = jnp.concatenate([ukv[..., :QK_NOPE], zk], axis=-1).reshape(KV_LORA, N_HEADS * LANES).astype(BF16)
    wv = ukv[..., QK_NOPE:].reshape(KV_LORA, N_HEADS * V_HEAD).astype(BF16)

    gq = jnp.pad(q_norm_g[l], (0, Q_PAD - Q_LORA)).reshape(1, Q_PAD)
    return dict(
        w1=w1, wq=wq, wk=wk, wv=wv, gq=gq, gkv=kv_norm_g[l].reshape(1, KV_LORA),
        wg=wi[:, 4 * W_BR:].astype(BF16), wa=w_a[l].astype(BF16),
        wb=w_b[l].astype(BF16), wc=w_c[l].astype(BF16), wd=w_d[l].astype(BF16), wo=w_out[l].astype(BF16),
        conv_w=conv_w[l], conv_b=conv_b[l].reshape(1, CONV_CH),
        ln_g=conv_ln_g[l].reshape(1, CONV_CH), ln_b=conv_ln_b[l].reshape(1, CONV_CH),
        pool_w=pool_w[l].astype(BF16), pool_scale=pool_scale[l].reshape(1, W_BR))


def _tiles(seq):
    t = lambda n: min(n, seq)
    return dict(ffn=t(1024), mix=t(1024), mixout=t(512), tq=t(1024), tk=t(1024), local=t(512))


def kernel(x, c, ada_w, ada_b, norm_g, ffn1_w_in, ffn1_w_out, ffn2_w_in, ffn2_w_out, w_in, q_norm_g,
           w_uq, kv_norm_g, w_ukv, w_a, conv_w, conv_b, conv_ln_g, conv_ln_b, w_b, pool_w, pool_scale,
           w_c, w_d, w_out):
    B, S, D = x.shape
    L = ada_w.shape[0]
    assert D == D_MODEL and S % (FFT_N1 * 16) == 0
    ts = _tiles(S)
    tabs = _tables(S)
    mod = _mod_call(c, ada_w, ada_b).reshape(L, B, N_MOD, D)
    ffn_w = [(_to_bf16(wi), _to_bf16(wo)) for wi, wo in ((ffn1_w_in, ffn1_w_out), (ffn2_w_in, ffn2_w_out))]
    for l in range(L):
        lw = _layer_weights(l, w_in, q_norm_g, w_uq, kv_norm_g, w_ukv, w_a, conv_w, conv_b, conv_ln_g,
                            conv_ln_b, w_b, pool_w, pool_scale, w_c, w_d, w_out)
        mod_l, g_l = mod[l], norm_g[l]
        x = _ffn_call(x, mod_l, g_l, *ffn_w[0], l, 0, ts["ffn"])
        q, k, v, kn, u, zc, p, qq = _mixin_call(x, mod_l, g_l, lw, tabs, ts["mix"])
        oa = _attn_call(q, k, v, kn, ts["tq"], ts["tk"])
        cb, pc = _local_call(u, zc, lw, ts["local"])
        a = _fft1_call(p, qq, tabs)
        f = _fft2_call(a, tabs, S, 8)
        x = _mixout_call(x, mod_l, g_l, lw, oa, cb, pc, f, ts["mixout"])
        x = _ffn_call(x, mod_l, g_l, *ffn_w[1], l, 2, ts["ffn"])
    return x
```

```python
import functools
import math

import jax
import jax.numpy as jnp
import numpy as np
from jax import lax
from jax.experimental import pallas as pl
from jax.experimental.pallas import tpu as pltpu

F32 = jnp.float32
BF16 = jnp.bfloat16

D_MODEL = 1024
D_FF = 2816
N_HEADS = 8
Q_LORA = 352
KV_LORA = 128
QK_NOPE = 64
QK_ROPE = 32
V_HEAD = 64
QK_DIM = QK_NOPE + QK_ROPE
BF16_ROWS = 16
V_ROWS = 80
DENOM_MIN = 2.0 ** -80
ROPE_THETA = 10000.0
CONV_CH = 256
CONV_WIDTH = 31
POOL_WINDOWS = (2, 4, 8, 16)
GC = 128
N_GROUPS = 4
W_BR = 512
N_MOD = 9
EPS = 1e-6

LANES = 128
SUBLANES = 8
Q_PAD = 384
HALO = 16
FFT_N1 = 128
VMEM_LIMIT = 60 * 1024 * 1024

C_CKV = 0
C_CQ = C_CKV + KV_LORA
C_ZB = C_CQ + Q_PAD
C_ZC = C_ZB + W_BR
C_ZD = C_ZC + W_BR
W1_COLS = C_ZD + W_BR


def _params(*sem):
    return pltpu.CompilerParams(dimension_semantics=sem, vmem_limit_bytes=VMEM_LIMIT)


def _const_spec(shape):
    zeros = (0,) * len(shape)
    return pl.BlockSpec(shape, lambda *_: zeros, pipeline_mode=pl.Buffered(1))


def _dot(a, b):
    return jnp.dot(a, b, preferred_element_type=F32)


def _rms(x):
    return x * lax.rsqrt(jnp.mean(x * x, axis=-1, keepdims=True) + EPS)


def _sigmoid(x):
    return 1.0 / (1.0 + jnp.exp(-x))


def _mod_kernel(c_ref, w_ref, b_ref, o_ref):
    c = c_ref[...]
    ca = c * _sigmoid(c)
    o_ref[0] = jnp.dot(ca, w_ref[0], preferred_element_type=F32,
                       precision=lax.Precision.HIGHEST) + b_ref[0]


def _mod_call(c, ada_w, ada_b):
    L, D, N = ada_w.shape
    B = c.shape[0]
    tn = 1152
    return pl.pallas_call(
        _mod_kernel,
        grid=(L, N // tn),
        in_specs=[pl.BlockSpec((B, D), lambda l, j: (0, 0)),
                  pl.BlockSpec((1, D, tn), lambda l, j: (l, 0, j)),
                  pl.BlockSpec((1, 1, tn), lambda l, j: (l, 0, j))],
        out_specs=pl.BlockSpec((1, B, tn), lambda l, j: (l, 0, j)),
        out_shape=jax.ShapeDtypeStruct((L, B, N), F32),
        compiler_params=_params("parallel", "parallel"),
        name="adaln_mod",
    )(c, ada_w, ada_b.reshape(L, 1, N))


def _ffn_kernel(x_ref, mod_ref, g_ref, win_ref, wout_ref, o_ref, a_scr, *, sub, chunk, parts):
    shift = mod_ref[0, 3 * sub:3 * sub + 1, :]
    scale = mod_ref[0, 3 * sub + 1:3 * sub + 2, :]
    gate = mod_ref[0, 3 * sub + 2:3 * sub + 3, :]
    g_in = g_ref[2 * sub:2 * sub + 1, :]
    g_out = g_ref[2 * sub + 1:2 * sub + 2, :]
    rows = x_ref.shape[1] // parts
    hbs = []
    for p in range(parts):
        x = x_ref[0, p * rows:(p + 1) * rows, :]
        hbs.append(((_rms(x) * g_in) * (1.0 + scale) + shift).astype(BF16))
    for p in range(parts):
        for j in range(D_FF // chunk):
            gj = _dot(hbs[p], win_ref[0, :, j * chunk:(j + 1) * chunk])
            uj = _dot(hbs[p], win_ref[0, :, D_FF + j * chunk:D_FF + (j + 1) * chunk])
            a_scr[p, :, j * chunk:(j + 1) * chunk] = (gj * _sigmoid(gj) * uj).astype(BF16)
    for p in range(parts):
        y = _dot(a_scr[p], wout_ref[0])
        x = x_ref[0, p * rows:(p + 1) * rows, :]
        o_ref[0, p * rows:(p + 1) * rows, :] = x + (0.5 * gate) * (_rms(y) * g_out)


def _cast_kernel(w_ref, o_ref):
    o_ref[...] = w_ref[...].astype(o_ref.dtype)


def _to_bf16(w):
    L, R, C = w.shape
    rows = R // 4
    assert R % 4 == 0 and rows % BF16_ROWS == 0
    blk = pl.BlockSpec((1, rows, C), lambda l, i: (l, i, 0))
    return pl.pallas_call(
        _cast_kernel,
        grid=(L, R // rows),
        in_specs=[blk],
        out_specs=blk,
        out_shape=jax.ShapeDtypeStruct(w.shape, BF16),
        compiler_params=_params("parallel", "parallel"),
        name="weight_cast",
    )(w)


def _layer_spec(w, l):
    return pl.BlockSpec((1,) + w.shape[1:], lambda *_: (l,) + (0,) * (w.ndim - 1), pipeline_mode=pl.Buffered(1))


def _ffn_call(x, mod_l, g_l, w_in, w_out, l, sub, tm):
    B, S, D = x.shape
    parts = 4 if tm % 1024 == 0 else 1
    kern = functools.partial(_ffn_kernel, sub=sub, chunk=256, parts=parts)
    return pl.pallas_call(
        kern,
        grid=(B, S // tm),
        in_specs=[pl.BlockSpec((1, tm, D), lambda b, i: (b, i, 0)),
                  pl.BlockSpec((1, N_MOD, D), lambda b, i: (b, 0, 0)),
                  _const_spec(g_l.shape),
                  _layer_spec(w_in, l),
                  _layer_spec(w_out, l)],
        out_specs=pl.BlockSpec((1, tm, D), lambda b, i: (b, i, 0)),
        out_shape=jax.ShapeDtypeStruct(x.shape, F32),
        scratch_shapes=[pltpu.VMEM((parts, tm // parts, D_FF), BF16)],
        compiler_params=_params("parallel", "parallel"),
        name=f"ffn{sub}",
    )(x, mod_l, g_l, w_in, w_out)


def _mixin_kernel(x_ref, mod_ref, g_ref, w1_ref, gq_ref, gkv_ref, wq_ref, wk_ref, wv_ref,
                  cr_ref, sr_ref, ck_ref, sk_ref, cs_ref,
                  q_ref, k_ref, v_ref, kn_ref, u_ref, zc_ref, p_ref, qq_ref, *, q_scale):
    x = x_ref[0]
    tm = x.shape[0]
    shift = mod_ref[0, 3:4, :]
    scale = mod_ref[0, 4:5, :]
    h = (_rms(x) * g_ref[2:3, :]) * (1.0 + scale) + shift
    hb = h.astype(BF16)
    z = _dot(hb, w1_ref[...])

    cq = z[:, C_CQ:C_CQ + Q_PAD]
    qlane = lax.broadcasted_iota(jnp.int32, (1, Q_PAD), 1)
    cqm = jnp.where(qlane < Q_LORA, cq, 0.0)
    cqn = cqm * lax.rsqrt(jnp.sum(cqm * cqm, axis=-1, keepdims=True) * (1.0 / Q_LORA) + EPS)
    cqn = (cqn * gq_ref[...]).astype(BF16)
    qq = _dot(cqn, wq_ref[...])
    nope = qq[:, :N_HEADS * QK_NOPE] * q_scale
    w_rope = N_HEADS * QK_ROPE
    roped = (qq[:, N_HEADS * QK_NOPE:N_HEADS * QK_NOPE + w_rope] * cr_ref[...]
             + qq[:, N_HEADS * QK_NOPE + w_rope:] * sr_ref[...])
    for blk in range(N_HEADS * QK_NOPE // LANES):
        nt = nope[:, blk * LANES:(blk + 1) * LANES].T.astype(BF16)
        for i in range(LANES // QK_NOPE):
            q_ref[0, blk * (LANES // QK_NOPE) + i, 0:QK_NOPE, :] = nt[i * QK_NOPE:(i + 1) * QK_NOPE]
    for blk in range(w_rope // LANES):
        rt = roped[:, blk * LANES:(blk + 1) * LANES].T.astype(BF16)
        for i in range(LANES // QK_ROPE):
            q_ref[0, blk * (LANES // QK_ROPE) + i, QK_NOPE:QK_NOPE + QK_ROPE, :] = rt[i * QK_ROPE:(i + 1) * QK_ROPE]
    zero_rows = jnp.zeros((LANES - QK_NOPE - QK_ROPE, tm), BF16)
    for hd in range(N_HEADS):
        q_ref[0, hd, QK_NOPE + QK_ROPE:, :] = zero_rows

    kr_at = Q_LORA - (Q_PAD - LANES)
    kr = pltpu.roll(cq[:, Q_PAD - LANES:], (QK_NOPE - kr_at) % LANES, 1)
    klane = lax.broadcasted_iota(jnp.int32, (1, LANES), 1)
    kr_rot = jnp.where(klane < QK_NOPE + QK_ROPE // 2,
                       -pltpu.roll(kr, LANES - QK_ROPE // 2, 1), pltpu.roll(kr, QK_ROPE // 2, 1))
    k_rope = kr * ck_ref[...] + kr_rot * sk_ref[...]

    ckv = z[:, C_CKV:C_CKV + KV_LORA]
    ckvn = (_rms(ckv) * gkv_ref[...]).astype(BF16)
    kk = _dot(ckvn, wk_ref[...])
    ones_lane = (klane == QK_DIM).astype(F32)
    for hd in range(N_HEADS):
        kh = kk[:, hd * LANES:(hd + 1) * LANES] + k_rope
        k_ref[0, hd] = (kh + ones_lane).astype(BF16)
        norm2 = jnp.max(jnp.sum(kh * kh, axis=-1, keepdims=True), axis=0, keepdims=True)
        kn_ref[0, hd] = jnp.broadcast_to(norm2, (SUBLANES, LANES))
    vv = _dot(ckvn, wv_ref[...])
    ones_rows = (lax.broadcasted_iota(jnp.int32, (V_ROWS - V_HEAD, tm), 0) == 0).astype(BF16)
    for blk in range(N_HEADS * V_HEAD // LANES):
        vt = vv[:, blk * LANES:(blk + 1) * LANES].T.astype(BF16)
        for i in range(LANES // V_HEAD):
            v_ref[0, blk * (LANES // V_HEAD) + i, 0:V_HEAD, :] = vt[i * V_HEAD:(i + 1) * V_HEAD]
    for hd in range(N_HEADS):
        v_ref[0, hd, V_HEAD:, :] = ones_rows

    a = z[:, C_ZB:C_ZB + CONV_CH]
    gt = z[:, C_ZB + CONV_CH:C_ZB + 2 * CONV_CH]
    u_ref[0] = a * _sigmoid(gt)

    zc_ref[0] = z[:, C_ZC:C_ZC + W_BR]

    for g in range(N_GROUPS):
        zd = z[:, C_ZD + g * GC:C_ZD + (g + 1) * GC].astype(BF16)
        pq = _dot(zd, cs_ref[...])
        p_ref[0, :, g * GC:(g + 1) * GC] = pq[:, :GC]
        qq_ref[0, :, g * GC:(g + 1) * GC] = pq[:, GC:]


def _mixin_call(x, mod_l, g_l, lw, tabs, tm):
    B, S, D = x.shape
    tok = lambda w: pl.BlockSpec((1, tm, w), lambda b, i: (b, i, 0))
    head = pl.BlockSpec((1, N_HEADS, tm, LANES), lambda b, i: (b, 0, i, 0))
    head_t = pl.BlockSpec((1, N_HEADS, LANES, tm), lambda b, i: (b, 0, 0, i))
    tab = lambda w: pl.BlockSpec((tm, w), lambda b, i: (i, 0))
    hshape = jax.ShapeDtypeStruct((B, N_HEADS, S, LANES), BF16)
    hshape_t = jax.ShapeDtypeStruct((B, N_HEADS, LANES, S), BF16)
    return pl.pallas_call(
        functools.partial(_mixin_kernel, q_scale=tabs["q_scale"]),
        grid=(B, S // tm),
        in_specs=[tok(D),
                  pl.BlockSpec((1, N_MOD, D), lambda b, i: (b, 0, 0)),
                  _const_spec(g_l.shape),
                  _const_spec(lw["w1"].shape),
                  _const_spec(lw["gq"].shape),
                  _const_spec(lw["gkv"].shape),
                  _const_spec(lw["wq"].shape),
                  _const_spec(lw["wk"].shape),
                  _const_spec(lw["wv"].shape),
                  tab(N_HEADS * QK_ROPE), tab(N_HEADS * QK_ROPE), tab(LANES), tab(LANES),
                  _const_spec(tabs["cs"].shape)],
        out_specs=[head_t, head, pl.BlockSpec((1, N_HEADS, V_ROWS, tm), lambda b, i: (b, 0, 0, i)),
                   pl.BlockSpec((1, N_HEADS, SUBLANES, LANES), lambda b, i: (b, 0, i, 0)),
                   tok(CONV_CH), tok(W_BR), tok(W_BR), tok(W_BR)],
        out_shape=[hshape_t, hshape, jax.ShapeDtypeStruct((B, N_HEADS, V_ROWS, S), BF16),
                   jax.ShapeDtypeStruct((B, N_HEADS, (S // tm) * SUBLANES, LANES), F32),
                   jax.ShapeDtypeStruct((B, S, CONV_CH), F32),
                   jax.ShapeDtypeStruct((B, S, W_BR), F32),
                   jax.ShapeDtypeStruct((B, S, W_BR), F32),
                   jax.ShapeDtypeStruct((B, S, W_BR), F32)],
        compiler_params=_params("parallel", "parallel"),
        name="mixer_in",
    )(x, mod_l, g_l, lw["w1"], lw["gq"], lw["gkv"], lw["wq"], lw["wk"], lw["wv"],
      tabs["cosr"], tabs["sinr"], tabs["cosk"], tabs["sink"], tabs["cs"])


def _attn_kernel(qt_ref, k_ref, vt_ref, kn_ref, o_ref, acc_scr, m_scr, *, tq, tk):
    seq = k_ref.shape[2]
    nq, nk = seq // tq, seq // tk
    k_norm2 = jnp.max(kn_ref[0, 0])

    def exact_tile(t, qoff):
        qt = qt_ref[0, 0, :, pl.ds(qoff, tq)]
        m_scr[...] = jnp.full(m_scr.shape, -jnp.inf, F32)
        acc_scr[t, 0:V_ROWS, :] = jnp.zeros((V_ROWS, tq), F32)

        def chunk(j, carry):
            off = pl.multiple_of(j * tk, tk)
            st = _dot(k_ref[0, 0, pl.ds(off, tk), :], qt)
            m_prev = m_scr[...]
            m_new = jnp.maximum(m_prev, jnp.max(st, axis=0, keepdims=True))
            pt = jnp.exp2(st - m_new).astype(BF16)
            pv = _dot(vt_ref[0, 0, :, pl.ds(off, tk)], pt)
            acc_scr[t, 0:V_ROWS, :] = jnp.exp2(m_prev - m_new) * acc_scr[t, 0:V_ROWS, :] + pv
            m_scr[...] = m_new
            return carry

        lax.fori_loop(0, nk, chunk, 0)

    def bounded_tile(t, qoff):
        qt = qt_ref[0, 0, :, pl.ds(qoff, tq)]
        q32 = qt.astype(F32)
        bound = jnp.sqrt(jnp.sum(q32 * q32, axis=0, keepdims=True) * k_norm2)
        first = lax.broadcasted_iota(jnp.int32, (BF16_ROWS, tq), 0) == 0
        stab = jnp.where(first, -bound, 0.0).astype(BF16)
        qs = jnp.concatenate([qt[0:QK_DIM], stab, qt[QK_DIM + BF16_ROWS:]], axis=0)
        acc = None
        for j in range(nk):
            st = _dot(k_ref[0, 0, j * tk:(j + 1) * tk, :], qs)
            pv = _dot(vt_ref[0, 0, :, j * tk:(j + 1) * tk], jnp.exp2(st).astype(BF16))
            acc = pv if acc is None else acc + pv
        acc_scr[t, 0:V_ROWS, :] = acc
        return jnp.min(acc[V_HEAD:V_HEAD + 1, :]) >= DENOM_MIN

    group = acc_scr.shape[0]

    def q_group(gi, carry):
        offs = [pl.multiple_of((gi * group + t) * tq, tq) for t in range(group)]
        ok = None
        for t in range(group):
            ok_t = bounded_tile(t, offs[t])
            ok = ok_t if ok is None else jnp.logical_and(ok, ok_t)

        @pl.when(jnp.logical_not(ok))
        def _():
            for t in range(group):
                exact_tile(t, offs[t])

        for t in range(group):
            denom = acc_scr[t, V_HEAD:V_HEAD + 1, :]
            o_ref[0, :, pl.ds(offs[t], tq)] = (acc_scr[t, 0:V_HEAD, :] / denom).astype(o_ref.dtype)
        return carry

    lax.fori_loop(0, nq // group, q_group, 0)


def _attn_call(qt, k, vt, kn, tq, tk):
    B, H, S, _ = k.shape
    kern = functools.partial(_attn_kernel, tq=tq, tk=tk)
    return pl.pallas_call(
        kern,
        grid=(B, H),
        in_specs=[pl.BlockSpec((1, 1, LANES, S), lambda b, h: (b, h, 0, 0)),
                  pl.BlockSpec((1, 1, S, LANES), lambda b, h: (b, h, 0, 0)),
                  pl.BlockSpec((1, 1, V_ROWS, S), lambda b, h: (b, h, 0, 0)),
                  pl.BlockSpec((1, 1) + kn.shape[2:], lambda b, h: (b, h, 0, 0))],
        out_specs=pl.BlockSpec((1, V_HEAD, S), lambda b, h: (b, h, 0)),
        out_shape=jax.ShapeDtypeStruct((B, H * V_HEAD, S), BF16),
        scratch_shapes=[pltpu.VMEM((2 if (S // tq) % 2 == 0 else 1, V_ROWS, tq), F32), pltpu.VMEM((1, tq), F32)],
        compiler_params=_params("parallel", "arbitrary"),
        name="attention",
    )(qt, k, vt, kn)


def _local_kernel(u_ref, up_ref, un_ref, z_ref, zp_ref, zn_ref, cw_ref, cb_ref, lg_ref, lb_ref,
                  pw_ref, ps_ref, oc_ref, op_ref, ubuf, zbuf, ush, sbuf, *, ts, seq, rc, rp):
    i = pl.program_id(1)
    has_prev = i > 0
    has_next = i < pl.num_programs(1) - 1
    ubuf[0:HALO, :] = jnp.where(has_prev, up_ref[0], 0.0)
    ubuf[HALO:HALO + ts, :] = u_ref[0]
    ubuf[HALO + ts:, :] = jnp.where(has_next, un_ref[0], 0.0)
    zbuf[0:HALO, :] = jnp.where(has_prev, zp_ref[0], 0.0)
    zbuf[HALO:HALO + ts, :] = z_ref[0]
    zbuf[HALO + ts:2 * HALO + ts, :] = jnp.where(has_next, zn_ref[0], 0.0)
    zbuf[2 * HALO + ts:, :] = jnp.zeros((SUBLANES, W_BR), F32)

    span = ts + 2 * HALO - SUBLANES
    for s in range(1, SUBLANES):
        ush[s - 1, 0:span, :] = ubuf[s:s + span, :]

    pad = CONV_WIDTH // 2
    for r in range(0, ts, rc):
        acc = jnp.zeros((rc, CONV_CH), F32)
        for kk in range(CONV_WIDTH):
            st = HALO + r + kk - pad
            s, base = st % SUBLANES, st - st % SUBLANES
            tap = ubuf[base:base + rc, :] if s == 0 else ush[s - 1, base:base + rc, :]
            acc = acc + tap * cw_ref[kk:kk + 1, :]
        y = acc + cb_ref[...]
        mu = jnp.mean(y, axis=-1, keepdims=True)
        yc = y - mu
        yn = yc * lax.rsqrt(jnp.mean(yc * yc, axis=-1, keepdims=True) + EPS)
        yn = yn * lg_ref[...] + lb_ref[...]
        oc_ref[0, r:r + rc, :] = (yn * _sigmoid(yn)).astype(oc_ref.dtype)

    assert all(w == 2 ** (g + 1) for g, w in enumerate(POOL_WINDOWS))
    rows_m = ts + 2 * HALO
    for m in range(1, N_GROUPS):
        step = 2 ** (m - 1)
        src = zbuf if m == 1 else sbuf.at[m - 2]
        lanes = slice(m * GC, N_GROUPS * GC)
        for c in range(0, rows_m, LANES):
            n = min(LANES, rows_m - c)
            sbuf[m - 1, c:c + n, lanes] = src[c:c + n, lanes] + src[c + step:c + step + n, lanes]
        rows_m -= SUBLANES

    for r in range(0, ts, rp):
        t = i * ts + r + lax.broadcasted_iota(jnp.int32, (rp, 1), 0)
        for g, w in enumerate(POOL_WINDOWS):
            lo = w // 2
            hi = w - 1 - lo
            cols = slice(g * GC, (g + 1) * GC)
            half = zbuf if g == 0 else sbuf.at[g - 1]
            win = half[HALO + r - lo:HALO + r - lo + rp, cols] + half[HALO + r:HALO + r + rp, cols]
            cnt = (jnp.minimum(t + hi + 1, seq) - jnp.maximum(t - lo, 0)).astype(F32)
            d = win / cnt - zbuf[HALO + r:HALO + r + rp, cols]
            yg = _dot(d.astype(BF16), pw_ref[g]) * ps_ref[:, cols]
            op_ref[0, r:r + rp, cols] = yg.astype(op_ref.dtype)


def _local_call(u, zc, lw, ts):
    B, S, _ = u.shape
    nh = ts // HALO
    last = S // HALO - 1
    cur = lambda w: pl.BlockSpec((1, ts, w), lambda b, i: (b, i, 0))
    prev = lambda w: pl.BlockSpec((1, HALO, w), lambda b, i: (b, jnp.maximum(i * nh - 1, 0), 0))
    nxt = lambda w: pl.BlockSpec((1, HALO, w), lambda b, i: (b, jnp.minimum((i + 1) * nh, last), 0))
    kern = functools.partial(_local_kernel, ts=ts, seq=S, rc=64, rp=min(ts, 256))
    return pl.pallas_call(
        kern,
        grid=(B, S // ts),
        in_specs=[cur(CONV_CH), prev(CONV_CH), nxt(CONV_CH), cur(W_BR), prev(W_BR), nxt(W_BR),
                  _const_spec(lw["conv_w"].shape), _const_spec(lw["conv_b"].shape),
                  _const_spec(lw["ln_g"].shape), _const_spec(lw["ln_b"].shape),
                  _const_spec(lw["pool_w"].shape), _const_spec(lw["pool_scale"].shape)],
        out_specs=[cur(CONV_CH), cur(W_BR)],
        out_shape=[jax.ShapeDtypeStruct((B, S, CONV_CH), BF16),
                   jax.ShapeDtypeStruct((B, S, W_BR), BF16)],
        scratch_shapes=[pltpu.VMEM((ts + 2 * HALO, CONV_CH), F32),
                        pltpu.VMEM((ts + 2 * HALO + SUBLANES, W_BR), F32),
                        pltpu.VMEM((SUBLANES - 1, ts + 2 * HALO, CONV_CH), F32),
                        pltpu.VMEM((N_GROUPS - 1, ts + 2 * HALO, W_BR), F32)],
        compiler_params=_params("parallel", "parallel"),
        name="conv_pool",
    )(u, u, u, zc, zc, zc, lw["conv_w"], lw["conv_b"], lw["ln_g"], lw["ln_b"],
      lw["pool_w"], lw["pool_scale"])


def _fft1_kernel(p_ref, q_ref, f_ref, o_ref):
    for j in range(p_ref.shape[2]):
        rhs = jnp.concatenate([p_ref[0, :, j, :], q_ref[0, :, j, :]], axis=0).astype(BF16)
        o_ref[0, :, j, :] = _dot(f_ref[...], rhs)


def _fft1_call(p, q, tabs):
    B, S, W = p.shape
    n2 = S // FFT_N1
    pv = p.reshape(B, FFT_N1, n2, W)
    qv = q.reshape(B, FFT_N1, n2, W)
    blk = pl.BlockSpec((1, FFT_N1, SUBLANES, W), lambda b, j: (b, 0, j, 0))
    return pl.pallas_call(
        _fft1_kernel,
        grid=(B, n2 // SUBLANES),
        in_specs=[blk, blk, _const_spec(tabs["f1"].shape)],
        out_specs=pl.BlockSpec((1, 2 * FFT_N1, SUBLANES, W), lambda b, j: (b, 0, j, 0)),
        out_shape=jax.ShapeDtypeStruct((B, 2 * FFT_N1, n2, W), F32),
        compiler_params=_params("parallel", "parallel"),
        name="fft_stage1",
    )(pv, qv, tabs["f1"])


def _fft2_kernel(a_ref, gc_ref, gs_ref, o_ref, *, kb):
    for j in range(kb):
        f = _dot(gc_ref[j], a_ref[0, 0, j].astype(BF16)) + _dot(gs_ref[j], a_ref[0, 1, j].astype(BF16))
        o_ref[0, :, j, :] = f


def _fft2_call(a, tabs, seq, kb):
    B = a.shape[0]
    n2 = seq // FFT_N1
    av = a.reshape(B, 2, FFT_N1, n2, W_BR)
    kern = functools.partial(_fft2_kernel, kb=kb)
    out = pl.pallas_call(
        kern,
        grid=(B, FFT_N1 // kb),
        in_specs=[pl.BlockSpec((1, 2, kb, n2, W_BR), lambda b, j: (b, 0, j, 0, 0)),
                  pl.BlockSpec((kb, n2, n2), lambda b, j: (j, 0, 0)),
                  pl.BlockSpec((kb, n2, n2), lambda b, j: (j, 0, 0))],
        out_specs=pl.BlockSpec((1, n2, kb, W_BR), lambda b, j: (b, 0, j, 0)),
        out_shape=jax.ShapeDtypeStruct((B, n2, FFT_N1, W_BR), F32),
        compiler_params=_params("parallel", "parallel"),
        name="fft_stage2",
    )(av, tabs["g2c"], tabs["g2s"])
    return out.reshape(B, seq, W_BR)


def _mixout_kernel(x_ref, mod_ref, g_ref, wg_ref, oa_ref, wa_ref, cb_ref, wb_ref, pc_ref, wc_ref,
                   f_ref, wd_ref, wo_ref, o_ref, *, parts):
    shift = mod_ref[0, 3:4, :]
    scale = mod_ref[0, 4:5, :]
    gate = mod_ref[0, 5:6, :]
    d = x_ref.shape[-1]
    rows = x_ref.shape[1] // parts
    hbs = []
    for p in range(parts):
        x = x_ref[0, p * rows:(p + 1) * rows, :]
        hbs.append(((_rms(x) * g_ref[2:3, :]) * (1.0 + scale) + shift).astype(BF16))
    ys = []
    for p in range(parts):
        sl = slice(p * rows, (p + 1) * rows)
        merged = None
        for br, (b_ref, w_ref) in enumerate(((oa_ref, wa_ref), (cb_ref, wb_ref), (pc_ref, wc_ref), (f_ref, wd_ref))):
            gl = _sigmoid(_dot(hbs[p], wg_ref[:, br * d:(br + 1) * d]))
            if br == 0:
                yb = lax.dot_general(b_ref[0, :, sl], w_ref[...], (((0,), (0,)), ((), ())),
                                     preferred_element_type=F32)
            else:
                yb = _dot(b_ref[0, sl, :].astype(BF16), w_ref[...])
            term = gl * yb
            merged = term if merged is None else merged + term
        ys.append(_dot(merged.astype(BF16), wo_ref[...]))
    for p in range(parts):
        sl = slice(p * rows, (p + 1) * rows)
        o_ref[0, sl, :] = x_ref[0, sl, :] + gate * (_rms(ys[p]) * g_ref[3:4, :])


def _mixout_call(x, mod_l, g_l, lw, oa, cb, pc, f, tm):
    B, S, D = x.shape
    tok = lambda w: pl.BlockSpec((1, tm, w), lambda b, i: (b, i, 0))
    return pl.pallas_call(
        functools.partial(_mixout_kernel, parts=tm // 256 if tm % 256 == 0 else 1),
        grid=(B, S // tm),
        in_specs=[tok(D),
                  pl.BlockSpec((1, N_MOD, D), lambda b, i: (b, 0, 0)),
                  _const_spec(g_l.shape),
                  _const_spec(lw["wg"].shape),
                  pl.BlockSpec((1, N_HEADS * V_HEAD, tm), lambda b, i: (b, 0, i)), _const_spec(lw["wa"].shape),
                  tok(CONV_CH), _const_spec(lw["wb"].shape),
                  tok(W_BR), _const_spec(lw["wc"].shape),
                  tok(W_BR), _const_spec(lw["wd"].shape),
                  _const_spec(lw["wo"].shape)],
        out_specs=tok(D),
        out_shape=jax.ShapeDtypeStruct(x.shape, F32),
        compiler_params=_params("parallel", "parallel"),
        name="mixer_out",
    )(x, mod_l, g_l, lw["wg"], oa, lw["wa"], cb, lw["wb"], pc, lw["wc"], f, lw["wd"], lw["wo"])


def _rot_half_cols(w):
    half = w.shape[-1] // 2
    return jnp.concatenate([-w[..., half:], w[..., :half]], axis=-1)


def _tables(seq):
    f32 = np.float32
    pos = np.arange(seq, dtype=f32)
    inv = (f32(ROPE_THETA) ** (-np.arange(0, QK_ROPE, 2, dtype=f32) / f32(QK_ROPE))).astype(f32)
    ang = pos[:, None] * inv[None, :]
    cos, sin = np.cos(ang).astype(f32), np.sin(ang).astype(f32)
    cc = np.concatenate([cos, cos], axis=-1)
    ss = np.concatenate([sin, sin], axis=-1)
    z64 = np.zeros((seq, QK_NOPE), f32)
    z32 = np.zeros((seq, LANES - QK_NOPE - QK_ROPE), f32)
    sm_scale = QK_DIM ** -0.5 * math.log2(math.e)
    cosk = np.concatenate([z64, cc, z32], axis=-1)
    sink = np.concatenate([z64, ss, z32], axis=-1)
    cosr = np.tile(cc, (1, N_HEADS)) * f32(sm_scale)
    sinr = np.tile(ss, (1, N_HEADS)) * f32(sm_scale)

    def cos_sin(m, period):
        th = m.astype(np.float64) * (2.0 * math.pi / period)
        return np.cos(th), np.sin(th)

    def dft(n):
        a = np.arange(n, dtype=np.int64)
        return cos_sin((a[:, None] * a[None, :]) % n, n)

    bf = lambda a: a.astype(f32).astype(BF16)
    c_ch, s_ch = dft(GC)
    cs = bf(np.concatenate([c_ch, s_ch], axis=-1) * GC ** -0.5)

    n1 = FFT_N1
    n2 = seq // n1
    c1, s1 = dft(n1)
    f1 = bf(np.concatenate([np.concatenate([c1, -s1], axis=1), np.concatenate([-s1, -c1], axis=1)], axis=0)
            * n1 ** -0.5)
    k1 = np.arange(n1, dtype=np.int64)[:, None, None]
    k2 = np.arange(n2, dtype=np.int64)[None, :, None]
    t2 = np.arange(n2, dtype=np.int64)[None, None, :]
    c2, s2 = cos_sin(((k1 + n1 * k2) * t2) % seq, seq)
    g2c = bf(c2 * n2 ** -0.5)
    g2s = bf(s2 * n2 ** -0.5)
    return dict(cosr=cosr, sinr=sinr, q_scale=sm_scale, cosk=cosk, sink=sink, cs=cs, f1=f1, g2c=g2c, g2s=g2s)


def _layer_weights(l, w_in, q_norm_g, w_uq, kv_norm_g, w_ukv, w_a, conv_w, conv_b, conv_ln_g,
                   conv_ln_b, w_b, pool_w, pool_scale, w_c, w_d, w_out):
    D = w_in.shape[1]
    wi = w_in[l]
    assert Q_PAD - Q_LORA == QK_ROPE
    w1 = jnp.concatenate([wi[:, Q_LORA:Q_LORA + KV_LORA], wi[:, :Q_LORA], wi[:, Q_LORA + KV_LORA:4 * W_BR]],
                         axis=-1).astype(BF16)

    uq = w_uq[l].reshape(Q_LORA, N_HEADS, QK_NOPE + QK_ROPE)
    nope, rope = uq[..., :QK_NOPE], uq[..., QK_NOPE:]
    wq = jnp.concatenate([nope.reshape(Q_LORA, N_HEADS * QK_NOPE), rope.reshape(Q_LORA, N_HEADS * QK_ROPE),
                          _rot_half_cols(rope).reshape(Q_LORA, N_HEADS * QK_ROPE)], axis=-1)
    wq = jnp.pad(wq, ((0, Q_PAD - Q_LORA), (0, 0))).astype(BF16)

    ukv = w_ukv[l].reshape(KV_LORA, N_HEADS, QK_NOPE + V_HEAD)
    zk = jnp.zeros((KV_LORA, N_HEADS, LANES - QK_NOPE), F32)
    wk = jnp.concatenate([ukv[..., :QK_NOPE], zk], axis=-1).reshape(KV_LORA, N_HEADS * LANES).astype(BF16)
    wv = ukv[..., QK_NOPE:].reshape(KV_LORA, N_HEADS * V_HEAD).astype(BF16)

    gq = jnp.pad(q_norm_g[l], (0, Q_PAD - Q_LORA)).reshape(1, Q_PAD)
    return dict(
        w1=w1, wq=wq, wk=wk, wv=wv, gq=gq, gkv=kv_norm_g[l].reshape(1, KV_LORA),
        wg=wi[:, 4 * W_BR:].astype(BF16), wa=w_a[l].astype(BF16),
        wb=w_b[l].astype(BF16), wc=w_c[l].astype(BF16), wd=w_d[l].astype(BF16), wo=w_out[l].astype(BF16),
        conv_w=conv_w[l], conv_b=conv_b[l].reshape(1, CONV_CH),
        ln_g=conv_ln_g[l].reshape(1, CONV_CH), ln_b=conv_ln_b[l].reshape(1, CONV_CH),
        pool_w=pool_w[l].astype(BF16), pool_scale=pool_scale[l].reshape(1, W_BR))


def _tiles(seq):
    t = lambda n: min(n, seq)
    return dict(ffn=t(1024), mix=t(1024), mixout=t(1024), tq=t(1024), tk=t(1024), local=t(512))


def kernel(x, c, ada_w, ada_b, norm_g, ffn1_w_in, ffn1_w_out, ffn2_w_in, ffn2_w_out, w_in, q_norm_g,
           w_uq, kv_norm_g, w_ukv, w_a, conv_w, conv_b, conv_ln_g, conv_ln_b, w_b, pool_w, pool_scale,
           w_c, w_d, w_out):
    B, S, D = x.shape
    L = ada_w.shape[0]
    assert D == D_MODEL and S % (FFT_N1 * 16) == 0
    ts = _tiles(S)
    tabs = _tables(S)
    mod = _mod_call(c, ada_w, ada_b).reshape(L, B, N_MOD, D)
    ffn_w = [(_to_bf16(wi), _to_bf16(wo)) for wi, wo in ((ffn1_w_in, ffn1_w_out), (ffn2_w_in, ffn2_w_out))]
    for l in range(L):
        lw = _layer_weights(l, w_in, q_norm_g, w_uq, kv_norm_g, w_ukv, w_a, conv_w, conv_b, conv_ln_g,
                            conv_ln_b, w_b, pool_w, pool_scale, w_c, w_d, w_out)
        mod_l, g_l = mod[l], norm_g[l]
        x = _ffn_call(x, mod_l, g_l, *ffn_w[0], l, 0, ts["ffn"])
        q, k, v, kn, u, zc, p, qq = _mixin_call(x, mod_l, g_l, lw, tabs, ts["mix"])
        oa = _attn_call(q, k, v, kn, ts["tq"], ts["tk"])
        cb, pc = _local_call(u, zc, lw, ts["local"])
        a = _fft1_call(p, qq, tabs)
        f = _fft2_call(a, tabs, S, 8)
        x = _mixout_call(x, mod_l, g_l, lw, oa, cb, pc, f, ts["mixout"])
        x = _ffn_call(x, mod_l, g_l, *ffn_w[1], l, 2, ts["ffn"])
    return x
```

```python
import functools
import math

import jax
import jax.numpy as jnp
import numpy as np
from jax import lax
from jax.experimental import pallas as pl
from jax.experimental.pallas import tpu as pltpu

F32 = jnp.float32
BF16 = jnp.bfloat16

D_MODEL = 1024
D_FF = 2816
N_HEADS = 8
Q_LORA = 352
KV_LORA = 128
QK_NOPE = 64
QK_ROPE = 32
V_HEAD = 64
QK_DIM = QK_NOPE + QK_ROPE
BF16_ROWS = 16
V_ROWS = 80
DENOM_MIN = 2.0 ** -80
ROPE_THETA = 10000.0
CONV_CH = 256
CONV_WIDTH = 31
POOL_WINDOWS = (2, 4, 8, 16)
GC = 128
N_GROUPS = 4
W_BR = 512
N_MOD = 9
EPS = 1e-6

LANES = 128
SUBLANES = 8
Q_PAD = 384
HALO = 16
FFT_N1 = 128
VMEM_LIMIT = 56 * 1024 * 1024

C_CKV = 0
C_CQ = C_CKV + KV_LORA
C_ZB = C_CQ + Q_PAD
C_ZC = C_ZB + W_BR
C_ZD = C_ZC + W_BR
W1_COLS = C_ZD + W_BR


def _params(*sem):
    return pltpu.CompilerParams(dimension_semantics=sem, vmem_limit_bytes=VMEM_LIMIT)


def _const_spec(shape):
    zeros = (0,) * len(shape)
    return pl.BlockSpec(shape, lambda *_: zeros, pipeline_mode=pl.Buffered(1))


def _dot(a, b):
    return jnp.dot(a, b, preferred_element_type=F32)


def _rms(x):
    return x * lax.rsqrt(jnp.mean(x * x, axis=-1, keepdims=True) + EPS)


def _sigmoid(x):
    return 1.0 / (1.0 + jnp.exp(-x))


def _mod_kernel(c_ref, w_ref, b_ref, o_ref):
    c = c_ref[...]
    ca = c * _sigmoid(c)
    o_ref[0] = jnp.dot(ca, w_ref[0], preferred_element_type=F32,
                       precision=lax.Precision.HIGHEST) + b_ref[0]


def _mod_call(c, ada_w, ada_b):
    L, D, N = ada_w.shape
    B = c.shape[0]
    tn = 1152
    return pl.pallas_call(
        _mod_kernel,
        grid=(L, N // tn),
        in_specs=[pl.BlockSpec((B, D), lambda l, j: (0, 0)),
                  pl.BlockSpec((1, D, tn), lambda l, j: (l, 0, j)),
                  pl.BlockSpec((1, 1, tn), lambda l, j: (l, 0, j))],
        out_specs=pl.BlockSpec((1, B, tn), lambda l, j: (l, 0, j)),
        out_shape=jax.ShapeDtypeStruct((L, B, N), F32),
        compiler_params=_params("parallel", "parallel"),
        name="adaln_mod",
    )(c, ada_w, ada_b.reshape(L, 1, N))


def _ffn_kernel(x_ref, mod_ref, g_ref, win_ref, wout_ref, o_ref, a_scr, *, sub, chunk, parts):
    shift = mod_ref[0, 3 * sub:3 * sub + 1, :]
    scale = mod_ref[0, 3 * sub + 1:3 * sub + 2, :]
    gate = mod_ref[0, 3 * sub + 2:3 * sub + 3, :]
    g_in = g_ref[2 * sub:2 * sub + 1, :]
    g_out = g_ref[2 * sub + 1:2 * sub + 2, :]
    rows = x_ref.shape[1] // parts
    hbs = []
    for p in range(parts):
        x = x_ref[0, p * rows:(p + 1) * rows, :]
        hbs.append(((_rms(x) * g_in) * (1.0 + scale) + shift).astype(BF16))
    for p in range(parts):
        for j in range(D_FF // chunk):
            gj = _dot(hbs[p], win_ref[0, :, j * chunk:(j + 1) * chunk])
            uj = _dot(hbs[p], win_ref[0, :, D_FF + j * chunk:D_FF + (j + 1) * chunk])
            a_scr[p, :, j * chunk:(j + 1) * chunk] = (gj * _sigmoid(gj) * uj).astype(BF16)
    for p in range(parts):
        y = _dot(a_scr[p], wout_ref[0])
        x = x_ref[0, p * rows:(p + 1) * rows, :]
        o_ref[0, p * rows:(p + 1) * rows, :] = x + (0.5 * gate) * (_rms(y) * g_out)


def _cast_kernel(w_ref, o_ref):
    o_ref[...] = w_ref[...].astype(o_ref.dtype)


def _to_bf16(w):
    L, R, C = w.shape
    rows = R // 4
    assert R % 4 == 0 and rows % BF16_ROWS == 0
    blk = pl.BlockSpec((1, rows, C), lambda l, i: (l, i, 0))
    return pl.pallas_call(
        _cast_kernel,
        grid=(L, R // rows),
        in_specs=[blk],
        out_specs=blk,
        out_shape=jax.ShapeDtypeStruct(w.shape, BF16),
        compiler_params=_params("parallel", "parallel"),
        name="weight_cast",
    )(w)


def _layer_spec(w, l):
    return pl.BlockSpec((1,) + w.shape[1:], lambda *_: (l,) + (0,) * (w.ndim - 1), pipeline_mode=pl.Buffered(1))


def _ffn_call(x, mod_l, g_l, w_in, w_out, l, sub, tm):
    B, S, D = x.shape
    parts = 2 if tm % 1024 == 0 else 1
    kern = functools.partial(_ffn_kernel, sub=sub, chunk=256, parts=parts)
    return pl.pallas_call(
        kern,
        grid=(B, S // tm),
        in_specs=[pl.BlockSpec((1, tm, D), lambda b, i: (b, i, 0)),
                  pl.BlockSpec((1, N_MOD, D), lambda b, i: (b, 0, 0)),
                  _const_spec(g_l.shape),
                  _layer_spec(w_in, l),
                  _layer_spec(w_out, l)],
        out_specs=pl.BlockSpec((1, tm, D), lambda b, i: (b, i, 0)),
        out_shape=jax.ShapeDtypeStruct(x.shape, F32),
        scratch_shapes=[pltpu.VMEM((parts, tm // parts, D_FF), BF16)],
        compiler_params=_params("parallel", "parallel"),
        name=f"ffn{sub}",
    )(x, mod_l, g_l, w_in, w_out)


def _mixin_kernel(x_ref, mod_ref, g_ref, w1_ref, gq_ref, gkv_ref, wq_ref, wk_ref, wv_ref,
                  cr_ref, sr_ref, ck_ref, sk_ref, cs_ref,
                  q_ref, k_ref, v_ref, kn_ref, u_ref, zc_ref, p_ref, qq_ref, *, q_scale):
    x = x_ref[0]
    tm = x.shape[0]
    shift = mod_ref[0, 3:4, :]
    scale = mod_ref[0, 4:5, :]
    h = (_rms(x) * g_ref[2:3, :]) * (1.0 + scale) + shift
    hb = h.astype(BF16)
    z = _dot(hb, w1_ref[...])

    cq = z[:, C_CQ:C_CQ + Q_PAD]
    qlane = lax.broadcasted_iota(jnp.int32, (1, Q_PAD), 1)
    cqm = jnp.where(qlane < Q_LORA, cq, 0.0)
    cqn = cqm * lax.rsqrt(jnp.sum(cqm * cqm, axis=-1, keepdims=True) * (1.0 / Q_LORA) + EPS)
    cqn = (cqn * gq_ref[...]).astype(BF16)
    qq = _dot(cqn, wq_ref[...])
    nope = qq[:, :N_HEADS * QK_NOPE] * q_scale
    w_rope = N_HEADS * QK_ROPE
    roped = (qq[:, N_HEADS * QK_NOPE:N_HEADS * QK_NOPE + w_rope] * cr_ref[...]
             + qq[:, N_HEADS * QK_NOPE + w_rope:] * sr_ref[...])
    for blk in range(N_HEADS * QK_NOPE // LANES):
        nt = nope[:, blk * LANES:(blk + 1) * LANES].T.astype(BF16)
        for i in range(LANES // QK_NOPE):
            q_ref[0, blk * (LANES // QK_NOPE) + i, 0:QK_NOPE, :] = nt[i * QK_NOPE:(i + 1) * QK_NOPE]
    for blk in range(w_rope // LANES):
        rt = roped[:, blk * LANES:(blk + 1) * LANES].T.astype(BF16)
        for i in range(LANES // QK_ROPE):
            q_ref[0, blk * (LANES // QK_ROPE) + i, QK_NOPE:QK_NOPE + QK_ROPE, :] = rt[i * QK_ROPE:(i + 1) * QK_ROPE]
    zero_rows = jnp.zeros((LANES - QK_NOPE - QK_ROPE, tm), BF16)
    for hd in range(N_HEADS):
        q_ref[0, hd, QK_NOPE + QK_ROPE:, :] = zero_rows

    kr_at = Q_LORA - (Q_PAD - LANES)
    kr = pltpu.roll(cq[:, Q_PAD - LANES:], (QK_NOPE - kr_at) % LANES, 1)
    klane = lax.broadcasted_iota(jnp.int32, (1, LANES), 1)
    kr_rot = jnp.where(klane < QK_NOPE + QK_ROPE // 2,
                       -pltpu.roll(kr, LANES - QK_ROPE // 2, 1), pltpu.roll(kr, QK_ROPE // 2, 1))
    k_rope = kr * ck_ref[...] + kr_rot * sk_ref[...]

    ckv = z[:, C_CKV:C_CKV + KV_LORA]
    ckvn = (_rms(ckv) * gkv_ref[...]).astype(BF16)
    kk = _dot(ckvn, wk_ref[...])
    ones_lane = (klane == QK_DIM).astype(F32)
    for hd in range(N_HEADS):
        kh = kk[:, hd * LANES:(hd + 1) * LANES] + k_rope
        k_ref[0, hd] = (kh + ones_lane).astype(BF16)
        norm2 = jnp.max(jnp.sum(kh * kh, axis=-1, keepdims=True), axis=0, keepdims=True)
        kn_ref[0, hd] = jnp.broadcast_to(norm2, (SUBLANES, LANES))
    vv = _dot(ckvn, wv_ref[...])
    ones_rows = (lax.broadcasted_iota(jnp.int32, (V_ROWS - V_HEAD, tm), 0) == 0).astype(BF16)
    for blk in range(N_HEADS * V_HEAD // LANES):
        vt = vv[:, blk * LANES:(blk + 1) * LANES].T.astype(BF16)
        for i in range(LANES // V_HEAD):
            v_ref[0, blk * (LANES // V_HEAD) + i, 0:V_HEAD, :] = vt[i * V_HEAD:(i + 1) * V_HEAD]
    for hd in range(N_HEADS):
        v_ref[0, hd, V_HEAD:, :] = ones_rows

    a = z[:, C_ZB:C_ZB + CONV_CH]
    gt = z[:, C_ZB + CONV_CH:C_ZB + 2 * CONV_CH]
    u_ref[0] = a * _sigmoid(gt)

    zc_ref[0] = z[:, C_ZC:C_ZC + W_BR]

    for g in range(N_GROUPS):
        zd = z[:, C_ZD + g * GC:C_ZD + (g + 1) * GC].astype(BF16)
        pq = _dot(zd, cs_ref[...])
        p_ref[0, :, g * GC:(g + 1) * GC] = pq[:, :GC]
        qq_ref[0, :, g * GC:(g + 1) * GC] = pq[:, GC:]


def _mixin_call(x, mod_l, g_l, lw, tabs, tm):
    B, S, D = x.shape
    tok = lambda w: pl.BlockSpec((1, tm, w), lambda b, i: (b, i, 0))
    head = pl.BlockSpec((1, N_HEADS, tm, LANES), lambda b, i: (b, 0, i, 0))
    head_t = pl.BlockSpec((1, N_HEADS, LANES, tm), lambda b, i: (b, 0, 0, i))
    tab = lambda w: pl.BlockSpec((tm, w), lambda b, i: (i, 0))
    hshape = jax.ShapeDtypeStruct((B, N_HEADS, S, LANES), BF16)
    hshape_t = jax.ShapeDtypeStruct((B, N_HEADS, LANES, S), BF16)
    return pl.pallas_call(
        functools.partial(_mixin_kernel, q_scale=tabs["q_scale"]),
        grid=(B, S // tm),
        in_specs=[tok(D),
                  pl.BlockSpec((1, N_MOD, D), lambda b, i: (b, 0, 0)),
                  _const_spec(g_l.shape),
                  _const_spec(lw["w1"].shape),
                  _const_spec(lw["gq"].shape),
                  _const_spec(lw["gkv"].shape),
                  _const_spec(lw["wq"].shape),
                  _const_spec(lw["wk"].shape),
                  _const_spec(lw["wv"].shape),
                  tab(N_HEADS * QK_ROPE), tab(N_HEADS * QK_ROPE), tab(LANES), tab(LANES),
                  _const_spec(tabs["cs"].shape)],
        out_specs=[head_t, head, pl.BlockSpec((1, N_HEADS, V_ROWS, tm), lambda b, i: (b, 0, 0, i)),
                   pl.BlockSpec((1, N_HEADS, SUBLANES, LANES), lambda b, i: (b, 0, i, 0)),
                   tok(CONV_CH), tok(W_BR), tok(W_BR), tok(W_BR)],
        out_shape=[hshape_t, hshape, jax.ShapeDtypeStruct((B, N_HEADS, V_ROWS, S), BF16),
                   jax.ShapeDtypeStruct((B, N_HEADS, (S // tm) * SUBLANES, LANES), F32),
                   jax.ShapeDtypeStruct((B, S, CONV_CH), F32),
                   jax.ShapeDtypeStruct((B, S, W_BR), F32),
                   jax.ShapeDtypeStruct((B, S, W_BR), F32),
                   jax.ShapeDtypeStruct((B, S, W_BR), F32)],
        compiler_params=_params("parallel", "parallel"),
        name="mixer_in",
    )(x, mod_l, g_l, lw["w1"], lw["gq"], lw["gkv"], lw["wq"], lw["wk"], lw["wv"],
      tabs["cosr"], tabs["sinr"], tabs["cosk"], tabs["sink"], tabs["cs"])


def _attn_kernel(qt_ref, k_ref, vt_ref, kn_ref, o_ref, acc_scr, m_scr, *, tq, tk):
    seq = k_ref.shape[2]
    nq, nk = seq // tq, seq // tk
    k_norm2 = jnp.max(kn_ref[0, 0])

    def exact_tile(t, qoff):
        qt = qt_ref[0, 0, :, pl.ds(qoff, tq)]
        m_scr[...] = jnp.full(m_scr.shape, -jnp.inf, F32)
        acc_scr[t, 0:V_ROWS, :] = jnp.zeros((V_ROWS, tq), F32)

        def chunk(j, carry):
            off = pl.multiple_of(j * tk, tk)
            st = _dot(k_ref[0, 0, pl.ds(off, tk), :], qt)
            m_prev = m_scr[...]
            m_new = jnp.maximum(m_prev, jnp.max(st, axis=0, keepdims=True))
            pt = jnp.exp2(st - m_new).astype(BF16)
            pv = _dot(vt_ref[0, 0, :, pl.ds(off, tk)], pt)
            acc_scr[t, 0:V_ROWS, :] = jnp.exp2(m_prev - m_new) * acc_scr[t, 0:V_ROWS, :] + pv
            m_scr[...] = m_new
            return carry

        lax.fori_loop(0, nk, chunk, 0)

    def bounded_tile(t, qoff):
        qt = qt_ref[0, 0, :, pl.ds(qoff, tq)]
        q32 = qt.astype(F32)
        bound = jnp.sqrt(jnp.sum(q32 * q32, axis=0, keepdims=True) * k_norm2)
        first = lax.broadcasted_iota(jnp.int32, (BF16_ROWS, tq), 0) == 0
        stab = jnp.where(first, -bound, 0.0).astype(BF16)
        qs = jnp.concatenate([qt[0:QK_DIM], stab, qt[QK_DIM + BF16_ROWS:]], axis=0)
        acc = None
        for j in range(nk):
            st = _dot(k_ref[0, 0, j * tk:(j + 1) * tk, :], qs)
            pv = _dot(vt_ref[0, 0, :, j * tk:(j + 1) * tk], jnp.exp2(st).astype(BF16))
            acc = pv if acc is None else acc + pv
        acc_scr[t, 0:V_ROWS, :] = acc
        return jnp.min(acc[V_HEAD:V_HEAD + 1, :]) >= DENOM_MIN

    group = acc_scr.shape[0]

    def q_group(gi, carry):
        offs = [pl.multiple_of((gi * group + t) * tq, tq) for t in range(group)]
        ok = None
        for t in range(group):
            ok_t = bounded_tile(t, offs[t])
            ok = ok_t if ok is None else jnp.logical_and(ok, ok_t)

        @pl.when(jnp.logical_not(ok))
        def _():
            for t in range(group):
                exact_tile(t, offs[t])

        for t in range(group):
            denom = acc_scr[t, V_HEAD:V_HEAD + 1, :]
            o_ref[0, :, pl.ds(offs[t], tq)] = (acc_scr[t, 0:V_HEAD, :] / denom).astype(o_ref.dtype)
        return carry

    lax.fori_loop(0, nq // group, q_group, 0)


def _attn_call(qt, k, vt, kn, tq, tk):
    B, H, S, _ = k.shape
    kern = functools.partial(_attn_kernel, tq=tq, tk=tk)
    return pl.pallas_call(
        kern,
        grid=(B, H),
        in_specs=[pl.BlockSpec((1, 1, LANES, S), lambda b, h: (b, h, 0, 0)),
                  pl.BlockSpec((1, 1, S, LANES), lambda b, h: (b, h, 0, 0)),
                  pl.BlockSpec((1, 1, V_ROWS, S), lambda b, h: (b, h, 0, 0)),
                  pl.BlockSpec((1, 1) + kn.shape[2:], lambda b, h: (b, h, 0, 0))],
        out_specs=pl.BlockSpec((1, V_HEAD, S), lambda b, h: (b, h, 0)),
        out_shape=jax.ShapeDtypeStruct((B, H * V_HEAD, S), BF16),
        scratch_shapes=[pltpu.VMEM((2 if (S // tq) % 2 == 0 else 1, V_ROWS, tq), F32), pltpu.VMEM((1, tq), F32)],
        compiler_params=_params("parallel", "arbitrary"),
        name="attention",
    )(qt, k, vt, kn)


def _local_kernel(u_ref, up_ref, un_ref, z_ref, zp_ref, zn_ref, cw_ref, cb_ref, lg_ref, lb_ref,
                  pw_ref, ps_ref, oc_ref, op_ref, ubuf, zbuf, ush, sbuf, *, ts, seq, rc, rp):
    i = pl.program_id(1)
    has_prev = i > 0
    has_next = i < pl.num_programs(1) - 1
    ubuf[0:HALO, :] = jnp.where(has_prev, up_ref[0], 0.0)
    ubuf[HALO:HALO + ts, :] = u_ref[0]
    ubuf[HALO + ts:, :] = jnp.where(has_next, un_ref[0], 0.0)
    zbuf[0:HALO, :] = jnp.where(has_prev, zp_ref[0], 0.0)
    zbuf[HALO:HALO + ts, :] = z_ref[0]
    zbuf[HALO + ts:2 * HALO + ts, :] = jnp.where(has_next, zn_ref[0], 0.0)
    zbuf[2 * HALO + ts:, :] = jnp.zeros((SUBLANES, W_BR), F32)

    span = ts + 2 * HALO - SUBLANES
    for s in range(1, SUBLANES):
        ush[s - 1, 0:span, :] = ubuf[s:s + span, :]

    pad = CONV_WIDTH // 2
    for r in range(0, ts, rc):
        acc = jnp.zeros((rc, CONV_CH), F32)
        for kk in range(CONV_WIDTH):
            st = HALO + r + kk - pad
            s, base = st % SUBLANES, st - st % SUBLANES
            tap = ubuf[base:base + rc, :] if s == 0 else ush[s - 1, base:base + rc, :]
            acc = acc + tap * cw_ref[kk:kk + 1, :]
        y = acc + cb_ref[...]
        mu = jnp.mean(y, axis=-1, keepdims=True)
        yc = y - mu
        yn = yc * lax.rsqrt(jnp.mean(yc * yc, axis=-1, keepdims=True) + EPS)
        yn = yn * lg_ref[...] + lb_ref[...]
        oc_ref[0, r:r + rc, :] = (yn * _sigmoid(yn)).astype(oc_ref.dtype)

    assert all(w == 2 ** (g + 1) for g, w in enumerate(POOL_WINDOWS))
    rows_m = ts + 2 * HALO
    for m in range(1, N_GROUPS):
        step = 2 ** (m - 1)
        src = zbuf if m == 1 else sbuf.at[m - 2]
        lanes = slice(m * GC, N_GROUPS * GC)
        for c in range(0, rows_m, LANES):
            n = min(LANES, rows_m - c)
            sbuf[m - 1, c:c + n, lanes] = src[c:c + n, lanes] + src[c + step:c + step + n, lanes]
        rows_m -= SUBLANES

    for r in range(0, ts, rp):
        t = i * ts + r + lax.broadcasted_iota(jnp.int32, (rp, 1), 0)
        for g, w in enumerate(POOL_WINDOWS):
            lo = w // 2
            hi = w - 1 - lo
            cols = slice(g * GC, (g + 1) * GC)
            half = zbuf if g == 0 else sbuf.at[g - 1]
            win = half[HALO + r - lo:HALO + r - lo + rp, cols] + half[HALO + r:HALO + r + rp, cols]
            cnt = (jnp.minimum(t + hi + 1, seq) - jnp.maximum(t - lo, 0)).astype(F32)
            d = win / cnt - zbuf[HALO + r:HALO + r + rp, cols]
            yg = _dot(d.astype(BF16), pw_ref[g]) * ps_ref[:, cols]
            op_ref[0, r:r + rp, cols] = yg.astype(op_ref.dtype)


def _local_call(u, zc, lw, ts):
    B, S, _ = u.shape
    nh = ts // HALO
    last = S // HALO - 1
    cur = lambda w: pl.BlockSpec((1, ts, w), lambda b, i: (b, i, 0))
    prev = lambda w: pl.BlockSpec((1, HALO, w), lambda b, i: (b, jnp.maximum(i * nh - 1, 0), 0))
    nxt = lambda w: pl.BlockSpec((1, HALO, w), lambda b, i: (b, jnp.minimum((i + 1) * nh, last), 0))
    kern = functools.partial(_local_kernel, ts=ts, seq=S, rc=64, rp=min(ts, 256))
    return pl.pallas_call(
        kern,
        grid=(B, S // ts),
        in_specs=[cur(CONV_CH), prev(CONV_CH), nxt(CONV_CH), cur(W_BR), prev(W_BR), nxt(W_BR),
                  _const_spec(lw["conv_w"].shape), _const_spec(lw["conv_b"].shape),
                  _const_spec(lw["ln_g"].shape), _const_spec(lw["ln_b"].shape),
                  _const_spec(lw["pool_w"].shape), _const_spec(lw["pool_scale"].shape)],
        out_specs=[cur(CONV_CH), cur(W_BR)],
        out_shape=[jax.ShapeDtypeStruct((B, S, CONV_CH), BF16),
                   jax.ShapeDtypeStruct((B, S, W_BR), BF16)],
        scratch_shapes=[pltpu.VMEM((ts + 2 * HALO, CONV_CH), F32),
                        pltpu.VMEM((ts + 2 * HALO + SUBLANES, W_BR), F32),
                        pltpu.VMEM((SUBLANES - 1, ts + 2 * HALO, CONV_CH), F32),
                        pltpu.VMEM((N_GROUPS - 1, ts + 2 * HALO, W_BR), F32)],
        compiler_params=_params("parallel", "parallel"),
        name="conv_pool",
    )(u, u, u, zc, zc, zc, lw["conv_w"], lw["conv_b"], lw["ln_g"], lw["ln_b"],
      lw["pool_w"], lw["pool_scale"])


def _fft1_kernel(p_ref, q_ref, f_ref, o_ref):
    for j in range(p_ref.shape[2]):
        rhs = jnp.concatenate([p_ref[0, :, j, :], q_ref[0, :, j, :]], axis=0).astype(BF16)
        o_ref[0, :, j, :] = _dot(f_ref[...], rhs)


def _fft1_call(p, q, tabs):
    B, S, W = p.shape
    n2 = S // FFT_N1
    pv = p.reshape(B, FFT_N1, n2, W)
    qv = q.reshape(B, FFT_N1, n2, W)
    blk = pl.BlockSpec((1, FFT_N1, SUBLANES, W), lambda b, j: (b, 0, j, 0))
    return pl.pallas_call(
        _fft1_kernel,
        grid=(B, n2 // SUBLANES),
        in_specs=[blk, blk, _const_spec(tabs["f1"].shape)],
        out_specs=pl.BlockSpec((1, 2 * FFT_N1, SUBLANES, W), lambda b, j: (b, 0, j, 0)),
        out_shape=jax.ShapeDtypeStruct((B, 2 * FFT_N1, n2, W), F32),
        compiler_params=_params("parallel", "parallel"),
        name="fft_stage1",
    )(pv, qv, tabs["f1"])


def _fft2_kernel(a_ref, gc_ref, gs_ref, o_ref, *, kb):
    for j in range(kb):
        f = _dot(gc_ref[j], a_ref[0, 0, j].astype(BF16)) + _dot(gs_ref[j], a_ref[0, 1, j].astype(BF16))
        o_ref[0, :, j, :] = f


def _fft2_call(a, tabs, seq, kb):
    B = a.shape[0]
    n2 = seq // FFT_N1
    av = a.reshape(B, 2, FFT_N1, n2, W_BR)
    kern = functools.partial(_fft2_kernel, kb=kb)
    out = pl.pallas_call(
        kern,
        grid=(B, FFT_N1 // kb),
        in_specs=[pl.BlockSpec((1, 2, kb, n2, W_BR), lambda b, j: (b, 0, j, 0, 0)),
                  pl.BlockSpec((kb, n2, n2), lambda b, j: (j, 0, 0)),
                  pl.BlockSpec((kb, n2, n2), lambda b, j: (j, 0, 0))],
        out_specs=pl.BlockSpec((1, n2, kb, W_BR), lambda b, j: (b, 0, j, 0)),
        out_shape=jax.ShapeDtypeStruct((B, n2, FFT_N1, W_BR), F32),
        compiler_params=_params("parallel", "parallel"),
        name="fft_stage2",
    )(av, tabs["g2c"], tabs["g2s"])
    return out.reshape(B, seq, W_BR)


def _mixout_kernel(x_ref, mod_ref, g_ref, wg_ref, oa_ref, wa_ref, cb_ref, wb_ref, pc_ref, wc_ref,
                   f_ref, wd_ref, wo_ref, o_ref, *, parts):
    shift = mod_ref[0, 3:4, :]
    scale = mod_ref[0, 4:5, :]
    gate = mod_ref[0, 5:6, :]
    d = x_ref.shape[-1]
    rows = x_ref.shape[1] // parts
    hbs = []
    for p in range(parts):
        x = x_ref[0, p * rows:(p + 1) * rows, :]
        hbs.append(((_rms(x) * g_ref[2:3, :]) * (1.0 + scale) + shift).astype(BF16))
    ys = []
    for p in range(parts):
        sl = slice(p * rows, (p + 1) * rows)
        merged = None
        for br, (b_ref, w_ref) in enumerate(((oa_ref, wa_ref), (cb_ref, wb_ref), (pc_ref, wc_ref), (f_ref, wd_ref))):
            gl = _sigmoid(_dot(hbs[p], wg_ref[:, br * d:(br + 1) * d]))
            if br == 0:
                yb = lax.dot_general(b_ref[0, :, sl], w_ref[...], (((0,), (0,)), ((), ())),
                                     preferred_element_type=F32)
            else:
                yb = _dot(b_ref[0, sl, :].astype(BF16), w_ref[...])
            term = gl * yb
            merged = term if merged is None else merged + term
        ys.append(_dot(merged.astype(BF16), wo_ref[...]))
    for p in range(parts):
        sl = slice(p * rows, (p + 1) * rows)
        o_ref[0, sl, :] = x_ref[0, sl, :] + gate * (_rms(ys[p]) * g_ref[3:4, :])


def _mixout_call(x, mod_l, g_l, lw, oa, cb, pc, f, tm):
    B, S, D = x.shape
    tok = lambda w: pl.BlockSpec((1, tm, w), lambda b, i: (b, i, 0))
    return pl.pallas_call(
        functools.partial(_mixout_kernel, parts=2 if tm % 512 == 0 else 1),
        grid=(B, S // tm),
        in_specs=[tok(D),
                  pl.BlockSpec((1, N_MOD, D), lambda b, i: (b, 0, 0)),
                  _const_spec(g_l.shape),
                  _const_spec(lw["wg"].shape),
                  pl.BlockSpec((1, N_HEADS * V_HEAD, tm), lambda b, i: (b, 0, i)), _const_spec(lw["wa"].shape),
                  tok(CONV_CH), _const_spec(lw["wb"].shape),
                  tok(W_BR), _const_spec(lw["wc"].shape),
                  tok(W_BR), _const_spec(lw["wd"].shape),
                  _const_spec(lw["wo"].shape)],
        out_specs=tok(D),
        out_shape=jax.ShapeDtypeStruct(x.shape, F32),
        compiler_params=_params("parallel", "parallel"),
        name="mixer_out",
    )(x, mod_l, g_l, lw["wg"], oa, lw["wa"], cb, lw["wb"], pc, lw["wc"], f, lw["wd"], lw["wo"])


def _rot_half_cols(w):
    half = w.shape[-1] // 2
    return jnp.concatenate([-w[..., half:], w[..., :half]], axis=-1)


def _tables(seq):
    f32 = np.float32
    pos = np.arange(seq, dtype=f32)
    inv = (f32(ROPE_THETA) ** (-np.arange(0, QK_ROPE, 2, dtype=f32) / f32(QK_ROPE))).astype(f32)
    ang = pos[:, None] * inv[None, :]
    cos, sin = np.cos(ang).astype(f32), np.sin(ang).astype(f32)
    cc = np.concatenate([cos, cos], axis=-1)
    ss = np.concatenate([sin, sin], axis=-1)
    z64 = np.zeros((seq, QK_NOPE), f32)
    z32 = np.zeros((seq, LANES - QK_NOPE - QK_ROPE), f32)
    sm_scale = QK_DIM ** -0.5 * math.log2(math.e)
    cosk = np.concatenate([z64, cc, z32], axis=-1)
    sink = np.concatenate([z64, ss, z32], axis=-1)
    cosr = np.tile(cc, (1, N_HEADS)) * f32(sm_scale)
    sinr = np.tile(ss, (1, N_HEADS)) * f32(sm_scale)

    def cos_sin(m, period):
        th = m.astype(np.float64) * (2.0 * math.pi / period)
        return np.cos(th), np.sin(th)

    def dft(n):
        a = np.arange(n, dtype=np.int64)
        return cos_sin((a[:, None] * a[None, :]) % n, n)

    bf = lambda a: a.astype(f32).astype(BF16)
    c_ch, s_ch = dft(GC)
    cs = bf(np.concatenate([c_ch, s_ch], axis=-1) * GC ** -0.5)

    n1 = FFT_N1
    n2 = seq // n1
    c1, s1 = dft(n1)
    f1 = bf(np.concatenate([np.concatenate([c1, -s1], axis=1), np.concatenate([-s1, -c1], axis=1)], axis=0)
            * n1 ** -0.5)
    k1 = np.arange(n1, dtype=np.int64)[:, None, None]
    k2 = np.arange(n2, dtype=np.int64)[None, :, None]
    t2 = np.arange(n2, dtype=np.int64)[None, None, :]
    c2, s2 = cos_sin(((k1 + n1 * k2) * t2) % seq, seq)
    g2c = bf(c2 * n2 ** -0.5)
    g2s = bf(s2 * n2 ** -0.5)
    return dict(cosr=cosr, sinr=sinr, q_scale=sm_scale, cosk=cosk, sink=sink, cs=cs, f1=f1, g2c=g2c, g2s=g2s)


def _layer_weights(l, w_in, q_norm_g, w_uq, kv_norm_g, w_ukv, w_a, conv_w, conv_b, conv_ln_g,
                   conv_ln_b, w_b, pool_w, pool_scale, w_c, w_d, w_out):
    D = w_in.shape[1]
    wi = w_in[l]
    assert Q_PAD - Q_LORA == QK_ROPE
    w1 = jnp.concatenate([wi[:, Q_LORA:Q_LORA + KV_LORA], wi[:, :Q_LORA], wi[:, Q_LORA + KV_LORA:4 * W_BR]],
                         axis=-1).astype(BF16)

    uq = w_uq[l].reshape(Q_LORA, N_HEADS, QK_NOPE + QK_ROPE)
    nope, rope = uq[..., :QK_NOPE], uq[..., QK_NOPE:]
    wq = jnp.concatenate([nope.reshape(Q_LORA, N_HEADS * QK_NOPE), rope.reshape(Q_LORA, N_HEADS * QK_ROPE),
                          _rot_half_cols(rope).reshape(Q_LORA, N_HEADS * QK_ROPE)], axis=-1)
    wq = jnp.pad(wq, ((0, Q_PAD - Q_LORA), (0, 0))).astype(BF16)

    ukv = w_ukv[l].reshape(KV_LORA, N_HEADS, QK_NOPE + V_HEAD)
    zk = jnp.zeros((KV_LORA, N_HEADS, LANES - QK_NOPE), F32)
    wk = jnp.concatenate([ukv[..., :QK_NOPE], zk], axis=-1).reshape(KV_LORA, N_HEADS * LANES).astype(BF16)
    wv = ukv[..., QK_NOPE:].reshape(KV_LORA, N_HEADS * V_HEAD).astype(BF16)

    gq = jnp.pad(q_norm_g[l], (0, Q_PAD - Q_LORA)).reshape(1, Q_PAD)
    return dict(
        w1=w1, wq=wq, wk=wk, wv=wv, gq=gq, gkv=kv_norm_g[l].reshape(1, KV_LORA),
        wg=wi[:, 4 * W_BR:].astype(BF16), wa=w_a[l].astype(BF16),
        wb=w_b[l].astype(BF16), wc=w_c[l].astype(BF16), wd=w_d[l].astype(BF16), wo=w_out[l].astype(BF16),
        conv_w=conv_w[l], conv_b=conv_b[l].reshape(1, CONV_CH),
        ln_g=conv_ln_g[l].reshape(1, CONV_CH), ln_b=conv_ln_b[l].reshape(1, CONV_CH),
        pool_w=pool_w[l].astype(BF16), pool_scale=pool_scale[l].reshape(1, W_BR))


def _tiles(seq):
    t = lambda n: min(n, seq)
    return dict(ffn=t(1024), mix=t(1024), mixout=t(512), tq=t(1024), tk=t(2048), local=t(512))


def kernel(x, c, ada_w, ada_b, norm_g, ffn1_w_in, ffn1_w_out, ffn2_w_in, ffn2_w_out, w_in, q_norm_g,
           w_uq, kv_norm_g, w_ukv, w_a, conv_w, conv_b, conv_ln_g, conv_ln_b, w_b, pool_w, pool_scale,
           w_c, w_d, w_out):
    B, S, D = x.shape
    L = ada_w.shape[0]
    assert D == D_MODEL and S % (FFT_N1 * 16) == 0
    ts = _tiles(S)
    tabs = _tables(S)
    mod = _mod_call(c, ada_w, ada_b).reshape(L, B, N_MOD, D)
    ffn_w = [(_to_bf16(wi), _to_bf16(wo)) for wi, wo in ((ffn1_w_in, ffn1_w_out), (ffn2_w_in, ffn2_w_out))]
    for l in range(L):
        lw = _layer_weights(l, w_in, q_norm_g, w_uq, kv_norm_g, w_ukv, w_a, conv_w, conv_b, conv_ln_g,
                            conv_ln_b, w_b, pool_w, pool_scale, w_c, w_d, w_out)
        mod_l, g_l = mod[l], norm_g[l]
        x = _ffn_call(x, mod_l, g_l, *ffn_w[0], l, 0, ts["ffn"])
        q, k, v, kn, u, zc, p, qq = _mixin_call(x, mod_l, g_l, lw, tabs, ts["mix"])
        oa = _attn_call(q, k, v, kn, ts["tq"], ts["tk"])
        cb, pc = _local_call(u, zc, lw, ts["local"])
        a = _fft1_call(p, qq, tabs)
        f = _fft2_call(a, tabs, S, 8)
        x = _mixout_call(x, mod_l, g_l, lw, oa, cb, pc, f, ts["mixout"])
        x = _ffn_call(x, mod_l, g_l, *ffn_w[1], l, 2, ts["ffn"])
    return x
```

```python
import functools
import math

import jax
import jax.numpy as jnp
import numpy as np
from jax import lax
from jax.experimental import pallas as pl
from jax.experimental.pallas import tpu as pltpu

F32 = jnp.float32
BF16 = jnp.bfloat16

D_MODEL = 1024
D_FF = 2816
N_HEADS = 8
Q_LORA = 352
KV_LORA = 128
QK_NOPE = 64
QK_ROPE = 32
V_HEAD = 64
QK_DIM = QK_NOPE + QK_ROPE
BF16_ROWS = 16
V_ROWS = 80
DENOM_MIN = 2.0 ** -80
ROPE_THETA = 10000.0
CONV_CH = 256
CONV_WIDTH = 31
POOL_WINDOWS = (2, 4, 8, 16)
GC = 128
N_GROUPS = 4
W_BR = 512
N_MOD = 9
EPS = 1e-6

LANES = 128
SUBLANES = 8
Q_PAD = 384
HALO = 16
FFT_N1 = 128
VMEM_LIMIT = 56 * 1024 * 1024

C_CKV = 0
C_CQ = C_CKV + KV_LORA
C_ZB = C_CQ + Q_PAD
C_ZC = C_ZB + W_BR
C_ZD = C_ZC + W_BR
W1_COLS = C_ZD + W_BR


def _params(*sem):
    return pltpu.CompilerParams(dimension_semantics=sem, vmem_limit_bytes=VMEM_LIMIT)


def _const_spec(shape):
    zeros = (0,) * len(shape)
    return pl.BlockSpec(shape, lambda *_: zeros, pipeline_mode=pl.Buffered(1))


def _dot(a, b):
    return jnp.dot(a, b, preferred_element_type=F32)


def _rms(x):
    return x * lax.rsqrt(jnp.mean(x * x, axis=-1, keepdims=True) + EPS)


def _sigmoid(x):
    return 1.0 / (1.0 + jnp.exp(-x))


def _mod_kernel(c_ref, w_ref, b_ref, o_ref):
    c = c_ref[...]
    ca = c * _sigmoid(c)
    o_ref[0] = jnp.dot(ca, w_ref[0], preferred_element_type=F32,
                       precision=lax.Precision.HIGHEST) + b_ref[0]


def _mod_call(c, ada_w, ada_b):
    L, D, N = ada_w.shape
    B = c.shape[0]
    tn = 1152
    return pl.pallas_call(
        _mod_kernel,
        grid=(L, N // tn),
        in_specs=[pl.BlockSpec((B, D), lambda l, j: (0, 0)),
                  pl.BlockSpec((1, D, tn), lambda l, j: (l, 0, j)),
                  pl.BlockSpec((1, 1, tn), lambda l, j: (l, 0, j))],
        out_specs=pl.BlockSpec((1, B, tn), lambda l, j: (l, 0, j)),
        out_shape=jax.ShapeDtypeStruct((L, B, N), F32),
        compiler_params=_params("parallel", "parallel"),
        name="adaln_mod",
    )(c, ada_w, ada_b.reshape(L, 1, N))


def _ffn_kernel(x_ref, mod_ref, g_ref, win_ref, wout_ref, o_ref, a_scr, *, sub, chunk, parts):
    shift = mod_ref[0, 3 * sub:3 * sub + 1, :]
    scale = mod_ref[0, 3 * sub + 1:3 * sub + 2, :]
    gate = mod_ref[0, 3 * sub + 2:3 * sub + 3, :]
    g_in = g_ref[2 * sub:2 * sub + 1, :]
    g_out = g_ref[2 * sub + 1:2 * sub + 2, :]
    rows = x_ref.shape[1] // parts
    hbs = []
    for p in range(parts):
        x = x_ref[0, p * rows:(p + 1) * rows, :]
        hbs.append(((_rms(x) * g_in) * (1.0 + scale) + shift).astype(BF16))
    for p in range(parts):
        for j in range(D_FF // chunk):
            gj = _dot(hbs[p], win_ref[0, :, j * chunk:(j + 1) * chunk])
            uj = _dot(hbs[p], win_ref[0, :, D_FF + j * chunk:D_FF + (j + 1) * chunk])
            a_scr[p, :, j * chunk:(j + 1) * chunk] = (gj * _sigmoid(gj) * uj).astype(BF16)
    for p in range(parts):
        y = _dot(a_scr[p], wout_ref[0])
        x = x_ref[0, p * rows:(p + 1) * rows, :]
        o_ref[0, p * rows:(p + 1) * rows, :] = x + (0.5 * gate) * (_rms(y) * g_out)


def _cast_kernel(w_ref, o_ref):
    o_ref[...] = w_ref[...].astype(o_ref.dtype)


def _to_bf16(w):
    L, R, C = w.shape
    rows = R // 4
    assert R % 4 == 0 and rows % BF16_ROWS == 0
    blk = pl.BlockSpec((1, rows, C), lambda l, i: (l, i, 0))
    return pl.pallas_call(
        _cast_kernel,
        grid=(L, R // rows),
        in_specs=[blk],
        out_specs=blk,
        out_shape=jax.ShapeDtypeStruct(w.shape, BF16),
        compiler_params=_params("parallel", "parallel"),
        name="weight_cast",
    )(w)


def _layer_spec(w, l):
    return pl.BlockSpec((1,) + w.shape[1:], lambda *_: (l,) + (0,) * (w.ndim - 1), pipeline_mode=pl.Buffered(1))


def _ffn_call(x, mod_l, g_l, w_in, w_out, l, sub, tm):
    B, S, D = x.shape
    parts = 4 if tm % 1024 == 0 else 1
    kern = functools.partial(_ffn_kernel, sub=sub, chunk=256, parts=parts)
    return pl.pallas_call(
        kern,
        grid=(B, S // tm),
        in_specs=[pl.BlockSpec((1, tm, D), lambda b, i: (b, i, 0)),
                  pl.BlockSpec((1, N_MOD, D), lambda b, i: (b, 0, 0)),
                  _const_spec(g_l.shape),
                  _layer_spec(w_in, l),
                  _layer_spec(w_out, l)],
        out_specs=pl.BlockSpec((1, tm, D), lambda b, i: (b, i, 0)),
        out_shape=jax.ShapeDtypeStruct(x.shape, F32),
        scratch_shapes=[pltpu.VMEM((parts, tm // parts, D_FF), BF16)],
        compiler_params=_params("parallel", "parallel"),
        name=f"ffn{sub}",
    )(x, mod_l, g_l, w_in, w_out)


def _mixin_kernel(x_ref, mod_ref, g_ref, w1_ref, gq_ref, gkv_ref, wq_ref, wk_ref, wv_ref,
                  cr_ref, sr_ref, ck_ref, sk_ref, cs_ref,
                  q_ref, k_ref, v_ref, kn_ref, u_ref, zc_ref, p_ref, qq_ref, *, q_scale):
    x = x_ref[0]
    tm = x.shape[0]
    shift = mod_ref[0, 3:4, :]
    scale = mod_ref[0, 4:5, :]
    h = (_rms(x) * g_ref[2:3, :]) * (1.0 + scale) + shift
    hb = h.astype(BF16)
    z = _dot(hb, w1_ref[...])

    cq = z[:, C_CQ:C_CQ + Q_PAD]
    qlane = lax.broadcasted_iota(jnp.int32, (1, Q_PAD), 1)
    cqm = jnp.where(qlane < Q_LORA, cq, 0.0)
    cqn = cqm * lax.rsqrt(jnp.sum(cqm * cqm, axis=-1, keepdims=True) * (1.0 / Q_LORA) + EPS)
    cqn = (cqn * gq_ref[...]).astype(BF16)
    qq = _dot(cqn, wq_ref[...])
    nope = qq[:, :N_HEADS * QK_NOPE] * q_scale
    w_rope = N_HEADS * QK_ROPE
    roped = (qq[:, N_HEADS * QK_NOPE:N_HEADS * QK_NOPE + w_rope] * cr_ref[...]
             + qq[:, N_HEADS * QK_NOPE + w_rope:] * sr_ref[...])
    for blk in range(N_HEADS * QK_NOPE // LANES):
        nt = nope[:, blk * LANES:(blk + 1) * LANES].T.astype(BF16)
        for i in range(LANES // QK_NOPE):
            q_ref[0, blk * (LANES // QK_NOPE) + i, 0:QK_NOPE, :] = nt[i * QK_NOPE:(i + 1) * QK_NOPE]
    for blk in range(w_rope // LANES):
        rt = roped[:, blk * LANES:(blk + 1) * LANES].T.astype(BF16)
        for i in range(LANES // QK_ROPE):
            q_ref[0, blk * (LANES // QK_ROPE) + i, QK_NOPE:QK_NOPE + QK_ROPE, :] = rt[i * QK_ROPE:(i + 1) * QK_ROPE]
    zero_rows = jnp.zeros((LANES - QK_NOPE - QK_ROPE, tm), BF16)
    for hd in range(N_HEADS):
        q_ref[0, hd, QK_NOPE + QK_ROPE:, :] = zero_rows

    kr_at = Q_LORA - (Q_PAD - LANES)
    kr = pltpu.roll(cq[:, Q_PAD - LANES:], (QK_NOPE - kr_at) % LANES, 1)
    klane = lax.broadcasted_iota(jnp.int32, (1, LANES), 1)
    kr_rot = jnp.where(klane < QK_NOPE + QK_ROPE // 2,
                       -pltpu.roll(kr, LANES - QK_ROPE // 2, 1), pltpu.roll(kr, QK_ROPE // 2, 1))
    k_rope = kr * ck_ref[...] + kr_rot * sk_ref[...]

    ckv = z[:, C_CKV:C_CKV + KV_LORA]
    ckvn = (_rms(ckv) * gkv_ref[...]).astype(BF16)
    kk = _dot(ckvn, wk_ref[...])
    ones_lane = (klane == QK_DIM).astype(F32)
    for hd in range(N_HEADS):
        kh = kk[:, hd * LANES:(hd + 1) * LANES] + k_rope
        k_ref[0, hd] = (kh + ones_lane).astype(BF16)
        norm2 = jnp.max(jnp.sum(kh * kh, axis=-1, keepdims=True), axis=0, keepdims=True)
        kn_ref[0, hd] = jnp.broadcast_to(norm2, (SUBLANES, LANES))
    vv = _dot(ckvn, wv_ref[...])
    ones_rows = (lax.broadcasted_iota(jnp.int32, (V_ROWS - V_HEAD, tm), 0) == 0).astype(BF16)
    for blk in range(N_HEADS * V_HEAD // LANES):
        vt = vv[:, blk * LANES:(blk + 1) * LANES].T.astype(BF16)
        for i in range(LANES // V_HEAD):
            v_ref[0, blk * (LANES // V_HEAD) + i, 0:V_HEAD, :] = vt[i * V_HEAD:(i + 1) * V_HEAD]
    for hd in range(N_HEADS):
        v_ref[0, hd, V_HEAD:, :] = ones_rows

    a = z[:, C_ZB:C_ZB + CONV_CH]
    gt = z[:, C_ZB + CONV_CH:C_ZB + 2 * CONV_CH]
    u_ref[0] = a * _sigmoid(gt)

    zc_ref[0] = z[:, C_ZC:C_ZC + W_BR]

    for g in range(N_GROUPS):
        zd = z[:, C_ZD + g * GC:C_ZD + (g + 1) * GC].astype(BF16)
        pq = _dot(zd, cs_ref[...])
        p_ref[0, :, g * GC:(g + 1) * GC] = pq[:, :GC]
        qq_ref[0, :, g * GC:(g + 1) * GC] = pq[:, GC:]


def _mixin_call(x, mod_l, g_l, lw, tabs, tm):
    B, S, D = x.shape
    tok = lambda w: pl.BlockSpec((1, tm, w), lambda b, i: (b, i, 0))
    head = pl.BlockSpec((1, N_HEADS, tm, LANES), lambda b, i: (b, 0, i, 0))
    head_t = pl.BlockSpec((1, N_HEADS, LANES, tm), lambda b, i: (b, 0, 0, i))
    tab = lambda w: pl.BlockSpec((tm, w), lambda b, i: (i, 0))
    hshape = jax.ShapeDtypeStruct((B, N_HEADS, S, LANES), BF16)
    hshape_t = jax.ShapeDtypeStruct((B, N_HEADS, LANES, S), BF16)
    return pl.pallas_call(
        functools.partial(_mixin_kernel, q_scale=tabs["q_scale"]),
        grid=(B, S // tm),
        in_specs=[tok(D),
                  pl.BlockSpec((1, N_MOD, D), lambda b, i: (b, 0, 0)),
                  _const_spec(g_l.shape),
                  _const_spec(lw["w1"].shape),
                  _const_spec(lw["gq"].shape),
                  _const_spec(lw["gkv"].shape),
                  _const_spec(lw["wq"].shape),
                  _const_spec(lw["wk"].shape),
                  _const_spec(lw["wv"].shape),
                  tab(N_HEADS * QK_ROPE), tab(N_HEADS * QK_ROPE), tab(LANES), tab(LANES),
                  _const_spec(tabs["cs"].shape)],
        out_specs=[head_t, head, pl.BlockSpec((1, N_HEADS, V_ROWS, tm), lambda b, i: (b, 0, 0, i)),
                   pl.BlockSpec((1, N_HEADS, SUBLANES, LANES), lambda b, i: (b, 0, i, 0)),
                   tok(CONV_CH), tok(W_BR), tok(W_BR), tok(W_BR)],
        out_shape=[hshape_t, hshape, jax.ShapeDtypeStruct((B, N_HEADS, V_ROWS, S), BF16),
                   jax.ShapeDtypeStruct((B, N_HEADS, (S // tm) * SUBLANES, LANES), F32),
                   jax.ShapeDtypeStruct((B, S, CONV_CH), F32),
                   jax.ShapeDtypeStruct((B, S, W_BR), F32),
                   jax.ShapeDtypeStruct((B, S, W_BR), F32),
                   jax.ShapeDtypeStruct((B, S, W_BR), F32)],
        compiler_params=_params("parallel", "parallel"),
        name="mixer_in",
    )(x, mod_l, g_l, lw["w1"], lw["gq"], lw["gkv"], lw["wq"], lw["wk"], lw["wv"],
      tabs["cosr"], tabs["sinr"], tabs["cosk"], tabs["sink"], tabs["cs"])


def _attn_kernel(qt_ref, k_ref, vt_ref, kn_ref, o_ref, acc_scr, m_scr, *, tq, tk):
    seq = k_ref.shape[2]
    nq, nk = seq // tq, seq // tk
    k_norm2 = jnp.max(kn_ref[0, 0])

    def exact_tile(t, qoff):
        qt = qt_ref[0, 0, :, pl.ds(qoff, tq)]
        m_scr[...] = jnp.full(m_scr.shape, -jnp.inf, F32)
        acc_scr[t, 0:V_ROWS, :] = jnp.zeros((V_ROWS, tq), F32)

        def chunk(j, carry):
            off = pl.multiple_of(j * tk, tk)
            st = _dot(k_ref[0, 0, pl.ds(off, tk), :], qt)
            m_prev = m_scr[...]
            m_new = jnp.maximum(m_prev, jnp.max(st, axis=0, keepdims=True))
            pt = jnp.exp2(st - m_new).astype(BF16)
            pv = _dot(vt_ref[0, 0, :, pl.ds(off, tk)], pt)
            acc_scr[t, 0:V_ROWS, :] = jnp.exp2(m_prev - m_new) * acc_scr[t, 0:V_ROWS, :] + pv
            m_scr[...] = m_new
            return carry

        lax.fori_loop(0, nk, chunk, 0)

    def bounded_tile(t, qoff):
        qt = qt_ref[0, 0, :, pl.ds(qoff, tq)]
        q32 = qt.astype(F32)
        bound = jnp.sqrt(jnp.sum(q32 * q32, axis=0, keepdims=True) * k_norm2)
        first = lax.broadcasted_iota(jnp.int32, (BF16_ROWS, tq), 0) == 0
        stab = jnp.where(first, -bound, 0.0).astype(BF16)
        qs = jnp.concatenate([qt[0:QK_DIM], stab, qt[QK_DIM + BF16_ROWS:]], axis=0)
        acc = None
        for j in range(nk):
            st = _dot(k_ref[0, 0, j * tk:(j + 1) * tk, :], qs)
            pv = _dot(vt_ref[0, 0, :, j * tk:(j + 1) * tk], jnp.exp2(st).astype(BF16))
            acc = pv if acc is None else acc + pv
        acc_scr[t, 0:V_ROWS, :] = acc
        return jnp.min(acc[V_HEAD:V_HEAD + 1, :]) >= DENOM_MIN

    group = acc_scr.shape[0]

    def q_group(gi, carry):
        offs = [pl.multiple_of((gi * group + t) * tq, tq) for t in range(group)]
        ok = None
        for t in range(group):
            ok_t = bounded_tile(t, offs[t])
            ok = ok_t if ok is None else jnp.logical_and(ok, ok_t)

        @pl.when(jnp.logical_not(ok))
        def _():
            for t in range(group):
                exact_tile(t, offs[t])

        for t in range(group):
            denom = acc_scr[t, V_HEAD:V_HEAD + 1, :]
            o_ref[0, :, pl.ds(offs[t], tq)] = (acc_scr[t, 0:V_HEAD, :] / denom).astype(o_ref.dtype)
        return carry

    lax.fori_loop(0, nq // group, q_group, 0)


def _attn_call(qt, k, vt, kn, tq, tk):
    B, H, S, _ = k.shape
    kern = functools.partial(_attn_kernel, tq=tq, tk=tk)
    return pl.pallas_call(
        kern,
        grid=(B, H),
        in_specs=[pl.BlockSpec((1, 1, LANES, S), lambda b, h: (b, h, 0, 0)),
                  pl.BlockSpec((1, 1, S, LANES), lambda b, h: (b, h, 0, 0)),
                  pl.BlockSpec((1, 1, V_ROWS, S), lambda b, h: (b, h, 0, 0)),
                  pl.BlockSpec((1, 1) + kn.shape[2:], lambda b, h: (b, h, 0, 0))],
        out_specs=pl.BlockSpec((1, V_HEAD, S), lambda b, h: (b, h, 0)),
        out_shape=jax.ShapeDtypeStruct((B, H * V_HEAD, S), BF16),
        scratch_shapes=[pltpu.VMEM((2 if (S // tq) % 2 == 0 else 1, V_ROWS, tq), F32), pltpu.VMEM((1, tq), F32)],
        compiler_params=_params("parallel", "arbitrary"),
        name="attention",
    )(qt, k, vt, kn)


def _local_kernel(u_ref, up_ref, un_ref, z_ref, zp_ref, zn_ref, cw_ref, cb_ref, lg_ref, lb_ref,
                  pw_ref, ps_ref, oc_ref, op_ref, ubuf, zbuf, ush, sbuf, *, ts, seq, rc, rp):
    i = pl.program_id(1)
    has_prev = i > 0
    has_next = i < pl.num_programs(1) - 1
    ubuf[0:HALO, :] = jnp.where(has_prev, up_ref[0], 0.0)
    ubuf[HALO:HALO + ts, :] = u_ref[0]
    ubuf[HALO + ts:, :] = jnp.where(has_next, un_ref[0], 0.0)
    zbuf[0:HALO, :] = jnp.where(has_prev, zp_ref[0], 0.0)
    zbuf[HALO:HALO + ts, :] = z_ref[0]
    zbuf[HALO + ts:2 * HALO + ts, :] = jnp.where(has_next, zn_ref[0], 0.0)
    zbuf[2 * HALO + ts:, :] = jnp.zeros((SUBLANES, W_BR), F32)

    span = ts + 2 * HALO - SUBLANES
    for s in range(1, SUBLANES):
        ush[s - 1, 0:span, :] = ubuf[s:s + span, :]

    pad = CONV_WIDTH // 2
    for r in range(0, ts, rc):
        acc = jnp.zeros((rc, CONV_CH), F32)
        for kk in range(CONV_WIDTH):
            st = HALO + r + kk - pad
            s, base = st % SUBLANES, st - st % SUBLANES
            tap = ubuf[base:base + rc, :] if s == 0 else ush[s - 1, base:base + rc, :]
            acc = acc + tap * cw_ref[kk:kk + 1, :]
        y = acc + cb_ref[...]
        mu = jnp.mean(y, axis=-1, keepdims=True)
        yc = y - mu
        yn = yc * lax.rsqrt(jnp.mean(yc * yc, axis=-1, keepdims=True) + EPS)
        yn = yn * lg_ref[...] + lb_ref[...]
        oc_ref[0, r:r + rc, :] = (yn * _sigmoid(yn)).astype(oc_ref.dtype)

    assert all(w == 2 ** (g + 1) for g, w in enumerate(POOL_WINDOWS))
    rows_m = ts + 2 * HALO
    for m in range(1, N_GROUPS):
        step = 2 ** (m - 1)
        src = zbuf if m == 1 else sbuf.at[m - 2]
        lanes = slice(m * GC, N_GROUPS * GC)
        for c in range(0, rows_m, LANES):
            n = min(LANES, rows_m - c)
            sbuf[m - 1, c:c + n, lanes] = src[c:c + n, lanes] + src[c + step:c + step + n, lanes]
        rows_m -= SUBLANES

    for r in range(0, ts, rp):
        t = i * ts + r + lax.broadcasted_iota(jnp.int32, (rp, 1), 0)
        for g, w in enumerate(POOL_WINDOWS):
            lo = w // 2
            hi = w - 1 - lo
            cols = slice(g * GC, (g + 1) * GC)
            half = zbuf if g == 0 else sbuf.at[g - 1]
            win = half[HALO + r - lo:HALO + r - lo + rp, cols] + half[HALO + r:HALO + r + rp, cols]
            cnt = (jnp.minimum(t + hi + 1, seq) - jnp.maximum(t - lo, 0)).astype(F32)
            d = win / cnt - zbuf[HALO + r:HALO + r + rp, cols]
            yg = _dot(d.astype(BF16), pw_ref[g]) * ps_ref[:, cols]
            op_ref[0, r:r + rp, cols] = yg.astype(op_ref.dtype)


def _local_call(u, zc, lw, ts):
    B, S, _ = u.shape
    nh = ts // HALO
    last = S // HALO - 1
    cur = lambda w: pl.BlockSpec((1, ts, w), lambda b, i: (b, i, 0))
    prev = lambda w: pl.BlockSpec((1, HALO, w), lambda b, i: (b, jnp.maximum(i * nh - 1, 0), 0))
    nxt = lambda w: pl.BlockSpec((1, HALO, w), lambda b, i: (b, jnp.minimum((i + 1) * nh, last), 0))
    kern = functools.partial(_local_kernel, ts=ts, seq=S, rc=64, rp=min(ts, 256))
    return pl.pallas_call(
        kern,
        grid=(B, S // ts),
        in_specs=[cur(CONV_CH), prev(CONV_CH), nxt(CONV_CH), cur(W_BR), prev(W_BR), nxt(W_BR),
                  _const_spec(lw["conv_w"].shape), _const_spec(lw["conv_b"].shape),
                  _const_spec(lw["ln_g"].shape), _const_spec(lw["ln_b"].shape),
                  _const_spec(lw["pool_w"].shape), _const_spec(lw["pool_scale"].shape)],
        out_specs=[cur(CONV_CH), cur(W_BR)],
        out_shape=[jax.ShapeDtypeStruct((B, S, CONV_CH), BF16),
                   jax.ShapeDtypeStruct((B, S, W_BR), BF16)],
        scratch_shapes=[pltpu.VMEM((ts + 2 * HALO, CONV_CH), F32),
                        pltpu.VMEM((ts + 2 * HALO + SUBLANES, W_BR), F32),
                        pltpu.VMEM((SUBLANES - 1, ts + 2 * HALO, CONV_CH), F32),
                        pltpu.VMEM((N_GROUPS - 1, ts + 2 * HALO, W_BR), F32)],
        compiler_params=_params("parallel", "parallel"),
        name="conv_pool",
    )(u, u, u, zc, zc, zc, lw["conv_w"], lw["conv_b"], lw["ln_g"], lw["ln_b"],
      lw["pool_w"], lw["pool_scale"])


def _fft1_kernel(p_ref, q_ref, f_ref, o_ref):
    for j in range(p_ref.shape[2]):
        rhs = jnp.concatenate([p_ref[0, :, j, :], q_ref[0, :, j, :]], axis=0).astype(BF16)
        o_ref[0, :, j, :] = _dot(f_ref[...], rhs)


def _fft1_call(p, q, tabs):
    B, S, W = p.shape
    n2 = S // FFT_N1
    pv = p.reshape(B, FFT_N1, n2, W)
    qv = q.reshape(B, FFT_N1, n2, W)
    blk = pl.BlockSpec((1, FFT_N1, SUBLANES, W), lambda b, j: (b, 0, j, 0))
    return pl.pallas_call(
        _fft1_kernel,
        grid=(B, n2 // SUBLANES),
        in_specs=[blk, blk, _const_spec(tabs["f1"].shape)],
        out_specs=pl.BlockSpec((1, 2 * FFT_N1, SUBLANES, W), lambda b, j: (b, 0, j, 0)),
        out_shape=jax.ShapeDtypeStruct((B, 2 * FFT_N1, n2, W), F32),
        compiler_params=_params("parallel", "parallel"),
        name="fft_stage1",
    )(pv, qv, tabs["f1"])


def _fft2_kernel(a_ref, gc_ref, gs_ref, o_ref, *, kb):
    for j in range(kb):
        f = _dot(gc_ref[j], a_ref[0, 0, j].astype(BF16)) + _dot(gs_ref[j], a_ref[0, 1, j].astype(BF16))
        o_ref[0, :, j, :] = f


def _fft2_call(a, tabs, seq, kb):
    B = a.shape[0]
    n2 = seq // FFT_N1
    av = a.reshape(B, 2, FFT_N1, n2, W_BR)
    kern = functools.partial(_fft2_kernel, kb=kb)
    out = pl.pallas_call(
        kern,
        grid=(B, FFT_N1 // kb),
        in_specs=[pl.BlockSpec((1, 2, kb, n2, W_BR), lambda b, j: (b, 0, j, 0, 0)),
                  pl.BlockSpec((kb, n2, n2), lambda b, j: (j, 0, 0)),
                  pl.BlockSpec((kb, n2, n2), lambda b, j: (j, 0, 0))],
        out_specs=pl.BlockSpec((1, n2, kb, W_BR), lambda b, j: (b, 0, j, 0)),
        out_shape=jax.ShapeDtypeStruct((B, n2, FFT_N1, W_BR), F32),
        compiler_params=_params("parallel", "parallel"),
        name="fft_stage2",
    )(av, tabs["g2c"], tabs["g2s"])
    return out.reshape(B, seq, W_BR)


def _mixout_kernel(x_ref, mod_ref, g_ref, wg_ref, oa_ref, wa_ref, cb_ref, wb_ref, pc_ref, wc_ref,
                   f_ref, wd_ref, wo_ref, o_ref, *, parts):
    shift = mod_ref[0, 3:4, :]
    scale = mod_ref[0, 4:5, :]
    gate = mod_ref[0, 5:6, :]
    d = x_ref.shape[-1]
    rows = x_ref.shape[1] // parts
    hbs = []
    for p in range(parts):
        x = x_ref[0, p * rows:(p + 1) * rows, :]
        hbs.append(((_rms(x) * g_ref[2:3, :]) * (1.0 + scale) + shift).astype(BF16))
    ys = []
    for p in range(parts):
        sl = slice(p * rows, (p + 1) * rows)
        merged = None
        for br, (b_ref, w_ref) in enumerate(((oa_ref, wa_ref), (cb_ref, wb_ref), (pc_ref, wc_ref), (f_ref, wd_ref))):
            gl = _sigmoid(_dot(hbs[p], wg_ref[:, br * d:(br + 1) * d]))
            if br == 0:
                yb = lax.dot_general(b_ref[0, :, sl], w_ref[...], (((0,), (0,)), ((), ())),
                                     preferred_element_type=F32)
            else:
                yb = _dot(b_ref[0, sl, :].astype(BF16), w_ref[...])
            term = gl * yb
            merged = term if merged is None else merged + term
        ys.append(_dot(merged.astype(BF16), wo_ref[...]))
    for p in range(parts):
        sl = slice(p * rows, (p + 1) * rows)
        o_ref[0, sl, :] = x_ref[0, sl, :] + gate * (_rms(ys[p]) * g_ref[3:4, :])


def _mixout_call(x, mod_l, g_l, lw, oa, cb, pc, f, tm):
    B, S, D = x.shape
    tok = lambda w: pl.BlockSpec((1, tm, w), lambda b, i: (b, i, 0))
    return pl.pallas_call(
        functools.partial(_mixout_kernel, parts=2 if tm % 512 == 0 else 1),
        grid=(B, S // tm),
        in_specs=[tok(D),
                  pl.BlockSpec((1, N_MOD, D), lambda b, i: (b, 0, 0)),
                  _const_spec(g_l.shape),
                  _const_spec(lw["wg"].shape),
                  pl.BlockSpec((1, N_HEADS * V_HEAD, tm), lambda b, i: (b, 0, i)), _const_spec(lw["wa"].shape),
                  tok(CONV_CH), _const_spec(lw["wb"].shape),
                  tok(W_BR), _const_spec(lw["wc"].shape),
                  tok(W_BR), _const_spec(lw["wd"].shape),
                  _const_spec(lw["wo"].shape)],
        out_specs=tok(D),
        out_shape=jax.ShapeDtypeStruct(x.shape, F32),
        compiler_params=_params("parallel", "parallel"),
        name="mixer_out",
    )(x, mod_l, g_l, lw["wg"], oa, lw["wa"], cb, lw["wb"], pc, lw["wc"], f, lw["wd"], lw["wo"])


def _rot_half_cols(w):
    half = w.shape[-1] // 2
    return jnp.concatenate([-w[..., half:], w[..., :half]], axis=-1)


def _tables(seq):
    f32 = np.float32
    pos = np.arange(seq, dtype=f32)
    inv = (f32(ROPE_THETA) ** (-np.arange(0, QK_ROPE, 2, dtype=f32) / f32(QK_ROPE))).astype(f32)
    ang = pos[:, None] * inv[None, :]
    cos, sin = np.cos(ang).astype(f32), np.sin(ang).astype(f32)
    cc = np.concatenate([cos, cos], axis=-1)
    ss = np.concatenate([sin, sin], axis=-1)
    z64 = np.zeros((seq, QK_NOPE), f32)
    z32 = np.zeros((seq, LANES - QK_NOPE - QK_ROPE), f32)
    sm_scale = QK_DIM ** -0.5 * math.log2(math.e)
    cosk = np.concatenate([z64, cc, z32], axis=-1)
    sink = np.concatenate([z64, ss, z32], axis=-1)
    cosr = np.tile(cc, (1, N_HEADS)) * f32(sm_scale)
    sinr = np.tile(ss, (1, N_HEADS)) * f32(sm_scale)

    def cos_sin(m, period):
        th = m.astype(np.float64) * (2.0 * math.pi / period)
        return np.cos(th), np.sin(th)

    def dft(n):
        a = np.arange(n, dtype=np.int64)
        return cos_sin((a[:, None] * a[None, :]) % n, n)

    bf = lambda a: a.astype(f32).astype(BF16)
    c_ch, s_ch = dft(GC)
    cs = bf(np.concatenate([c_ch, s_ch], axis=-1) * GC ** -0.5)

    n1 = FFT_N1
    n2 = seq // n1
    c1, s1 = dft(n1)
    f1 = bf(np.concatenate([np.concatenate([c1, -s1], axis=1), np.concatenate([-s1, -c1], axis=1)], axis=0)
            * n1 ** -0.5)
    k1 = np.arange(n1, dtype=np.int64)[:, None, None]
    k2 = np.arange(n2, dtype=np.int64)[None, :, None]
    t2 = np.arange(n2, dtype=np.int64)[None, None, :]
    c2, s2 = cos_sin(((k1 + n1 * k2) * t2) % seq, seq)
    g2c = bf(c2 * n2 ** -0.5)
    g2s = bf(s2 * n2 ** -0.5)
    return dict(cosr=cosr, sinr=sinr, q_scale=sm_scale, cosk=cosk, sink=sink, cs=cs, f1=f1, g2c=g2c, g2s=g2s)


def _layer_weights(l, w_in, q_norm_g, w_uq, kv_norm_g, w_ukv, w_a, conv_w, conv_b, conv_ln_g,
                   conv_ln_b, w_b, pool_w, pool_scale, w_c, w_d, w_out):
    D = w_in.shape[1]
    wi = w_in[l]
    assert Q_PAD - Q_LORA == QK_ROPE
    w1 = jnp.concatenate([wi[:, Q_LORA:Q_LORA + KV_LORA], wi[:, :Q_LORA], wi[:, Q_LORA + KV_LORA:4 * W_BR]],
                         axis=-1).astype(BF16)

    uq = w_uq[l].reshape(Q_LORA, N_HEADS, QK_NOPE + QK_ROPE)
    nope, rope = uq[..., :QK_NOPE], uq[..., QK_NOPE:]
    wq = jnp.concatenate([nope.reshape(Q_LORA, N_HEADS * QK_NOPE), rope.reshape(Q_LORA, N_HEADS * QK_ROPE),
                          _rot_half_cols(rope).reshape(Q_LORA, N_HEADS * QK_ROPE)], axis=-1)
    wq = jnp.pad(wq, ((0, Q_PAD - Q_LORA), (0, 0))).astype(BF16)

    ukv = w_ukv[l].reshape(KV_LORA, N_HEADS, QK_NOPE + V_HEAD)
    zk = jnp.zeros((KV_LORA, N_HEADS, LANES - QK_NOPE), F32)
    wk = jnp.concatenate([ukv[..., :QK_NOPE], zk], axis=-1).reshape(KV_LORA, N_HEADS * LANES).astype(BF16)
    wv = ukv[..., QK_NOPE:].reshape(KV_LORA, N_HEADS * V_HEAD).astype(BF16)

    gq = jnp.pad(q_norm_g[l], (0, Q_PAD - Q_LORA)).reshape(1, Q_PAD)
    return dict(
        w1=w1, wq=wq, wk=wk, wv=wv, gq=gq, gkv=kv_norm_g[l].reshape(1, KV_LORA),
        wg=wi[:, 4 * W_BR:].astype(BF16), wa=w_a[l].astype(BF16),
        wb=w_b[l].astype(BF16), wc=w_c[l].astype(BF16), wd=w_d[l].astype(BF16), wo=w_out[l].astype(BF16),
        conv_w=conv_w[l], conv_b=conv_b[l].reshape(1, CONV_CH),
        ln_g=conv_ln_g[l].reshape(1, CONV_CH), ln_b=conv_ln_b[l].reshape(1, CONV_CH),
        pool_w=pool_w[l].astype(BF16), pool_scale=pool_scale[l].reshape(1, W_BR))


def _tiles(seq):
    t = lambda n: min(n, seq)
    return dict(ffn=t(1024), mix=t(1024), mixout=t(512), tq=t(1024), tk=t(2048), local=t(512))


def kernel(x, c, ada_w, ada_b, norm_g, ffn1_w_in, ffn1_w_out, ffn2_w_in, ffn2_w_out, w_in, q_norm_g,
           w_uq, kv_norm_g, w_ukv, w_a, conv_w, conv_b, conv_ln_g, conv_ln_b, w_b, pool_w, pool_scale,
           w_c, w_d, w_out):
    B, S, D = x.shape
    L = ada_w.shape[0]
    assert D == D_MODEL and S % (FFT_N1 * 16) == 0
    ts = _tiles(S)
    tabs = _tables(S)
    mod = _mod_call(c, ada_w, ada_b).reshape(L, B, N_MOD, D)
    ffn_w = [(_to_bf16(wi), _to_bf16(wo)) for wi, wo in ((ffn1_w_in, ffn1_w_out), (ffn2_w_in, ffn2_w_out))]
    for l in range(L):
        lw = _layer_weights(l, w_in, q_norm_g, w_uq, kv_norm_g, w_ukv, w_a, conv_w, conv_b, conv_ln_g,
                            conv_ln_b, w_b, pool_w, pool_scale, w_c, w_d, w_out)
        mod_l, g_l = mod[l], norm_g[l]
        x = _ffn_call(x, mod_l, g_l, *ffn_w[0], l, 0, ts["ffn"])
        q, k, v, kn, u, zc, p, qq = _mixin_call(x, mod_l, g_l, lw, tabs, ts["mix"])
        oa = _attn_call(q, k, v, kn, ts["tq"], ts["tk"])
        cb, pc = _local_call(u, zc, lw, ts["local"])
        a = _fft1_call(p, qq, tabs)
        f = _fft2_call(a, tabs, S, 8)
        x = _mixout_call(x, mod_l, g_l, lw, oa, cb, pc, f, ts["mixout"])
        x = _ffn_call(x, mod_l, g_l, *ffn_w[1], l, 2, ts["ffn"])
    return x
```

```python
import functools
import math

import jax
import jax.numpy as jnp
import numpy as np
from jax import lax
from jax.experimental import pallas as pl
from jax.experimental.pallas import tpu as pltpu

F32 = jnp.float32
BF16 = jnp.bfloat16

D_MODEL = 1024
D_FF = 2816
N_HEADS = 8
Q_LORA = 352
KV_LORA = 128
QK_NOPE = 64
QK_ROPE = 32
V_HEAD = 64
QK_DIM = QK_NOPE + QK_ROPE
BF16_ROWS = 16
V_ROWS = 80
DENOM_MIN = 2.0 ** -80
ROPE_THETA = 10000.0
CONV_CH = 256
CONV_WIDTH = 31
POOL_WINDOWS = (2, 4, 8, 16)
GC = 128
N_GROUPS = 4
W_BR = 512
N_MOD = 9
EPS = 1e-6

LANES = 128
SUBLANES = 8
Q_PAD = 384
HALO = 16
FFT_N1 = 128
VMEM_LIMIT = 56 * 1024 * 1024

C_CKV = 0
C_CQ = C_CKV + KV_LORA
C_ZB = C_CQ + Q_PAD
C_ZC = C_ZB + W_BR
C_ZD = C_ZC + W_BR
W1_COLS = C_ZD + W_BR


def _params(*sem):
    return pltpu.CompilerParams(dimension_semantics=sem, vmem_limit_bytes=VMEM_LIMIT)


def _const_spec(shape):
    zeros = (0,) * len(shape)
    return pl.BlockSpec(shape, lambda *_: zeros, pipeline_mode=pl.Buffered(1))


def _dot(a, b):
    return jnp.dot(a, b, preferred_element_type=F32)


def _rms(x):
    return x * lax.rsqrt(jnp.mean(x * x, axis=-1, keepdims=True) + EPS)


def _sigmoid(x):
    return 1.0 / (1.0 + jnp.exp(-x))


def _mod_kernel(c_ref, w_ref, b_ref, o_ref):
    c = c_ref[...]
    ca = c * _sigmoid(c)
    o_ref[0] = jnp.dot(ca, w_ref[0], preferred_element_type=F32,
                       precision=lax.Precision.HIGHEST) + b_ref[0]


def _mod_call(c, ada_w, ada_b):
    L, D, N = ada_w.shape
    B = c.shape[0]
    tn = 1152
    return pl.pallas_call(
        _mod_kernel,
        grid=(L, N // tn),
        in_specs=[pl.BlockSpec((B, D), lambda l, j: (0, 0)),
                  pl.BlockSpec((1, D, tn), lambda l, j: (l, 0, j)),
                  pl.BlockSpec((1, 1, tn), lambda l, j: (l, 0, j))],
        out_specs=pl.BlockSpec((1, B, tn), lambda l, j: (l, 0, j)),
        out_shape=jax.ShapeDtypeStruct((L, B, N), F32),
        compiler_params=_params("parallel", "parallel"),
        name="adaln_mod",
    )(c, ada_w, ada_b.reshape(L, 1, N))


def _ffn_kernel(x_ref, mod_ref, g_ref, win_ref, wout_ref, o_ref, a_scr, *, sub, chunk, parts):
    shift = mod_ref[0, 3 * sub:3 * sub + 1, :]
    scale = mod_ref[0, 3 * sub + 1:3 * sub + 2, :]
    gate = mod_ref[0, 3 * sub + 2:3 * sub + 3, :]
    g_in = g_ref[2 * sub:2 * sub + 1, :]
    g_out = g_ref[2 * sub + 1:2 * sub + 2, :]
    rows = x_ref.shape[1] // parts
    hbs = []
    for p in range(parts):
        x = x_ref[0, p * rows:(p + 1) * rows, :]
        hbs.append(((_rms(x) * g_in) * (1.0 + scale) + shift).astype(BF16))
    for p in range(parts):
        for j in range(D_FF // chunk):
            gj = _dot(hbs[p], win_ref[0, :, j * chunk:(j + 1) * chunk])
            uj = _dot(hbs[p], win_ref[0, :, D_FF + j * chunk:D_FF + (j + 1) * chunk])
            a_scr[p, :, j * chunk:(j + 1) * chunk] = (gj * _sigmoid(gj) * uj).astype(BF16)
    for p in range(parts):
        y = _dot(a_scr[p], wout_ref[0])
        x = x_ref[0, p * rows:(p + 1) * rows, :]
        o_ref[0, p * rows:(p + 1) * rows, :] = x + (0.5 * gate) * (_rms(y) * g_out)


def _cast_kernel(w_ref, o_ref):
    o_ref[...] = w_ref[...].astype(o_ref.dtype)


def _to_bf16(w):
    L, R, C = w.shape
    rows = R // 4
    assert R % 4 == 0 and rows % BF16_ROWS == 0
    blk = pl.BlockSpec((1, rows, C), lambda l, i: (l, i, 0))
    return pl.pallas_call(
        _cast_kernel,
        grid=(L, R // rows),
        in_specs=[blk],
        out_specs=blk,
        out_shape=jax.ShapeDtypeStruct(w.shape, BF16),
        compiler_params=_params("parallel", "parallel"),
        name="weight_cast",
    )(w)


def _layer_spec(w, l):
    return pl.BlockSpec((1,) + w.shape[1:], lambda *_: (l,) + (0,) * (w.ndim - 1), pipeline_mode=pl.Buffered(1))


def _ffn_call(x, mod_l, g_l, w_in, w_out, l, sub, tm):
    B, S, D = x.shape
    parts = 4 if tm % 1024 == 0 else 1
    kern = functools.partial(_ffn_kernel, sub=sub, chunk=256, parts=parts)
    return pl.pallas_call(
        kern,
        grid=(B, S // tm),
        in_specs=[pl.BlockSpec((1, tm, D), lambda b, i: (b, i, 0)),
                  pl.BlockSpec((1, N_MOD, D), lambda b, i: (b, 0, 0)),
                  _const_spec(g_l.shape),
                  _layer_spec(w_in, l),
                  _layer_spec(w_out, l)],
        out_specs=pl.BlockSpec((1, tm, D), lambda b, i: (b, i, 0)),
        out_shape=jax.ShapeDtypeStruct(x.shape, F32),
        scratch_shapes=[pltpu.VMEM((parts, tm // parts, D_FF), BF16)],
        compiler_params=_params("parallel", "parallel"),
        name=f"ffn{sub}",
    )(x, mod_l, g_l, w_in, w_out)


def _mixin_kernel(x_ref, mod_ref, g_ref, w1_ref, gq_ref, gkv_ref, wq_ref, wk_ref, wv_ref,
                  cr_ref, sr_ref, ck_ref, sk_ref, cs_ref,
                  q_ref, k_ref, v_ref, kn_ref, u_ref, zc_ref, p_ref, qq_ref, *, q_scale):
    x = x_ref[0]
    tm = x.shape[0]
    shift = mod_ref[0, 3:4, :]
    scale = mod_ref[0, 4:5, :]
    h = (_rms(x) * g_ref[2:3, :]) * (1.0 + scale) + shift
    hb = h.astype(BF16)
    z = _dot(hb, w1_ref[...])

    cq = z[:, C_CQ:C_CQ + Q_PAD]
    qlane = lax.broadcasted_iota(jnp.int32, (1, Q_PAD), 1)
    cqm = jnp.where(qlane < Q_LORA, cq, 0.0)
    cqn = cqm * lax.rsqrt(jnp.sum(cqm * cqm, axis=-1, keepdims=True) * (1.0 / Q_LORA) + EPS)
    cqn = (cqn * gq_ref[...]).astype(BF16)
    qq = _dot(cqn, wq_ref[...])
    nope = qq[:, :N_HEADS * QK_NOPE] * q_scale
    w_rope = N_HEADS * QK_ROPE
    roped = (qq[:, N_HEADS * QK_NOPE:N_HEADS * QK_NOPE + w_rope] * cr_ref[...]
             + qq[:, N_HEADS * QK_NOPE + w_rope:] * sr_ref[...])
    for blk in range(N_HEADS * QK_NOPE // LANES):
        nt = nope[:, blk * LANES:(blk + 1) * LANES].T.astype(BF16)
        for i in range(LANES // QK_NOPE):
            q_ref[0, blk * (LANES // QK_NOPE) + i, 0:QK_NOPE, :] = nt[i * QK_NOPE:(i + 1) * QK_NOPE]
    for blk in range(w_rope // LANES):
        rt = roped[:, blk * LANES:(blk + 1) * LANES].T.astype(BF16)
        for i in range(LANES // QK_ROPE):
            q_ref[0, blk * (LANES // QK_ROPE) + i, QK_NOPE:QK_NOPE + QK_ROPE, :] = rt[i * QK_ROPE:(i + 1) * QK_ROPE]
    zero_rows = jnp.zeros((LANES - QK_NOPE - QK_ROPE, tm), BF16)
    for hd in range(N_HEADS):
        q_ref[0, hd, QK_NOPE + QK_ROPE:, :] = zero_rows

    kr_at = Q_LORA - (Q_PAD - LANES)
    kr = pltpu.roll(cq[:, Q_PAD - LANES:], (QK_NOPE - kr_at) % LANES, 1)
    klane = lax.broadcasted_iota(jnp.int32, (1, LANES), 1)
    kr_rot = jnp.where(klane < QK_NOPE + QK_ROPE // 2,
                       -pltpu.roll(kr, LANES - QK_ROPE // 2, 1), pltpu.roll(kr, QK_ROPE // 2, 1))
    k_rope = kr * ck_ref[...] + kr_rot * sk_ref[...]

    ckv = z[:, C_CKV:C_CKV + KV_LORA]
    ckvn = (_rms(ckv) * gkv_ref[...]).astype(BF16)
    kk = _dot(ckvn, wk_ref[...])
    ones_lane = (klane == QK_DIM).astype(F32)
    for hd in range(N_HEADS):
        kh = kk[:, hd * LANES:(hd + 1) * LANES] + k_rope
        k_ref[0, hd] = (kh + ones_lane).astype(BF16)
        norm2 = jnp.max(jnp.sum(kh * kh, axis=-1, keepdims=True), axis=0, keepdims=True)
        kn_ref[0, hd] = jnp.broadcast_to(norm2, (SUBLANES, LANES))
    vv = _dot(ckvn, wv_ref[...])
    ones_rows = (lax.broadcasted_iota(jnp.int32, (V_ROWS - V_HEAD, tm), 0) == 0).astype(BF16)
    for blk in range(N_HEADS * V_HEAD // LANES):
        vt = vv[:, blk * LANES:(blk + 1) * LANES].T.astype(BF16)
        for i in range(LANES // V_HEAD):
            v_ref[0, blk * (LANES // V_HEAD) + i, 0:V_HEAD, :] = vt[i * V_HEAD:(i + 1) * V_HEAD]
    for hd in range(N_HEADS):
        v_ref[0, hd, V_HEAD:, :] = ones_rows

    a = z[:, C_ZB:C_ZB + CONV_CH]
    gt = z[:, C_ZB + CONV_CH:C_ZB + 2 * CONV_CH]
    u_ref[0] = a * _sigmoid(gt)

    zc_ref[0] = z[:, C_ZC:C_ZC + W_BR]

    for g in range(N_GROUPS):
        zd = z[:, C_ZD + g * GC:C_ZD + (g + 1) * GC].astype(BF16)
        pq = _dot(zd, cs_ref[...])
        p_ref[0, :, g * GC:(g + 1) * GC] = pq[:, :GC]
        qq_ref[0, :, g * GC:(g + 1) * GC] = pq[:, GC:]


def _mixin_call(x, mod_l, g_l, lw, tabs, tm):
    B, S, D = x.shape
    tok = lambda w: pl.BlockSpec((1, tm, w), lambda b, i: (b, i, 0))
    head = pl.BlockSpec((1, N_HEADS, tm, LANES), lambda b, i: (b, 0, i, 0))
    head_t = pl.BlockSpec((1, N_HEADS, LANES, tm), lambda b, i: (b, 0, 0, i))
    tab = lambda w: pl.BlockSpec((tm, w), lambda b, i: (i, 0))
    hshape = jax.ShapeDtypeStruct((B, N_HEADS, S, LANES), BF16)
    hshape_t = jax.ShapeDtypeStruct((B, N_HEADS, LANES, S), BF16)
    return pl.pallas_call(
        functools.partial(_mixin_kernel, q_scale=tabs["q_scale"]),
        grid=(B, S // tm),
        in_specs=[tok(D),
                  pl.BlockSpec((1, N_MOD, D), lambda b, i: (b, 0, 0)),
                  _const_spec(g_l.shape),
                  _const_spec(lw["w1"].shape),
                  _const_spec(lw["gq"].shape),
                  _const_spec(lw["gkv"].shape),
                  _const_spec(lw["wq"].shape),
                  _const_spec(lw["wk"].shape),
                  _const_spec(lw["wv"].shape),
                  tab(N_HEADS * QK_ROPE), tab(N_HEADS * QK_ROPE), tab(LANES), tab(LANES),
                  _const_spec(tabs["cs"].shape)],
        out_specs=[head_t, head, pl.BlockSpec((1, N_HEADS, V_ROWS, tm), lambda b, i: (b, 0, 0, i)),
                   pl.BlockSpec((1, N_HEADS, SUBLANES, LANES), lambda b, i: (b, 0, i, 0)),
                   tok(CONV_CH), tok(W_BR), tok(W_BR), tok(W_BR)],
        out_shape=[hshape_t, hshape, jax.ShapeDtypeStruct((B, N_HEADS, V_ROWS, S), BF16),
                   jax.ShapeDtypeStruct((B, N_HEADS, (S // tm) * SUBLANES, LANES), F32),
                   jax.ShapeDtypeStruct((B, S, CONV_CH), F32),
                   jax.ShapeDtypeStruct((B, S, W_BR), F32),
                   jax.ShapeDtypeStruct((B, S, W_BR), F32),
                   jax.ShapeDtypeStruct((B, S, W_BR), F32)],
        compiler_params=_params("parallel", "parallel"),
        name="mixer_in",
    )(x, mod_l, g_l, lw["w1"], lw["gq"], lw["gkv"], lw["wq"], lw["wk"], lw["wv"],
      tabs["cosr"], tabs["sinr"], tabs["cosk"], tabs["sink"], tabs["cs"])


def _attn_kernel(qt_ref, k_ref, vt_ref, kn_ref, o_ref, acc_scr, m_scr, *, tq, tk):
    seq = k_ref.shape[2]
    nq, nk = seq // tq, seq // tk
    k_norm2 = jnp.max(kn_ref[0, 0])

    def exact_tile(t, qoff):
        qt = qt_ref[0, 0, :, pl.ds(qoff, tq)]
        m_scr[...] = jnp.full(m_scr.shape, -jnp.inf, F32)
        acc_scr[t, 0:V_ROWS, :] = jnp.zeros((V_ROWS, tq), F32)

        def chunk(j, carry):
            off = pl.multiple_of(j * tk, tk)
            st = _dot(k_ref[0, 0, pl.ds(off, tk), :], qt)
            m_prev = m_scr[...]
            m_new = jnp.maximum(m_prev, jnp.max(st, axis=0, keepdims=True))
            pt = jnp.exp2(st - m_new).astype(BF16)
            pv = _dot(vt_ref[0, 0, :, pl.ds(off, tk)], pt)
            acc_scr[t, 0:V_ROWS, :] = jnp.exp2(m_prev - m_new) * acc_scr[t, 0:V_ROWS, :] + pv
            m_scr[...] = m_new
            return carry

        lax.fori_loop(0, nk, chunk, 0)

    def bounded_tile(t, qoff):
        qt = qt_ref[0, 0, :, pl.ds(qoff, tq)]
        q32 = qt.astype(F32)
        bound = jnp.sqrt(jnp.sum(q32 * q32, axis=0, keepdims=True) * k_norm2)
        first = lax.broadcasted_iota(jnp.int32, (BF16_ROWS, tq), 0) == 0
        stab = jnp.where(first, -bound, 0.0).astype(BF16)
        qs = jnp.concatenate([qt[0:QK_DIM], stab, qt[QK_DIM + BF16_ROWS:]], axis=0)
        acc = None
        for j in range(nk):
            st = _dot(k_ref[0, 0, j * tk:(j + 1) * tk, :], qs)
            pv = _dot(vt_ref[0, 0, :, j * tk:(j + 1) * tk], jnp.exp2(st).astype(BF16))
            acc = pv if acc is None else acc + pv
        acc_scr[t, 0:V_ROWS, :] = acc
        return jnp.min(acc[V_HEAD:V_HEAD + 1, :]) >= DENOM_MIN

    group = acc_scr.shape[0]

    def q_group(gi, carry):
        offs = [pl.multiple_of((gi * group + t) * tq, tq) for t in range(group)]
        ok = None
        for t in range(group):
            ok_t = bounded_tile(t, offs[t])
            ok = ok_t if ok is None else jnp.logical_and(ok, ok_t)

        @pl.when(jnp.logical_not(ok))
        def _():
            for t in range(group):
                exact_tile(t, offs[t])

        for t in range(group):
            denom = acc_scr[t, V_HEAD:V_HEAD + 1, :]
            o_ref[0, :, pl.ds(offs[t], tq)] = (acc_scr[t, 0:V_HEAD, :] / denom).astype(o_ref.dtype)
        return carry

    lax.fori_loop(0, nq // group, q_group, 0)


def _attn_call(qt, k, vt, kn, tq, tk):
    B, H, S, _ = k.shape
    kern = functools.partial(_attn_kernel, tq=tq, tk=tk)
    return pl.pallas_call(
        kern,
        grid=(B, H),
        in_specs=[pl.BlockSpec((1, 1, LANES, S), lambda b, h: (b, h, 0, 0)),
                  pl.BlockSpec((1, 1, S, LANES), lambda b, h: (b, h, 0, 0)),
                  pl.BlockSpec((1, 1, V_ROWS, S), lambda b, h: (b, h, 0, 0)),
                  pl.BlockSpec((1, 1) + kn.shape[2:], lambda b, h: (b, h, 0, 0))],
        out_specs=pl.BlockSpec((1, V_HEAD, S), lambda b, h: (b, h, 0)),
        out_shape=jax.ShapeDtypeStruct((B, H * V_HEAD, S), BF16),
        scratch_shapes=[pltpu.VMEM((2 if (S // tq) % 2 == 0 else 1, V_ROWS, tq), F32), pltpu.VMEM((1, tq), F32)],
        compiler_params=_params("parallel", "arbitrary"),
        name="attention",
    )(qt, k, vt, kn)


def _local_kernel(u_ref, up_ref, un_ref, z_ref, zp_ref, zn_ref, cw_ref, cb_ref, lg_ref, lb_ref,
                  pw_ref, ps_ref, oc_ref, op_ref, ubuf, zbuf, ush, sbuf, *, ts, seq, rc, rp):
    i = pl.program_id(1)
    has_prev = i > 0
    has_next = i < pl.num_programs(1) - 1
    ubuf[0:HALO, :] = jnp.where(has_prev, up_ref[0], 0.0)
    ubuf[HALO:HALO + ts, :] = u_ref[0]
    ubuf[HALO + ts:, :] = jnp.where(has_next, un_ref[0], 0.0)
    zbuf[0:HALO, :] = jnp.where(has_prev, zp_ref[0], 0.0)
    zbuf[HALO:HALO + ts, :] = z_ref[0]
    zbuf[HALO + ts:2 * HALO + ts, :] = jnp.where(has_next, zn_ref[0], 0.0)
    zbuf[2 * HALO + ts:, :] = jnp.zeros((SUBLANES, W_BR), F32)

    span = ts + 2 * HALO - SUBLANES
    for s in range(1, SUBLANES):
        ush[s - 1, 0:span, :] = ubuf[s:s + span, :]

    pad = CONV_WIDTH // 2
    for r in range(0, ts, rc):
        acc = jnp.zeros((rc, CONV_CH), F32)
        for kk in range(CONV_WIDTH):
            st = HALO + r + kk - pad
            s, base = st % SUBLANES, st - st % SUBLANES
            tap = ubuf[base:base + rc, :] if s == 0 else ush[s - 1, base:base + rc, :]
            acc = acc + tap * cw_ref[kk:kk + 1, :]
        y = acc + cb_ref[...]
        mu = jnp.mean(y, axis=-1, keepdims=True)
        yc = y - mu
        yn = yc * lax.rsqrt(jnp.mean(yc * yc, axis=-1, keepdims=True) + EPS)
        yn = yn * lg_ref[...] + lb_ref[...]
        oc_ref[0, r:r + rc, :] = (yn * _sigmoid(yn)).astype(oc_ref.dtype)

    assert all(w == 2 ** (g + 1) for g, w in enumerate(POOL_WINDOWS))
    rows_m = ts + 2 * HALO
    for m in range(1, N_GROUPS):
        step = 2 ** (m - 1)
        src = zbuf if m == 1 else sbuf.at[m - 2]
        lanes = slice(m * GC, N_GROUPS * GC)
        for c in range(0, rows_m, LANES):
            n = min(LANES, rows_m - c)
            sbuf[m - 1, c:c + n, lanes] = src[c:c + n, lanes] + src[c + step:c + step + n, lanes]
        rows_m -= SUBLANES

    for r in range(0, ts, rp):
        t = i * ts + r + lax.broadcasted_iota(jnp.int32, (rp, 1), 0)
        for g, w in enumerate(POOL_WINDOWS):
            lo = w // 2
            hi = w - 1 - lo
            cols = slice(g * GC, (g + 1) * GC)
            half = zbuf if g == 0 else sbuf.at[g - 1]
            win = half[HALO + r - lo:HALO + r - lo + rp, cols] + half[HALO + r:HALO + r + rp, cols]
            cnt = (jnp.minimum(t + hi + 1, seq) - jnp.maximum(t - lo, 0)).astype(F32)
            d = win / cnt - zbuf[HALO + r:HALO + r + rp, cols]
            yg = _dot(d.astype(BF16), pw_ref[g]) * ps_ref[:, cols]
            op_ref[0, r:r + rp, cols] = yg.astype(op_ref.dtype)


def _local_call(u, zc, lw, ts):
    B, S, _ = u.shape
    nh = ts // HALO
    last = S // HALO - 1
    cur = lambda w: pl.BlockSpec((1, ts, w), lambda b, i: (b, i, 0))
    prev = lambda w: pl.BlockSpec((1, HALO, w), lambda b, i: (b, jnp.maximum(i * nh - 1, 0), 0))
    nxt = lambda w: pl.BlockSpec((1, HALO, w), lambda b, i: (b, jnp.minimum((i + 1) * nh, last), 0))
    kern = functools.partial(_local_kernel, ts=ts, seq=S, rc=64, rp=min(ts, 256))
    return pl.pallas_call(
        kern,
        grid=(B, S // ts),
        in_specs=[cur(CONV_CH), prev(CONV_CH), nxt(CONV_CH), cur(W_BR), prev(W_BR), nxt(W_BR),
                  _const_spec(lw["conv_w"].shape), _const_spec(lw["conv_b"].shape),
                  _const_spec(lw["ln_g"].shape), _const_spec(lw["ln_b"].shape),
                  _const_spec(lw["pool_w"].shape), _const_spec(lw["pool_scale"].shape)],
        out_specs=[cur(CONV_CH), cur(W_BR)],
        out_shape=[jax.ShapeDtypeStruct((B, S, CONV_CH), BF16),
                   jax.ShapeDtypeStruct((B, S, W_BR), BF16)],
        scratch_shapes=[pltpu.VMEM((ts + 2 * HALO, CONV_CH), F32),
                        pltpu.VMEM((ts + 2 * HALO + SUBLANES, W_BR), F32),
                        pltpu.VMEM((SUBLANES - 1, ts + 2 * HALO, CONV_CH), F32),
                        pltpu.VMEM((N_GROUPS - 1, ts + 2 * HALO, W_BR), F32)],
        compiler_params=_params("parallel", "parallel"),
        name="conv_pool",
    )(u, u, u, zc, zc, zc, lw["conv_w"], lw["conv_b"], lw["ln_g"], lw["ln_b"],
      lw["pool_w"], lw["pool_scale"])


def _fft1_kernel(p_ref, q_ref, f_ref, o_ref):
    for j in range(p_ref.shape[2]):
        rhs = jnp.concatenate([p_ref[0, :, j, :], q_ref[0, :, j, :]], axis=0).astype(BF16)
        o_ref[0, :, j, :] = _dot(f_ref[...], rhs)


def _fft1_call(p, q, tabs):
    B, S, W = p.shape
    n2 = S // FFT_N1
    pv = p.reshape(B, FFT_N1, n2, W)
    qv = q.reshape(B, FFT_N1, n2, W)
    blk = pl.BlockSpec((1, FFT_N1, SUBLANES, W), lambda b, j: (b, 0, j, 0))
    return pl.pallas_call(
        _fft1_kernel,
        grid=(B, n2 // SUBLANES),
        in_specs=[blk, blk, _const_spec(tabs["f1"].shape)],
        out_specs=pl.BlockSpec((1, 2 * FFT_N1, SUBLANES, W), lambda b, j: (b, 0, j, 0)),
        out_shape=jax.ShapeDtypeStruct((B, 2 * FFT_N1, n2, W), F32),
        compiler_params=_params("parallel", "parallel"),
        name="fft_stage1",
    )(pv, qv, tabs["f1"])


def _fft2_kernel(a_ref, gc_ref, gs_ref, o_ref, *, kb):
    for j in range(kb):
        f = _dot(gc_ref[j], a_ref[0, 0, j].astype(BF16)) + _dot(gs_ref[j], a_ref[0, 1, j].astype(BF16))
        o_ref[0, :, j, :] = f


def _fft2_call(a, tabs, seq, kb):
    B = a.shape[0]
    n2 = seq // FFT_N1
    av = a.reshape(B, 2, FFT_N1, n2, W_BR)
    kern = functools.partial(_fft2_kernel, kb=kb)
    out = pl.pallas_call(
        kern,
        grid=(B, FFT_N1 // kb),
        in_specs=[pl.BlockSpec((1, 2, kb, n2, W_BR), lambda b, j: (b, 0, j, 0, 0)),
                  pl.BlockSpec((kb, n2, n2), lambda b, j: (j, 0, 0)),
                  pl.BlockSpec((kb, n2, n2), lambda b, j: (j, 0, 0))],
        out_specs=pl.BlockSpec((1, n2, kb, W_BR), lambda b, j: (b, 0, j, 0)),
        out_shape=jax.ShapeDtypeStruct((B, n2, FFT_N1, W_BR), F32),
        compiler_params=_params("parallel", "parallel"),
        name="fft_stage2",
    )(av, tabs["g2c"], tabs["g2s"])
    return out.reshape(B, seq, W_BR)


def _mixout_kernel(x_ref, mod_ref, g_ref, wg_ref, oa_ref, wa_ref, cb_ref, wb_ref, pc_ref, wc_ref,
                   f_ref, wd_ref, wo_ref, o_ref, *, parts):
    shift = mod_ref[0, 3:4, :]
    scale = mod_ref[0, 4:5, :]
    gate = mod_ref[0, 5:6, :]
    d = x_ref.shape[-1]
    rows = x_ref.shape[1] // parts
    hbs = []
    for p in range(parts):
        x = x_ref[0, p * rows:(p + 1) * rows, :]
        hbs.append(((_rms(x) * g_ref[2:3, :]) * (1.0 + scale) + shift).astype(BF16))
    ys = []
    for p in range(parts):
        sl = slice(p * rows, (p + 1) * rows)
        merged = None
        for br, (b_ref, w_ref) in enumerate(((oa_ref, wa_ref), (cb_ref, wb_ref), (pc_ref, wc_ref), (f_ref, wd_ref))):
            gl = _sigmoid(_dot(hbs[p], wg_ref[:, br * d:(br + 1) * d]))
            if br == 0:
                yb = lax.dot_general(b_ref[0, :, sl], w_ref[...], (((0,), (0,)), ((), ())),
                                     preferred_element_type=F32)
            else:
                yb = _dot(b_ref[0, sl, :].astype(BF16), w_ref[...])
            term = gl * yb
            merged = term if merged is None else merged + term
        ys.append(_dot(merged.astype(BF16), wo_ref[...]))
    for p in range(parts):
        sl = slice(p * rows, (p + 1) * rows)
        o_ref[0, sl, :] = x_ref[0, sl, :] + gate * (_rms(ys[p]) * g_ref[3:4, :])


def _mixout_call(x, mod_l, g_l, lw, oa, cb, pc, f, tm):
    B, S, D = x.shape
    tok = lambda w: pl.BlockSpec((1, tm, w), lambda b, i: (b, i, 0))
    return pl.pallas_call(
        functools.partial(_mixout_kernel, parts=2 if tm % 512 == 0 else 1),
        grid=(B, S // tm),
        in_specs=[tok(D),
                  pl.BlockSpec((1, N_MOD, D), lambda b, i: (b, 0, 0)),
                  _const_spec(g_l.shape),
                  _const_spec(lw["wg"].shape),
                  pl.BlockSpec((1, N_HEADS * V_HEAD, tm), lambda b, i: (b, 0, i)), _const_spec(lw["wa"].shape),
                  tok(CONV_CH), _const_spec(lw["wb"].shape),
                  tok(W_BR), _const_spec(lw["wc"].shape),
                  tok(W_BR), _const_spec(lw["wd"].shape),
                  _const_spec(lw["wo"].shape)],
        out_specs=tok(D),
        out_shape=jax.ShapeDtypeStruct(x.shape, F32),
        compiler_params=_params("parallel", "parallel"),
        name="mixer_out",
    )(x, mod_l, g_l, lw["wg"], oa, lw["wa"], cb, lw["wb"], pc, lw["wc"], f, lw["wd"], lw["wo"])


def _rot_half_cols(w):
    half = w.shape[-1] // 2
    return jnp.concatenate([-w[..., half:], w[..., :half]], axis=-1)


def _tables(seq):
    f32 = np.float32
    pos = np.arange(seq, dtype=f32)
    inv = (f32(ROPE_THETA) ** (-np.arange(0, QK_ROPE, 2, dtype=f32) / f32(QK_ROPE))).astype(f32)
    ang = pos[:, None] * inv[None, :]
    cos, sin = np.cos(ang).astype(f32), np.sin(ang).astype(f32)
    cc = np.concatenate([cos, cos], axis=-1)
    ss = np.concatenate([sin, sin], axis=-1)
    z64 = np.zeros((seq, QK_NOPE), f32)
    z32 = np.zeros((seq, LANES - QK_NOPE - QK_ROPE), f32)
    sm_scale = QK_DIM ** -0.5 * math.log2(math.e)
    cosk = np.concatenate([z64, cc, z32], axis=-1)
    sink = np.concatenate([z64, ss, z32], axis=-1)
    cosr = np.tile(cc, (1, N_HEADS)) * f32(sm_scale)
    sinr = np.tile(ss, (1, N_HEADS)) * f32(sm_scale)

    def cos_sin(m, period):
        th = m.astype(np.float64) * (2.0 * math.pi / period)
        return np.cos(th), np.sin(th)

    def dft(n):
        a = np.arange(n, dtype=np.int64)
        return cos_sin((a[:, None] * a[None, :]) % n, n)

    bf = lambda a: a.astype(f32).astype(BF16)
    c_ch, s_ch = dft(GC)
    cs = bf(np.concatenate([c_ch, s_ch], axis=-1) * GC ** -0.5)

    n1 = FFT_N1
    n2 = seq // n1
    c1, s1 = dft(n1)
    f1 = bf(np.concatenate([np.concatenate([c1, -s1], axis=1), np.concatenate([-s1, -c1], axis=1)], axis=0)
            * n1 ** -0.5)
    k1 = np.arange(n1, dtype=np.int64)[:, None, None]
    k2 = np.arange(n2, dtype=np.int64)[None, :, None]
    t2 = np.arange(n2, dtype=np.int64)[None, None, :]
    c2, s2 = cos_sin(((k1 + n1 * k2) * t2) % seq, seq)
    g2c = bf(c2 * n2 ** -0.5)
    g2s = bf(s2 * n2 ** -0.5)
    return dict(cosr=cosr, sinr=sinr, q_scale=sm_scale, cosk=cosk, sink=sink, cs=cs, f1=f1, g2c=g2c, g2s=g2s)


def _layer_weights(l, w_in, q_norm_g, w_uq, kv_norm_g, w_ukv, w_a, conv_w, conv_b, conv_ln_g,
                   conv_ln_b, w_b, pool_w, pool_scale, w_c, w_d, w_out):
    D = w_in.shape[1]
    wi = w_in[l]
    assert Q_PAD - Q_LORA == QK_ROPE
    w1 = jnp.concatenate([wi[:, Q_LORA:Q_LORA + KV_LORA], wi[:, :Q_LORA], wi[:, Q_LORA + KV_LORA:4 * W_BR]],
                         axis=-1).astype(BF16)

    uq = w_uq[l].reshape(Q_LORA, N_HEADS, QK_NOPE + QK_ROPE)
    nope, rope = uq[..., :QK_NOPE], uq[..., QK_NOPE:]
    wq = jnp.concatenate([nope.reshape(Q_LORA, N_HEADS * QK_NOPE), rope.reshape(Q_LORA, N_HEADS * QK_ROPE),
                          _rot_half_cols(rope).reshape(Q_LORA, N_HEADS * QK_ROPE)], axis=-1)
    wq = jnp.pad(wq, ((0, Q_PAD - Q_LORA), (0, 0))).astype(BF16)

    ukv = w_ukv[l].reshape(KV_LORA, N_HEADS, QK_NOPE + V_HEAD)
    zk = jnp.zeros((KV_LORA, N_HEADS, LANES - QK_NOPE), F32)
    wk = jnp.concatenate([ukv[..., :QK_NOPE], zk], axis=-1).reshape(KV_LORA, N_HEADS * LANES).astype(BF16)
    wv = ukv[..., QK_NOPE:].reshape(KV_LORA, N_HEADS * V_HEAD).astype(BF16)

    gq = jnp.pad(q_norm_g[l], (0, Q_PAD - Q_LORA)).reshape(1, Q_PAD)
    return dict(
        w1=w1, wq=wq, wk=wk, wv=wv, gq=gq, gkv=kv_norm_g[l].reshape(1, KV_LORA),
        wg=wi[:, 4 * W_BR:].astype(BF16), wa=w_a[l].astype(BF16),
        wb=w_b[l].astype(BF16), wc=w_c[l].astype(BF16), wd=w_d[l].astype(BF16), wo=w_out[l].astype(BF16),
        conv_w=conv_w[l], conv_b=conv_b[l].reshape(1, CONV_CH),
        ln_g=conv_ln_g[l].reshape(1, CONV_CH), ln_b=conv_ln_b[l].reshape(1, CONV_CH),
        pool_w=pool_w[l].astype(BF16), pool_scale=pool_scale[l].reshape(1, W_BR))


def _tiles(seq):
    t = lambda n: min(n, seq)
    return dict(ffn=t(1024), mix=t(1024), mixout=t(512), tq=t(512), tk=t(4096), local=t(512))


def kernel(x, c, ada_w, ada_b, norm_g, ffn1_w_in, ffn1_w_out, ffn2_w_in, ffn2_w_out, w_in, q_norm_g,
           w_uq, kv_norm_g, w_ukv, w_a, conv_w, conv_b, conv_ln_g, conv_ln_b, w_b, pool_w, pool_scale,
           w_c, w_d, w_out):
    B, S, D = x.shape
    L = ada_w.shape[0]
    assert D == D_MODEL and S % (FFT_N1 * 16) == 0
    ts = _tiles(S)
    tabs = _tables(S)
    mod = _mod_call(c, ada_w, ada_b).reshape(L, B, N_MOD, D)
    ffn_w = [(_to_bf16(wi), _to_bf16(wo)) for wi, wo in ((ffn1_w_in, ffn1_w_out), (ffn2_w_in, ffn2_w_out))]
    for l in range(L):
        lw = _layer_weights(l, w_in, q_norm_g, w_uq, kv_norm_g, w_ukv, w_a, conv_w, conv_b, conv_ln_g,
                            conv_ln_b, w_b, pool_w, pool_scale, w_c, w_d, w_out)
        mod_l, g_l = mod[l], norm_g[l]
        x = _ffn_call(x, mod_l, g_l, *ffn_w[0], l, 0, ts["ffn"])
        q, k, v, kn, u, zc, p, qq = _mixin_call(x, mod_l, g_l, lw, tabs, ts["mix"])
        oa = _attn_call(q, k, v, kn, ts["tq"], ts["tk"])
        cb, pc = _local_call(u, zc, lw, ts["local"])
        a = _fft1_call(p, qq, tabs)
        f = _fft2_call(a, tabs, S, 8)
        x = _mixout_call(x, mod_l, g_l, lw, oa, cb, pc, f, ts["mixout"])
        x = _ffn_call(x, mod_l, g_l, *ffn_w[1], l, 2, ts["ffn"])
    return x
```

```python
import functools
import math

import jax
import jax.numpy as jnp
import numpy as np
from jax import lax
from jax.experimental import pallas as pl
from jax.experimental.pallas import tpu as pltpu

F32 = jnp.float32
BF16 = jnp.bfloat16

D_MODEL = 1024
D_FF = 2816
N_HEADS = 8
Q_LORA = 352
KV_LORA = 128
QK_NOPE = 64
QK_ROPE = 32
V_HEAD = 64
QK_DIM = QK_NOPE + QK_ROPE
BF16_ROWS = 16
V_ROWS = 80
DENOM_MIN = 2.0 ** -80
ROPE_THETA = 10000.0
CONV_CH = 256
CONV_WIDTH = 31
POOL_WINDOWS = (2, 4, 8, 16)
GC = 128
N_GROUPS = 4
W_BR = 512
N_MOD = 9
EPS = 1e-6

LANES = 128
SUBLANES = 8
Q_PAD = 384
HALO = 16
FFT_N1 = 128
VMEM_LIMIT = 56 * 1024 * 1024

C_CKV = 0
C_CQ = C_CKV + KV_LORA
C_ZB = C_CQ + Q_PAD
C_ZC = C_ZB + W_BR
C_ZD = C_ZC + W_BR
W1_COLS = C_ZD + W_BR


def _params(*sem):
    return pltpu.CompilerParams(dimension_semantics=sem, vmem_limit_bytes=VMEM_LIMIT)


def _const_spec(shape):
    zeros = (0,) * len(shape)
    return pl.BlockSpec(shape, lambda *_: zeros, pipeline_mode=pl.Buffered(1))


def _dot(a, b):
    return jnp.dot(a, b, preferred_element_type=F32)


def _rms(x):
    return x * lax.rsqrt(jnp.mean(x * x, axis=-1, keepdims=True) + EPS)


def _sigmoid(x):
    return 1.0 / (1.0 + jnp.exp(-x))


def _mod_kernel(c_ref, w_ref, b_ref, o_ref):
    c = c_ref[...]
    ca = c * _sigmoid(c)
    o_ref[0] = jnp.dot(ca, w_ref[0], preferred_element_type=F32,
                       precision=lax.Precision.HIGHEST) + b_ref[0]


def _mod_call(c, ada_w, ada_b):
    L, D, N = ada_w.shape
    B = c.shape[0]
    tn = 1152
    return pl.pallas_call(
        _mod_kernel,
        grid=(L, N // tn),
        in_specs=[pl.BlockSpec((B, D), lambda l, j: (0, 0)),
                  pl.BlockSpec((1, D, tn), lambda l, j: (l, 0, j)),
                  pl.BlockSpec((1, 1, tn), lambda l, j: (l, 0, j))],
        out_specs=pl.BlockSpec((1, B, tn), lambda l, j: (l, 0, j)),
        out_shape=jax.ShapeDtypeStruct((L, B, N), F32),
        compiler_params=_params("parallel", "parallel"),
        name="adaln_mod",
    )(c, ada_w, ada_b.reshape(L, 1, N))


def _ffn_kernel(x_ref, mod_ref, g_ref, win_ref, wout_ref, o_ref, a_scr, *, sub, chunk, parts):
    shift = mod_ref[0, 3 * sub:3 * sub + 1, :]
    scale = mod_ref[0, 3 * sub + 1:3 * sub + 2, :]
    gate = mod_ref[0, 3 * sub + 2:3 * sub + 3, :]
    g_in = g_ref[2 * sub:2 * sub + 1, :]
    g_out = g_ref[2 * sub + 1:2 * sub + 2, :]
    rows = x_ref.shape[1] // parts
    hbs = []
    for p in range(parts):
        x = x_ref[0, p * rows:(p + 1) * rows, :]
        hbs.append(((_rms(x) * g_in) * (1.0 + scale) + shift).astype(BF16))
    for p in range(parts):
        for c0 in range(0, D_FF, chunk):
            c1 = min(c0 + chunk, D_FF)
            gj = _dot(hbs[p], win_ref[0, :, c0:c1])
            uj = _dot(hbs[p], win_ref[0, :, D_FF + c0:D_FF + c1])
            a_scr[p, :, c0:c1] = (gj * _sigmoid(gj) * uj).astype(BF16)
    for p in range(parts):
        y = _dot(a_scr[p], wout_ref[0])
        x = x_ref[0, p * rows:(p + 1) * rows, :]
        o_ref[0, p * rows:(p + 1) * rows, :] = x + (0.5 * gate) * (_rms(y) * g_out)


def _cast_kernel(w_ref, o_ref):
    o_ref[...] = w_ref[...].astype(o_ref.dtype)


def _to_bf16(w):
    L, R, C = w.shape
    rows = R // 4
    assert R % 4 == 0 and rows % BF16_ROWS == 0
    blk = pl.BlockSpec((1, rows, C), lambda l, i: (l, i, 0))
    return pl.pallas_call(
        _cast_kernel,
        grid=(L, R // rows),
        in_specs=[blk],
        out_specs=blk,
        out_shape=jax.ShapeDtypeStruct(w.shape, BF16),
        compiler_params=_params("parallel", "parallel"),
        name="weight_cast",
    )(w)


def _layer_spec(w, l):
    return pl.BlockSpec((1,) + w.shape[1:], lambda *_: (l,) + (0,) * (w.ndim - 1), pipeline_mode=pl.Buffered(1))


def _ffn_call(x, mod_l, g_l, w_in, w_out, l, sub, tm):
    B, S, D = x.shape
    parts = 4 if tm % 1024 == 0 else 1
    kern = functools.partial(_ffn_kernel, sub=sub, chunk=512, parts=parts)
    return pl.pallas_call(
        kern,
        grid=(B, S // tm),
        in_specs=[pl.BlockSpec((1, tm, D), lambda b, i: (b, i, 0)),
                  pl.BlockSpec((1, N_MOD, D), lambda b, i: (b, 0, 0)),
                  _const_spec(g_l.shape),
                  _layer_spec(w_in, l),
                  _layer_spec(w_out, l)],
        out_specs=pl.BlockSpec((1, tm, D), lambda b, i: (b, i, 0)),
        out_shape=jax.ShapeDtypeStruct(x.shape, F32),
        scratch_shapes=[pltpu.VMEM((parts, tm // parts, D_FF), BF16)],
        compiler_params=_params("parallel", "parallel"),
        name=f"ffn{sub}",
    )(x, mod_l, g_l, w_in, w_out)


def _mixin_kernel(x_ref, mod_ref, g_ref, w1_ref, gq_ref, gkv_ref, wq_ref, wk_ref, wv_ref,
                  cr_ref, sr_ref, ck_ref, sk_ref, cs_ref,
                  q_ref, k_ref, v_ref, kn_ref, u_ref, zc_ref, p_ref, qq_ref, *, q_scale):
    x = x_ref[0]
    tm = x.shape[0]
    shift = mod_ref[0, 3:4, :]
    scale = mod_ref[0, 4:5, :]
    h = (_rms(x) * g_ref[2:3, :]) * (1.0 + scale) + shift
    hb = h.astype(BF16)
    z = _dot(hb, w1_ref[...])

    cq = z[:, C_CQ:C_CQ + Q_PAD]
    qlane = lax.broadcasted_iota(jnp.int32, (1, Q_PAD), 1)
    cqm = jnp.where(qlane < Q_LORA, cq, 0.0)
    cqn = cqm * lax.rsqrt(jnp.sum(cqm * cqm, axis=-1, keepdims=True) * (1.0 / Q_LORA) + EPS)
    cqn = (cqn * gq_ref[...]).astype(BF16)
    qq = _dot(cqn, wq_ref[...])
    nope = qq[:, :N_HEADS * QK_NOPE] * q_scale
    w_rope = N_HEADS * QK_ROPE
    roped = (qq[:, N_HEADS * QK_NOPE:N_HEADS * QK_NOPE + w_rope] * cr_ref[...]
             + qq[:, N_HEADS * QK_NOPE + w_rope:] * sr_ref[...])
    for blk in range(N_HEADS * QK_NOPE // LANES):
        nt = nope[:, blk * LANES:(blk + 1) * LANES].T.astype(BF16)
        for i in range(LANES // QK_NOPE):
            q_ref[0, blk * (LANES // QK_NOPE) + i, 0:QK_NOPE, :] = nt[i * QK_NOPE:(i + 1) * QK_NOPE]
    for blk in range(w_rope // LANES):
        rt = roped[:, blk * LANES:(blk + 1) * LANES].T.astype(BF16)
        for i in range(LANES // QK_ROPE):
            q_ref[0, blk * (LANES // QK_ROPE) + i, QK_NOPE:QK_NOPE + QK_ROPE, :] = rt[i * QK_ROPE:(i + 1) * QK_ROPE]
    zero_rows = jnp.zeros((LANES - QK_NOPE - QK_ROPE, tm), BF16)
    for hd in range(N_HEADS):
        q_ref[0, hd, QK_NOPE + QK_ROPE:, :] = zero_rows

    kr_at = Q_LORA - (Q_PAD - LANES)
    kr = pltpu.roll(cq[:, Q_PAD - LANES:], (QK_NOPE - kr_at) % LANES, 1)
    klane = lax.broadcasted_iota(jnp.int32, (1, LANES), 1)
    kr_rot = jnp.where(klane < QK_NOPE + QK_ROPE // 2,
                       -pltpu.roll(kr, LANES - QK_ROPE // 2, 1), pltpu.roll(kr, QK_ROPE // 2, 1))
    k_rope = kr * ck_ref[...] + kr_rot * sk_ref[...]

    ckv = z[:, C_CKV:C_CKV + KV_LORA]
    ckvn = (_rms(ckv) * gkv_ref[...]).astype(BF16)
    kk = _dot(ckvn, wk_ref[...])
    ones_lane = (klane == QK_DIM).astype(F32)
    for hd in range(N_HEADS):
        kh = kk[:, hd * LANES:(hd + 1) * LANES] + k_rope
        k_ref[0, hd] = (kh + ones_lane).astype(BF16)
        norm2 = jnp.max(jnp.sum(kh * kh, axis=-1, keepdims=True), axis=0, keepdims=True)
        kn_ref[0, hd] = jnp.broadcast_to(norm2, (SUBLANES, LANES))
    vv = _dot(ckvn, wv_ref[...])
    ones_rows = (lax.broadcasted_iota(jnp.int32, (V_ROWS - V_HEAD, tm), 0) == 0).astype(BF16)
    for blk in range(N_HEADS * V_HEAD // LANES):
        vt = vv[:, blk * LANES:(blk + 1) * LANES].T.astype(BF16)
        for i in range(LANES // V_HEAD):
            v_ref[0, blk * (LANES // V_HEAD) + i, 0:V_HEAD, :] = vt[i * V_HEAD:(i + 1) * V_HEAD]
    for hd in range(N_HEADS):
        v_ref[0, hd, V_HEAD:, :] = ones_rows

    a = z[:, C_ZB:C_ZB + CONV_CH]
    gt = z[:, C_ZB + CONV_CH:C_ZB + 2 * CONV_CH]
    u_ref[0] = a * _sigmoid(gt)

    zc_ref[0] = z[:, C_ZC:C_ZC + W_BR]

    for g in range(N_GROUPS):
        zd = z[:, C_ZD + g * GC:C_ZD + (g + 1) * GC].astype(BF16)
        pq = _dot(zd, cs_ref[...])
        p_ref[0, :, g * GC:(g + 1) * GC] = pq[:, :GC]
        qq_ref[0, :, g * GC:(g + 1) * GC] = pq[:, GC:]


def _mixin_call(x, mod_l, g_l, lw, tabs, tm):
    B, S, D = x.shape
    tok = lambda w: pl.BlockSpec((1, tm, w), lambda b, i: (b, i, 0))
    head = pl.BlockSpec((1, N_HEADS, tm, LANES), lambda b, i: (b, 0, i, 0))
    head_t = pl.BlockSpec((1, N_HEADS, LANES, tm), lambda b, i: (b, 0, 0, i))
    tab = lambda w: pl.BlockSpec((tm, w), lambda b, i: (i, 0))
    hshape = jax.ShapeDtypeStruct((B, N_HEADS, S, LANES), BF16)
    hshape_t = jax.ShapeDtypeStruct((B, N_HEADS, LANES, S), BF16)
    return pl.pallas_call(
        functools.partial(_mixin_kernel, q_scale=tabs["q_scale"]),
        grid=(B, S // tm),
        in_specs=[tok(D),
                  pl.BlockSpec((1, N_MOD, D), lambda b, i: (b, 0, 0)),
                  _const_spec(g_l.shape),
                  _const_spec(lw["w1"].shape),
                  _const_spec(lw["gq"].shape),
                  _const_spec(lw["gkv"].shape),
                  _const_spec(lw["wq"].shape),
                  _const_spec(lw["wk"].shape),
                  _const_spec(lw["wv"].shape),
                  tab(N_HEADS * QK_ROPE), tab(N_HEADS * QK_ROPE), tab(LANES), tab(LANES),
                  _const_spec(tabs["cs"].shape)],
        out_specs=[head_t, head, pl.BlockSpec((1, N_HEADS, V_ROWS, tm), lambda b, i: (b, 0, 0, i)),
                   pl.BlockSpec((1, N_HEADS, SUBLANES, LANES), lambda b, i: (b, 0, i, 0)),
                   tok(CONV_CH), tok(W_BR), tok(W_BR), tok(W_BR)],
        out_shape=[hshape_t, hshape, jax.ShapeDtypeStruct((B, N_HEADS, V_ROWS, S), BF16),
                   jax.ShapeDtypeStruct((B, N_HEADS, (S // tm) * SUBLANES, LANES), F32),
                   jax.ShapeDtypeStruct((B, S, CONV_CH), F32),
                   jax.ShapeDtypeStruct((B, S, W_BR), F32),
                   jax.ShapeDtypeStruct((B, S, W_BR), F32),
                   jax.ShapeDtypeStruct((B, S, W_BR), F32)],
        compiler_params=_params("parallel", "parallel"),
        name="mixer_in",
    )(x, mod_l, g_l, lw["w1"], lw["gq"], lw["gkv"], lw["wq"], lw["wk"], lw["wv"],
      tabs["cosr"], tabs["sinr"], tabs["cosk"], tabs["sink"], tabs["cs"])


def _attn_kernel(qt_ref, k_ref, vt_ref, kn_ref, o_ref, acc_scr, m_scr, *, tq, tk):
    seq = k_ref.shape[2]
    nq, nk = seq // tq, seq // tk
    k_norm2 = jnp.max(kn_ref[0, 0])

    def exact_tile(t, qoff):
        qt = qt_ref[0, 0, :, pl.ds(qoff, tq)]
        m_scr[...] = jnp.full(m_scr.shape, -jnp.inf, F32)
        acc_scr[t, 0:V_ROWS, :] = jnp.zeros((V_ROWS, tq), F32)

        def chunk(j, carry):
            off = pl.multiple_of(j * tk, tk)
            st = _dot(k_ref[0, 0, pl.ds(off, tk), :], qt)
            m_prev = m_scr[...]
            m_new = jnp.maximum(m_prev, jnp.max(st, axis=0, keepdims=True))
            pt = jnp.exp2(st - m_new).astype(BF16)
            pv = _dot(vt_ref[0, 0, :, pl.ds(off, tk)], pt)
            acc_scr[t, 0:V_ROWS, :] = jnp.exp2(m_prev - m_new) * acc_scr[t, 0:V_ROWS, :] + pv
            m_scr[...] = m_new
            return carry

        lax.fori_loop(0, nk, chunk, 0)

    def bounded_tile(t, qoff):
        qt = qt_ref[0, 0, :, pl.ds(qoff, tq)]
        q32 = qt.astype(F32)
        bound = jnp.sqrt(jnp.sum(q32 * q32, axis=0, keepdims=True) * k_norm2)
        first = lax.broadcasted_iota(jnp.int32, (BF16_ROWS, tq), 0) == 0
        stab = jnp.where(first, -bound, 0.0).astype(BF16)
        qs = jnp.concatenate([qt[0:QK_DIM], stab, qt[QK_DIM + BF16_ROWS:]], axis=0)
        acc = None
        for j in range(nk):
            st = _dot(k_ref[0, 0, j * tk:(j + 1) * tk, :], qs)
            pv = _dot(vt_ref[0, 0, :, j * tk:(j + 1) * tk], jnp.exp2(st).astype(BF16))
            acc = pv if acc is None else acc + pv
        acc_scr[t, 0:V_ROWS, :] = acc
        return jnp.min(acc[V_HEAD:V_HEAD + 1, :]) >= DENOM_MIN

    group = acc_scr.shape[0]

    def q_group(gi, carry):
        offs = [pl.multiple_of((gi * group + t) * tq, tq) for t in range(group)]
        ok = None
        for t in range(group):
            ok_t = bounded_tile(t, offs[t])
            ok = ok_t if ok is None else jnp.logical_and(ok, ok_t)

        @pl.when(jnp.logical_not(ok))
        def _():
            for t in range(group):
                exact_tile(t, offs[t])

        for t in range(group):
            denom = acc_scr[t, V_HEAD:V_HEAD + 1, :]
            o_ref[0, :, pl.ds(offs[t], tq)] = (acc_scr[t, 0:V_HEAD, :] / denom).astype(o_ref.dtype)
        return carry

    lax.fori_loop(0, nq // group, q_group, 0)


def _attn_call(qt, k, vt, kn, tq, tk):
    B, H, S, _ = k.shape
    kern = functools.partial(_attn_kernel, tq=tq, tk=tk)
    return pl.pallas_call(
        kern,
        grid=(B, H),
        in_specs=[pl.BlockSpec((1, 1, LANES, S), lambda b, h: (b, h, 0, 0)),
                  pl.BlockSpec((1, 1, S, LANES), lambda b, h: (b, h, 0, 0)),
                  pl.BlockSpec((1, 1, V_ROWS, S), lambda b, h: (b, h, 0, 0)),
                  pl.BlockSpec((1, 1) + kn.shape[2:], lambda b, h: (b, h, 0, 0))],
        out_specs=pl.BlockSpec((1, V_HEAD, S), lambda b, h: (b, h, 0)),
        out_shape=jax.ShapeDtypeStruct((B, H * V_HEAD, S), BF16),
        scratch_shapes=[pltpu.VMEM((math.gcd(S // tq, 4), V_ROWS, tq), F32), pltpu.VMEM((1, tq), F32)],
        compiler_params=_params("parallel", "arbitrary"),
        name="attention",
    )(qt, k, vt, kn)


def _local_kernel(u_ref, up_ref, un_ref, z_ref, zp_ref, zn_ref, cw_ref, cb_ref, lg_ref, lb_ref,
                  pw_ref, ps_ref, oc_ref, op_ref, ubuf, zbuf, ush, sbuf, *, ts, seq, rc, rp):
    i = pl.program_id(1)
    has_prev = i > 0
    has_next = i < pl.num_programs(1) - 1
    ubuf[0:HALO, :] = jnp.where(has_prev, up_ref[0], 0.0)
    ubuf[HALO:HALO + ts, :] = u_ref[0]
    ubuf[HALO + ts:, :] = jnp.where(has_next, un_ref[0], 0.0)
    zbuf[0:HALO, :] = jnp.where(has_prev, zp_ref[0], 0.0)
    zbuf[HALO:HALO + ts, :] = z_ref[0]
    zbuf[HALO + ts:2 * HALO + ts, :] = jnp.where(has_next, zn_ref[0], 0.0)
    zbuf[2 * HALO + ts:, :] = jnp.zeros((SUBLANES, W_BR), F32)

    span = ts + 2 * HALO - SUBLANES
    for s in range(1, SUBLANES):
        ush[s - 1, 0:span, :] = ubuf[s:s + span, :]

    pad = CONV_WIDTH // 2
    for r in range(0, ts, rc):
        acc = jnp.zeros((rc, CONV_CH), F32)
        for kk in range(CONV_WIDTH):
            st = HALO + r + kk - pad
            s, base = st % SUBLANES, st - st % SUBLANES
            tap = ubuf[base:base + rc, :] if s == 0 else ush[s - 1, base:base + rc, :]
            acc = acc + tap * cw_ref[kk:kk + 1, :]
        y = acc + cb_ref[...]
        mu = jnp.mean(y, axis=-1, keepdims=True)
        yc = y - mu
        yn = yc * lax.rsqrt(jnp.mean(yc * yc, axis=-1, keepdims=True) + EPS)
        yn = yn * lg_ref[...] + lb_ref[...]
        oc_ref[0, r:r + rc, :] = (yn * _sigmoid(yn)).astype(oc_ref.dtype)

    assert all(w == 2 ** (g + 1) for g, w in enumerate(POOL_WINDOWS))
    rows_m = ts + 2 * HALO
    for m in range(1, N_GROUPS):
        step = 2 ** (m - 1)
        src = zbuf if m == 1 else sbuf.at[m - 2]
        lanes = slice(m * GC, N_GROUPS * GC)
        for c in range(0, rows_m, LANES):
            n = min(LANES, rows_m - c)
            sbuf[m - 1, c:c + n, lanes] = src[c:c + n, lanes] + src[c + step:c + step + n, lanes]
        rows_m -= SUBLANES

    for r in range(0, ts, rp):
        t = i * ts + r + lax.broadcasted_iota(jnp.int32, (rp, 1), 0)
        for g, w in enumerate(POOL_WINDOWS):
            lo = w // 2
            hi = w - 1 - lo
            cols = slice(g * GC, (g + 1) * GC)
            half = zbuf if g == 0 else sbuf.at[g - 1]
            win = half[HALO + r - lo:HALO + r - lo + rp, cols] + half[HALO + r:HALO + r + rp, cols]
            cnt = (jnp.minimum(t + hi + 1, seq) - jnp.maximum(t - lo, 0)).astype(F32)
            d = win / cnt - zbuf[HALO + r:HALO + r + rp, cols]
            yg = _dot(d.astype(BF16), pw_ref[g]) * ps_ref[:, cols]
            op_ref[0, r:r + rp, cols] = yg.astype(op_ref.dtype)


def _local_call(u, zc, lw, ts):
    B, S, _ = u.shape
    nh = ts // HALO
    last = S // HALO - 1
    cur = lambda w: pl.BlockSpec((1, ts, w), lambda b, i: (b, i, 0))
    prev = lambda w: pl.BlockSpec((1, HALO, w), lambda b, i: (b, jnp.maximum(i * nh - 1, 0), 0))
    nxt = lambda w: pl.BlockSpec((1, HALO, w), lambda b, i: (b, jnp.minimum((i + 1) * nh, last), 0))
    kern = functools.partial(_local_kernel, ts=ts, seq=S, rc=64, rp=min(ts, 256))
    return pl.pallas_call(
        kern,
        grid=(B, S // ts),
        in_specs=[cur(CONV_CH), prev(CONV_CH), nxt(CONV_CH), cur(W_BR), prev(W_BR), nxt(W_BR),
                  _const_spec(lw["conv_w"].shape), _const_spec(lw["conv_b"].shape),
                  _const_spec(lw["ln_g"].shape), _const_spec(lw["ln_b"].shape),
                  _const_spec(lw["pool_w"].shape), _const_spec(lw["pool_scale"].shape)],
        out_specs=[cur(CONV_CH), cur(W_BR)],
        out_shape=[jax.ShapeDtypeStruct((B, S, CONV_CH), BF16),
                   jax.ShapeDtypeStruct((B, S, W_BR), BF16)],
        scratch_shapes=[pltpu.VMEM((ts + 2 * HALO, CONV_CH), F32),
                        pltpu.VMEM((ts + 2 * HALO + SUBLANES, W_BR), F32),
                        pltpu.VMEM((SUBLANES - 1, ts + 2 * HALO, CONV_CH), F32),
                        pltpu.VMEM((N_GROUPS - 1, ts + 2 * HALO, W_BR), F32)],
        compiler_params=_params("parallel", "parallel"),
        name="conv_pool",
    )(u, u, u, zc, zc, zc, lw["conv_w"], lw["conv_b"], lw["ln_g"], lw["ln_b"],
      lw["pool_w"], lw["pool_scale"])


def _fft1_kernel(p_ref, q_ref, f_ref, o_ref):
    for j in range(p_ref.shape[2]):
        rhs = jnp.concatenate([p_ref[0, :, j, :], q_ref[0, :, j, :]], axis=0).astype(BF16)
        o_ref[0, :, j, :] = _dot(f_ref[...], rhs)


def _fft1_call(p, q, tabs):
    B, S, W = p.shape
    n2 = S // FFT_N1
    pv = p.reshape(B, FFT_N1, n2, W)
    qv = q.reshape(B, FFT_N1, n2, W)
    blk = pl.BlockSpec((1, FFT_N1, SUBLANES, W), lambda b, j: (b, 0, j, 0))
    return pl.pallas_call(
        _fft1_kernel,
        grid=(B, n2 // SUBLANES),
        in_specs=[blk, blk, _const_spec(tabs["f1"].shape)],
        out_specs=pl.BlockSpec((1, 2 * FFT_N1, SUBLANES, W), lambda b, j: (b, 0, j, 0)),
        out_shape=jax.ShapeDtypeStruct((B, 2 * FFT_N1, n2, W), F32),
        compiler_params=_params("parallel", "parallel"),
        name="fft_stage1",
    )(pv, qv, tabs["f1"])


def _fft2_kernel(a_ref, gc_ref, gs_ref, o_ref, *, kb):
    for j in range(kb):
        f = _dot(gc_ref[j], a_ref[0, 0, j].astype(BF16)) + _dot(gs_ref[j], a_ref[0, 1, j].astype(BF16))
        o_ref[0, :, j, :] = f


def _fft2_call(a, tabs, seq, kb):
    B = a.shape[0]
    n2 = seq // FFT_N1
    av = a.reshape(B, 2, FFT_N1, n2, W_BR)
    kern = functools.partial(_fft2_kernel, kb=kb)
    out = pl.pallas_call(
        kern,
        grid=(B, FFT_N1 // kb),
        in_specs=[pl.BlockSpec((1, 2, kb, n2, W_BR), lambda b, j: (b, 0, j, 0, 0)),
                  pl.BlockSpec((kb, n2, n2), lambda b, j: (j, 0, 0)),
                  pl.BlockSpec((kb, n2, n2), lambda b, j: (j, 0, 0))],
        out_specs=pl.BlockSpec((1, n2, kb, W_BR), lambda b, j: (b, 0, j, 0)),
        out_shape=jax.ShapeDtypeStruct((B, n2, FFT_N1, W_BR), F32),
        compiler_params=_params("parallel", "parallel"),
        name="fft_stage2",
    )(av, tabs["g2c"], tabs["g2s"])
    return out.reshape(B, seq, W_BR)


def _mixout_kernel(x_ref, mod_ref, g_ref, wg_ref, oa_ref, wa_ref, cb_ref, wb_ref, pc_ref, wc_ref,
                   f_ref, wd_ref, wo_ref, o_ref, *, parts):
    shift = mod_ref[0, 3:4, :]
    scale = mod_ref[0, 4:5, :]
    gate = mod_ref[0, 5:6, :]
    d = x_ref.shape[-1]
    rows = x_ref.shape[1] // parts
    hbs = []
    for p in range(parts):
        x = x_ref[0, p * rows:(p + 1) * rows, :]
        hbs.append(((_rms(x) * g_ref[2:3, :]) * (1.0 + scale) + shift).astype(BF16))
    ys = []
    for p in range(parts):
        sl = slice(p * rows, (p + 1) * rows)
        merged = None
        for br, (b_ref, w_ref) in enumerate(((oa_ref, wa_ref), (cb_ref, wb_ref), (pc_ref, wc_ref), (f_ref, wd_ref))):
            gl = _sigmoid(_dot(hbs[p], wg_ref[:, br * d:(br + 1) * d]))
            if br == 0:
                yb = lax.dot_general(b_ref[0, :, sl], w_ref[...], (((0,), (0,)), ((), ())),
                                     preferred_element_type=F32)
            else:
                yb = _dot(b_ref[0, sl, :].astype(BF16), w_ref[...])
            term = gl * yb
            merged = term if merged is None else merged + term
        ys.append(_dot(merged.astype(BF16), wo_ref[...]))
    for p in range(parts):
        sl = slice(p * rows, (p + 1) * rows)
        o_ref[0, sl, :] = x_ref[0, sl, :] + gate * (_rms(ys[p]) * g_ref[3:4, :])


def _mixout_call(x, mod_l, g_l, lw, oa, cb, pc, f, tm):
    B, S, D = x.shape
    tok = lambda w: pl.BlockSpec((1, tm, w), lambda b, i: (b, i, 0))
    return pl.pallas_call(
        functools.partial(_mixout_kernel, parts=2 if tm % 512 == 0 else 1),
        grid=(B, S // tm),
        in_specs=[tok(D),
                  pl.BlockSpec((1, N_MOD, D), lambda b, i: (b, 0, 0)),
                  _const_spec(g_l.shape),
                  _const_spec(lw["wg"].shape),
                  pl.BlockSpec((1, N_HEADS * V_HEAD, tm), lambda b, i: (b, 0, i)), _const_spec(lw["wa"].shape),
                  tok(CONV_CH), _const_spec(lw["wb"].shape),
                  tok(W_BR), _const_spec(lw["wc"].shape),
                  tok(W_BR), _const_spec(lw["wd"].shape),
                  _const_spec(lw["wo"].shape)],
        out_specs=tok(D),
        out_shape=jax.ShapeDtypeStruct(x.shape, F32),
        compiler_params=_params("parallel", "parallel"),
        name="mixer_out",
    )(x, mod_l, g_l, lw["wg"], oa, lw["wa"], cb, lw["wb"], pc, lw["wc"], f, lw["wd"], lw["wo"])


def _rot_half_cols(w):
    half = w.shape[-1] // 2
    return jnp.concatenate([-w[..., half:], w[..., :half]], axis=-1)


def _tables(seq):
    f32 = np.float32
    pos = np.arange(seq, dtype=f32)
    inv = (f32(ROPE_THETA) ** (-np.arange(0, QK_ROPE, 2, dtype=f32) / f32(QK_ROPE))).astype(f32)
    ang = pos[:, None] * inv[None, :]
    cos, sin = np.cos(ang).astype(f32), np.sin(ang).astype(f32)
    cc = np.concatenate([cos, cos], axis=-1)
    ss = np.concatenate([sin, sin], axis=-1)
    z64 = np.zeros((seq, QK_NOPE), f32)
    z32 = np.zeros((seq, LANES - QK_NOPE - QK_ROPE), f32)
    sm_scale = QK_DIM ** -0.5 * math.log2(math.e)
    cosk = np.concatenate([z64, cc, z32], axis=-1)
    sink = np.concatenate([z64, ss, z32], axis=-1)
    cosr = np.tile(cc, (1, N_HEADS)) * f32(sm_scale)
    sinr = np.tile(ss, (1, N_HEADS)) * f32(sm_scale)

    def cos_sin(m, period):
        th = m.astype(np.float64) * (2.0 * math.pi / period)
        return np.cos(th), np.sin(th)

    def dft(n):
        a = np.arange(n, dtype=np.int64)
        return cos_sin((a[:, None] * a[None, :]) % n, n)

    bf = lambda a: a.astype(f32).astype(BF16)
    c_ch, s_ch = dft(GC)
    cs = bf(np.concatenate([c_ch, s_ch], axis=-1) * GC ** -0.5)

    n1 = FFT_N1
    n2 = seq // n1
    c1, s1 = dft(n1)
    f1 = bf(np.concatenate([np.concatenate([c1, -s1], axis=1), np.concatenate([-s1, -c1], axis=1)], axis=0)
            * n1 ** -0.5)
    k1 = np.arange(n1, dtype=np.int64)[:, None, None]
    k2 = np.arange(n2, dtype=np.int64)[None, :, None]
    t2 = np.arange(n2, dtype=np.int64)[None, None, :]
    c2, s2 = cos_sin(((k1 + n1 * k2) * t2) % seq, seq)
    g2c = bf(c2 * n2 ** -0.5)
    g2s = bf(s2 * n2 ** -0.5)
    return dict(cosr=cosr, sinr=sinr, q_scale=sm_scale, cosk=cosk, sink=sink, cs=cs, f1=f1, g2c=g2c, g2s=g2s)


def _layer_weights(l, w_in, q_norm_g, w_uq, kv_norm_g, w_ukv, w_a, conv_w, conv_b, conv_ln_g,
                   conv_ln_b, w_b, pool_w, pool_scale, w_c, w_d, w_out):
    D = w_in.shape[1]
    wi = w_in[l]
    assert Q_PAD - Q_LORA == QK_ROPE
    w1 = jnp.concatenate([wi[:, Q_LORA:Q_LORA + KV_LORA], wi[:, :Q_LORA], wi[:, Q_LORA + KV_LORA:4 * W_BR]],
                         axis=-1).astype(BF16)

    uq = w_uq[l].reshape(Q_LORA, N_HEADS, QK_NOPE + QK_ROPE)
    nope, rope = uq[..., :QK_NOPE], uq[..., QK_NOPE:]
    wq = jnp.concatenate([nope.reshape(Q_LORA, N_HEADS * QK_NOPE), rope.reshape(Q_LORA, N_HEADS * QK_ROPE),
                          _rot_half_cols(rope).reshape(Q_LORA, N_HEADS * QK_ROPE)], axis=-1)
    wq = jnp.pad(wq, ((0, Q_PAD - Q_LORA), (0, 0))).astype(BF16)

    ukv = w_ukv[l].reshape(KV_LORA, N_HEADS, QK_NOPE + V_HEAD)
    zk = jnp.zeros((KV_LORA, N_HEADS, LANES - QK_NOPE), F32)
    wk = jnp.concatenate([ukv[..., :QK_NOPE], zk], axis=-1).reshape(KV_LORA, N_HEADS * LANES).astype(BF16)
    wv = ukv[..., QK_NOPE:].reshape(KV_LORA, N_HEADS * V_HEAD).astype(BF16)

    gq = jnp.pad(q_norm_g[l], (0, Q_PAD - Q_LORA)).reshape(1, Q_PAD)
    return dict(
        w1=w1, wq=wq, wk=wk, wv=wv, gq=gq, gkv=kv_norm_g[l].reshape(1, KV_LORA),
        wg=wi[:, 4 * W_BR:].astype(BF16), wa=w_a[l].astype(BF16),
        wb=w_b[l].astype(BF16), wc=w_c[l].astype(BF16), wd=w_d[l].astype(BF16), wo=w_out[l].astype(BF16),
        conv_w=conv_w[l], conv_b=conv_b[l].reshape(1, CONV_CH),
        ln_g=conv_ln_g[l].reshape(1, CONV_CH), ln_b=conv_ln_b[l].reshape(1, CONV_CH),
        pool_w=pool_w[l].astype(BF16), pool_scale=pool_scale[l].reshape(1, W_BR))


def _tiles(seq):
    t = lambda n: min(n, seq)
    return dict(ffn=t(1024), mix=t(1024), mixout=t(512), tq=t(512), tk=t(4096), local=t(512))


def kernel(x, c, ada_w, ada_b, norm_g, ffn1_w_in, ffn1_w_out, ffn2_w_in, ffn2_w_out, w_in, q_norm_g,
           w_uq, kv_norm_g, w_ukv, w_a, conv_w, conv_b, conv_ln_g, conv_ln_b, w_b, pool_w, pool_scale,
           w_c, w_d, w_out):
    B, S, D = x.shape
    L = ada_w.shape[0]
    assert D == D_MODEL and S % (FFT_N1 * 16) == 0
    ts = _tiles(S)
    tabs = _tables(S)
    mod = _mod_call(c, ada_w, ada_b).reshape(L, B, N_MOD, D)
    ffn_w = [(_to_bf16(wi), _to_bf16(wo)) for wi, wo in ((ffn1_w_in, ffn1_w_out), (ffn2_w_in, ffn2_w_out))]
    for l in range(L):
        lw = _layer_weights(l, w_in, q_norm_g, w_uq, kv_norm_g, w_ukv, w_a, conv_w, conv_b, conv_ln_g,
                            conv_ln_b, w_b, pool_w, pool_scale, w_c, w_d, w_out)
        mod_l, g_l = mod[l], norm_g[l]
        x = _ffn_call(x, mod_l, g_l, *ffn_w[0], l, 0, ts["ffn"])
        q, k, v, kn, u, zc, p, qq = _mixin_call(x, mod_l, g_l, lw, tabs, ts["mix"])
        oa = _attn_call(q, k, v, kn, ts["tq"], ts["tk"])
        cb, pc = _local_call(u, zc, lw, ts["local"])
        a = _fft1_call(p, qq, tabs)
        f = _fft2_call(a, tabs, S, 8)
        x = _mixout_call(x, mod_l, g_l, lw, oa, cb, pc, f, ts["mixout"])
        x = _ffn_call(x, mod_l, g_l, *ffn_w[1], l, 2, ts["ffn"])
    return x
```

```python
import functools
import math

import jax
import jax.numpy as jnp
import numpy as np
from jax import lax
from jax.experimental import pallas as pl
from jax.experimental.pallas import tpu as pltpu

F32 = jnp.float32
BF16 = jnp.bfloat16

D_MODEL = 1024
D_FF = 2816
N_HEADS = 8
Q_LORA = 352
KV_LORA = 128
QK_NOPE = 64
QK_ROPE = 32
V_HEAD = 64
QK_DIM = QK_NOPE + QK_ROPE
BF16_ROWS = 16
V_ROWS = 80
DENOM_MIN = 2.0 ** -80
ROPE_THETA = 10000.0
CONV_CH = 256
CONV_WIDTH = 31
POOL_WINDOWS = (2, 4, 8, 16)
GC = 128
N_GROUPS = 4
W_BR = 512
N_MOD = 9
EPS = 1e-6

LANES = 128
SUBLANES = 8
Q_PAD = 384
HALO = 16
FFT_N1 = 128
VMEM_LIMIT = 56 * 1024 * 1024

C_CKV = 0
C_CQ = C_CKV + KV_LORA
C_ZB = C_CQ + Q_PAD
C_ZC = C_ZB + W_BR
C_ZD = C_ZC + W_BR
W1_COLS = C_ZD + W_BR


def _params(*sem):
    return pltpu.CompilerParams(dimension_semantics=sem, vmem_limit_bytes=VMEM_LIMIT)


def _const_spec(shape):
    zeros = (0,) * len(shape)
    return pl.BlockSpec(shape, lambda *_: zeros, pipeline_mode=pl.Buffered(1))


def _dot(a, b):
    return jnp.dot(a, b, preferred_element_type=F32)


def _rms(x):
    return x * lax.rsqrt(jnp.mean(x * x, axis=-1, keepdims=True) + EPS)


def _sigmoid(x):
    return 1.0 / (1.0 + jnp.exp(-x))


def _mod_kernel(c_ref, w_ref, b_ref, o_ref):
    c = c_ref[...]
    ca = c * _sigmoid(c)
    o_ref[0] = jnp.dot(ca, w_ref[0], preferred_element_type=F32,
                       precision=lax.Precision.HIGHEST) + b_ref[0]


def _mod_call(c, ada_w, ada_b):
    L, D, N = ada_w.shape
    B = c.shape[0]
    tn = 1152
    return pl.pallas_call(
        _mod_kernel,
        grid=(L, N // tn),
        in_specs=[pl.BlockSpec((B, D), lambda l, j: (0, 0)),
                  pl.BlockSpec((1, D, tn), lambda l, j: (l, 0, j)),
                  pl.BlockSpec((1, 1, tn), lambda l, j: (l, 0, j))],
        out_specs=pl.BlockSpec((1, B, tn), lambda l, j: (l, 0, j)),
        out_shape=jax.ShapeDtypeStruct((L, B, N), F32),
        compiler_params=_params("parallel", "parallel"),
        name="adaln_mod",
    )(c, ada_w, ada_b.reshape(L, 1, N))


def _ffn_kernel(x_ref, mod_ref, g_ref, win_ref, wout_ref, o_ref, a_scr, *, sub, chunk, parts):
    shift = mod_ref[0, 3 * sub:3 * sub + 1, :]
    scale = mod_ref[0, 3 * sub + 1:3 * sub + 2, :]
    gate = mod_ref[0, 3 * sub + 2:3 * sub + 3, :]
    g_in = g_ref[2 * sub:2 * sub + 1, :]
    g_out = g_ref[2 * sub + 1:2 * sub + 2, :]
    rows = x_ref.shape[1] // parts
    hbs = []
    for p in range(parts):
        x = x_ref[0, p * rows:(p + 1) * rows, :]
        hbs.append(((_rms(x) * g_in) * (1.0 + scale) + shift).astype(BF16))
    for p in range(parts):
        for c0 in range(0, D_FF, chunk):
            c1 = min(c0 + chunk, D_FF)
            gj = _dot(hbs[p], win_ref[0, :, c0:c1])
            uj = _dot(hbs[p], win_ref[0, :, D_FF + c0:D_FF + c1])
            a_scr[p, :, c0:c1] = (gj * _sigmoid(gj) * uj).astype(BF16)
    for p in range(parts):
        y = _dot(a_scr[p], wout_ref[0])
        x = x_ref[0, p * rows:(p + 1) * rows, :]
        o_ref[0, p * rows:(p + 1) * rows, :] = x + (0.5 * gate) * (_rms(y) * g_out)


def _cast_kernel(w_ref, o_ref):
    o_ref[...] = w_ref[...].astype(o_ref.dtype)


def _to_bf16(w):
    L, R, C = w.shape
    rows = R // 4
    assert R % 4 == 0 and rows % BF16_ROWS == 0
    blk = pl.BlockSpec((1, rows, C), lambda l, i: (l, i, 0))
    return pl.pallas_call(
        _cast_kernel,
        grid=(L, R // rows),
        in_specs=[blk],
        out_specs=blk,
        out_shape=jax.ShapeDtypeStruct(w.shape, BF16),
        compiler_params=_params("parallel", "parallel"),
        name="weight_cast",
    )(w)


def _layer_spec(w, l):
    return pl.BlockSpec((1,) + w.shape[1:], lambda *_: (l,) + (0,) * (w.ndim - 1), pipeline_mode=pl.Buffered(1))


def _ffn_call(x, mod_l, g_l, w_in, w_out, l, sub, tm):
    B, S, D = x.shape
    parts = 4 if tm % 1024 == 0 else 1
    kern = functools.partial(_ffn_kernel, sub=sub, chunk=256, parts=parts)
    return pl.pallas_call(
        kern,
        grid=(B, S // tm),
        in_specs=[pl.BlockSpec((1, tm, D), lambda b, i: (b, i, 0)),
                  pl.BlockSpec((1, N_MOD, D), lambda b, i: (b, 0, 0)),
                  _const_spec(g_l.shape),
                  _layer_spec(w_in, l),
                  _layer_spec(w_out, l)],
        out_specs=pl.BlockSpec((1, tm, D), lambda b, i: (b, i, 0)),
        out_shape=jax.ShapeDtypeStruct(x.shape, F32),
        scratch_shapes=[pltpu.VMEM((parts, tm // parts, D_FF), BF16)],
        compiler_params=_params("parallel", "parallel"),
        name=f"ffn{sub}",
    )(x, mod_l, g_l, w_in, w_out)


def _mixin_kernel(x_ref, mod_ref, g_ref, w1_ref, gq_ref, gkv_ref, wq_ref, wk_ref, wv_ref,
                  cr_ref, sr_ref, ck_ref, sk_ref, cs_ref,
                  q_ref, k_ref, v_ref, kn_ref, u_ref, zc_ref, p_ref, qq_ref, *, q_scale):
    x = x_ref[0]
    tm = x.shape[0]
    shift = mod_ref[0, 3:4, :]
    scale = mod_ref[0, 4:5, :]
    h = (_rms(x) * g_ref[2:3, :]) * (1.0 + scale) + shift
    hb = h.astype(BF16)
    z = _dot(hb, w1_ref[...])

    cq = z[:, C_CQ:C_CQ + Q_PAD]
    qlane = lax.broadcasted_iota(jnp.int32, (1, Q_PAD), 1)
    cqm = jnp.where(qlane < Q_LORA, cq, 0.0)
    cqn = cqm * lax.rsqrt(jnp.sum(cqm * cqm, axis=-1, keepdims=True) * (1.0 / Q_LORA) + EPS)
    cqn = (cqn * gq_ref[...]).astype(BF16)
    qq = _dot(cqn, wq_ref[...])
    nope = qq[:, :N_HEADS * QK_NOPE] * q_scale
    w_rope = N_HEADS * QK_ROPE
    roped = (qq[:, N_HEADS * QK_NOPE:N_HEADS * QK_NOPE + w_rope] * cr_ref[...]
             + qq[:, N_HEADS * QK_NOPE + w_rope:] * sr_ref[...])
    for blk in range(N_HEADS * QK_NOPE // LANES):
        nt = nope[:, blk * LANES:(blk + 1) * LANES].T.astype(BF16)
        for i in range(LANES // QK_NOPE):
            q_ref[0, blk * (LANES // QK_NOPE) + i, 0:QK_NOPE, :] = nt[i * QK_NOPE:(i + 1) * QK_NOPE]
    for blk in range(w_rope // LANES):
        rt = roped[:, blk * LANES:(blk + 1) * LANES].T.astype(BF16)
        for i in range(LANES // QK_ROPE):
            q_ref[0, blk * (LANES // QK_ROPE) + i, QK_NOPE:QK_NOPE + QK_ROPE, :] = rt[i * QK_ROPE:(i + 1) * QK_ROPE]
    zero_rows = jnp.zeros((LANES - QK_NOPE - QK_ROPE, tm), BF16)
    for hd in range(N_HEADS):
        q_ref[0, hd, QK_NOPE + QK_ROPE:, :] = zero_rows

    kr_at = Q_LORA - (Q_PAD - LANES)
    kr = pltpu.roll(cq[:, Q_PAD - LANES:], (QK_NOPE - kr_at) % LANES, 1)
    klane = lax.broadcasted_iota(jnp.int32, (1, LANES), 1)
    kr_rot = jnp.where(klane < QK_NOPE + QK_ROPE // 2,
                       -pltpu.roll(kr, LANES - QK_ROPE // 2, 1), pltpu.roll(kr, QK_ROPE // 2, 1))
    k_rope = kr * ck_ref[...] + kr_rot * sk_ref[...]

    ckv = z[:, C_CKV:C_CKV + KV_LORA]
    ckvn = (_rms(ckv) * gkv_ref[...]).astype(BF16)
    kk = _dot(ckvn, wk_ref[...])
    ones_lane = (klane == QK_DIM).astype(F32)
    for hd in range(N_HEADS):
        kh = kk[:, hd * LANES:(hd + 1) * LANES] + k_rope
        k_ref[0, hd] = (kh + ones_lane).astype(BF16)
        norm2 = jnp.max(jnp.sum(kh * kh, axis=-1, keepdims=True), axis=0, keepdims=True)
        kn_ref[0, hd] = jnp.broadcast_to(norm2, (SUBLANES, LANES))
    vv = _dot(ckvn, wv_ref[...])
    ones_rows = (lax.broadcasted_iota(jnp.int32, (V_ROWS - V_HEAD, tm), 0) == 0).astype(BF16)
    for blk in range(N_HEADS * V_HEAD // LANES):
        vt = vv[:, blk * LANES:(blk + 1) * LANES].T.astype(BF16)
        for i in range(LANES // V_HEAD):
            v_ref[0, blk * (LANES // V_HEAD) + i, 0:V_HEAD, :] = vt[i * V_HEAD:(i + 1) * V_HEAD]
    for hd in range(N_HEADS):
        v_ref[0, hd, V_HEAD:, :] = ones_rows

    a = z[:, C_ZB:C_ZB + CONV_CH]
    gt = z[:, C_ZB + CONV_CH:C_ZB + 2 * CONV_CH]
    u_ref[0] = a * _sigmoid(gt)

    zc_ref[0] = z[:, C_ZC:C_ZC + W_BR]

    for g in range(N_GROUPS):
        zd = z[:, C_ZD + g * GC:C_ZD + (g + 1) * GC].astype(BF16)
        pq = _dot(zd, cs_ref[...])
        p_ref[0, :, g * GC:(g + 1) * GC] = pq[:, :GC]
        qq_ref[0, :, g * GC:(g + 1) * GC] = pq[:, GC:]


def _mixin_call(x, mod_l, g_l, lw, tabs, tm):
    B, S, D = x.shape
    tok = lambda w: pl.BlockSpec((1, tm, w), lambda b, i: (b, i, 0))
    head = pl.BlockSpec((1, N_HEADS, tm, LANES), lambda b, i: (b, 0, i, 0))
    head_t = pl.BlockSpec((1, N_HEADS, LANES, tm), lambda b, i: (b, 0, 0, i))
    tab = lambda w: pl.BlockSpec((tm, w), lambda b, i: (i, 0))
    hshape = jax.ShapeDtypeStruct((B, N_HEADS, S, LANES), BF16)
    hshape_t = jax.ShapeDtypeStruct((B, N_HEADS, LANES, S), BF16)
    return pl.pallas_call(
        functools.partial(_mixin_kernel, q_scale=tabs["q_scale"]),
        grid=(B, S // tm),
        in_specs=[tok(D),
                  pl.BlockSpec((1, N_MOD, D), lambda b, i: (b, 0, 0)),
                  _const_spec(g_l.shape),
                  _const_spec(lw["w1"].shape),
                  _const_spec(lw["gq"].shape),
                  _const_spec(lw["gkv"].shape),
                  _const_spec(lw["wq"].shape),
                  _const_spec(lw["wk"].shape),
                  _const_spec(lw["wv"].shape),
                  tab(N_HEADS * QK_ROPE), tab(N_HEADS * QK_ROPE), tab(LANES), tab(LANES),
                  _const_spec(tabs["cs"].shape)],
        out_specs=[head_t, head, pl.BlockSpec((1, N_HEADS, V_ROWS, tm), lambda b, i: (b, 0, 0, i)),
                   pl.BlockSpec((1, N_HEADS, SUBLANES, LANES), lambda b, i: (b, 0, i, 0)),
                   tok(CONV_CH), tok(W_BR), tok(W_BR), tok(W_BR)],
        out_shape=[hshape_t, hshape, jax.ShapeDtypeStruct((B, N_HEADS, V_ROWS, S), BF16),
                   jax.ShapeDtypeStruct((B, N_HEADS, (S // tm) * SUBLANES, LANES), F32),
                   jax.ShapeDtypeStruct((B, S, CONV_CH), F32),
                   jax.ShapeDtypeStruct((B, S, W_BR), F32),
                   jax.ShapeDtypeStruct((B, S, W_BR), F32),
                   jax.ShapeDtypeStruct((B, S, W_BR), F32)],
        compiler_params=_params("parallel", "parallel"),
        name="mixer_in",
    )(x, mod_l, g_l, lw["w1"], lw["gq"], lw["gkv"], lw["wq"], lw["wk"], lw["wv"],
      tabs["cosr"], tabs["sinr"], tabs["cosk"], tabs["sink"], tabs["cs"])


def _attn_kernel(qt_ref, k_ref, vt_ref, kn_ref, o_ref, acc_scr, m_scr, *, tq, tk):
    seq = k_ref.shape[2]
    nq, nk = seq // tq, seq // tk
    k_norm2 = jnp.max(kn_ref[0, 0])

    def exact_tile(t, qoff):
        qt = qt_ref[0, 0, :, pl.ds(qoff, tq)]
        m_scr[...] = jnp.full(m_scr.shape, -jnp.inf, F32)
        acc_scr[t, 0:V_ROWS, :] = jnp.zeros((V_ROWS, tq), F32)

        def chunk(j, carry):
            off = pl.multiple_of(j * tk, tk)
            st = _dot(k_ref[0, 0, pl.ds(off, tk), :], qt)
            m_prev = m_scr[...]
            m_new = jnp.maximum(m_prev, jnp.max(st, axis=0, keepdims=True))
            pt = jnp.exp2(st - m_new).astype(BF16)
            pv = _dot(vt_ref[0, 0, :, pl.ds(off, tk)], pt)
            acc_scr[t, 0:V_ROWS, :] = jnp.exp2(m_prev - m_new) * acc_scr[t, 0:V_ROWS, :] + pv
            m_scr[...] = m_new
            return carry

        lax.fori_loop(0, nk, chunk, 0)

    def bounded_tile(t, qoff):
        qt = qt_ref[0, 0, :, pl.ds(qoff, tq)]
        q32 = qt.astype(F32)
        bound = jnp.sqrt(jnp.sum(q32 * q32, axis=0, keepdims=True) * k_norm2)
        first = lax.broadcasted_iota(jnp.int32, (BF16_ROWS, tq), 0) == 0
        stab = jnp.where(first, -bound, 0.0).astype(BF16)
        qs = jnp.concatenate([qt[0:QK_DIM], stab, qt[QK_DIM + BF16_ROWS:]], axis=0)
        acc = None
        for j in range(nk):
            st = _dot(k_ref[0, 0, j * tk:(j + 1) * tk, :], qs)
            pv = _dot(vt_ref[0, 0, :, j * tk:(j + 1) * tk], jnp.exp2(st).astype(BF16))
            acc = pv if acc is None else acc + pv
        acc_scr[t, 0:V_ROWS, :] = acc
        return jnp.min(acc[V_HEAD:V_HEAD + 1, :]) >= DENOM_MIN

    group = acc_scr.shape[0]

    def q_group(gi, carry):
        offs = [pl.multiple_of((gi * group + t) * tq, tq) for t in range(group)]
        ok = None
        for t in range(group):
            ok_t = bounded_tile(t, offs[t])
            ok = ok_t if ok is None else jnp.logical_and(ok, ok_t)

        @pl.when(jnp.logical_not(ok))
        def _():
            for t in range(group):
                exact_tile(t, offs[t])

        for t in range(group):
            denom = acc_scr[t, V_HEAD:V_HEAD + 1, :]
            o_ref[0, :, pl.ds(offs[t], tq)] = (acc_scr[t, 0:V_HEAD, :] / denom).astype(o_ref.dtype)
        return carry

    lax.fori_loop(0, nq // group, q_group, 0)


def _attn_call(qt, k, vt, kn, tq, tk):
    B, H, S, _ = k.shape
    kern = functools.partial(_attn_kernel, tq=tq, tk=tk)
    return pl.pallas_call(
        kern,
        grid=(B, H),
        in_specs=[pl.BlockSpec((1, 1, LANES, S), lambda b, h: (b, h, 0, 0)),
                  pl.BlockSpec((1, 1, S, LANES), lambda b, h: (b, h, 0, 0)),
                  pl.BlockSpec((1, 1, V_ROWS, S), lambda b, h: (b, h, 0, 0)),
                  pl.BlockSpec((1, 1) + kn.shape[2:], lambda b, h: (b, h, 0, 0))],
        out_specs=pl.BlockSpec((1, V_HEAD, S), lambda b, h: (b, h, 0)),
        out_shape=jax.ShapeDtypeStruct((B, H * V_HEAD, S), BF16),
        scratch_shapes=[pltpu.VMEM((math.gcd(S // tq, 4), V_ROWS, tq), F32), pltpu.VMEM((1, tq), F32)],
        compiler_params=_params("parallel", "arbitrary"),
        name="attention",
    )(qt, k, vt, kn)


def _local_kernel(u_ref, up_ref, un_ref, z_ref, zp_ref, zn_ref, cw_ref, cb_ref, lg_ref, lb_ref,
                  pw_ref, ps_ref, oc_ref, op_ref, ubuf, zbuf, ush, sbuf, *, ts, seq, rc, rp):
    i = pl.program_id(1)
    has_prev = i > 0
    has_next = i < pl.num_programs(1) - 1
    ubuf[0:HALO, :] = jnp.where(has_prev, up_ref[0], 0.0)
    ubuf[HALO:HALO + ts, :] = u_ref[0]
    ubuf[HALO + ts:, :] = jnp.where(has_next, un_ref[0], 0.0)
    zbuf[0:HALO, :] = jnp.where(has_prev, zp_ref[0], 0.0)
    zbuf[HALO:HALO + ts, :] = z_ref[0]
    zbuf[HALO + ts:2 * HALO + ts, :] = jnp.where(has_next, zn_ref[0], 0.0)
    zbuf[2 * HALO + ts:, :] = jnp.zeros((SUBLANES, W_BR), F32)

    span = ts + 2 * HALO - SUBLANES
    for s in range(1, SUBLANES):
        ush[s - 1, 0:span, :] = ubuf[s:s + span, :]

    pad = CONV_WIDTH // 2
    for r in range(0, ts, rc):
        acc = jnp.zeros((rc, CONV_CH), F32)
        for kk in range(CONV_WIDTH):
            st = HALO + r + kk - pad
            s, base = st % SUBLANES, st - st % SUBLANES
            tap = ubuf[base:base + rc, :] if s == 0 else ush[s - 1, base:base + rc, :]
            acc = acc + tap * cw_ref[kk:kk + 1, :]
        y = acc + cb_ref[...]
        mu = jnp.mean(y, axis=-1, keepdims=True)
        yc = y - mu
        yn = yc * lax.rsqrt(jnp.mean(yc * yc, axis=-1, keepdims=True) + EPS)
        yn = yn * lg_ref[...] + lb_ref[...]
        oc_ref[0, r:r + rc, :] = (yn * _sigmoid(yn)).astype(oc_ref.dtype)

    assert all(w == 2 ** (g + 1) for g, w in enumerate(POOL_WINDOWS))
    rows_m = ts + 2 * HALO
    for m in range(1, N_GROUPS):
        step = 2 ** (m - 1)
        src = zbuf if m == 1 else sbuf.at[m - 2]
        lanes = slice(m * GC, N_GROUPS * GC)
        for c in range(0, rows_m, LANES):
            n = min(LANES, rows_m - c)
            sbuf[m - 1, c:c + n, lanes] = src[c:c + n, lanes] + src[c + step:c + step + n, lanes]
        rows_m -= SUBLANES

    for r in range(0, ts, rp):
        t = i * ts + r + lax.broadcasted_iota(jnp.int32, (rp, 1), 0)
        for g, w in enumerate(POOL_WINDOWS):
            lo = w // 2
            hi = w - 1 - lo
            cols = slice(g * GC, (g + 1) * GC)
            half = zbuf if g == 0 else sbuf.at[g - 1]
            win = half[HALO + r - lo:HALO + r - lo + rp, cols] + half[HALO + r:HALO + r + rp, cols]
            cnt = (jnp.minimum(t + hi + 1, seq) - jnp.maximum(t - lo, 0)).astype(F32)
            d = win / cnt - zbuf[HALO + r:HALO + r + rp, cols]
            yg = _dot(d.astype(BF16), pw_ref[g]) * ps_ref[:, cols]
            op_ref[0, r:r + rp, cols] = yg.astype(op_ref.dtype)


def _local_call(u, zc, lw, ts):
    B, S, _ = u.shape
    nh = ts // HALO
    last = S // HALO - 1
    cur = lambda w: pl.BlockSpec((1, ts, w), lambda b, i: (b, i, 0))
    prev = lambda w: pl.BlockSpec((1, HALO, w), lambda b, i: (b, jnp.maximum(i * nh - 1, 0), 0))
    nxt = lambda w: pl.BlockSpec((1, HALO, w), lambda b, i: (b, jnp.minimum((i + 1) * nh, last), 0))
    kern = functools.partial(_local_kernel, ts=ts, seq=S, rc=64, rp=min(ts, 256))
    return pl.pallas_call(
        kern,
        grid=(B, S // ts),
        in_specs=[cur(CONV_CH), prev(CONV_CH), nxt(CONV_CH), cur(W_BR), prev(W_BR), nxt(W_BR),
                  _const_spec(lw["conv_w"].shape), _const_spec(lw["conv_b"].shape),
                  _const_spec(lw["ln_g"].shape), _const_spec(lw["ln_b"].shape),
                  _const_spec(lw["pool_w"].shape), _const_spec(lw["pool_scale"].shape)],
        out_specs=[cur(CONV_CH), cur(W_BR)],
        out_shape=[jax.ShapeDtypeStruct((B, S, CONV_CH), BF16),
                   jax.ShapeDtypeStruct((B, S, W_BR), BF16)],
        scratch_shapes=[pltpu.VMEM((ts + 2 * HALO, CONV_CH), F32),
                        pltpu.VMEM((ts + 2 * HALO + SUBLANES, W_BR), F32),
                        pltpu.VMEM((SUBLANES - 1, ts + 2 * HALO, CONV_CH), F32),
                        pltpu.VMEM((N_GROUPS - 1, ts + 2 * HALO, W_BR), F32)],
        compiler_params=_params("parallel", "parallel"),
        name="conv_pool",
    )(u, u, u, zc, zc, zc, lw["conv_w"], lw["conv_b"], lw["ln_g"], lw["ln_b"],
      lw["pool_w"], lw["pool_scale"])


def _fft1_kernel(p_ref, q_ref, f_ref, o_ref):
    for j in range(p_ref.shape[2]):
        rhs = jnp.concatenate([p_ref[0, :, j, :], q_ref[0, :, j, :]], axis=0).astype(BF16)
        o_ref[0, :, j, :] = _dot(f_ref[...], rhs)


def _fft1_call(p, q, tabs):
    B, S, W = p.shape
    n2 = S // FFT_N1
    pv = p.reshape(B, FFT_N1, n2, W)
    qv = q.reshape(B, FFT_N1, n2, W)
    blk = pl.BlockSpec((1, FFT_N1, SUBLANES, W), lambda b, j: (b, 0, j, 0))
    return pl.pallas_call(
        _fft1_kernel,
        grid=(B, n2 // SUBLANES),
        in_specs=[blk, blk, _const_spec(tabs["f1"].shape)],
        out_specs=pl.BlockSpec((1, 2 * FFT_N1, SUBLANES, W), lambda b, j: (b, 0, j, 0)),
        out_shape=jax.ShapeDtypeStruct((B, 2 * FFT_N1, n2, W), F32),
        compiler_params=_params("parallel", "parallel"),
        name="fft_stage1",
    )(pv, qv, tabs["f1"])


def _fft2_kernel(a_ref, gc_ref, gs_ref, o_ref, *, kb):
    for j in range(kb):
        f = _dot(gc_ref[j], a_ref[0, 0, j].astype(BF16)) + _dot(gs_ref[j], a_ref[0, 1, j].astype(BF16))
        o_ref[0, :, j, :] = f


def _fft2_call(a, tabs, seq, kb):
    B = a.shape[0]
    n2 = seq // FFT_N1
    av = a.reshape(B, 2, FFT_N1, n2, W_BR)
    kern = functools.partial(_fft2_kernel, kb=kb)
    out = pl.pallas_call(
        kern,
        grid=(B, FFT_N1 // kb),
        in_specs=[pl.BlockSpec((1, 2, kb, n2, W_BR), lambda b, j: (b, 0, j, 0, 0)),
                  pl.BlockSpec((kb, n2, n2), lambda b, j: (j, 0, 0)),
                  pl.BlockSpec((kb, n2, n2), lambda b, j: (j, 0, 0))],
        out_specs=pl.BlockSpec((1, n2, kb, W_BR), lambda b, j: (b, 0, j, 0)),
        out_shape=jax.ShapeDtypeStruct((B, n2, FFT_N1, W_BR), F32),
        compiler_params=_params("parallel", "parallel"),
        name="fft_stage2",
    )(av, tabs["g2c"], tabs["g2s"])
    return out.reshape(B, seq, W_BR)


def _mixout_kernel(x_ref, mod_ref, g_ref, wg_ref, oa_ref, wa_ref, cb_ref, wb_ref, pc_ref, wc_ref,
                   f_ref, wd_ref, wo_ref, o_ref, *, parts):
    shift = mod_ref[0, 3:4, :]
    scale = mod_ref[0, 4:5, :]
    gate = mod_ref[0, 5:6, :]
    d = x_ref.shape[-1]
    rows = x_ref.shape[1] // parts
    hbs = []
    for p in range(parts):
        x = x_ref[0, p * rows:(p + 1) * rows, :]
        hbs.append(((_rms(x) * g_ref[2:3, :]) * (1.0 + scale) + shift).astype(BF16))
    ys = []
    for p in range(parts):
        sl = slice(p * rows, (p + 1) * rows)
        merged = None
        for br, (b_ref, w_ref) in enumerate(((oa_ref, wa_ref), (cb_ref, wb_ref), (pc_ref, wc_ref), (f_ref, wd_ref))):
            gl = _sigmoid(_dot(hbs[p], wg_ref[:, br * d:(br + 1) * d]))
            if br == 0:
                yb = lax.dot_general(b_ref[0, :, sl], w_ref[...], (((0,), (0,)), ((), ())),
                                     preferred_element_type=F32)
            else:
                yb = _dot(b_ref[0, sl, :].astype(BF16), w_ref[...])
            term = gl * yb
            merged = term if merged is None else merged + term
        ys.append(_dot(merged.astype(BF16), wo_ref[...]))
    for p in range(parts):
        sl = slice(p * rows, (p + 1) * rows)
        o_ref[0, sl, :] = x_ref[0, sl, :] + gate * (_rms(ys[p]) * g_ref[3:4, :])


def _mixout_call(x, mod_l, g_l, lw, oa, cb, pc, f, tm):
    B, S, D = x.shape
    tok = lambda w: pl.BlockSpec((1, tm, w), lambda b, i: (b, i, 0))
    return pl.pallas_call(
        functools.partial(_mixout_kernel, parts=2 if tm % 512 == 0 else 1),
        grid=(B, S // tm),
        in_specs=[tok(D),
                  pl.BlockSpec((1, N_MOD, D), lambda b, i: (b, 0, 0)),
                  _const_spec(g_l.shape),
                  _const_spec(lw["wg"].shape),
                  pl.BlockSpec((1, N_HEADS * V_HEAD, tm), lambda b, i: (b, 0, i)), _const_spec(lw["wa"].shape),
                  tok(CONV_CH), _const_spec(lw["wb"].shape),
                  tok(W_BR), _const_spec(lw["wc"].shape),
                  tok(W_BR), _const_spec(lw["wd"].shape),
                  _const_spec(lw["wo"].shape)],
        out_specs=tok(D),
        out_shape=jax.ShapeDtypeStruct(x.shape, F32),
        compiler_params=_params("parallel", "parallel"),
        name="mixer_out",
    )(x, mod_l, g_l, lw["wg"], oa, lw["wa"], cb, lw["wb"], pc, lw["wc"], f, lw["wd"], lw["wo"])


def _rot_half_cols(w):
    half = w.shape[-1] // 2
    return jnp.concatenate([-w[..., half:], w[..., :half]], axis=-1)


def _tables(seq):
    f32 = np.float32
    pos = np.arange(seq, dtype=f32)
    inv = (f32(ROPE_THETA) ** (-np.arange(0, QK_ROPE, 2, dtype=f32) / f32(QK_ROPE))).astype(f32)
    ang = pos[:, None] * inv[None, :]
    cos, sin = np.cos(ang).astype(f32), np.sin(ang).astype(f32)
    cc = np.concatenate([cos, cos], axis=-1)
    ss = np.concatenate([sin, sin], axis=-1)
    z64 = np.zeros((seq, QK_NOPE), f32)
    z32 = np.zeros((seq, LANES - QK_NOPE - QK_ROPE), f32)
    sm_scale = QK_DIM ** -0.5 * math.log2(math.e)
    cosk = np.concatenate([z64, cc, z32], axis=-1)
    sink = np.concatenate([z64, ss, z32], axis=-1)
    cosr = np.tile(cc, (1, N_HEADS)) * f32(sm_scale)
    sinr = np.tile(ss, (1, N_HEADS)) * f32(sm_scale)

    def cos_sin(m, period):
        th = m.astype(np.float64) * (2.0 * math.pi / period)
        return np.cos(th), np.sin(th)

    def dft(n):
        a = np.arange(n, dtype=np.int64)
        return cos_sin((a[:, None] * a[None, :]) % n, n)

    bf = lambda a: a.astype(f32).astype(BF16)
    c_ch, s_ch = dft(GC)
    cs = bf(np.concatenate([c_ch, s_ch], axis=-1) * GC ** -0.5)

    n1 = FFT_N1
    n2 = seq // n1
    c1, s1 = dft(n1)
    f1 = bf(np.concatenate([np.concatenate([c1, -s1], axis=1), np.concatenate([-s1, -c1], axis=1)], axis=0)
            * n1 ** -0.5)
    k1 = np.arange(n1, dtype=np.int64)[:, None, None]
    k2 = np.arange(n2, dtype=np.int64)[None, :, None]
    t2 = np.arange(n2, dtype=np.int64)[None, None, :]
    c2, s2 = cos_sin(((k1 + n1 * k2) * t2) % seq, seq)
    g2c = bf(c2 * n2 ** -0.5)
    g2s = bf(s2 * n2 ** -0.5)
    return dict(cosr=cosr, sinr=sinr, q_scale=sm_scale, cosk=cosk, sink=sink, cs=cs, f1=f1, g2c=g2c, g2s=g2s)


def _layer_weights(l, w_in, q_norm_g, w_uq, kv_norm_g, w_ukv, w_a, conv_w, conv_b, conv_ln_g,
                   conv_ln_b, w_b, pool_w, pool_scale, w_c, w_d, w_out):
    D = w_in.shape[1]
    wi = w_in[l]
    assert Q_PAD - Q_LORA == QK_ROPE
    w1 = jnp.concatenate([wi[:, Q_LORA:Q_LORA + KV_LORA], wi[:, :Q_LORA], wi[:, Q_LORA + KV_LORA:4 * W_BR]],
                         axis=-1).astype(BF16)

    uq = w_uq[l].reshape(Q_LORA, N_HEADS, QK_NOPE + QK_ROPE)
    nope, rope = uq[..., :QK_NOPE], uq[..., QK_NOPE:]
    wq = jnp.concatenate([nope.reshape(Q_LORA, N_HEADS * QK_NOPE), rope.reshape(Q_LORA, N_HEADS * QK_ROPE),
                          _rot_half_cols(rope).reshape(Q_LORA, N_HEADS * QK_ROPE)], axis=-1)
    wq = jnp.pad(wq, ((0, Q_PAD - Q_LORA), (0, 0))).astype(BF16)

    ukv = w_ukv[l].reshape(KV_LORA, N_HEADS, QK_NOPE + V_HEAD)
    zk = jnp.zeros((KV_LORA, N_HEADS, LANES - QK_NOPE), F32)
    wk = jnp.concatenate([ukv[..., :QK_NOPE], zk], axis=-1).reshape(KV_LORA, N_HEADS * LANES).astype(BF16)
    wv = ukv[..., QK_NOPE:].reshape(KV_LORA, N_HEADS * V_HEAD).astype(BF16)

    gq = jnp.pad(q_norm_g[l], (0, Q_PAD - Q_LORA)).reshape(1, Q_PAD)
    return dict(
        w1=w1, wq=wq, wk=wk, wv=wv, gq=gq, gkv=kv_norm_g[l].reshape(1, KV_LORA),
        wg=wi[:, 4 * W_BR:].astype(BF16), wa=w_a[l].astype(BF16),
        wb=w_b[l].astype(BF16), wc=w_c[l].astype(BF16), wd=w_d[l].astype(BF16), wo=w_out[l].astype(BF16),
        conv_w=conv_w[l], conv_b=conv_b[l].reshape(1, CONV_CH),
        ln_g=conv_ln_g[l].reshape(1, CONV_CH), ln_b=conv_ln_b[l].reshape(1, CONV_CH),
        pool_w=pool_w[l].astype(BF16), pool_scale=pool_scale[l].reshape(1, W_BR))


def _tiles(seq):
    t = lambda n: min(n, seq)
    return dict(ffn=t(1024), mix=t(1024), mixout=t(512), tq=t(512), tk=t(4096), local=t(512))


def kernel(x, c, ada_w, ada_b, norm_g, ffn1_w_in, ffn1_w_out, ffn2_w_in, ffn2_w_out, w_in, q_norm_g,
           w_uq, kv_norm_g, w_ukv, w_a, conv_w, conv_b, conv_ln_g, conv_ln_b, w_b, pool_w, pool_scale,
           w_c, w_d, w_out):
    B, S, D = x.shape
    L = ada_w.shape[0]
    assert D == D_MODEL and S % (FFT_N1 * 16) == 0
    ts = _tiles(S)
    tabs = _tables(S)
    mod = _mod_call(c, ada_w, ada_b).reshape(L, B, N_MOD, D)
    ffn_w = [(_to_bf16(wi), _to_bf16(wo)) for wi, wo in ((ffn1_w_in, ffn1_w_out), (ffn2_w_in, ffn2_w_out))]
    for l in range(L):
        lw = _layer_weights(l, w_in, q_norm_g, w_uq, kv_norm_g, w_ukv, w_a, conv_w, conv_b, conv_ln_g,
                            conv_ln_b, w_b, pool_w, pool_scale, w_c, w_d, w_out)
        mod_l, g_l = mod[l], norm_g[l]
        x = _ffn_call(x, mod_l, g_l, *ffn_w[0], l, 0, ts["ffn"])
        q, k, v, kn, u, zc, p, qq = _mixin_call(x, mod_l, g_l, lw, tabs, ts["mix"])
        oa = _attn_call(q, k, v, kn, ts["tq"], ts["tk"])
        cb, pc = _local_call(u, zc, lw, ts["local"])
        a = _fft1_call(p, qq, tabs)
        f = _fft2_call(a, tabs, S, 8)
        x = _mixout_call(x, mod_l, g_l, lw, oa, cb, pc, f, ts["mixout"])
        x = _ffn_call(x, mod_l, g_l, *ffn_w[1], l, 2, ts["ffn"])
    return x
```

```python
import functools
import math

import jax
import jax.numpy as jnp
import numpy as np
from jax import lax
from jax.experimental import pallas as pl
from jax.experimental.pallas import tpu as pltpu

F32 = jnp.float32
BF16 = jnp.bfloat16

D_MODEL = 1024
D_FF = 2816
N_HEADS = 8
Q_LORA = 352
KV_LORA = 128
QK_NOPE = 64
QK_ROPE = 32
V_HEAD = 64
QK_DIM = QK_NOPE + QK_ROPE
BF16_ROWS = 16
V_ROWS = 80
DENOM_MIN = 2.0 ** -80
ROPE_THETA = 10000.0
CONV_CH = 256
CONV_WIDTH = 31
POOL_WINDOWS = (2, 4, 8, 16)
GC = 128
N_GROUPS = 4
W_BR = 512
N_MOD = 9
EPS = 1e-6

LANES = 128
SUBLANES = 8
Q_PAD = 384
HALO = 16
FFT_N1 = 128
VMEM_LIMIT = 56 * 1024 * 1024

C_CKV = 0
C_CQ = C_CKV + KV_LORA
C_ZB = C_CQ + Q_PAD
C_ZC = C_ZB + W_BR
C_ZD = C_ZC + W_BR
W1_COLS = C_ZD + W_BR


def _params(*sem):
    return pltpu.CompilerParams(dimension_semantics=sem, vmem_limit_bytes=VMEM_LIMIT)


def _const_spec(shape):
    zeros = (0,) * len(shape)
    return pl.BlockSpec(shape, lambda *_: zeros, pipeline_mode=pl.Buffered(1))


def _dot(a, b):
    return jnp.dot(a, b, preferred_element_type=F32)


def _rms(x):
    return x * lax.rsqrt(jnp.mean(x * x, axis=-1, keepdims=True) + EPS)


def _sigmoid(x):
    return 1.0 / (1.0 + jnp.exp(-x))


def _mod_kernel(c_ref, w_ref, b_ref, o_ref):
    c = c_ref[...]
    ca = c * _sigmoid(c)
    o_ref[0] = jnp.dot(ca, w_ref[0], preferred_element_type=F32,
                       precision=lax.Precision.HIGHEST) + b_ref[0]


def _mod_call(c, ada_w, ada_b):
    L, D, N = ada_w.shape
    B = c.shape[0]
    tn = 1152
    return pl.pallas_call(
        _mod_kernel,
        grid=(L, N // tn),
        in_specs=[pl.BlockSpec((B, D), lambda l, j: (0, 0)),
                  pl.BlockSpec((1, D, tn), lambda l, j: (l, 0, j)),
                  pl.BlockSpec((1, 1, tn), lambda l, j: (l, 0, j))],
        out_specs=pl.BlockSpec((1, B, tn), lambda l, j: (l, 0, j)),
        out_shape=jax.ShapeDtypeStruct((L, B, N), F32),
        compiler_params=_params("parallel", "parallel"),
        name="adaln_mod",
    )(c, ada_w, ada_b.reshape(L, 1, N))


def _ffn_kernel(x_ref, mod_ref, g_ref, win_ref, wout_ref, o_ref, a_scr, *, sub, chunk, parts):
    shift = mod_ref[0, 3 * sub:3 * sub + 1, :]
    scale = mod_ref[0, 3 * sub + 1:3 * sub + 2, :]
    gate = mod_ref[0, 3 * sub + 2:3 * sub + 3, :]
    g_in = g_ref[2 * sub:2 * sub + 1, :]
    g_out = g_ref[2 * sub + 1:2 * sub + 2, :]
    rows = x_ref.shape[1] // parts
    hbs = []
    for p in range(parts):
        x = x_ref[0, p * rows:(p + 1) * rows, :]
        hbs.append(((_rms(x) * g_in) * (1.0 + scale) + shift).astype(BF16))
    for p in range(parts):
        for c0 in range(0, D_FF, chunk):
            c1 = min(c0 + chunk, D_FF)
            gj = _dot(hbs[p], win_ref[0, :, c0:c1])
            uj = _dot(hbs[p], win_ref[0, :, D_FF + c0:D_FF + c1])
            a_scr[p, :, c0:c1] = (gj * _sigmoid(gj) * uj).astype(BF16)
    for p in range(parts):
        y = _dot(a_scr[p], wout_ref[0])
        x = x_ref[0, p * rows:(p + 1) * rows, :]
        o_ref[0, p * rows:(p + 1) * rows, :] = x + (0.5 * gate) * (_rms(y) * g_out)


def _cast_kernel(w_ref, o_ref):
    o_ref[...] = w_ref[...].astype(o_ref.dtype)


def _to_bf16(w):
    L, R, C = w.shape
    rows = R // 4
    assert R % 4 == 0 and rows % BF16_ROWS == 0
    blk = pl.BlockSpec((1, rows, C), lambda l, i: (l, i, 0))
    return pl.pallas_call(
        _cast_kernel,
        grid=(L, R // rows),
        in_specs=[blk],
        out_specs=blk,
        out_shape=jax.ShapeDtypeStruct(w.shape, BF16),
        compiler_params=_params("parallel", "parallel"),
        name="weight_cast",
    )(w)


def _layer_spec(w, l):
    return pl.BlockSpec((1,) + w.shape[1:], lambda *_: (l,) + (0,) * (w.ndim - 1), pipeline_mode=pl.Buffered(1))


def _ffn_call(x, mod_l, g_l, w_in, w_out, l, sub, tm):
    B, S, D = x.shape
    parts = 4 if tm % 1024 == 0 else 1
    kern = functools.partial(_ffn_kernel, sub=sub, chunk=256, parts=parts)
    return pl.pallas_call(
        kern,
        grid=(B, S // tm),
        in_specs=[pl.BlockSpec((1, tm, D), lambda b, i: (b, i, 0)),
                  pl.BlockSpec((1, N_MOD, D), lambda b, i: (b, 0, 0)),
                  _const_spec(g_l.shape),
                  _layer_spec(w_in, l),
                  _layer_spec(w_out, l)],
        out_specs=pl.BlockSpec((1, tm, D), lambda b, i: (b, i, 0)),
        out_shape=jax.ShapeDtypeStruct(x.shape, F32),
        scratch_shapes=[pltpu.VMEM((parts, tm // parts, D_FF), BF16)],
        compiler_params=_params("parallel", "parallel"),
        name=f"ffn{sub}",
    )(x, mod_l, g_l, w_in, w_out)


def _mixin_kernel(x_ref, mod_ref, g_ref, w1_ref, gq_ref, gkv_ref, wq_ref, wk_ref, wv_ref,
                  cr_ref, sr_ref, ck_ref, sk_ref, cs_ref,
                  q_ref, k_ref, v_ref, kn_ref, u_ref, zc_ref, p_ref, qq_ref, *, q_scale):
    x = x_ref[0]
    tm = x.shape[0]
    shift = mod_ref[0, 3:4, :]
    scale = mod_ref[0, 4:5, :]
    h = (_rms(x) * g_ref[2:3, :]) * (1.0 + scale) + shift
    hb = h.astype(BF16)
    z = _dot(hb, w1_ref[...])

    cq = z[:, C_CQ:C_CQ + Q_PAD]
    qlane = lax.broadcasted_iota(jnp.int32, (1, Q_PAD), 1)
    cqm = jnp.where(qlane < Q_LORA, cq, 0.0)
    cqn = cqm * lax.rsqrt(jnp.sum(cqm * cqm, axis=-1, keepdims=True) * (1.0 / Q_LORA) + EPS)
    cqn = (cqn * gq_ref[...]).astype(BF16)
    qq = _dot(cqn, wq_ref[...])
    nope = qq[:, :N_HEADS * QK_NOPE] * q_scale
    w_rope = N_HEADS * QK_ROPE
    roped = (qq[:, N_HEADS * QK_NOPE:N_HEADS * QK_NOPE + w_rope] * cr_ref[...]
             + qq[:, N_HEADS * QK_NOPE + w_rope:] * sr_ref[...])
    for blk in range(N_HEADS * QK_NOPE // LANES):
        nt = nope[:, blk * LANES:(blk + 1) * LANES].T.astype(BF16)
        for i in range(LANES // QK_NOPE):
            q_ref[0, blk * (LANES // QK_NOPE) + i, 0:QK_NOPE, :] = nt[i * QK_NOPE:(i + 1) * QK_NOPE]
    for blk in range(w_rope // LANES):
        rt = roped[:, blk * LANES:(blk + 1) * LANES].T.astype(BF16)
        for i in range(LANES // QK_ROPE):
            q_ref[0, blk * (LANES // QK_ROPE) + i, QK_NOPE:QK_NOPE + QK_ROPE, :] = rt[i * QK_ROPE:(i + 1) * QK_ROPE]
    zero_rows = jnp.zeros((LANES - QK_NOPE - QK_ROPE, tm), BF16)
    for hd in range(N_HEADS):
        q_ref[0, hd, QK_NOPE + QK_ROPE:, :] = zero_rows

    kr_at = Q_LORA - (Q_PAD - LANES)
    kr = pltpu.roll(cq[:, Q_PAD - LANES:], (QK_NOPE - kr_at) % LANES, 1)
    klane = lax.broadcasted_iota(jnp.int32, (1, LANES), 1)
    kr_rot = jnp.where(klane < QK_NOPE + QK_ROPE // 2,
                       -pltpu.roll(kr, LANES - QK_ROPE // 2, 1), pltpu.roll(kr, QK_ROPE // 2, 1))
    k_rope = kr * ck_ref[...] + kr_rot * sk_ref[...]

    ckv = z[:, C_CKV:C_CKV + KV_LORA]
    ckvn = (_rms(ckv) * gkv_ref[...]).astype(BF16)
    kk = _dot(ckvn, wk_ref[...])
    ones_lane = (klane == QK_DIM).astype(F32)
    for hd in range(N_HEADS):
        kh = kk[:, hd * LANES:(hd + 1) * LANES] + k_rope
        k_ref[0, hd] = (kh + ones_lane).astype(BF16)
        norm2 = jnp.max(jnp.sum(kh * kh, axis=-1, keepdims=True), axis=0, keepdims=True)
        kn_ref[0, hd] = jnp.broadcast_to(norm2, (SUBLANES, LANES))
    vv = _dot(ckvn, wv_ref[...])
    ones_rows = (lax.broadcasted_iota(jnp.int32, (V_ROWS - V_HEAD, tm), 0) == 0).astype(BF16)
    for blk in range(N_HEADS * V_HEAD // LANES):
        vt = vv[:, blk * LANES:(blk + 1) * LANES].T.astype(BF16)
        for i in range(LANES // V_HEAD):
            v_ref[0, blk * (LANES // V_HEAD) + i, 0:V_HEAD, :] = vt[i * V_HEAD:(i + 1) * V_HEAD]
    for hd in range(N_HEADS):
        v_ref[0, hd, V_HEAD:, :] = ones_rows

    a = z[:, C_ZB:C_ZB + CONV_CH]
    gt = z[:, C_ZB + CONV_CH:C_ZB + 2 * CONV_CH]
    u_ref[0] = a * _sigmoid(gt)

    zc_ref[0] = z[:, C_ZC:C_ZC + W_BR]

    for g in range(N_GROUPS):
        zd = z[:, C_ZD + g * GC:C_ZD + (g + 1) * GC].astype(BF16)
        pq = _dot(zd, cs_ref[...])
        p_ref[0, :, g * GC:(g + 1) * GC] = pq[:, :GC]
        qq_ref[0, :, g * GC:(g + 1) * GC] = pq[:, GC:]


def _mixin_call(x, mod_l, g_l, lw, tabs, tm):
    B, S, D = x.shape
    tok = lambda w: pl.BlockSpec((1, tm, w), lambda b, i: (b, i, 0))
    head = pl.BlockSpec((1, N_HEADS, tm, LANES), lambda b, i: (b, 0, i, 0))
    head_t = pl.BlockSpec((1, N_HEADS, LANES, tm), lambda b, i: (b, 0, 0, i))
    tab = lambda w: pl.BlockSpec((tm, w), lambda b, i: (i, 0))
    hshape = jax.ShapeDtypeStruct((B, N_HEADS, S, LANES), BF16)
    hshape_t = jax.ShapeDtypeStruct((B, N_HEADS, LANES, S), BF16)
    return pl.pallas_call(
        functools.partial(_mixin_kernel, q_scale=tabs["q_scale"]),
        grid=(B, S // tm),
        in_specs=[tok(D),
                  pl.BlockSpec((1, N_MOD, D), lambda b, i: (b, 0, 0)),
                  _const_spec(g_l.shape),
                  _const_spec(lw["w1"].shape),
                  _const_spec(lw["gq"].shape),
                  _const_spec(lw["gkv"].shape),
                  _const_spec(lw["wq"].shape),
                  _const_spec(lw["wk"].shape),
                  _const_spec(lw["wv"].shape),
                  tab(N_HEADS * QK_ROPE), tab(N_HEADS * QK_ROPE), tab(LANES), tab(LANES),
                  _const_spec(tabs["cs"].shape)],
        out_specs=[head_t, head, pl.BlockSpec((1, N_HEADS, V_ROWS, tm), lambda b, i: (b, 0, 0, i)),
                   pl.BlockSpec((1, N_HEADS, SUBLANES, LANES), lambda b, i: (b, 0, i, 0)),
                   tok(CONV_CH), tok(W_BR), tok(W_BR), tok(W_BR)],
        out_shape=[hshape_t, hshape, jax.ShapeDtypeStruct((B, N_HEADS, V_ROWS, S), BF16),
                   jax.ShapeDtypeStruct((B, N_HEADS, (S // tm) * SUBLANES, LANES), F32),
                   jax.ShapeDtypeStruct((B, S, CONV_CH), F32),
                   jax.ShapeDtypeStruct((B, S, W_BR), F32),
                   jax.ShapeDtypeStruct((B, S, W_BR), F32),
                   jax.ShapeDtypeStruct((B, S, W_BR), F32)],
        compiler_params=_params("parallel", "parallel"),
        name="mixer_in",
    )(x, mod_l, g_l, lw["w1"], lw["gq"], lw["gkv"], lw["wq"], lw["wk"], lw["wv"],
      tabs["cosr"], tabs["sinr"], tabs["cosk"], tabs["sink"], tabs["cs"])


def _attn_kernel(qt_ref, k_ref, vt_ref, kn_ref, o_ref, acc_scr, m_scr, *, tq, tk):
    seq = k_ref.shape[2]
    nq, nk = seq // tq, seq // tk
    k_norm2 = jnp.max(kn_ref[0, 0])

    def exact_tile(t, qoff):
        qt = qt_ref[0, 0, :, pl.ds(qoff, tq)]
        m_scr[...] = jnp.full(m_scr.shape, -jnp.inf, F32)
        acc_scr[t, 0:V_ROWS, :] = jnp.zeros((V_ROWS, tq), F32)

        def chunk(j, carry):
            off = pl.multiple_of(j * tk, tk)
            st = _dot(k_ref[0, 0, pl.ds(off, tk), :], qt)
            m_prev = m_scr[...]
            m_new = jnp.maximum(m_prev, jnp.max(st, axis=0, keepdims=True))
            pt = jnp.exp2(st - m_new).astype(BF16)
            pv = _dot(vt_ref[0, 0, :, pl.ds(off, tk)], pt)
            acc_scr[t, 0:V_ROWS, :] = jnp.exp2(m_prev - m_new) * acc_scr[t, 0:V_ROWS, :] + pv
            m_scr[...] = m_new
            return carry

        lax.fori_loop(0, nk, chunk, 0)

    def bounded_tile(t, qoff):
        qt = qt_ref[0, 0, :, pl.ds(qoff, tq)]
        q32 = qt.astype(F32)
        bound = jnp.sqrt(jnp.sum(q32 * q32, axis=0, keepdims=True) * k_norm2)
        first = lax.broadcasted_iota(jnp.int32, (BF16_ROWS, tq), 0) == 0
        stab = jnp.where(first, -bound, 0.0).astype(BF16)
        qs = jnp.concatenate([qt[0:QK_DIM], stab, qt[QK_DIM + BF16_ROWS:]], axis=0)
        acc = None
        for j in range(nk):
            st = _dot(k_ref[0, 0, j * tk:(j + 1) * tk, :], qs)
            pv = _dot(vt_ref[0, 0, :, j * tk:(j + 1) * tk], jnp.exp2(st).astype(BF16))
            acc = pv if acc is None else acc + pv
        acc_scr[t, 0:V_ROWS, :] = acc
        return jnp.min(acc[V_HEAD:V_HEAD + 1, :]) >= DENOM_MIN

    group = acc_scr.shape[0]

    def q_group(gi, carry):
        offs = [pl.multiple_of((gi * group + t) * tq, tq) for t in range(group)]
        ok = None
        for t in range(group):
            ok_t = bounded_tile(t, offs[t])
            ok = ok_t if ok is None else jnp.logical_and(ok, ok_t)

        @pl.when(jnp.logical_not(ok))
        def _():
            for t in range(group):
                exact_tile(t, offs[t])

        for t in range(group):
            denom = acc_scr[t, V_HEAD:V_HEAD + 1, :]
            o_ref[0, :, pl.ds(offs[t], tq)] = (acc_scr[t, 0:V_HEAD, :] / denom).astype(o_ref.dtype)
        return carry

    lax.fori_loop(0, nq // group, q_group, 0)


def _attn_call(qt, k, vt, kn, tq, tk):
    B, H, S, _ = k.shape
    kern = functools.partial(_attn_kernel, tq=tq, tk=tk)
    return pl.pallas_call(
        kern,
        grid=(B, H),
        in_specs=[pl.BlockSpec((1, 1, LANES, S), lambda b, h: (b, h, 0, 0)),
                  pl.BlockSpec((1, 1, S, LANES), lambda b, h: (b, h, 0, 0)),
                  pl.BlockSpec((1, 1, V_ROWS, S), lambda b, h: (b, h, 0, 0)),
                  pl.BlockSpec((1, 1) + kn.shape[2:], lambda b, h: (b, h, 0, 0))],
        out_specs=pl.BlockSpec((1, V_HEAD, S), lambda b, h: (b, h, 0)),
        out_shape=jax.ShapeDtypeStruct((B, H * V_HEAD, S), BF16),
        scratch_shapes=[pltpu.VMEM((math.gcd(S // tq, 4), V_ROWS, tq), F32), pltpu.VMEM((1, tq), F32)],
        compiler_params=_params("parallel", "arbitrary"),
        name="attention",
    )(qt, k, vt, kn)


def _local_kernel(u_ref, up_ref, un_ref, z_ref, zp_ref, zn_ref, cw_ref, cb_ref, lg_ref, lb_ref,
                  pw_ref, ps_ref, oc_ref, op_ref, ubuf, zbuf, ush, sbuf, *, ts, seq, rc, rp):
    i = pl.program_id(1)
    has_prev = i > 0
    has_next = i < pl.num_programs(1) - 1
    ubuf[0:HALO, :] = jnp.where(has_prev, up_ref[0], 0.0)
    ubuf[HALO:HALO + ts, :] = u_ref[0]
    ubuf[HALO + ts:, :] = jnp.where(has_next, un_ref[0], 0.0)
    zbuf[0:HALO, :] = jnp.where(has_prev, zp_ref[0], 0.0)
    zbuf[HALO:HALO + ts, :] = z_ref[0]
    zbuf[HALO + ts:2 * HALO + ts, :] = jnp.where(has_next, zn_ref[0], 0.0)
    zbuf[2 * HALO + ts:, :] = jnp.zeros((SUBLANES, W_BR), F32)

    span = ts + 2 * HALO - SUBLANES
    for s in range(1, SUBLANES):
        ush[s - 1, 0:span, :] = ubuf[s:s + span, :]

    pad = CONV_WIDTH // 2
    for r in range(0, ts, rc):
        acc = jnp.zeros((rc, CONV_CH), F32)
        for kk in range(CONV_WIDTH):
            st = HALO + r + kk - pad
            s, base = st % SUBLANES, st - st % SUBLANES
            tap = ubuf[base:base + rc, :] if s == 0 else ush[s - 1, base:base + rc, :]
            acc = acc + tap * cw_ref[kk:kk + 1, :]
        y = acc + cb_ref[...]
        mu = jnp.mean(y, axis=-1, keepdims=True)
        yc = y - mu
        yn = yc * lax.rsqrt(jnp.mean(yc * yc, axis=-1, keepdims=True) + EPS)
        yn = yn * lg_ref[...] + lb_ref[...]
        oc_ref[0, r:r + rc, :] = (yn * _sigmoid(yn)).astype(oc_ref.dtype)

    assert all(w == 2 ** (g + 1) for g, w in enumerate(POOL_WINDOWS))
    rows_m = ts + 2 * HALO
    for m in range(1, N_GROUPS):
        step = 2 ** (m - 1)
        src = zbuf if m == 1 else sbuf.at[m - 2]
        lanes = slice(m * GC, N_GROUPS * GC)
        for c in range(0, rows_m, LANES):
            n = min(LANES, rows_m - c)
            sbuf[m - 1, c:c + n, lanes] = src[c:c + n, lanes] + src[c + step:c + step + n, lanes]
        rows_m -= SUBLANES

    for r in range(0, ts, rp):
        t = i * ts + r + lax.broadcasted_iota(jnp.int32, (rp, 1), 0)
        for g, w in enumerate(POOL_WINDOWS):
            lo = w // 2
            hi = w - 1 - lo
            cols = slice(g * GC, (g + 1) * GC)
            half = zbuf if g == 0 else sbuf.at[g - 1]
            win = half[HALO + r - lo:HALO + r - lo + rp, cols] + half[HALO + r:HALO + r + rp, cols]
            cnt = (jnp.minimum(t + hi + 1, seq) - jnp.maximum(t - lo, 0)).astype(F32)
            d = win / cnt - zbuf[HALO + r:HALO + r + rp, cols]
            yg = _dot(d.astype(BF16), pw_ref[g]) * ps_ref[:, cols]
            op_ref[0, r:r + rp, cols] = yg.astype(op_ref.dtype)


def _local_call(u, zc, lw, ts):
    B, S, _ = u.shape
    nh = ts // HALO
    last = S // HALO - 1
    cur = lambda w: pl.BlockSpec((1, ts, w), lambda b, i: (b, i, 0))
    prev = lambda w: pl.BlockSpec((1, HALO, w), lambda b, i: (b, jnp.maximum(i * nh - 1, 0), 0))
    nxt = lambda w: pl.BlockSpec((1, HALO, w), lambda b, i: (b, jnp.minimum((i + 1) * nh, last), 0))
    kern = functools.partial(_local_kernel, ts=ts, seq=S, rc=64, rp=min(ts, 256))
    return pl.pallas_call(
        kern,
        grid=(B, S // ts),
        in_specs=[cur(CONV_CH), prev(CONV_CH), nxt(CONV_CH), cur(W_BR), prev(W_BR), nxt(W_BR),
                  _const_spec(lw["conv_w"].shape), _const_spec(lw["conv_b"].shape),
                  _const_spec(lw["ln_g"].shape), _const_spec(lw["ln_b"].shape),
                  _const_spec(lw["pool_w"].shape), _const_spec(lw["pool_scale"].shape)],
        out_specs=[cur(CONV_CH), cur(W_BR)],
        out_shape=[jax.ShapeDtypeStruct((B, S, CONV_CH), BF16),
                   jax.ShapeDtypeStruct((B, S, W_BR), BF16)],
        scratch_shapes=[pltpu.VMEM((ts + 2 * HALO, CONV_CH), F32),
                        pltpu.VMEM((ts + 2 * HALO + SUBLANES, W_BR), F32),
                        pltpu.VMEM((SUBLANES - 1, ts + 2 * HALO, CONV_CH), F32),
                        pltpu.VMEM((N_GROUPS - 1, ts + 2 * HALO, W_BR), F32)],
        compiler_params=_params("parallel", "parallel"),
        name="conv_pool",
    )(u, u, u, zc, zc, zc, lw["conv_w"], lw["conv_b"], lw["ln_g"], lw["ln_b"],
      lw["pool_w"], lw["pool_scale"])


def _fft1_kernel(p_ref, q_ref, f_ref, o_ref):
    for j in range(p_ref.shape[2]):
        rhs = jnp.concatenate([p_ref[0, :, j, :], q_ref[0, :, j, :]], axis=0).astype(BF16)
        o_ref[0, :, j, :] = _dot(f_ref[...], rhs)


def _fft1_call(p, q, tabs):
    B, S, W = p.shape
    n2 = S // FFT_N1
    pv = p.reshape(B, FFT_N1, n2, W)
    qv = q.reshape(B, FFT_N1, n2, W)
    blk = pl.BlockSpec((1, FFT_N1, SUBLANES, W), lambda b, j: (b, 0, j, 0))
    return pl.pallas_call(
        _fft1_kernel,
        grid=(B, n2 // SUBLANES),
        in_specs=[blk, blk, _const_spec(tabs["f1"].shape)],
        out_specs=pl.BlockSpec((1, 2 * FFT_N1, SUBLANES, W), lambda b, j: (b, 0, j, 0)),
        out_shape=jax.ShapeDtypeStruct((B, 2 * FFT_N1, n2, W), F32),
        compiler_params=_params("parallel", "parallel"),
        name="fft_stage1",
    )(pv, qv, tabs["f1"])


def _fft2_kernel(a_ref, gc_ref, gs_ref, o_ref, *, kb):
    for j in range(kb):
        f = _dot(gc_ref[j], a_ref[0, 0, j].astype(BF16)) + _dot(gs_ref[j], a_ref[0, 1, j].astype(BF16))
        o_ref[0, :, j, :] = f


def _fft2_call(a, tabs, seq, kb):
    B = a.shape[0]
    n2 = seq // FFT_N1
    av = a.reshape(B, 2, FFT_N1, n2, W_BR)
    kern = functools.partial(_fft2_kernel, kb=kb)
    out = pl.pallas_call(
        kern,
        grid=(B, FFT_N1 // kb),
        in_specs=[pl.BlockSpec((1, 2, kb, n2, W_BR), lambda b, j: (b, 0, j, 0, 0)),
                  pl.BlockSpec((kb, n2, n2), lambda b, j: (j, 0, 0)),
                  pl.BlockSpec((kb, n2, n2), lambda b, j: (j, 0, 0))],
        out_specs=pl.BlockSpec((1, n2, kb, W_BR), lambda b, j: (b, 0, j, 0)),
        out_shape=jax.ShapeDtypeStruct((B, n2, FFT_N1, W_BR), F32),
        compiler_params=_params("parallel", "parallel"),
        name="fft_stage2",
    )(av, tabs["g2c"], tabs["g2s"])
    return out.reshape(B, seq, W_BR)


def _mixout_kernel(x_ref, mod_ref, g_ref, wg_ref, oa_ref, wa_ref, cb_ref, wb_ref, pc_ref, wc_ref,
                   f_ref, wd_ref, wo_ref, o_ref, *, parts):
    shift = mod_ref[0, 3:4, :]
    scale = mod_ref[0, 4:5, :]
    gate = mod_ref[0, 5:6, :]
    d = x_ref.shape[-1]
    rows = x_ref.shape[1] // parts
    hbs = []
    for p in range(parts):
        x = x_ref[0, p * rows:(p + 1) * rows, :]
        hbs.append(((_rms(x) * g_ref[2:3, :]) * (1.0 + scale) + shift).astype(BF16))
    ys = []
    for p in range(parts):
        sl = slice(p * rows, (p + 1) * rows)
        merged = None
        for br, (b_ref, w_ref) in enumerate(((oa_ref, wa_ref), (cb_ref, wb_ref), (pc_ref, wc_ref), (f_ref, wd_ref))):
            gl = _sigmoid(_dot(hbs[p], wg_ref[:, br * d:(br + 1) * d]))
            if br == 0:
                yb = lax.dot_general(b_ref[0, :, sl], w_ref[...], (((0,), (0,)), ((), ())),
                                     preferred_element_type=F32)
            else:
                yb = _dot(b_ref[0, sl, :].astype(BF16), w_ref[...])
            term = gl * yb
            merged = term if merged is None else merged + term
        ys.append(_dot(merged.astype(BF16), wo_ref[...]))
    for p in range(parts):
        sl = slice(p * rows, (p + 1) * rows)
        o_ref[0, sl, :] = x_ref[0, sl, :] + gate * (_rms(ys[p]) * g_ref[3:4, :])


def _mixout_call(x, mod_l, g_l, lw, oa, cb, pc, f, tm):
    B, S, D = x.shape
    tok = lambda w: pl.BlockSpec((1, tm, w), lambda b, i: (b, i, 0))
    return pl.pallas_call(
        functools.partial(_mixout_kernel, parts=2 if tm % 512 == 0 else 1),
        grid=(B, S // tm),
        in_specs=[tok(D),
                  pl.BlockSpec((1, N_MOD, D), lambda b, i: (b, 0, 0)),
                  _const_spec(g_l.shape),
                  _const_spec(lw["wg"].shape),
                  pl.BlockSpec((1, N_HEADS * V_HEAD, tm), lambda b, i: (b, 0, i)), _const_spec(lw["wa"].shape),
                  tok(CONV_CH), _const_spec(lw["wb"].shape),
                  tok(W_BR), _const_spec(lw["wc"].shape),
                  tok(W_BR), _const_spec(lw["wd"].shape),
                  _const_spec(lw["wo"].shape)],
        out_specs=tok(D),
        out_shape=jax.ShapeDtypeStruct(x.shape, F32),
        compiler_params=_params("parallel", "parallel"),
        name="mixer_out",
    )(x, mod_l, g_l, lw["wg"], oa, lw["wa"], cb, lw["wb"], pc, lw["wc"], f, lw["wd"], lw["wo"])


def _rot_half_cols(w):
    half = w.shape[-1] // 2
    return jnp.concatenate([-w[..., half:], w[..., :half]], axis=-1)


def _tables(seq):
    f32 = np.float32
    pos = np.arange(seq, dtype=f32)
    inv = (f32(ROPE_THETA) ** (-np.arange(0, QK_ROPE, 2, dtype=f32) / f32(QK_ROPE))).astype(f32)
    ang = pos[:, None] * inv[None, :]
    cos, sin = np.cos(ang).astype(f32), np.sin(ang).astype(f32)
    cc = np.concatenate([cos, cos], axis=-1)
    ss = np.concatenate([sin, sin], axis=-1)
    z64 = np.zeros((seq, QK_NOPE), f32)
    z32 = np.zeros((seq, LANES - QK_NOPE - QK_ROPE), f32)
    sm_scale = QK_DIM ** -0.5 * math.log2(math.e)
    cosk = np.concatenate([z64, cc, z32], axis=-1)
    sink = np.concatenate([z64, ss, z32], axis=-1)
    cosr = np.tile(cc, (1, N_HEADS)) * f32(sm_scale)
    sinr = np.tile(ss, (1, N_HEADS)) * f32(sm_scale)

    def cos_sin(m, period):
        th = m.astype(np.float64) * (2.0 * math.pi / period)
        return np.cos(th), np.sin(th)

    def dft(n):
        a = np.arange(n, dtype=np.int64)
        return cos_sin((a[:, None] * a[None, :]) % n, n)

    bf = lambda a: a.astype(f32).astype(BF16)
    c_ch, s_ch = dft(GC)
    cs = bf(np.concatenate([c_ch, s_ch], axis=-1) * GC ** -0.5)

    n1 = FFT_N1
    n2 = seq // n1
    c1, s1 = dft(n1)
    f1 = bf(np.concatenate([np.concatenate([c1, -s1], axis=1), np.concatenate([-s1, -c1], axis=1)], axis=0)
            * n1 ** -0.5)
    k1 = np.arange(n1, dtype=np.int64)[:, None, None]
    k2 = np.arange(n2, dtype=np.int64)[None, :, None]
    t2 = np.arange(n2, dtype=np.int64)[None, None, :]
    c2, s2 = cos_sin(((k1 + n1 * k2) * t2) % seq, seq)
    g2c = bf(c2 * n2 ** -0.5)
    g2s = bf(s2 * n2 ** -0.5)
    return dict(cosr=cosr, sinr=sinr, q_scale=sm_scale, cosk=cosk, sink=sink, cs=cs, f1=f1, g2c=g2c, g2s=g2s)


def _layer_weights(l, w_in, q_norm_g, w_uq, kv_norm_g, w_ukv, w_a, conv_w, conv_b, conv_ln_g,
                   conv_ln_b, w_b, pool_w, pool_scale, w_c, w_d, w_out):
    D = w_in.shape[1]
    wi = w_in[l]
    assert Q_PAD - Q_LORA == QK_ROPE
    w1 = jnp.concatenate([wi[:, Q_LORA:Q_LORA + KV_LORA], wi[:, :Q_LORA], wi[:, Q_LORA + KV_LORA:4 * W_BR]],
                         axis=-1).astype(BF16)

    uq = w_uq[l].reshape(Q_LORA, N_HEADS, QK_NOPE + QK_ROPE)
    nope, rope = uq[..., :QK_NOPE], uq[..., QK_NOPE:]
    wq = jnp.concatenate([nope.reshape(Q_LORA, N_HEADS * QK_NOPE), rope.reshape(Q_LORA, N_HEADS * QK_ROPE),
                          _rot_half_cols(rope).reshape(Q_LORA, N_HEADS * QK_ROPE)], axis=-1)
    wq = jnp.pad(wq, ((0, Q_PAD - Q_LORA), (0, 0))).astype(BF16)

    ukv = w_ukv[l].reshape(KV_LORA, N_HEADS, QK_NOPE + V_HEAD)
    zk = jnp.zeros((KV_LORA, N_HEADS, LANES - QK_NOPE), F32)
    wk = jnp.concatenate([ukv[..., :QK_NOPE], zk], axis=-1).reshape(KV_LORA, N_HEADS * LANES).astype(BF16)
    wv = ukv[..., QK_NOPE:].reshape(KV_LORA, N_HEADS * V_HEAD).astype(BF16)

    gq = jnp.pad(q_norm_g[l], (0, Q_PAD - Q_LORA)).reshape(1, Q_PAD)
    return dict(
        w1=w1, wq=wq, wk=wk, wv=wv, gq=gq, gkv=kv_norm_g[l].reshape(1, KV_LORA),
        wg=wi[:, 4 * W_BR:].astype(BF16), wa=w_a[l].astype(BF16),
        wb=w_b[l].astype(BF16), wc=w_c[l].astype(BF16), wd=w_d[l].astype(BF16), wo=w_out[l].astype(BF16),
        conv_w=conv_w[l], conv_b=conv_b[l].reshape(1, CONV_CH),
        ln_g=conv_ln_g[l].reshape(1, CONV_CH), ln_b=conv_ln_b[l].reshape(1, CONV_CH),
        pool_w=pool_w[l].astype(BF16), pool_scale=pool_scale[l].reshape(1, W_BR))


def _tiles(seq):
    t = lambda n: min(n, seq)
    return dict(ffn=t(1024), mix=t(1024), mixout=t(512), tq=t(512), tk=t(4096), local=t(1024))


def kernel(x, c, ada_w, ada_b, norm_g, ffn1_w_in, ffn1_w_out, ffn2_w_in, ffn2_w_out, w_in, q_norm_g,
           w_uq, kv_norm_g, w_ukv, w_a, conv_w, conv_b, conv_ln_g, conv_ln_b, w_b, pool_w, pool_scale,
           w_c, w_d, w_out):
    B, S, D = x.shape
    L = ada_w.shape[0]
    assert D == D_MODEL and S % (FFT_N1 * 16) == 0
    ts = _tiles(S)
    tabs = _tables(S)
    mod = _mod_call(c, ada_w, ada_b).reshape(L, B, N_MOD, D)
    ffn_w = [(_to_bf16(wi), _to_bf16(wo)) for wi, wo in ((ffn1_w_in, ffn1_w_out), (ffn2_w_in, ffn2_w_out))]
    for l in range(L):
        lw = _layer_weights(l, w_in, q_norm_g, w_uq, kv_norm_g, w_ukv, w_a, conv_w, conv_b, conv_ln_g,
                            conv_ln_b, w_b, pool_w, pool_scale, w_c, w_d, w_out)
        mod_l, g_l = mod[l], norm_g[l]
        x = _ffn_call(x, mod_l, g_l, *ffn_w[0], l, 0, ts["ffn"])
        q, k, v, kn, u, zc, p, qq = _mixin_call(x, mod_l, g_l, lw, tabs, ts["mix"])
        oa = _attn_call(q, k, v, kn, ts["tq"], ts["tk"])
        cb, pc = _local_call(u, zc, lw, ts["local"])
        a = _fft1_call(p, qq, tabs)
        f = _fft2_call(a, tabs, S, 16)
        x = _mixout_call(x, mod_l, g_l, lw, oa, cb, pc, f, ts["mixout"])
        x = _ffn_call(x, mod_l, g_l, *ffn_w[1], l, 2, ts["ffn"])
    return x
```
